```python
import jax
import jax.numpy as jnp
from jax import lax
import numpy as np

D_MODEL = 1024
BATCH = 8
SEQ = 4096
DEPTH = 1

D_MIX = D_MODEL
D_ATT = D_MIX // 2
ATT_HEAD_DIM = 64
ATT_HEADS = D_ATT // ATT_HEAD_DIM
D_MLSTM = D_MIX - D_ATT
MLSTM_HEADS = 4
MLSTM_HEAD_DIM = D_MLSTM // MLSTM_HEADS
IN_WIDTH = 3 * D_ATT + 4 * D_MLSTM + 2 * MLSTM_HEADS

MOBA_BLOCK = 256
MOBA_TOPK = 3
MOBA_Q_CHUNK = 32
ROPE_THETA = 10000.0

MLSTM_CHUNK = 128
CONV_WIDTH = 4

N_GROUPS = 4
EXPERTS_PER_GROUP = 8
N_EXPERTS = N_GROUPS * EXPERTS_PER_GROUP
TOP_K_EXPERTS = 2
D_EXPERT = 512
MOE_BLOCK_ROWS = 256

NORM_EPS = 1e-6
NEG_INF = -1e30

kernel_name = "hymba_moba_mlstm_hmoe_layer"


def rms_norm(x, g):
    xf = x.astype(jnp.float32)
    y = xf * lax.rsqrt(jnp.mean(xf * xf, axis=-1, keepdims=True) + NORM_EPS)
    return (y * g.astype(jnp.float32)).astype(x.dtype)


def head_rms_norm(h, g, n_heads):
    b, s, w = h.shape
    hf = h.astype(jnp.float32).reshape(b, s, n_heads, w // n_heads)
    hf = hf * lax.rsqrt(jnp.mean(hf * hf, axis=-1, keepdims=True) + NORM_EPS)
    return (hf.reshape(b, s, w) * g.astype(jnp.float32)).astype(h.dtype)


def modulate(x, g, shift, scale):
    return rms_norm(x, g) * (1.0 + scale[:, None, :]) + shift[:, None, :]


def split_heads(t, n_heads):
    b, s, w = t.shape
    return t.reshape(b, s, n_heads, w // n_heads).transpose(0, 2, 1, 3)


def merge_heads(t):
    b, h, s, d = t.shape
    return t.transpose(0, 2, 1, 3).reshape(b, s, h * d)


def rope(t, positions):
    half = t.shape[-1] // 2
    inv_freq = ROPE_THETA ** (-jnp.arange(half, dtype=jnp.float32) / half)
    ang = positions.astype(jnp.float32)[:, None, :, None] * inv_freq
    cos, sin = jnp.cos(ang), jnp.sin(ang)
    tf = t.astype(jnp.float32)
    t1, t2 = tf[..., :half], tf[..., half:]
    return jnp.concatenate([t1 * cos - t2 * sin, t2 * cos + t1 * sin], axis=-1).astype(t.dtype)


def moba_attention(q, k, v):
    b, h, s, dh = q.shape
    nb = -(-s // MOBA_BLOCK)
    s_pad = nb * MOBA_BLOCK
    pad = ((0, 0), (0, 0), (0, s_pad - s), (0, 0))
    kb = jnp.pad(k, pad).reshape(b, h, nb, MOBA_BLOCK, dh)
    vb = jnp.pad(v, pad).reshape(b, h, nb, MOBA_BLOCK, dh)
    scale = dh ** -0.5

    k_mean = jnp.mean(kb.astype(jnp.float32), axis=3)
    gate = jnp.einsum('bhsd,bhnd->bhsn', q.astype(jnp.float32), k_mean)
    q_blk = jnp.arange(s) // MOBA_BLOCK
    past = jnp.arange(nb)[None, :] < q_blk[:, None]
    gate = jnp.where(past, gate, NEG_INF)
    topk = min(MOBA_TOPK, nb)
    _, sel = lax.top_k(gate, topk)
    sel_valid = sel < q_blk[:, None]

    nqc = s // MOBA_Q_CHUNK

    def to_chunks(t):
        t = t.reshape((b, h, nqc, MOBA_Q_CHUNK) + t.shape[3:])
        return jnp.moveaxis(t, 2, 0)

    bi = jnp.arange(b)[:, None, None]
    hi = jnp.arange(h)[None, :, None]

    def chunk(args):
        q_c, sel_c, valid_c, ci = args
        start = ci * MOBA_Q_CHUNK
        j = start // MOBA_BLOCK
        k_own = lax.dynamic_index_in_dim(kb, j, axis=2, keepdims=False)
        v_own = lax.dynamic_index_in_dim(vb, j, axis=2, keepdims=False)
        q_pos = start + jnp.arange(MOBA_Q_CHUNK)
        k_pos = j * MOBA_BLOCK + jnp.arange(MOBA_BLOCK)
        causal = k_pos[None, :] <= q_pos[:, None]
        qf = q_c.astype(jnp.float32) * scale
        s_own = jnp.where(causal, jnp.einsum('bhqd,bhkd->bhqk', qf, k_own.astype(jnp.float32)), NEG_INF)
        scores = [s_own]
        for r in range(topk):
            k_r = kb[bi, hi, sel_c[..., r]]
            s_r = jnp.einsum('bhqd,bhqkd->bhqk', qf, k_r.astype(jnp.float32))
            scores.append(jnp.where(valid_c[..., r, None], s_r, NEG_INF))
        s_all = jnp.stack(scores, axis=3)
        p = jax.nn.softmax(s_all.reshape(b, h, MOBA_Q_CHUNK, -1), axis=-1).reshape(s_all.shape)
        out = jnp.einsum('bhqk,bhkd->bhqd', p[..., 0, :], v_own.astype(jnp.float32))
        for r in range(topk):
            v_r = vb[bi, hi, sel_c[..., r]]
            out = out + jnp.einsum('bhqk,bhqkd->bhqd', p[..., r + 1, :], v_r.astype(jnp.float32))
        return out.astype(v.dtype)

    out = lax.map(chunk, (to_chunks(q), to_chunks(sel), to_chunks(sel_valid),
                          jnp.arange(nqc, dtype=jnp.int32)))
    return jnp.moveaxis(out, 0, 2).reshape(b, h, s, dh)


def mlstm_chunkwise(q, k, v, i_pre, f_pre):
    b, nh, s, dh = q.shape
    L = MLSTM_CHUNK
    nc = s // L
    q = q.reshape(b, nh, nc, L, dh)
    k = k.reshape(b, nh, nc, L, dh)
    v = v.reshape(b, nh, nc, L, dh)
    logf = jax.nn.log_sigmoid(f_pre).reshape(b, nh, nc, L)
    ig = i_pre.reshape(b, nh, nc, L)
    cum = jnp.cumsum(logf, axis=-1)
    a = cum[..., -1]

    g = a[..., None] - cum + ig
    m_loc = jnp.max(g, axis=-1)
    w = jnp.exp(g - m_loc[..., None])
    c_loc = jnp.einsum('bhcs,bhcse,bhcsd->bhced', w, v, k)
    n_loc = jnp.einsum('bhcs,bhcsd->bhcd', w, k)

    def step(carry, inp):
        c_st, n_st, m_st = carry
        a_c, m_loc_c, c_loc_c, n_loc_c = inp
        m_new = jnp.maximum(a_c + m_st, m_loc_c)
        s_prev = jnp.exp(a_c + m_st - m_new)
        s_loc = jnp.exp(m_loc_c - m_new)
        c_new = s_prev[..., None, None] * c_st + s_loc[..., None, None] * c_loc_c
        n_new = s_prev[..., None] * n_st + s_loc[..., None] * n_loc_c
        return (c_new, n_new, m_new), (c_st, n_st, m_st)

    init = (jnp.zeros((b, nh, dh, dh), jnp.float32),
            jnp.zeros((b, nh, dh), jnp.float32),
            jnp.zeros((b, nh), jnp.float32))
    xs = (jnp.moveaxis(a, 2, 0), jnp.moveaxis(m_loc, 2, 0),
          jnp.moveaxis(c_loc, 2, 0), jnp.moveaxis(n_loc, 2, 0))
    _, (c_prev, n_prev, m_prev) = lax.scan(step, init, xs)
    c_prev = jnp.moveaxis(c_prev, 0, 2)
    n_prev = jnp.moveaxis(n_prev, 0, 2)
    m_prev = jnp.moveaxis(m_prev, 0, 2)

    causal = jnp.tril(jnp.ones((L, L), dtype=bool))
    dmat = jnp.where(causal, cum[..., :, None] - cum[..., None, :] + ig[..., None, :], NEG_INF)
    inter = cum + m_prev[..., None]
    m_t = jnp.maximum(inter, jnp.max(dmat, axis=-1))
    s_qk = jnp.einsum('bhctd,bhcsd->bhcts', q, k) * jnp.exp(dmat - m_t[..., None])
    w_inter = jnp.exp(inter - m_t)
    num = (jnp.einsum('bhcts,bhcse->bhcte', s_qk, v)
           + w_inter[..., None] * jnp.einsum('bhced,bhctd->bhcte', c_prev, q))
    den = jnp.sum(s_qk, axis=-1) + w_inter * jnp.einsum('bhcd,bhctd->bhct', n_prev, q)
    h = num / jnp.maximum(jnp.abs(den), jnp.exp(-m_t))[..., None]
    return h.reshape(b, nh, s, dh)


def causal_conv(u, w, bias):
    ch = u.shape[-1]
    out = lax.conv_general_dilated(u, w[:, None, :], window_strides=(1,),
                                   padding=[(CONV_WIDTH - 1, 0)],
                                   dimension_numbers=('NWC', 'WIO', 'NWC'),
                                   feature_group_count=ch)
    return out + bias


def hier_moe(h, w_rg, b_rg, w_re, b_re, w1, w3, w2):
    t, d = h.shape
    gp = jax.nn.softmax((h @ w_rg + b_rg).astype(jnp.float32), axis=-1)
    g_w, g_idx = lax.top_k(gp, 1)
    el = (h @ w_re + b_re).astype(jnp.float32).reshape(t, N_GROUPS, EXPERTS_PER_GROUP)
    el_sel = jnp.take_along_axis(el, g_idx[:, :, None], axis=1)[:, 0]
    ep = jax.nn.softmax(el_sel, axis=-1)
    e_w, e_loc = lax.top_k(ep, TOP_K_EXPERTS)
    weights = g_w * e_w / jnp.sum(e_w, axis=-1, keepdims=True)
    expert_id = g_idx * EXPERTS_PER_GROUP + e_loc

    n_assign = t * TOP_K_EXPERTS
    flat_e = expert_id.reshape(-1)
    flat_w = weights.reshape(-1)
    flat_tok = jnp.repeat(jnp.arange(t, dtype=jnp.int32), TOP_K_EXPERTS)
    order = jnp.argsort(flat_e)
    sorted_e = flat_e[order]
    counts = jnp.bincount(flat_e, length=N_EXPERTS)
    padded = ((counts + MOE_BLOCK_ROWS - 1) // MOE_BLOCK_ROWS) * MOE_BLOCK_ROWS
    pad_end = jnp.cumsum(padded)
    pad_start = pad_end - padded
    cnt_start = jnp.cumsum(counts) - counts
    dest = pad_start[sorted_e] + (jnp.arange(n_assign) - cnt_start[sorted_e])
    n_rows = n_assign + N_EXPERTS * MOE_BLOCK_ROWS
    n_blocks = n_rows // MOE_BLOCK_ROWS
    row_tok = jnp.zeros((n_rows,), jnp.int32).at[dest].set(flat_tok[order])
    row_w = jnp.zeros((n_rows,), jnp.float32).at[dest].set(flat_w[order])
    block_e = jnp.minimum(jnp.searchsorted(pad_end, jnp.arange(n_blocks) * MOE_BLOCK_ROWS, side='right'),
                          N_EXPERTS - 1)
    xs = h[row_tok].reshape(n_blocks, MOE_BLOCK_ROWS, d)

    def expert_block(args):
        xb, e = args
        return (jax.nn.silu(xb @ w1[e]) * (xb @ w3[e])) @ w2[e]

    ys = lax.map(expert_block, (xs, block_e)).reshape(n_rows, d)
    ys = ys * row_w[:, None].astype(h.dtype)
    return jnp.zeros((t, d), h.dtype).at[row_tok].add(ys)


def setup_inputs(seed: int = 0) -> dict:
    key = jax.random.key(seed)
    ks = jax.random.split(key, 24)

    def nrm(k, shape, scale):
        return jax.random.normal(k, shape, jnp.float32) * scale

    x = nrm(ks[0], (BATCH, SEQ, D_MODEL), 1.0)
    c = nrm(ks[1], (BATCH, D_MODEL), 1.0)
    positions = (jnp.arange(SEQ, dtype=jnp.int32)[None, :]
                 + jax.random.randint(ks[2], (BATCH, 1), 0, 1024, dtype=jnp.int32))
    w_ada = nrm(ks[3], (DEPTH, D_MODEL, 6 * D_MODEL), 0.5 * D_MODEL ** -0.5)
    b_ada = nrm(ks[4], (DEPTH, 6 * D_MODEL), 0.02)
    norm1_g = 1.0 + nrm(ks[5], (DEPTH, D_MODEL), 0.02)
    w_in = nrm(ks[6], (DEPTH, D_MODEL, IN_WIDTH), D_MODEL ** -0.5)
    b_gate = jnp.concatenate([
        nrm(ks[7], (DEPTH, MLSTM_HEADS), 0.1),
        jnp.linspace(3.0, 6.0, MLSTM_HEADS, dtype=jnp.float32)[None, :]
        + nrm(ks[8], (DEPTH, MLSTM_HEADS), 0.1)], axis=-1)
    conv_w = nrm(ks[9], (DEPTH, CONV_WIDTH, 2 * D_MLSTM), CONV_WIDTH ** -0.5)
    conv_b = nrm(ks[10], (DEPTH, 2 * D_MLSTM), 0.02)
    attn_out_g = 1.0 + nrm(ks[11], (DEPTH, D_ATT), 0.02)
    mlstm_out_g = 1.0 + nrm(ks[12], (DEPTH, D_MLSTM), 0.02)
    w_out = nrm(ks[13], (DEPTH, D_MIX, D_MODEL), D_MIX ** -0.5)
    norm2_g = 1.0 + nrm(ks[14], (DEPTH, D_MODEL), 0.02)
    w_rg = nrm(ks[15], (DEPTH, D_MODEL, N_GROUPS), D_MODEL ** -0.5)
    b_rg = nrm(ks[16], (DEPTH, N_GROUPS), 0.01)
    w_re = nrm(ks[17], (DEPTH, D_MODEL, N_EXPERTS), D_MODEL ** -0.5)
    b_re = nrm(ks[18], (DEPTH, N_EXPERTS), 0.01)
    w1 = nrm(ks[19], (DEPTH, N_EXPERTS, D_MODEL, D_EXPERT), D_MODEL ** -0.5)
    w3 = nrm(ks[20], (DEPTH, N_EXPERTS, D_MODEL, D_EXPERT), D_MODEL ** -0.5)
    w2 = nrm(ks[21], (DEPTH, N_EXPERTS, D_EXPERT, D_MODEL), D_EXPERT ** -0.5)
    norm_f_g = 1.0 + nrm(ks[22], (D_MODEL,), 0.02)
    return {"x": x, "c": c, "positions": positions, "w_ada": w_ada, "b_ada": b_ada,
            "norm1_g": norm1_g, "w_in": w_in, "b_gate": b_gate, "conv_w": conv_w,
            "conv_b": conv_b, "attn_out_g": attn_out_g, "mlstm_out_g": mlstm_out_g,
            "w_out": w_out, "norm2_g": norm2_g, "w_rg": w_rg, "b_rg": b_rg,
            "w_re": w_re, "b_re": b_re, "w1": w1, "w3": w3, "w2": w2, "norm_f_g": norm_f_g}


def reference(x, c, positions, w_ada, b_ada, norm1_g, w_in, b_gate, conv_w, conv_b,
              attn_out_g, mlstm_out_g, w_out, norm2_g, w_rg, b_rg, w_re, b_re,
              w1, w3, w2, norm_f_g):
    b, s, d = x.shape
    offs = [D_ATT, 2 * D_ATT, 3 * D_ATT, 3 * D_ATT + 2 * D_MLSTM,
            3 * D_ATT + 3 * D_MLSTM, 3 * D_ATT + 4 * D_MLSTM]
    for l in range(DEPTH):
        mod = jax.nn.silu(c) @ w_ada[l] + b_ada[l]
        sh1, sc1, g1, sh2, sc2, g2 = jnp.split(mod, 6, axis=-1)

        h = modulate(x, norm1_g[l], sh1, sc1)
        proj = h @ w_in[l]
        q_a, k_a, v_a, qk_m, v_m, o_m, gates = jnp.split(proj, offs, axis=-1)

        qa = rope(split_heads(q_a, ATT_HEADS), positions)
        ka = rope(split_heads(k_a, ATT_HEADS), positions)
        attn = merge_heads(moba_attention(qa, ka, split_heads(v_a, ATT_HEADS)))
        attn = head_rms_norm(attn, attn_out_g[l], ATT_HEADS)

        qk_m = jax.nn.silu(causal_conv(qk_m, conv_w[l], conv_b[l]))
        q_m, k_m = jnp.split(qk_m, 2, axis=-1)
        gates = gates.astype(jnp.float32) + b_gate[l].astype(jnp.float32)
        i_pre = jnp.transpose(gates[..., :MLSTM_HEADS], (0, 2, 1))
        f_pre = jnp.transpose(gates[..., MLSTM_HEADS:], (0, 2, 1))
        hm = mlstm_chunkwise(split_heads(q_m, MLSTM_HEADS).astype(jnp.float32),
                             split_heads(k_m, MLSTM_HEADS).astype(jnp.float32) * MLSTM_HEAD_DIM ** -0.5,
                             split_heads(v_m, MLSTM_HEADS).astype(jnp.float32),
                             i_pre, f_pre)
        hm = head_rms_norm(merge_heads(hm), mlstm_out_g[l], MLSTM_HEADS)
        hm = (hm * jax.nn.sigmoid(o_m.astype(jnp.float32))).astype(x.dtype)

        y = jnp.concatenate([attn.astype(x.dtype), hm], axis=-1) @ w_out[l]
        x = x + g1[:, None, :] * y

        h2 = modulate(x, norm2_g[l], sh2, sc2)
        moe = hier_moe(h2.reshape(b * s, d), w_rg[l], b_rg[l], w_re[l], b_re[l],
                       w1[l], w3[l], w2[l]).reshape(b, s, d)
        x = x + g2[:, None, :] * moe
    return rms_norm(x, norm_f_g)
```

```python
import functools

import jax
import jax.numpy as jnp
from jax import lax
from jax.experimental import pallas as pl
from jax.experimental.pallas import tpu as pltpu

F32 = jnp.float32
BF16 = jnp.bfloat16

D_MODEL = 1024
D_ATT = 512
ATT_HEADS = 8
ATT_HEAD_DIM = 64
D_MLSTM = 512
MLSTM_HEADS = 4
MLSTM_HEAD_DIM = 128
MOBA_BLOCK = 256
MOBA_TOPK = 3
ROPE_THETA = 10000.0
MLSTM_CHUNK = 128
CONV_WIDTH = 4
N_GROUPS = 4
EXPERTS_PER_GROUP = 8
N_EXPERTS = N_GROUPS * EXPERTS_PER_GROUP
TOP_K_EXPERTS = 2
D_EXPERT = 512
MOE_BLOCK_ROWS = 256
NORM_EPS = 1e-6
NEG_INF = -1e30

LANES = 128
SUBLANES = 8
VMEM_LIMIT_BYTES = 56 * 1024 * 1024

TOKEN_TILE = 256
DISPATCH_TILE = 512
GATE_LANES = LANES
ROUTER_LANES = LANES
GROUP_LANE0 = N_EXPERTS


def _dot(a, b):
    return jnp.dot(a, b, preferred_element_type=F32)


def _dot_nt(a, b):
    return lax.dot_general(a, b, (((1,), (1,)), ((), ())), preferred_element_type=F32)


def _dot_tn(a, b):
    return lax.dot_general(a, b, (((0,), (0,)), ((), ())), preferred_element_type=F32)


def _split3(x):
    a = x.astype(BF16)
    r = x - a.astype(F32)
    b = r.astype(BF16)
    c = (r - b.astype(F32)).astype(BF16)
    return a, b, c


def _silu(x):
    return x * jax.nn.sigmoid(x)


def _log_sigmoid(x):
    return jnp.minimum(x, 0.0) - jnp.log1p(jnp.exp(-jnp.abs(x)))


def _params(semantics, vmem=VMEM_LIMIT_BYTES):
    return pltpu.CompilerParams(dimension_semantics=semantics, vmem_limit_bytes=vmem)


def _mod_kernel(c_ref, w_ref, b_ref, o_ref):
    sc = _silu(c_ref[...])
    o_ref[...] = jnp.dot(sc, w_ref[...], precision=lax.Precision.HIGHEST,
                         preferred_element_type=F32) + b_ref[...]


def _mod(c, w_ada, b_ada):
    b, d = c.shape
    n = w_ada.shape[1]
    tn = D_MODEL
    return pl.pallas_call(
        _mod_kernel,
        grid=(n // tn,),
        in_specs=[pl.BlockSpec((b, d), lambda i: (0, 0)),
                  pl.BlockSpec((d, tn), lambda i: (0, i)),
                  pl.BlockSpec((1, tn), lambda i: (0, i))],
        out_specs=pl.BlockSpec((b, tn), lambda i: (0, i)),
        out_shape=jax.ShapeDtypeStruct((b, n), F32),
        compiler_params=_params(("arbitrary",)),
        name="mod",
    )(c, w_ada, b_ada.reshape(1, n))


def _rms_modulate(x, g, shift, scale):
    y = x * lax.rsqrt(jnp.mean(x * x, axis=-1, keepdims=True) + NORM_EPS)
    return (y * g) * (1.0 + scale) + shift


def _rope(t, cos, sin, first_half):
    outs = []
    for c in range(t.shape[1] // LANES):
        tc = t[:, c * LANES:(c + 1) * LANES]
        rot = jnp.where(first_half, -pltpu.roll(tc, LANES - ATT_HEAD_DIM // 2, 1),
                        pltpu.roll(tc, ATT_HEAD_DIM // 2, 1))
        outs.append(tc * cos + rot * sin)
    return jnp.concatenate(outs, axis=1)


def _inproj_kernel(x_ref, pos_ref, mod_ref, g_ref, invf_ref, wq_ref, wk_ref, wvt_ref, wqk_ref,
                   wv_ref, wo_ref, wg_ref, q_ref, k_ref, vt_ref, qkm_ref, vm_ref, om_ref, gt_ref):
    d = x_ref.shape[2]
    x = x_ref[0]
    mod = mod_ref[0]
    h = _rms_modulate(x, g_ref[...], mod[:, 0:d], mod[:, d:2 * d])
    hb = h.astype(BF16)

    ang = pos_ref[0].astype(F32) * invf_ref[...]
    cos = jnp.cos(ang)
    sin = jnp.sin(ang)
    lane = lax.broadcasted_iota(jnp.int32, cos.shape, 1)
    first_half = (lane & (ATT_HEAD_DIM // 2)) == 0

    q = _rope(_dot(hb, wq_ref[...]), cos, sin, first_half)
    q_ref[0] = (q * (ATT_HEAD_DIM ** -0.5)).astype(BF16)
    k_ref[0] = _rope(_dot(hb, wk_ref[...]), cos, sin, first_half).astype(BF16)
    vt_ref[0, 0] = _dot_nt(wvt_ref[...], hb).astype(BF16)
    qkm_ref[0] = _dot(hb, wqk_ref[...])
    vm_ref[0] = _dot(hb, wv_ref[...])
    om_ref[0] = _dot(hb, wo_ref[...])
    gt_ref[0] = _dot(hb, wg_ref[...])


def _inproj(x, positions, mod3, norm_g, w_in):
    b, s, d = x.shape
    tm = MOBA_BLOCK
    nb = s // tm
    o = [0, D_ATT, 2 * D_ATT, 3 * D_ATT, 3 * D_ATT + 2 * D_MLSTM, 3 * D_ATT + 3 * D_MLSTM,
         3 * D_ATT + 4 * D_MLSTM, 3 * D_ATT + 4 * D_MLSTM + 2 * MLSTM_HEADS]
    wb = w_in.astype(BF16)
    wq, wk, wv_a, wqk, wv, wo, wg = (wb[:, o[i]:o[i + 1]] for i in range(7))
    wvt = wv_a.T
    wg = jnp.pad(wg, ((0, 0), (0, GATE_LANES - 2 * MLSTM_HEADS)))
    half = ATT_HEAD_DIM // 2
    inv_freq = ROPE_THETA ** (-jnp.arange(half, dtype=F32) / half)
    invf = jnp.tile(inv_freq, LANES // half).reshape(1, LANES)

    full = lambda a: pl.BlockSpec(a.shape, lambda bi, i: (0,) * a.ndim)
    tok = lambda w: pl.BlockSpec((1, tm, w), lambda bi, i: (bi, i, 0))
    out_shape = [jax.ShapeDtypeStruct((b, s, D_ATT), BF16),
                 jax.ShapeDtypeStruct((b, s, D_ATT), BF16),
                 jax.ShapeDtypeStruct((b, nb, D_ATT, tm), BF16),
                 jax.ShapeDtypeStruct((b, s, 2 * D_MLSTM), F32),
                 jax.ShapeDtypeStruct((b, s, D_MLSTM), F32),
                 jax.ShapeDtypeStruct((b, s, D_MLSTM), F32),
                 jax.ShapeDtypeStruct((b, s, GATE_LANES), F32)]
    return pl.pallas_call(
        _inproj_kernel,
        grid=(b, nb),
        in_specs=[tok(d), tok(1),
                  pl.BlockSpec((1, 1, mod3.shape[2]), lambda bi, i: (bi, 0, 0)),
                  full(norm_g), full(invf), full(wq), full(wk), full(wvt), full(wqk), full(wv),
                  full(wo), full(wg)],
        out_specs=[tok(D_ATT), tok(D_ATT),
                   pl.BlockSpec((1, 1, D_ATT, tm), lambda bi, i: (bi, i, 0, 0)),
                   tok(2 * D_MLSTM), tok(D_MLSTM), tok(D_MLSTM), tok(GATE_LANES)],
        out_shape=out_shape,
        compiler_params=_params(("arbitrary", "arbitrary")),
        name="inproj",
    )(x, positions.reshape(b, s, 1), mod3, norm_g, invf, wq, wk, wvt, wqk, wv, wo, wg)


def _moba_kernel(q_ref, k_ref, vt_ref, g_ref, o_ref, kmean_ref, bias_ref):
    blk = MOBA_BLOCK
    nb = k_ref.shape[1] // blk
    j = pl.program_id(2)
    heads = LANES // ATT_HEAD_DIM

    @pl.when(j == 0)
    def _():
        for n in range(nb):
            kb = k_ref[0, n * blk:(n + 1) * blk, :].astype(F32)
            kmean_ref[n:n + 1, :] = jnp.mean(kb, axis=0, keepdims=True)

    q = q_ref[0]
    lane = lax.broadcasted_iota(jnp.int32, q.shape, 1)
    blk_id = lax.broadcasted_iota(jnp.int32, (nb, blk), 0)
    km_hi = kmean_ref[...].astype(BF16)
    km_lo = (kmean_ref[...] - km_hi.astype(F32)).astype(BF16)
    past = blk_id < j

    qh = []
    for hh in range(heads):
        in_head = (lane >= hh * ATT_HEAD_DIM) & (lane < (hh + 1) * ATT_HEAD_DIM)
        qh.append(jnp.where(in_head, q, jnp.zeros_like(q)))
        gate = _dot_nt(km_hi, qh[hh]) + _dot_nt(km_lo, qh[hh])
        g = jnp.where(past, gate, NEG_INF)
        sel = jnp.zeros(g.shape, F32)
        for _ in range(min(MOBA_TOPK, nb)):
            top = jnp.max(g, axis=0, keepdims=True)
            idx = jnp.min(jnp.where(g == top, blk_id, nb), axis=0, keepdims=True)
            pick = blk_id == idx
            sel = jnp.where(pick, 1.0, sel)
            g = jnp.where(pick, -jnp.inf, g)
        bias_ref[hh] = jnp.where((sel > 0.0) & past, 0.0, NEG_INF)

    k_own = k_ref[0, pl.ds(pl.multiple_of(j * blk, blk), blk), :]
    v_own = vt_ref[0, j]
    kpos = lax.broadcasted_iota(jnp.int32, (blk, blk), 0)
    qpos = lax.broadcasted_iota(jnp.int32, (blk, blk), 1)
    state = []
    for hh in range(heads):
        st = jnp.where(kpos <= qpos, _dot_nt(k_own, qh[hh]), NEG_INF)
        m = jnp.max(st, axis=0, keepdims=True)
        p = jnp.exp(st - m)
        l = jnp.sum(p, axis=0, keepdims=True)
        acc = _dot(v_own, p.astype(BF16))
        state += [m, l, acc]

    def body(n, carry):
        kb = k_ref[0, pl.ds(pl.multiple_of(n * blk, blk), blk), :]
        vb = vt_ref[0, n]
        new = []
        for hh in range(heads):
            m, l, acc = carry[3 * hh:3 * hh + 3]
            st = _dot_nt(kb, qh[hh]) + bias_ref[hh, pl.ds(n, 1), :]
            m_new = jnp.maximum(m, jnp.max(st, axis=0, keepdims=True))
            alpha = jnp.exp(m - m_new)
            p = jnp.exp(st - m_new)
            l = alpha * l + jnp.sum(p, axis=0, keepdims=True)
            acc = alpha * acc + _dot(vb, p.astype(BF16))
            new += [m_new, l, acc]
        return tuple(new)

    state = lax.fori_loop(0, j, body, tuple(state))

    row = lax.broadcasted_iota(jnp.int32, (LANES, blk), 0)
    out_t = jnp.zeros((LANES, blk), F32)
    for hh in range(heads):
        in_head = (row >= hh * ATT_HEAD_DIM) & (row < (hh + 1) * ATT_HEAD_DIM)
        o_h = jnp.where(in_head, state[3 * hh + 2] / state[3 * hh + 1], 0.0)
        ms = jnp.sum(o_h * o_h, axis=0, keepdims=True) * (1.0 / ATT_HEAD_DIM)
        out_t = out_t + o_h * lax.rsqrt(ms + NORM_EPS)
    o_ref[0] = out_t.T * g_ref[...]


def _moba(q, k, vt, g_att):
    b, s, w = q.shape
    blk = MOBA_BLOCK
    nb = s // blk
    pairs = w // LANES
    return pl.pallas_call(
        _moba_kernel,
        grid=(b, pairs, nb),
        in_specs=[pl.BlockSpec((1, blk, LANES), lambda bi, p, j: (bi, j, p)),
                  pl.BlockSpec((1, s, LANES), lambda bi, p, j: (bi, 0, p)),
                  pl.BlockSpec((1, nb, LANES, blk), lambda bi, p, j: (bi, 0, p, 0)),
                  pl.BlockSpec((1, LANES), lambda bi, p, j: (0, p))],
        out_specs=pl.BlockSpec((1, blk, LANES), lambda bi, p, j: (bi, j, p)),
        out_shape=jax.ShapeDtypeStruct((b, s, w), F32),
        scratch_shapes=[pltpu.VMEM((nb, LANES), F32),
                        pltpu.VMEM((LANES // ATT_HEAD_DIM, nb, blk), F32)],
        compiler_params=_params(("arbitrary", "arbitrary", "arbitrary")),
        name="moba",
    )(q, k, vt, g_att)


def _mlstm_kernel(qkm_ref, vm_ref, om_ref, gt_ref, cw_ref, cb_ref, bg_ref, gm_ref, o_ref,
                  uext_ref, c_ref, n_ref, m_ref):
    L = MLSTM_CHUNK
    dh = MLSTM_HEAD_DIM
    nh = MLSTM_HEADS
    c = pl.program_id(1)

    @pl.when(c == 0)
    def _():
        uext_ref[0:SUBLANES, :] = jnp.zeros((SUBLANES, uext_ref.shape[1]), F32)
        c_ref[...] = jnp.zeros(c_ref.shape, F32)
        n_ref[...] = jnp.zeros(n_ref.shape, F32)
        m_ref[...] = jnp.zeros(m_ref.shape, F32)

    u = qkm_ref[0]
    uext_ref[SUBLANES:SUBLANES + L, :] = u
    cw = cw_ref[...]
    conv = cb_ref[...] + cw[CONV_WIDTH - 1:CONV_WIDTH, :] * u
    for dlt in range(1, CONV_WIDTH):
        conv = conv + cw[CONV_WIDTH - 1 - dlt:CONV_WIDTH - dlt, :] * uext_ref[SUBLANES - dlt:SUBLANES - dlt + L, :]
    uext_ref[0:SUBLANES, :] = u[L - SUBLANES:L, :]
    act = _silu(conv)

    gates = gt_ref[0] + bg_ref[...]
    gates_t = gates.T
    tpos = lax.broadcasted_iota(jnp.int32, (L, L), 0)
    spos = lax.broadcasted_iota(jnp.int32, (L, L), 1)
    causal = spos <= tpos
    tril = jnp.where(causal, 1.0, 0.0).astype(BF16)
    triu = jnp.where(tpos <= spos, 1.0, 0.0).astype(BF16)
    cum_c = sum(_dot(tril, part) for part in _split3(_log_sigmoid(gates)))
    cum_r = sum(_dot(part, triu) for part in _split3(_log_sigmoid(gates_t)))

    vm = vm_ref[0]
    om = om_ref[0]
    for h in range(nh):
        sl = slice(h * dh, (h + 1) * dh)
        q = act[:, sl]
        k = act[:, nh * dh + h * dh:nh * dh + (h + 1) * dh] * (dh ** -0.5)
        v = vm[:, sl]
        qb, kb, vb = q.astype(BF16), k.astype(BF16), v.astype(BF16)
        cum_col = cum_c[:, nh + h:nh + h + 1]
        ig_col = gates[:, h:h + 1]
        cum_row = cum_r[nh + h:nh + h + 1, :]
        ig_row = gates_t[h:h + 1, :]
        a_tot = cum_row[:, L - 1:L]
        m_prev = m_ref[h:h + 1, 0:1]
        c_prev = c_ref[h]
        n_prev = n_ref[h:h + 1, :]

        g_row = a_tot - cum_row + ig_row
        g_col = a_tot - cum_col + ig_col
        m_loc = jnp.max(g_row, axis=1, keepdims=True)
        w_row = jnp.exp(g_row - m_loc)
        w_col = jnp.exp(g_col - m_loc)
        c_loc = _dot_tn((v * w_col).astype(BF16), kb)
        n_loc = _dot(jnp.broadcast_to(w_row, (SUBLANES, L)).astype(BF16), kb)[0:1, :]

        dmat = jnp.where(causal, cum_col - cum_row + ig_row, NEG_INF)
        inter = cum_col + m_prev
        m_t = jnp.maximum(inter, jnp.max(dmat, axis=1, keepdims=True))
        s_qk = _dot_nt(qb, kb) * jnp.exp(dmat - m_t)
        w_inter = jnp.exp(inter - m_t)
        num = _dot(s_qk.astype(BF16), vb) + w_inter * _dot_nt(qb, c_prev.astype(BF16))
        den = jnp.sum(s_qk, axis=1, keepdims=True) + w_inter * jnp.sum(q * n_prev, axis=1, keepdims=True)
        hout = num / jnp.maximum(jnp.abs(den), jnp.exp(-m_t))

        hn = hout * lax.rsqrt(jnp.mean(hout * hout, axis=1, keepdims=True) + NORM_EPS) * gm_ref[:, sl]
        o_ref[0, :, sl] = hn * jax.nn.sigmoid(om[:, sl])

        m_new = jnp.maximum(a_tot + m_prev, m_loc)
        s_prev = jnp.exp(a_tot + m_prev - m_new)
        s_loc = jnp.exp(m_loc - m_new)
        c_ref[h] = s_prev * c_prev + s_loc * c_loc
        n_ref[h:h + 1, :] = s_prev * n_prev + s_loc * n_loc
        m_ref[h:h + 1, :] = jnp.broadcast_to(m_new, (1, m_ref.shape[1]))


def _mlstm(qkm, vm, om, gates, conv_w, conv_b, b_gate, g_m):
    b, s, _ = qkm.shape
    L = MLSTM_CHUNK
    nc = s // L
    bg = jnp.pad(b_gate.reshape(1, -1), ((0, 0), (0, GATE_LANES - b_gate.shape[-1])))
    tok = lambda w: pl.BlockSpec((1, L, w), lambda bi, c: (bi, c, 0))
    full = lambda a: pl.BlockSpec(a.shape, lambda bi, c: (0,) * a.ndim)
    cb = conv_b.reshape(1, -1)
    gm = g_m.reshape(1, -1)
    return pl.pallas_call(
        _mlstm_kernel,
        grid=(b, nc),
        in_specs=[tok(2 * D_MLSTM), tok(D_MLSTM), tok(D_MLSTM), tok(GATE_LANES),
                  full(conv_w), full(cb), full(bg), full(gm)],
        out_specs=tok(D_MLSTM),
        out_shape=jax.ShapeDtypeStruct((b, s, D_MLSTM), F32),
        scratch_shapes=[pltpu.VMEM((SUBLANES + L, 2 * D_MLSTM), F32),
                        pltpu.VMEM((MLSTM_HEADS, MLSTM_HEAD_DIM, MLSTM_HEAD_DIM), F32),
                        pltpu.VMEM((SUBLANES, MLSTM_HEAD_DIM), F32),
                        pltpu.VMEM((SUBLANES, LANES), F32)],
        compiler_params=_params(("arbitrary", "arbitrary")),
        name="mlstm",
    )(qkm, vm, om, gates, conv_w, cb, bg, gm)


META_E0, META_E1, META_W0, META_W1, META_R0, META_R1 = range(6)


def _outproj_kernel(attn_ref, hm_ref, x_ref, mod_ref, g_ref, wout_ref, wr_hi_ref, wr_lo_ref, br_ref,
                    x1_ref, h2_ref, meta_ref, cnt_ref):
    d = x_ref.shape[1]
    tm = x_ref.shape[0]
    i = pl.program_id(0)

    @pl.when(i == 0)
    def _():
        cnt_ref[...] = jnp.zeros(cnt_ref.shape, F32)

    mod = mod_ref[0]
    y = (_dot(attn_ref[...].astype(BF16), wout_ref[0:D_ATT, :])
         + _dot(hm_ref[...].astype(BF16), wout_ref[D_ATT:, :]))
    x1 = x_ref[...] + mod[:, 2 * d:3 * d] * y
    x1_ref[...] = x1
    h2 = _rms_modulate(x1, g_ref[...], mod[:, 3 * d:4 * d], mod[:, 4 * d:5 * d])
    h2_ref[...] = h2

    h_hi = h2.astype(BF16)
    h_lo = (h2 - h_hi.astype(F32)).astype(BF16)
    logit = (_dot(h_hi, wr_hi_ref[...]) + _dot(h_lo, wr_hi_ref[...]) + _dot(h_hi, wr_lo_ref[...])
             + br_ref[...])
    lane = lax.broadcasted_iota(jnp.int32, logit.shape, 1)
    big = jnp.int32(ROUTER_LANES)

    is_g = (lane >= GROUP_LANE0) & (lane < GROUP_LANE0 + N_GROUPS)
    gmax = jnp.max(jnp.where(is_g, logit, -jnp.inf), axis=1, keepdims=True)
    gsum = jnp.sum(jnp.where(is_g, jnp.exp(logit - gmax), 0.0), axis=1, keepdims=True)
    g_w = 1.0 / gsum
    g_idx = jnp.min(jnp.where(is_g & (logit == gmax), lane, big), axis=1, keepdims=True) - GROUP_LANE0

    in_grp = (lane < N_EXPERTS) & ((lane // EXPERTS_PER_GROUP) == g_idx)
    emax = jnp.max(jnp.where(in_grp, logit, -jnp.inf), axis=1, keepdims=True)
    esum = jnp.sum(jnp.where(in_grp, jnp.exp(logit - emax), 0.0), axis=1, keepdims=True)
    e0 = jnp.min(jnp.where(in_grp & (logit == emax), lane, big), axis=1, keepdims=True)
    rest = in_grp & (lane != e0)
    e2max = jnp.max(jnp.where(rest, logit, -jnp.inf), axis=1, keepdims=True)
    e1 = jnp.min(jnp.where(rest & (logit == e2max), lane, big), axis=1, keepdims=True)
    p0 = 1.0 / esum
    p1 = jnp.exp(e2max - emax) / esum
    w0 = g_w * p0 / (p0 + p1)
    w1 = g_w * p1 / (p0 + p1)

    memb = jnp.where((lane == e0) | (lane == e1), 1.0, 0.0)
    tpos = lax.broadcasted_iota(jnp.int32, (tm, tm), 0)
    spos = lax.broadcasted_iota(jnp.int32, (tm, tm), 1)
    before = jnp.where(spos < tpos, 1.0, 0.0).astype(BF16)
    cex = _dot(before, memb.astype(BF16)) + cnt_ref[0:1, :]
    r0 = jnp.sum(jnp.where(lane == e0, cex, 0.0), axis=1, keepdims=True)
    r1 = jnp.sum(jnp.where(lane == e1, cex, 0.0), axis=1, keepdims=True)
    cnt_ref[...] = cnt_ref[...] + jnp.sum(memb, axis=0, keepdims=True)

    meta = jnp.zeros(logit.shape, F32)
    for slot, val in ((META_E0, e0.astype(F32)), (META_E1, e1.astype(F32)), (META_W0, w0), (META_W1, w1),
                      (META_R0, r0), (META_R1, r1)):
        meta = jnp.where(lane == slot, val, meta)
    meta_ref[...] = meta


def _outproj(attn, hm, x2d, mod3, norm_g, w_out, w_rg, b_rg, w_re, b_re, seq):
    t, d = x2d.shape
    tm = TOKEN_TILE
    per_batch = seq // tm
    wr = jnp.pad(jnp.concatenate([w_re, w_rg], axis=1), ((0, 0), (0, ROUTER_LANES - N_EXPERTS - N_GROUPS)))
    br = jnp.pad(jnp.concatenate([b_re, b_rg]).reshape(1, -1), ((0, 0), (0, ROUTER_LANES - N_EXPERTS - N_GROUPS)))
    wr_hi = wr.astype(BF16)
    wr_lo = (wr - wr_hi.astype(F32)).astype(BF16)
    wout = w_out.astype(BF16)
    tok = lambda w: pl.BlockSpec((tm, w), lambda i: (i, 0))
    full = lambda a: pl.BlockSpec(a.shape, lambda i: (0,) * a.ndim)
    return pl.pallas_call(
        _outproj_kernel,
        grid=(t // tm,),
        in_specs=[tok(D_ATT), tok(D_MLSTM), tok(d),
                  pl.BlockSpec((1, 1, mod3.shape[2]), lambda i: (i // per_batch, 0, 0)),
                  full(norm_g), full(wout), full(wr_hi), full(wr_lo), full(br)],
        out_specs=[tok(d), tok(d), tok(ROUTER_LANES),
                   pl.BlockSpec((SUBLANES, ROUTER_LANES), lambda i: (0, 0))],
        out_shape=[jax.ShapeDtypeStruct((t, d), F32), jax.ShapeDtypeStruct((t, d), F32),
                   jax.ShapeDtypeStruct((t, ROUTER_LANES), F32),
                   jax.ShapeDtypeStruct((SUBLANES, ROUTER_LANES), F32)],
        compiler_params=_params(("arbitrary",)),
        name="outproj",
    )(attn, hm, x2d, mod3, norm_g, wout, wr_hi, wr_lo, br)


def _row_copy(src_ref, src_row, dst_ref, dst_row, sem):
    return pltpu.make_async_copy(src_ref.at[pl.ds(src_row, 1)], dst_ref.at[pl.ds(dst_row, 1)], sem)


def _dispatch_kernel(dest_ref, h2_ref, zeros_ref, xs_ref, sem):
    del zeros_ref
    td = dest_ref.shape[0] // TOP_K_EXPERTS
    base = pl.program_id(0) * td

    def issue(r, carry):
        for kk in range(TOP_K_EXPERTS):
            _row_copy(h2_ref, base + r, xs_ref, dest_ref[TOP_K_EXPERTS * r + kk], sem).start()
        return carry

    lax.fori_loop(0, td, issue, 0)

    def drain(r, carry):
        _row_copy(h2_ref, 0, xs_ref, 0, sem).wait()
        return carry

    lax.fori_loop(0, TOP_K_EXPERTS * td, drain, 0)


def _dispatch(dest, h2, n_rows):
    t, d = h2.shape
    td = DISPATCH_TILE
    zeros = jnp.zeros((n_rows, d), h2.dtype)
    return pl.pallas_call(
        _dispatch_kernel,
        grid=(t // td,),
        in_specs=[pl.BlockSpec((TOP_K_EXPERTS * td,), lambda i: (i,), memory_space=pltpu.SMEM),
                  pl.BlockSpec(memory_space=pl.ANY),
                  pl.BlockSpec(memory_space=pl.ANY)],
        out_specs=pl.BlockSpec(memory_space=pl.ANY),
        out_shape=jax.ShapeDtypeStruct((n_rows, d), h2.dtype),
        scratch_shapes=[pltpu.SemaphoreType.DMA(())],
        input_output_aliases={2: 0},
        compiler_params=_params(("arbitrary",)),
        name="dispatch",
    )(dest, h2, zeros)


def _expert_kernel(be_ref, nu_ref, xs_ref, w1_ref, w3_ref, w2_ref, y_ref):
    i = pl.program_id(0)

    @pl.when(i < nu_ref[0])
    def _():
        x = xs_ref[...].astype(BF16)
        a = _dot(x, w1_ref[0])
        b = _dot(x, w3_ref[0])
        y_ref[...] = _dot((_silu(a) * b).astype(BF16), w2_ref[0])

    @pl.when(i >= nu_ref[0])
    def _():
        y_ref[...] = jnp.zeros(y_ref.shape, F32)


def _experts(block_e, n_used, xs, w1, w3, w2):
    n_rows, d = xs.shape
    rb = MOE_BLOCK_ROWS
    de = w1.shape[2]
    used = lambda i, be, nu: jnp.minimum(i, nu[0] - 1)
    grid_spec = pltpu.PrefetchScalarGridSpec(
        num_scalar_prefetch=2,
        grid=(n_rows // rb,),
        in_specs=[pl.BlockSpec((rb, d), lambda i, be, nu: (used(i, be, nu), 0)),
                  pl.BlockSpec((1, d, de), lambda i, be, nu: (be[used(i, be, nu)], 0, 0)),
                  pl.BlockSpec((1, d, de), lambda i, be, nu: (be[used(i, be, nu)], 0, 0)),
                  pl.BlockSpec((1, de, d), lambda i, be, nu: (be[used(i, be, nu)], 0, 0))],
        out_specs=pl.BlockSpec((rb, d), lambda i, be, nu: (i, 0)),
    )
    return pl.pallas_call(
        _expert_kernel,
        grid_spec=grid_spec,
        out_shape=jax.ShapeDtypeStruct((n_rows, d), F32),
        compiler_params=_params(("arbitrary",)),
        name="experts",
    )(block_e, n_used, xs, w1, w3, w2)


def _combine_kernel(final_norm, dest_ref, x1_ref, meta_ref, mod_ref, g_ref, y_ref, o_ref, ybuf_ref, sem):
    tc, d = x1_ref.shape

    def issue(r, carry):
        for kk in range(TOP_K_EXPERTS):
            _row_copy(y_ref, dest_ref[TOP_K_EXPERTS * r + kk], ybuf_ref.at[kk], r, sem).start()
        return carry

    lax.fori_loop(0, tc, issue, 0)

    def drain(r, carry):
        _row_copy(y_ref, 0, ybuf_ref.at[0], 0, sem).wait()
        return carry

    lax.fori_loop(0, TOP_K_EXPERTS * tc, drain, 0)

    meta = meta_ref[...]
    moe = (meta[:, META_W0:META_W0 + 1] * ybuf_ref[0] + meta[:, META_W1:META_W1 + 1] * ybuf_ref[1])
    x2 = x1_ref[...] + mod_ref[0][:, 5 * d:6 * d] * moe
    if final_norm:
        x2 = x2 * lax.rsqrt(jnp.mean(x2 * x2, axis=-1, keepdims=True) + NORM_EPS) * g_ref[...]
    o_ref[...] = x2


def _combine(dest, x1, meta, mod3, norm_f_g, y, seq, final_norm):
    t, d = x1.shape
    tc = TOKEN_TILE
    per_batch = seq // tc
    gf = norm_f_g.reshape(1, d)
    return pl.pallas_call(
        functools.partial(_combine_kernel, final_norm),
        grid=(t // tc,),
        in_specs=[pl.BlockSpec((TOP_K_EXPERTS * tc,), lambda i: (i,), memory_space=pltpu.SMEM),
                  pl.BlockSpec((tc, d), lambda i: (i, 0)),
                  pl.BlockSpec((tc, ROUTER_LANES), lambda i: (i, 0)),
                  pl.BlockSpec((1, 1, mod3.shape[2]), lambda i: (i // per_batch, 0, 0)),
                  pl.BlockSpec(gf.shape, lambda i: (0, 0)),
                  pl.BlockSpec(memory_space=pl.ANY)],
        out_specs=pl.BlockSpec((tc, d), lambda i: (i, 0)),
        out_shape=jax.ShapeDtypeStruct((t, d), F32),
        scratch_shapes=[pltpu.VMEM((TOP_K_EXPERTS, tc, d), F32), pltpu.SemaphoreType.DMA(())],
        compiler_params=_params(("arbitrary",)),
        name="combine",
    )(dest, x1, meta, mod3, gf, y)


def _routing_plan(meta, counts, n_tokens):
    rb = MOE_BLOCK_ROWS
    cnt = counts[0, :N_EXPERTS].astype(jnp.int32)
    padded = ((cnt + rb - 1) // rb) * rb
    pad_end = jnp.cumsum(padded)
    pad_start = pad_end - padded
    n_rows = n_tokens * TOP_K_EXPERTS + N_EXPERTS * rb
    n_blocks = n_rows // rb
    block_e = jnp.minimum(jnp.searchsorted(pad_end, jnp.arange(n_blocks, dtype=jnp.int32) * rb, side='right'),
                          N_EXPERTS - 1).astype(jnp.int32)
    n_used = (pad_end[-1:] // rb).astype(jnp.int32)
    e = meta[:, META_E0:META_E1 + 1].astype(jnp.int32)
    rank = meta[:, META_R0:META_R1 + 1].astype(jnp.int32)
    start = jnp.sum(jnp.where(e[..., None] == jnp.arange(N_EXPERTS, dtype=jnp.int32), pad_start, 0), axis=-1)
    dest = (start + rank).reshape(-1)
    return dest, block_e, n_used, n_rows


def kernel(x, c, positions, w_ada, b_ada, norm1_g, w_in, b_gate, conv_w, conv_b, attn_out_g, mlstm_out_g,
           w_out, norm2_g, w_rg, b_rg, w_re, b_re, w1, w3, w2, norm_f_g):
    b, s, d = x.shape
    depth = w_ada.shape[0]
    assert d == D_MODEL and s % MOBA_BLOCK == 0 and (b * s) % DISPATCH_TILE == 0
    x2d = x.reshape(b * s, d)
    for l in range(depth):
        mod3 = _mod(c, w_ada[l], b_ada[l]).reshape(b, 1, 6 * d)
        q, k, vt, qkm, vm, om, gates = _inproj(x2d.reshape(b, s, d), positions, mod3,
                                              norm1_g[l].reshape(1, d), w_in[l])
        attn = _moba(q, k, vt, attn_out_g[l].reshape(1, D_ATT))
        hm = _mlstm(qkm, vm, om, gates, conv_w[l], conv_b[l], b_gate[l], mlstm_out_g[l])
        x1, h2, meta, counts = _outproj(attn.reshape(b * s, D_ATT), hm.reshape(b * s, D_MLSTM), x2d, mod3,
                                        norm2_g[l].reshape(1, d), w_out[l], w_rg[l], b_rg[l], w_re[l],
                                        b_re[l], s)
        dest, block_e, n_used, n_rows = _routing_plan(meta, counts, b * s)
        xs = _dispatch(dest, h2, n_rows)
        y = _experts(block_e, n_used, xs, w1[l].astype(BF16), w3[l].astype(BF16), w2[l].astype(BF16))
        x2d = _combine(dest, x1, meta, mod3, norm_f_g, y, s, final_norm=(l == depth - 1))
    return x2d.reshape(b, s, d)
```

```python
import functools

import jax
import jax.numpy as jnp
from jax import lax
from jax.experimental import pallas as pl
from jax.experimental.pallas import tpu as pltpu

F32 = jnp.float32
BF16 = jnp.bfloat16

D_MODEL = 1024
D_ATT = 512
ATT_HEADS = 8
ATT_HEAD_DIM = 64
D_MLSTM = 512
MLSTM_HEADS = 4
MLSTM_HEAD_DIM = 128
MOBA_BLOCK = 256
MOBA_TOPK = 3
ROPE_THETA = 10000.0
MLSTM_CHUNK = 128
CONV_WIDTH = 4
N_GROUPS = 4
EXPERTS_PER_GROUP = 8
N_EXPERTS = N_GROUPS * EXPERTS_PER_GROUP
TOP_K_EXPERTS = 2
D_EXPERT = 512
MOE_BLOCK_ROWS = 256
NORM_EPS = 1e-6
NEG_INF = -1e30

LANES = 128
SUBLANES = 8
VMEM_LIMIT_BYTES = 56 * 1024 * 1024

MOE_TILE = 512
RUN_ALIGN = SUBLANES
PERM_ROWS = TOP_K_EXPERTS * MOE_TILE + N_EXPERTS * RUN_ALIGN
RUN_PIECES = tuple(RUN_ALIGN << p for p in reversed(range(7)))
TAIL_PIECES = tuple(RUN_ALIGN << p for p in reversed(range(5)))
ZERO_ROWS = TAIL_PIECES[0]
GATE_LANES = LANES
ROUTER_LANES = LANES
GROUP_LANE0 = N_EXPERTS


def _dot(a, b):
    return jnp.dot(a, b, preferred_element_type=F32)


def _dot_nt(a, b):
    return lax.dot_general(a, b, (((1,), (1,)), ((), ())), preferred_element_type=F32)


def _dot_tn(a, b):
    return lax.dot_general(a, b, (((0,), (0,)), ((), ())), preferred_element_type=F32)


def _split3(x):
    a = x.astype(BF16)
    r = x - a.astype(F32)
    b = r.astype(BF16)
    c = (r - b.astype(F32)).astype(BF16)
    return a, b, c


def _silu(x):
    return x * jax.nn.sigmoid(x)


def _log_sigmoid(x):
    return jnp.minimum(x, 0.0) - jnp.log1p(jnp.exp(-jnp.abs(x)))


def _params(semantics, vmem=VMEM_LIMIT_BYTES):
    return pltpu.CompilerParams(dimension_semantics=semantics, vmem_limit_bytes=vmem)


def _mod_kernel(c_ref, w_ref, b_ref, o_ref):
    sc = _silu(c_ref[...])
    o_ref[...] = jnp.dot(sc, w_ref[...], precision=lax.Precision.HIGHEST,
                         preferred_element_type=F32) + b_ref[...]


def _mod(c, w_ada, b_ada):
    b, d = c.shape
    n = w_ada.shape[1]
    tn = D_MODEL
    return pl.pallas_call(
        _mod_kernel,
        grid=(n // tn,),
        in_specs=[pl.BlockSpec((b, d), lambda i: (0, 0)),
                  pl.BlockSpec((d, tn), lambda i: (0, i)),
                  pl.BlockSpec((1, tn), lambda i: (0, i))],
        out_specs=pl.BlockSpec((b, tn), lambda i: (0, i)),
        out_shape=jax.ShapeDtypeStruct((b, n), F32),
        compiler_params=_params(("arbitrary",)),
        name="mod",
    )(c, w_ada, b_ada.reshape(1, n))


def _rms_modulate(x, g, shift, scale):
    y = x * lax.rsqrt(jnp.mean(x * x, axis=-1, keepdims=True) + NORM_EPS)
    return (y * g) * (1.0 + scale) + shift


def _rope(t, cos, sin, first_half):
    outs = []
    for c in range(t.shape[1] // LANES):
        tc = t[:, c * LANES:(c + 1) * LANES]
        rot = jnp.where(first_half, -pltpu.roll(tc, LANES - ATT_HEAD_DIM // 2, 1),
                        pltpu.roll(tc, ATT_HEAD_DIM // 2, 1))
        outs.append(tc * cos + rot * sin)
    return jnp.concatenate(outs, axis=1)


def _inproj_kernel(x_ref, pos_ref, mod_ref, g_ref, invf_ref, wq_ref, wk_ref, wvt_ref, wqk_ref,
                   wv_ref, wo_ref, wg_ref, q_ref, k_ref, vt_ref, qkm_ref, vm_ref, om_ref, gt_ref):
    d = x_ref.shape[2]
    x = x_ref[0]
    mod = mod_ref[0]
    h = _rms_modulate(x, g_ref[...], mod[:, 0:d], mod[:, d:2 * d])
    hb = h.astype(BF16)

    ang = pos_ref[0].astype(F32) * invf_ref[...]
    cos = jnp.cos(ang)
    sin = jnp.sin(ang)
    lane = lax.broadcasted_iota(jnp.int32, cos.shape, 1)
    first_half = (lane & (ATT_HEAD_DIM // 2)) == 0

    q = _rope(_dot(hb, wq_ref[...]), cos, sin, first_half)
    q_ref[0] = (q * (ATT_HEAD_DIM ** -0.5)).astype(BF16)
    k_ref[0] = _rope(_dot(hb, wk_ref[...]), cos, sin, first_half).astype(BF16)
    vt_ref[0, 0] = _dot_nt(wvt_ref[...], hb).astype(BF16)
    qkm_ref[0] = _dot(hb, wqk_ref[...])
    vm_ref[0] = _dot(hb, wv_ref[...])
    om_ref[0] = _dot(hb, wo_ref[...])
    gt_ref[0] = _dot(hb, wg_ref[...])


def _inproj(x, positions, mod3, norm_g, w_in):
    b, s, d = x.shape
    tm = MOBA_BLOCK
    nb = s // tm
    o = [0, D_ATT, 2 * D_ATT, 3 * D_ATT, 3 * D_ATT + 2 * D_MLSTM, 3 * D_ATT + 3 * D_MLSTM,
         3 * D_ATT + 4 * D_MLSTM, 3 * D_ATT + 4 * D_MLSTM + 2 * MLSTM_HEADS]
    wb = w_in.astype(BF16)
    wq, wk, wv_a, wqk, wv, wo, wg = (wb[:, o[i]:o[i + 1]] for i in range(7))
    wvt = wv_a.T
    wg = jnp.pad(wg, ((0, 0), (0, GATE_LANES - 2 * MLSTM_HEADS)))
    half = ATT_HEAD_DIM // 2
    inv_freq = ROPE_THETA ** (-jnp.arange(half, dtype=F32) / half)
    invf = jnp.tile(inv_freq, LANES // half).reshape(1, LANES)

    full = lambda a: pl.BlockSpec(a.shape, lambda bi, i: (0,) * a.ndim)
    tok = lambda w: pl.BlockSpec((1, tm, w), lambda bi, i: (bi, i, 0))
    out_shape = [jax.ShapeDtypeStruct((b, s, D_ATT), BF16),
                 jax.ShapeDtypeStruct((b, s, D_ATT), BF16),
                 jax.ShapeDtypeStruct((b, nb, D_ATT, tm), BF16),
                 jax.ShapeDtypeStruct((b, s, 2 * D_MLSTM), F32),
                 jax.ShapeDtypeStruct((b, s, D_MLSTM), F32),
                 jax.ShapeDtypeStruct((b, s, D_MLSTM), F32),
                 jax.ShapeDtypeStruct((b, s, GATE_LANES), F32)]
    return pl.pallas_call(
        _inproj_kernel,
        grid=(b, nb),
        in_specs=[tok(d), tok(1),
                  pl.BlockSpec((1, 1, mod3.shape[2]), lambda bi, i: (bi, 0, 0)),
                  full(norm_g), full(invf), full(wq), full(wk), full(wvt), full(wqk), full(wv),
                  full(wo), full(wg)],
        out_specs=[tok(D_ATT), tok(D_ATT),
                   pl.BlockSpec((1, 1, D_ATT, tm), lambda bi, i: (bi, i, 0, 0)),
                   tok(2 * D_MLSTM), tok(D_MLSTM), tok(D_MLSTM), tok(GATE_LANES)],
        out_shape=out_shape,
        compiler_params=_params(("arbitrary", "arbitrary")),
        name="inproj",
    )(x, positions.reshape(b, s, 1), mod3, norm_g, invf, wq, wk, wvt, wqk, wv, wo, wg)


def _moba_kernel(q_ref, k_ref, vt_ref, g_ref, o_ref, kmean_ref, bias_ref):
    blk = MOBA_BLOCK
    nb = k_ref.shape[1] // blk
    j = pl.program_id(2)
    heads = LANES // ATT_HEAD_DIM

    @pl.when(j == 0)
    def _():
        for n in range(nb):
            kb = k_ref[0, n * blk:(n + 1) * blk, :].astype(F32)
            kmean_ref[n:n + 1, :] = jnp.mean(kb, axis=0, keepdims=True)

    q = q_ref[0]
    lane = lax.broadcasted_iota(jnp.int32, q.shape, 1)
    blk_id = lax.broadcasted_iota(jnp.int32, (nb, blk), 0)
    km_hi = kmean_ref[...].astype(BF16)
    km_lo = (kmean_ref[...] - km_hi.astype(F32)).astype(BF16)
    past = blk_id < j

    qh = []
    for hh in range(heads):
        in_head = (lane >= hh * ATT_HEAD_DIM) & (lane < (hh + 1) * ATT_HEAD_DIM)
        qh.append(jnp.where(in_head, q, jnp.zeros_like(q)))
        gate = _dot_nt(km_hi, qh[hh]) + _dot_nt(km_lo, qh[hh])
        g = jnp.where(past, gate, NEG_INF)
        sel = jnp.zeros(g.shape, F32)
        for _ in range(min(MOBA_TOPK, nb)):
            top = jnp.max(g, axis=0, keepdims=True)
            idx = jnp.min(jnp.where(g == top, blk_id, nb), axis=0, keepdims=True)
            pick = blk_id == idx
            sel = jnp.where(pick, 1.0, sel)
            g = jnp.where(pick, -jnp.inf, g)
        bias_ref[hh] = jnp.where((sel > 0.0) & past, 0.0, NEG_INF)

    k_own = k_ref[0, pl.ds(pl.multiple_of(j * blk, blk), blk), :]
    v_own = vt_ref[0, j]
    kpos = lax.broadcasted_iota(jnp.int32, (blk, blk), 0)
    qpos = lax.broadcasted_iota(jnp.int32, (blk, blk), 1)
    state = []
    for hh in range(heads):
        st = jnp.where(kpos <= qpos, _dot_nt(k_own, qh[hh]), NEG_INF)
        m = jnp.max(st, axis=0, keepdims=True)
        p = jnp.exp(st - m)
        l = jnp.sum(p, axis=0, keepdims=True)
        acc = _dot(v_own, p.astype(BF16))
        state += [m, l, acc]

    def body(n, carry):
        kb = k_ref[0, pl.ds(pl.multiple_of(n * blk, blk), blk), :]
        vb = vt_ref[0, n]
        new = []
        for hh in range(heads):
            m, l, acc = carry[3 * hh:3 * hh + 3]
            st = _dot_nt(kb, qh[hh]) + bias_ref[hh, pl.ds(n, 1), :]
            m_new = jnp.maximum(m, jnp.max(st, axis=0, keepdims=True))
            alpha = jnp.exp(m - m_new)
            p = jnp.exp(st - m_new)
            l = alpha * l + jnp.sum(p, axis=0, keepdims=True)
            acc = alpha * acc + _dot(vb, p.astype(BF16))
            new += [m_new, l, acc]
        return tuple(new)

    state = lax.fori_loop(0, j, body, tuple(state))

    row = lax.broadcasted_iota(jnp.int32, (LANES, blk), 0)
    out_t = jnp.zeros((LANES, blk), F32)
    for hh in range(heads):
        in_head = (row >= hh * ATT_HEAD_DIM) & (row < (hh + 1) * ATT_HEAD_DIM)
        o_h = jnp.where(in_head, state[3 * hh + 2] / state[3 * hh + 1], 0.0)
        ms = jnp.sum(o_h * o_h, axis=0, keepdims=True) * (1.0 / ATT_HEAD_DIM)
        out_t = out_t + o_h * lax.rsqrt(ms + NORM_EPS)
    o_ref[0] = out_t.T * g_ref[...]


def _moba(q, k, vt, g_att):
    b, s, w = q.shape
    blk = MOBA_BLOCK
    nb = s // blk
    pairs = w // LANES
    return pl.pallas_call(
        _moba_kernel,
        grid=(b, pairs, nb),
        in_specs=[pl.BlockSpec((1, blk, LANES), lambda bi, p, j: (bi, j, p)),
                  pl.BlockSpec((1, s, LANES), lambda bi, p, j: (bi, 0, p)),
                  pl.BlockSpec((1, nb, LANES, blk), lambda bi, p, j: (bi, 0, p, 0)),
                  pl.BlockSpec((1, LANES), lambda bi, p, j: (0, p))],
        out_specs=pl.BlockSpec((1, blk, LANES), lambda bi, p, j: (bi, j, p)),
        out_shape=jax.ShapeDtypeStruct((b, s, w), F32),
        scratch_shapes=[pltpu.VMEM((nb, LANES), F32),
                        pltpu.VMEM((LANES // ATT_HEAD_DIM, nb, blk), F32)],
        compiler_params=_params(("arbitrary", "arbitrary", "arbitrary")),
        name="moba",
    )(q, k, vt, g_att)


def _mlstm_kernel(qkm_ref, vm_ref, om_ref, gt_ref, cw_ref, cb_ref, bg_ref, gm_ref, o_ref,
                  uext_ref, c_ref, n_ref, m_ref):
    L = MLSTM_CHUNK
    dh = MLSTM_HEAD_DIM
    nh = MLSTM_HEADS
    c = pl.program_id(1)

    @pl.when(c == 0)
    def _():
        uext_ref[0:SUBLANES, :] = jnp.zeros((SUBLANES, uext_ref.shape[1]), F32)
        c_ref[...] = jnp.zeros(c_ref.shape, F32)
        n_ref[...] = jnp.zeros(n_ref.shape, F32)
        m_ref[...] = jnp.zeros(m_ref.shape, F32)

    u = qkm_ref[0]
    uext_ref[SUBLANES:SUBLANES + L, :] = u
    cw = cw_ref[...]
    conv = cb_ref[...] + cw[CONV_WIDTH - 1:CONV_WIDTH, :] * u
    for dlt in range(1, CONV_WIDTH):
        conv = conv + cw[CONV_WIDTH - 1 - dlt:CONV_WIDTH - dlt, :] * uext_ref[SUBLANES - dlt:SUBLANES - dlt + L, :]
    uext_ref[0:SUBLANES, :] = u[L - SUBLANES:L, :]
    act = _silu(conv)

    gates = gt_ref[0] + bg_ref[...]
    gates_t = gates.T
    tpos = lax.broadcasted_iota(jnp.int32, (L, L), 0)
    spos = lax.broadcasted_iota(jnp.int32, (L, L), 1)
    causal = spos <= tpos
    tril = jnp.where(causal, 1.0, 0.0).astype(BF16)
    triu = jnp.where(tpos <= spos, 1.0, 0.0).astype(BF16)
    cum_c = sum(_dot(tril, part) for part in _split3(_log_sigmoid(gates)))
    cum_r = sum(_dot(part, triu) for part in _split3(_log_sigmoid(gates_t)))

    vm = vm_ref[0]
    om = om_ref[0]
    for h in range(nh):
        sl = slice(h * dh, (h + 1) * dh)
        q = act[:, sl]
        k = act[:, nh * dh + h * dh:nh * dh + (h + 1) * dh] * (dh ** -0.5)
        v = vm[:, sl]
        qb, kb, vb = q.astype(BF16), k.astype(BF16), v.astype(BF16)
        cum_col = cum_c[:, nh + h:nh + h + 1]
        ig_col = gates[:, h:h + 1]
        cum_row = cum_r[nh + h:nh + h + 1, :]
        ig_row = gates_t[h:h + 1, :]
        a_tot = cum_row[:, L - 1:L]
        m_prev = m_ref[h:h + 1, 0:1]
        c_prev = c_ref[h]
        n_prev = n_ref[h:h + 1, :]

        g_row = a_tot - cum_row + ig_row
        g_col = a_tot - cum_col + ig_col
        m_loc = jnp.max(g_row, axis=1, keepdims=True)
        w_row = jnp.exp(g_row - m_loc)
        w_col = jnp.exp(g_col - m_loc)
        c_loc = _dot_tn((v * w_col).astype(BF16), kb)
        n_loc = _dot(jnp.broadcast_to(w_row, (SUBLANES, L)).astype(BF16), kb)[0:1, :]

        dmat = jnp.where(causal, cum_col - cum_row + ig_row, NEG_INF)
        inter = cum_col + m_prev
        m_t = jnp.maximum(inter, jnp.max(dmat, axis=1, keepdims=True))
        s_qk = _dot_nt(qb, kb) * jnp.exp(dmat - m_t)
        w_inter = jnp.exp(inter - m_t)
        num = _dot(s_qk.astype(BF16), vb) + w_inter * _dot_nt(qb, c_prev.astype(BF16))
        den = jnp.sum(s_qk, axis=1, keepdims=True) + w_inter * jnp.sum(q * n_prev, axis=1, keepdims=True)
        hout = num / jnp.maximum(jnp.abs(den), jnp.exp(-m_t))

        hn = hout * lax.rsqrt(jnp.mean(hout * hout, axis=1, keepdims=True) + NORM_EPS) * gm_ref[:, sl]
        o_ref[0, :, sl] = hn * jax.nn.sigmoid(om[:, sl])

        m_new = jnp.maximum(a_tot + m_prev, m_loc)
        s_prev = jnp.exp(a_tot + m_prev - m_new)
        s_loc = jnp.exp(m_loc - m_new)
        c_ref[h] = s_prev * c_prev + s_loc * c_loc
        n_ref[h:h + 1, :] = s_prev * n_prev + s_loc * n_loc
        m_ref[h:h + 1, :] = jnp.broadcast_to(m_new, (1, m_ref.shape[1]))


def _mlstm(qkm, vm, om, gates, conv_w, conv_b, b_gate, g_m):
    b, s, _ = qkm.shape
    L = MLSTM_CHUNK
    nc = s // L
    bg = jnp.pad(b_gate.reshape(1, -1), ((0, 0), (0, GATE_LANES - b_gate.shape[-1])))
    tok = lambda w: pl.BlockSpec((1, L, w), lambda bi, c: (bi, c, 0))
    full = lambda a: pl.BlockSpec(a.shape, lambda bi, c: (0,) * a.ndim)
    cb = conv_b.reshape(1, -1)
    gm = g_m.reshape(1, -1)
    return pl.pallas_call(
        _mlstm_kernel,
        grid=(b, nc),
        in_specs=[tok(2 * D_MLSTM), tok(D_MLSTM), tok(D_MLSTM), tok(GATE_LANES),
                  full(conv_w), full(cb), full(bg), full(gm)],
        out_specs=tok(D_MLSTM),
        out_shape=jax.ShapeDtypeStruct((b, s, D_MLSTM), F32),
        scratch_shapes=[pltpu.VMEM((SUBLANES + L, 2 * D_MLSTM), F32),
                        pltpu.VMEM((MLSTM_HEADS, MLSTM_HEAD_DIM, MLSTM_HEAD_DIM), F32),
                        pltpu.VMEM((SUBLANES, MLSTM_HEAD_DIM), F32),
                        pltpu.VMEM((SUBLANES, LANES), F32)],
        compiler_params=_params(("arbitrary", "arbitrary")),
        name="mlstm",
    )(qkm, vm, om, gates, conv_w, cb, bg, gm)


META_E0, META_E1, META_W0, META_W1, META_L0, META_L1 = range(6)
STAT_LEN, STAT_START = 0, 1


def _outproj_kernel(attn_ref, hm_ref, x_ref, mod_ref, g_ref, wout_ref, wr_hi_ref, wr_lo_ref, br_ref,
                    x1_ref, h2_ref, meta_ref, stat_ref):
    d = x_ref.shape[1]
    tm = x_ref.shape[0]

    mod = mod_ref[0]
    y = (_dot(attn_ref[...].astype(BF16), wout_ref[0:D_ATT, :])
         + _dot(hm_ref[...].astype(BF16), wout_ref[D_ATT:, :]))
    x1 = x_ref[...] + mod[:, 2 * d:3 * d] * y
    x1_ref[...] = x1
    h2 = _rms_modulate(x1, g_ref[...], mod[:, 3 * d:4 * d], mod[:, 4 * d:5 * d])
    h2_ref[...] = h2.astype(BF16)

    h_hi = h2.astype(BF16)
    h_lo = (h2 - h_hi.astype(F32)).astype(BF16)
    logit = (_dot(h_hi, wr_hi_ref[...]) + _dot(h_lo, wr_hi_ref[...]) + _dot(h_hi, wr_lo_ref[...])
             + br_ref[...])
    lane = lax.broadcasted_iota(jnp.int32, logit.shape, 1)
    big = jnp.int32(ROUTER_LANES)

    is_g = (lane >= GROUP_LANE0) & (lane < GROUP_LANE0 + N_GROUPS)
    gmax = jnp.max(jnp.where(is_g, logit, -jnp.inf), axis=1, keepdims=True)
    gsum = jnp.sum(jnp.where(is_g, jnp.exp(logit - gmax), 0.0), axis=1, keepdims=True)
    g_w = 1.0 / gsum
    g_idx = jnp.min(jnp.where(is_g & (logit == gmax), lane, big), axis=1, keepdims=True) - GROUP_LANE0

    in_grp = (lane < N_EXPERTS) & ((lane // EXPERTS_PER_GROUP) == g_idx)
    emax = jnp.max(jnp.where(in_grp, logit, -jnp.inf), axis=1, keepdims=True)
    esum = jnp.sum(jnp.where(in_grp, jnp.exp(logit - emax), 0.0), axis=1, keepdims=True)
    e0 = jnp.min(jnp.where(in_grp & (logit == emax), lane, big), axis=1, keepdims=True)
    rest = in_grp & (lane != e0)
    e2max = jnp.max(jnp.where(rest, logit, -jnp.inf), axis=1, keepdims=True)
    e1 = jnp.min(jnp.where(rest & (logit == e2max), lane, big), axis=1, keepdims=True)
    p0 = 1.0 / esum
    p1 = jnp.exp(e2max - emax) / esum
    w0 = g_w * p0 / (p0 + p1)
    w1 = g_w * p1 / (p0 + p1)

    memb = jnp.where((lane == e0) | (lane == e1), 1.0, 0.0)
    tpos = lax.broadcasted_iota(jnp.int32, (tm, tm), 0)
    spos = lax.broadcasted_iota(jnp.int32, (tm, tm), 1)
    before = jnp.where(spos < tpos, 1.0, 0.0).astype(BF16)
    earlier = _dot(before, memb.astype(BF16))
    count = jnp.sum(memb, axis=0, keepdims=True).astype(jnp.int32)
    run_len = (((count + (RUN_ALIGN - 1)) // RUN_ALIGN) * RUN_ALIGN).astype(F32)
    epos = lax.broadcasted_iota(jnp.int32, (ROUTER_LANES, ROUTER_LANES), 0)
    fpos = lax.broadcasted_iota(jnp.int32, (ROUTER_LANES, ROUTER_LANES), 1)
    lower_e = jnp.where(epos < fpos, 1.0, 0.0).astype(BF16)
    run_start = _dot(jnp.broadcast_to(run_len, (SUBLANES, ROUTER_LANES)).astype(BF16), lower_e)[0:1, :]
    row = run_start + earlier
    l0 = jnp.sum(jnp.where(lane == e0, row, 0.0), axis=1, keepdims=True)
    l1 = jnp.sum(jnp.where(lane == e1, row, 0.0), axis=1, keepdims=True)

    meta = jnp.zeros(logit.shape, F32)
    for slot, val in ((META_E0, e0.astype(F32)), (META_E1, e1.astype(F32)), (META_W0, w0), (META_W1, w1),
                      (META_L0, l0), (META_L1, l1)):
        meta = jnp.where(lane == slot, val, meta)
    meta_ref[...] = meta
    srow = lax.broadcasted_iota(jnp.int32, (SUBLANES, ROUTER_LANES), 0)
    stat_ref[0] = jnp.where(srow == STAT_LEN, run_len, jnp.where(srow == STAT_START, run_start, 0.0))


def _outproj(attn, hm, x2d, mod3, norm_g, w_out, w_rg, b_rg, w_re, b_re, seq):
    t, d = x2d.shape
    tm = MOE_TILE
    per_batch = seq // tm
    wr = jnp.pad(jnp.concatenate([w_re, w_rg], axis=1), ((0, 0), (0, ROUTER_LANES - N_EXPERTS - N_GROUPS)))
    br = jnp.pad(jnp.concatenate([b_re, b_rg]).reshape(1, -1), ((0, 0), (0, ROUTER_LANES - N_EXPERTS - N_GROUPS)))
    wr_hi = wr.astype(BF16)
    wr_lo = (wr - wr_hi.astype(F32)).astype(BF16)
    wout = w_out.astype(BF16)
    tok = lambda w: pl.BlockSpec((tm, w), lambda i: (i, 0))
    full = lambda a: pl.BlockSpec(a.shape, lambda i: (0,) * a.ndim)
    return pl.pallas_call(
        _outproj_kernel,
        grid=(t // tm,),
        in_specs=[tok(D_ATT), tok(D_MLSTM), tok(d),
                  pl.BlockSpec((1, 1, mod3.shape[2]), lambda i: (i // per_batch, 0, 0)),
                  full(norm_g), full(wout), full(wr_hi), full(wr_lo), full(br)],
        out_specs=[tok(d), tok(d), tok(ROUTER_LANES),
                   pl.BlockSpec((1, SUBLANES, ROUTER_LANES), lambda i: (i, 0, 0))],
        out_shape=[jax.ShapeDtypeStruct((t, d), F32), jax.ShapeDtypeStruct((t, d), BF16),
                   jax.ShapeDtypeStruct((t, ROUTER_LANES), F32),
                   jax.ShapeDtypeStruct((t // tm, SUBLANES, ROUTER_LANES), F32)],
        compiler_params=_params(("arbitrary",)),
        name="outproj",
    )(attn, hm, x2d, mod3, norm_g, wout, wr_hi, wr_lo, br)


def _for_each_piece(length, pieces, fn):
    pos = jnp.int32(0)
    for size in pieces:
        take = length & size

        @pl.when(take != 0)
        def _(pos=pos, size=size):
            fn(pl.multiple_of(pos, RUN_ALIGN), size)

        pos = pos + take


def _for_each_run(tile, len_ref, loc_ref, glob_ref, fn):
    def per_expert(e, carry):
        idx = tile * N_EXPERTS + e
        loc = loc_ref[idx]
        glob = glob_ref[idx]
        _for_each_piece(len_ref[idx], RUN_PIECES,
                        lambda pos, size: fn(pl.multiple_of(loc + pos, RUN_ALIGN),
                                             pl.multiple_of(glob + pos, RUN_ALIGN), size))
        return carry

    lax.fori_loop(0, N_EXPERTS, per_expert, 0)


def _dispatch_kernel(len_ref, loc_ref, glob_ref, tail_ref, tlen_ref, nu_ref, meta_ref, h2_ref, xs_ref,
                     perm_ref, zero_ref, sem, zsem):
    i = pl.program_id(0)
    rb = MOE_BLOCK_ROWS
    n_blocks = xs_ref.shape[0] // rb

    @pl.when(i == 0)
    def _():
        zero_ref[...] = jnp.zeros(zero_ref.shape, zero_ref.dtype)

        def zero_copy(row, size):
            return pltpu.make_async_copy(zero_ref.at[pl.ds(0, size)], xs_ref.at[pl.ds(row, size)], zsem)

        def fill(op):
            def per_expert(e, carry):
                start = tail_ref[e]
                _for_each_piece(tlen_ref[e], TAIL_PIECES,
                                lambda pos, size: op(zero_copy(pl.multiple_of(start + pos, RUN_ALIGN), size)))
                return carry

            lax.fori_loop(0, N_EXPERTS, per_expert, 0)

            def per_block(blk, carry):
                for part in range(rb // ZERO_ROWS):
                    op(zero_copy(pl.multiple_of(blk * rb + part * ZERO_ROWS, ZERO_ROWS), ZERO_ROWS))
                return carry

            lax.fori_loop(nu_ref[0], n_blocks, per_block, 0)

        fill(lambda cp: cp.start())
        fill(lambda cp: cp.wait())

    meta_t = meta_ref[...].T
    l0 = meta_t[META_L0:META_L0 + 1, :].astype(jnp.int32)
    l1 = meta_t[META_L1:META_L1 + 1, :].astype(jnp.int32)
    rpos = lax.broadcasted_iota(jnp.int32, (perm_ref.shape[0], meta_ref.shape[0]), 0)
    onehot = jnp.where((rpos == l0) | (rpos == l1), 1.0, 0.0).astype(BF16)
    perm_ref[...] = _dot(onehot, h2_ref[...])

    def run_copy(loc, glob, size):
        return pltpu.make_async_copy(perm_ref.at[pl.ds(loc, size)], xs_ref.at[pl.ds(glob, size)], sem)

    _for_each_run(i, len_ref, loc_ref, glob_ref, lambda loc, glob, size: run_copy(loc, glob, size).start())
    _for_each_run(i, len_ref, loc_ref, glob_ref, lambda loc, glob, size: run_copy(loc, glob, size).wait())


def _dispatch(plan, meta, h2):
    t, d = h2.shape
    tm = MOE_TILE
    grid_spec = pltpu.PrefetchScalarGridSpec(
        num_scalar_prefetch=6,
        grid=(t // tm,),
        in_specs=[pl.BlockSpec((tm, ROUTER_LANES), lambda i, *_: (i, 0)),
                  pl.BlockSpec((tm, d), lambda i, *_: (i, 0))],
        out_specs=pl.BlockSpec(memory_space=pl.ANY),
        scratch_shapes=[pltpu.VMEM((PERM_ROWS, d), F32), pltpu.VMEM((ZERO_ROWS, d), F32),
                        pltpu.SemaphoreType.DMA(()), pltpu.SemaphoreType.DMA(())],
    )
    return pl.pallas_call(
        _dispatch_kernel,
        grid_spec=grid_spec,
        out_shape=jax.ShapeDtypeStruct((plan["n_rows"], d), F32),
        compiler_params=_params(("arbitrary",)),
        name="dispatch",
    )(plan["run_len"], plan["run_loc"], plan["run_glob"], plan["tail_start"], plan["tail_len"], plan["n_used"],
      meta, h2)


def _expert_kernel(be_ref, nu_ref, xs_ref, w1_ref, w3_ref, w2_ref, y_ref):
    i = pl.program_id(0)

    @pl.when(i < nu_ref[0])
    def _():
        x = xs_ref[...].astype(BF16)
        a = _dot(x, w1_ref[0])
        b = _dot(x, w3_ref[0])
        y_ref[...] = _dot((_silu(a) * b).astype(BF16), w2_ref[0])

    @pl.when(i >= nu_ref[0])
    def _():
        y_ref[...] = jnp.zeros(y_ref.shape, F32)


def _experts(block_e, n_used, xs, w1, w3, w2):
    n_rows, d = xs.shape
    rb = MOE_BLOCK_ROWS
    de = w1.shape[2]
    used = lambda i, be, nu: jnp.minimum(i, nu[0] - 1)
    grid_spec = pltpu.PrefetchScalarGridSpec(
        num_scalar_prefetch=2,
        grid=(n_rows // rb,),
        in_specs=[pl.BlockSpec((rb, d), lambda i, be, nu: (used(i, be, nu), 0)),
                  pl.BlockSpec((1, d, de), lambda i, be, nu: (be[used(i, be, nu)], 0, 0)),
                  pl.BlockSpec((1, d, de), lambda i, be, nu: (be[used(i, be, nu)], 0, 0)),
                  pl.BlockSpec((1, de, d), lambda i, be, nu: (be[used(i, be, nu)], 0, 0))],
        out_specs=pl.BlockSpec((rb, d), lambda i, be, nu: (i, 0)),
    )
    return pl.pallas_call(
        _expert_kernel,
        grid_spec=grid_spec,
        out_shape=jax.ShapeDtypeStruct((n_rows, d), F32),
        compiler_params=_params(("arbitrary",)),
        name="experts",
    )(block_e, n_used, xs, w1, w3, w2)


def _combine_kernel(final_norm, len_ref, loc_ref, glob_ref, x1_ref, meta_ref, mod_ref, g_ref, y_ref, o_ref,
                    ybuf_ref, sem):
    tc, d = x1_ref.shape
    i = pl.program_id(0)

    @pl.when(i == 0)
    def _():
        ybuf_ref[...] = jnp.zeros(ybuf_ref.shape, ybuf_ref.dtype)

    def run_copy(loc, glob, size):
        return pltpu.make_async_copy(y_ref.at[pl.ds(glob, size)], ybuf_ref.at[pl.ds(loc, size)], sem)

    _for_each_run(i, len_ref, loc_ref, glob_ref, lambda loc, glob, size: run_copy(loc, glob, size).start())
    _for_each_run(i, len_ref, loc_ref, glob_ref, lambda loc, glob, size: run_copy(loc, glob, size).wait())

    meta = meta_ref[...]
    yb = ybuf_ref[...].astype(BF16)
    rpos = lax.broadcasted_iota(jnp.int32, (tc, ybuf_ref.shape[0]), 1)
    moe = jnp.zeros((tc, d), F32)
    for l_lane, w_lane in ((META_L0, META_W0), (META_L1, META_W1)):
        sel = jnp.where(rpos == meta[:, l_lane:l_lane + 1].astype(jnp.int32), 1.0, 0.0).astype(BF16)
        moe = moe + meta[:, w_lane:w_lane + 1] * _dot(sel, yb)
    x2 = x1_ref[...] + mod_ref[0][:, 5 * d:6 * d] * moe
    if final_norm:
        x2 = x2 * lax.rsqrt(jnp.mean(x2 * x2, axis=-1, keepdims=True) + NORM_EPS) * g_ref[...]
    o_ref[...] = x2


def _combine(plan, x1, meta, mod3, norm_f_g, y, seq, final_norm):
    t, d = x1.shape
    tc = MOE_TILE
    per_batch = seq // tc
    gf = norm_f_g.reshape(1, d)
    grid_spec = pltpu.PrefetchScalarGridSpec(
        num_scalar_prefetch=3,
        grid=(t // tc,),
        in_specs=[pl.BlockSpec((tc, d), lambda i, *_: (i, 0)),
                  pl.BlockSpec((tc, ROUTER_LANES), lambda i, *_: (i, 0)),
                  pl.BlockSpec((1, 1, mod3.shape[2]), lambda i, *_: (i // per_batch, 0, 0)),
                  pl.BlockSpec(gf.shape, lambda i, *_: (0, 0)),
                  pl.BlockSpec(memory_space=pl.ANY)],
        out_specs=pl.BlockSpec((tc, d), lambda i, *_: (i, 0)),
        scratch_shapes=[pltpu.VMEM((PERM_ROWS, d), F32), pltpu.SemaphoreType.DMA(())],
    )
    return pl.pallas_call(
        functools.partial(_combine_kernel, final_norm),
        grid_spec=grid_spec,
        out_shape=jax.ShapeDtypeStruct((t, d), F32),
        compiler_params=_params(("arbitrary",)),
        name="combine",
    )(plan["run_len"], plan["run_loc"], plan["run_glob"], x1, meta, mod3, gf, y)


def _routing_plan(stats, n_tokens):
    rb = MOE_BLOCK_ROWS
    run_len = stats[:, STAT_LEN, :N_EXPERTS].astype(jnp.int32)
    run_loc = stats[:, STAT_START, :N_EXPERTS].astype(jnp.int32)
    n_tiles = run_len.shape[0]
    total = jnp.sum(run_len, axis=0)
    padded = ((total + rb - 1) // rb) * rb
    pad_end = jnp.cumsum(padded)
    pad_start = pad_end - padded
    run_glob = pad_start[None, :] + jnp.cumsum(run_len, axis=0) - run_len
    n_rows = n_tokens * TOP_K_EXPERTS + n_tiles * N_EXPERTS * (RUN_ALIGN - 1) + N_EXPERTS * (rb - 1)
    n_rows = ((n_rows + rb - 1) // rb) * rb
    block_row = jnp.arange(n_rows // rb, dtype=jnp.int32) * rb
    block_e = jnp.minimum(jnp.sum((pad_end[None, :] <= block_row[:, None]).astype(jnp.int32), axis=1),
                          N_EXPERTS - 1)
    return dict(run_len=run_len.reshape(-1), run_loc=run_loc.reshape(-1), run_glob=run_glob.reshape(-1),
                tail_start=pad_start + total, tail_len=padded - total,
                n_used=(pad_end[-1:] // rb).astype(jnp.int32), block_e=block_e.astype(jnp.int32), n_rows=n_rows)


def kernel(x, c, positions, w_ada, b_ada, norm1_g, w_in, b_gate, conv_w, conv_b, attn_out_g, mlstm_out_g,
           w_out, norm2_g, w_rg, b_rg, w_re, b_re, w1, w3, w2, norm_f_g):
    b, s, d = x.shape
    depth = w_ada.shape[0]
    assert d == D_MODEL and s % MOE_TILE == 0
    x2d = x.reshape(b * s, d)
    for l in range(depth):
        mod3 = _mod(c, w_ada[l], b_ada[l]).reshape(b, 1, 6 * d)
        q, k, vt, qkm, vm, om, gates = _inproj(x2d.reshape(b, s, d), positions, mod3,
                                              norm1_g[l].reshape(1, d), w_in[l])
        attn = _moba(q, k, vt, attn_out_g[l].reshape(1, D_ATT))
        hm = _mlstm(qkm, vm, om, gates, conv_w[l], conv_b[l], b_gate[l], mlstm_out_g[l])
        x1, h2, meta, stats = _outproj(attn.reshape(b * s, D_ATT), hm.reshape(b * s, D_MLSTM), x2d, mod3,
                                       norm2_g[l].reshape(1, d), w_out[l], w_rg[l], b_rg[l], w_re[l],
                                       b_re[l], s)
        plan = _routing_plan(stats, b * s)
        xs = _dispatch(plan, meta, h2)
        y = _experts(plan["block_e"], plan["n_used"], xs, w1[l].astype(BF16), w3[l].astype(BF16),
                     w2[l].astype(BF16))
        x2d = _combine(plan, x1, meta, mod3, norm_f_g, y, s, final_norm=(l == depth - 1))
    return x2d.reshape(b, s, d)
```

```python
import functools

import jax
import jax.numpy as jnp
from jax import lax
from jax.experimental import pallas as pl
from jax.experimental.pallas import tpu as pltpu

F32 = jnp.float32
BF16 = jnp.bfloat16

D_MODEL = 1024
D_ATT = 512
ATT_HEADS = 8
ATT_HEAD_DIM = 64
D_MLSTM = 512
MLSTM_HEADS = 4
MLSTM_HEAD_DIM = 128
MOBA_BLOCK = 256
MOBA_TOPK = 3
ROPE_THETA = 10000.0
MLSTM_CHUNK = 128
CONV_WIDTH = 4
N_GROUPS = 4
EXPERTS_PER_GROUP = 8
N_EXPERTS = N_GROUPS * EXPERTS_PER_GROUP
TOP_K_EXPERTS = 2
D_EXPERT = 512
MOE_BLOCK_ROWS = 256
NORM_EPS = 1e-6
NEG_INF = -1e30
LOG2_E = 1.4426950408889634
KV_CHUNK = 4

LANES = 128
SUBLANES = 8
VMEM_LIMIT_BYTES = 56 * 1024 * 1024

MOE_TILE = 512
RUN_ALIGN = SUBLANES
PERM_ROWS = TOP_K_EXPERTS * MOE_TILE + N_EXPERTS * RUN_ALIGN
RUN_PIECES = tuple(RUN_ALIGN << p for p in reversed(range(7)))
TAIL_PIECES = tuple(RUN_ALIGN << p for p in reversed(range(5)))
ZERO_ROWS = TAIL_PIECES[0]
GATE_LANES = LANES
ROUTER_LANES = LANES
GROUP_LANE0 = N_EXPERTS


def _dot(a, b):
    return jnp.dot(a, b, preferred_element_type=F32)


def _dot_nt(a, b):
    return lax.dot_general(a, b, (((1,), (1,)), ((), ())), preferred_element_type=F32)


def _dot_tn(a, b):
    return lax.dot_general(a, b, (((0,), (0,)), ((), ())), preferred_element_type=F32)


def _split3(x):
    a = x.astype(BF16)
    r = x - a.astype(F32)
    b = r.astype(BF16)
    c = (r - b.astype(F32)).astype(BF16)
    return a, b, c


def _silu(x):
    return x * jax.nn.sigmoid(x)


def _log_sigmoid(x):
    return jnp.minimum(x, 0.0) - jnp.log1p(jnp.exp(-jnp.abs(x)))


def _params(semantics, vmem=VMEM_LIMIT_BYTES):
    return pltpu.CompilerParams(dimension_semantics=semantics, vmem_limit_bytes=vmem)


def _mod_kernel(c_ref, w_ref, b_ref, o_ref):
    sc = _silu(c_ref[...])
    o_ref[...] = jnp.dot(sc, w_ref[...], precision=lax.Precision.HIGHEST,
                         preferred_element_type=F32) + b_ref[...]


def _mod(c, w_ada, b_ada):
    b, d = c.shape
    n = w_ada.shape[1]
    tn = D_MODEL
    return pl.pallas_call(
        _mod_kernel,
        grid=(n // tn,),
        in_specs=[pl.BlockSpec((b, d), lambda i: (0, 0)),
                  pl.BlockSpec((d, tn), lambda i: (0, i)),
                  pl.BlockSpec((1, tn), lambda i: (0, i))],
        out_specs=pl.BlockSpec((b, tn), lambda i: (0, i)),
        out_shape=jax.ShapeDtypeStruct((b, n), F32),
        compiler_params=_params(("arbitrary",)),
        name="mod",
    )(c, w_ada, b_ada.reshape(1, n))


def _rms_modulate(x, g, shift, scale):
    y = x * lax.rsqrt(jnp.mean(x * x, axis=-1, keepdims=True) + NORM_EPS)
    return (y * g) * (1.0 + scale) + shift


def _rope(t, cos, sin, first_half):
    outs = []
    for c in range(t.shape[1] // LANES):
        tc = t[:, c * LANES:(c + 1) * LANES]
        rot = jnp.where(first_half, -pltpu.roll(tc, LANES - ATT_HEAD_DIM // 2, 1),
                        pltpu.roll(tc, ATT_HEAD_DIM // 2, 1))
        outs.append(tc * cos + rot * sin)
    return jnp.concatenate(outs, axis=1)


def _inproj_kernel(x_ref, pos_ref, mod_ref, g_ref, invf_ref, wq_ref, wk_ref, wvt_ref, wqk_ref,
                   wv_ref, wo_ref, wg_ref, q_ref, k_ref, vt_ref, qkm_ref, vm_ref, om_ref, gt_ref):
    d = x_ref.shape[2]
    x = x_ref[0]
    mod = mod_ref[0]
    h = _rms_modulate(x, g_ref[...], mod[:, 0:d], mod[:, d:2 * d])
    hb = h.astype(BF16)

    ang = pos_ref[0].astype(F32) * invf_ref[...]
    cos = jnp.cos(ang)
    sin = jnp.sin(ang)
    lane = lax.broadcasted_iota(jnp.int32, cos.shape, 1)
    first_half = (lane & (ATT_HEAD_DIM // 2)) == 0

    q = _rope(_dot(hb, wq_ref[...]), cos, sin, first_half)
    q_ref[0] = (q * (ATT_HEAD_DIM ** -0.5 * LOG2_E)).astype(BF16)
    k_ref[0] = _rope(_dot(hb, wk_ref[...]), cos, sin, first_half).astype(BF16)
    vt_ref[0, 0] = _dot_nt(wvt_ref[...], hb).astype(BF16)
    qkm_ref[0] = _dot(hb, wqk_ref[...])
    vm_ref[0] = _dot(hb, wv_ref[...])
    om_ref[0] = _dot(hb, wo_ref[...])
    gt_ref[0] = _dot(hb, wg_ref[...])


def _inproj(x, positions, mod3, norm_g, w_in):
    b, s, d = x.shape
    tm = MOBA_BLOCK
    nb = s // tm
    o = [0, D_ATT, 2 * D_ATT, 3 * D_ATT, 3 * D_ATT + 2 * D_MLSTM, 3 * D_ATT + 3 * D_MLSTM,
         3 * D_ATT + 4 * D_MLSTM, 3 * D_ATT + 4 * D_MLSTM + 2 * MLSTM_HEADS]
    wb = w_in.astype(BF16)
    wq, wk, wv_a, wqk, wv, wo, wg = (wb[:, o[i]:o[i + 1]] for i in range(7))
    wvt = wv_a.T
    wg = jnp.pad(wg, ((0, 0), (0, GATE_LANES - 2 * MLSTM_HEADS)))
    half = ATT_HEAD_DIM // 2
    inv_freq = ROPE_THETA ** (-jnp.arange(half, dtype=F32) / half)
    invf = jnp.tile(inv_freq, LANES // half).reshape(1, LANES)

    full = lambda a: pl.BlockSpec(a.shape, lambda bi, i: (0,) * a.ndim)
    tok = lambda w: pl.BlockSpec((1, tm, w), lambda bi, i: (bi, i, 0))
    out_shape = [jax.ShapeDtypeStruct((b, s, D_ATT), BF16),
                 jax.ShapeDtypeStruct((b, s, D_ATT), BF16),
                 jax.ShapeDtypeStruct((b, nb, D_ATT, tm), BF16),
                 jax.ShapeDtypeStruct((b, s, 2 * D_MLSTM), F32),
                 jax.ShapeDtypeStruct((b, s, D_MLSTM), F32),
                 jax.ShapeDtypeStruct((b, s, D_MLSTM), F32),
                 jax.ShapeDtypeStruct((b, s, GATE_LANES), F32)]
    return pl.pallas_call(
        _inproj_kernel,
        grid=(b, nb),
        in_specs=[tok(d), tok(1),
                  pl.BlockSpec((1, 1, mod3.shape[2]), lambda bi, i: (bi, 0, 0)),
                  full(norm_g), full(invf), full(wq), full(wk), full(wvt), full(wqk), full(wv),
                  full(wo), full(wg)],
        out_specs=[tok(D_ATT), tok(D_ATT),
                   pl.BlockSpec((1, 1, D_ATT, tm), lambda bi, i: (bi, i, 0, 0)),
                   tok(2 * D_MLSTM), tok(D_MLSTM), tok(D_MLSTM), tok(GATE_LANES)],
        out_shape=out_shape,
        compiler_params=_params(("arbitrary", "arbitrary")),
        name="inproj",
    )(x, positions.reshape(b, s, 1), mod3, norm_g, invf, wq, wk, wvt, wqk, wv, wo, wg)


def _moba_kernel(q_ref, k_ref, vt_ref, g_ref, o_ref, kmean_ref, bias_ref, st_ref):
    blk = MOBA_BLOCK
    nb = k_ref.shape[1] // blk
    j = pl.program_id(2)
    heads = LANES // ATT_HEAD_DIM

    @pl.when(j == 0)
    def _():
        for n in range(nb):
            kb = k_ref[0, n * blk:(n + 1) * blk, :].astype(F32)
            kmean_ref[n:n + 1, :] = jnp.mean(kb, axis=0, keepdims=True)

    q = q_ref[0]
    lane = lax.broadcasted_iota(jnp.int32, q.shape, 1)
    blk_id = lax.broadcasted_iota(jnp.int32, (nb, blk), 0)
    km_hi = kmean_ref[...].astype(BF16)
    km_lo = (kmean_ref[...] - km_hi.astype(F32)).astype(BF16)
    past = blk_id < j

    qh = []
    for hh in range(heads):
        in_head = (lane >= hh * ATT_HEAD_DIM) & (lane < (hh + 1) * ATT_HEAD_DIM)
        qh.append(jnp.where(in_head, q, jnp.zeros_like(q)))
        gate = _dot_nt(km_hi, qh[hh]) + _dot_nt(km_lo, qh[hh])
        g = jnp.where(past, gate, NEG_INF)
        sel = jnp.zeros(g.shape, F32)
        for _ in range(min(MOBA_TOPK, nb)):
            top = jnp.max(g, axis=0, keepdims=True)
            idx = jnp.min(jnp.where(g == top, blk_id, nb), axis=0, keepdims=True)
            pick = blk_id == idx
            sel = jnp.where(pick, 1.0, sel)
            g = jnp.where(pick, -jnp.inf, g)
        bias = jnp.where((sel > 0.0) & past, 0.0, NEG_INF)
        for cc in range(nb // KV_CHUNK):
            bias_ref[hh, cc] = bias[cc * KV_CHUNK:(cc + 1) * KV_CHUNK, :]

    kpos = lax.broadcasted_iota(jnp.int32, (blk, blk), 0)
    qpos = lax.broadcasted_iota(jnp.int32, (blk, blk), 1)
    causal_bias = jnp.where(kpos <= qpos, 0.0, NEG_INF)

    def score(c, slot):
        kb = k_ref[0, pl.ds(pl.multiple_of(c * (KV_CHUNK * blk), KV_CHUNK * blk), KV_CHUNK * blk), :]
        tops = []
        for hh in range(heads):
            bias = bias_ref[hh, c]
            top = None
            for i in range(KV_CHUNK):
                tile_bias = jnp.where(c * KV_CHUNK + i == j, causal_bias, bias[i:i + 1, :])
                st = _dot_nt(kb[i * blk:(i + 1) * blk, :], qh[hh]) + tile_bias
                st_ref[slot, hh, i * blk:(i + 1) * blk, :] = st
                smax = jnp.max(st, axis=0, keepdims=True)
                top = smax if top is None else jnp.maximum(top, smax)
            tops.append(top)
        return tuple(tops)

    def accumulate(c, slot, state, tops):
        new = []
        for hh in range(heads):
            m, l, acc = state[3 * hh:3 * hh + 3]
            m_new = jnp.maximum(m, tops[hh])
            alpha = jnp.exp2(m - m_new)
            l = alpha * l
            pv = None
            for i in range(KV_CHUNK):
                p = jnp.exp2(st_ref[slot, hh, i * blk:(i + 1) * blk, :] - m_new)
                l = l + jnp.sum(p, axis=0, keepdims=True)
                d_i = _dot(vt_ref[0, c * KV_CHUNK + i], p.astype(BF16))
                pv = d_i if pv is None else pv + d_i
            new += [m_new, l, alpha * acc + pv]
        return tuple(new)

    state = ()
    for hh in range(heads):
        state += (jnp.full((1, blk), NEG_INF, F32), jnp.zeros((1, blk), F32), jnp.zeros((LANES, blk), F32))
    last = j // KV_CHUNK

    def stage_pair(c, slot, state, tops):
        nxt = score(c + 1, 1 - slot)
        return accumulate(c, slot, state, tops), nxt

    def body(c, carry):
        return lax.cond(c % 2 == 0, functools.partial(stage_pair, c, 0), functools.partial(stage_pair, c, 1),
                        *carry)

    state, tops = lax.fori_loop(0, last, body, (state, score(0, 0)))
    state = lax.cond(last % 2 == 0, functools.partial(accumulate, last, 0), functools.partial(accumulate, last, 1),
                     state, tops)

    row = lax.broadcasted_iota(jnp.int32, (LANES, blk), 0)
    out_t = jnp.zeros((LANES, blk), F32)
    for hh in range(heads):
        in_head = (row >= hh * ATT_HEAD_DIM) & (row < (hh + 1) * ATT_HEAD_DIM)
        o_h = jnp.where(in_head, state[3 * hh + 2] / state[3 * hh + 1], 0.0)
        ms = jnp.sum(o_h * o_h, axis=0, keepdims=True) * (1.0 / ATT_HEAD_DIM)
        out_t = out_t + o_h * lax.rsqrt(ms + NORM_EPS)
    o_ref[0] = out_t.T * g_ref[...]


def _moba(q, k, vt, g_att):
    b, s, w = q.shape
    blk = MOBA_BLOCK
    nb = s // blk
    pairs = w // LANES
    return pl.pallas_call(
        _moba_kernel,
        grid=(b, pairs, nb),
        in_specs=[pl.BlockSpec((1, blk, LANES), lambda bi, p, j: (bi, j, p)),
                  pl.BlockSpec((1, s, LANES), lambda bi, p, j: (bi, 0, p)),
                  pl.BlockSpec((1, nb, LANES, blk), lambda bi, p, j: (bi, 0, p, 0)),
                  pl.BlockSpec((1, LANES), lambda bi, p, j: (0, p))],
        out_specs=pl.BlockSpec((1, blk, LANES), lambda bi, p, j: (bi, j, p)),
        out_shape=jax.ShapeDtypeStruct((b, s, w), F32),
        scratch_shapes=[pltpu.VMEM((nb, LANES), F32),
                        pltpu.VMEM((LANES // ATT_HEAD_DIM, nb // KV_CHUNK, KV_CHUNK, blk), F32),
                        pltpu.VMEM((2, LANES // ATT_HEAD_DIM, KV_CHUNK * blk, blk), F32)],
        compiler_params=_params(("arbitrary", "arbitrary", "arbitrary")),
        name="moba",
    )(q, k, vt, g_att)


def _mlstm_kernel(qkm_ref, vm_ref, om_ref, gt_ref, cw_ref, cb_ref, bg_ref, gm_ref, o_ref,
                  uext_ref, c_ref, n_ref, m_ref):
    L = MLSTM_CHUNK
    dh = MLSTM_HEAD_DIM
    nh = MLSTM_HEADS
    c = pl.program_id(1)

    @pl.when(c == 0)
    def _():
        uext_ref[0:SUBLANES, :] = jnp.zeros((SUBLANES, uext_ref.shape[1]), F32)
        c_ref[...] = jnp.zeros(c_ref.shape, F32)
        n_ref[...] = jnp.zeros(n_ref.shape, F32)
        m_ref[...] = jnp.zeros(m_ref.shape, F32)

    u = qkm_ref[0]
    uext_ref[SUBLANES:SUBLANES + L, :] = u
    cw = cw_ref[...]
    conv = cb_ref[...] + cw[CONV_WIDTH - 1:CONV_WIDTH, :] * u
    for dlt in range(1, CONV_WIDTH):
        conv = conv + cw[CONV_WIDTH - 1 - dlt:CONV_WIDTH - dlt, :] * uext_ref[SUBLANES - dlt:SUBLANES - dlt + L, :]
    uext_ref[0:SUBLANES, :] = u[L - SUBLANES:L, :]
    act = _silu(conv)

    gates = gt_ref[0] + bg_ref[...]
    gates_t = gates.T
    tpos = lax.broadcasted_iota(jnp.int32, (L, L), 0)
    spos = lax.broadcasted_iota(jnp.int32, (L, L), 1)
    causal = spos <= tpos
    tril = jnp.where(causal, 1.0, 0.0).astype(BF16)
    triu = jnp.where(tpos <= spos, 1.0, 0.0).astype(BF16)
    cum_c = sum(_dot(tril, part) for part in _split3(_log_sigmoid(gates)))
    cum_r = sum(_dot(part, triu) for part in _split3(_log_sigmoid(gates_t)))

    vm = vm_ref[0]
    om = om_ref[0]
    for h in range(nh):
        sl = slice(h * dh, (h + 1) * dh)
        q = act[:, sl]
        k = act[:, nh * dh + h * dh:nh * dh + (h + 1) * dh] * (dh ** -0.5)
        v = vm[:, sl]
        qb, kb, vb = q.astype(BF16), k.astype(BF16), v.astype(BF16)
        cum_col = cum_c[:, nh + h:nh + h + 1]
        ig_col = gates[:, h:h + 1]
        cum_row = cum_r[nh + h:nh + h + 1, :]
        ig_row = gates_t[h:h + 1, :]
        a_tot = cum_row[:, L - 1:L]
        m_prev = m_ref[h:h + 1, 0:1]
        c_prev = c_ref[h]
        n_prev = n_ref[h:h + 1, :]

        g_row = a_tot - cum_row + ig_row
        g_col = a_tot - cum_col + ig_col
        m_loc = jnp.max(g_row, axis=1, keepdims=True)
        w_row = jnp.exp(g_row - m_loc)
        w_col = jnp.exp(g_col - m_loc)
        c_loc = _dot_tn((v * w_col).astype(BF16), kb)
        n_loc = _dot(jnp.broadcast_to(w_row, (SUBLANES, L)).astype(BF16), kb)[0:1, :]

        dmat = jnp.where(causal, cum_col - cum_row + ig_row, NEG_INF)
        inter = cum_col + m_prev
        m_t = jnp.maximum(inter, jnp.max(dmat, axis=1, keepdims=True))
        s_qk = _dot_nt(qb, kb) * jnp.exp(dmat - m_t)
        w_inter = jnp.exp(inter - m_t)
        num = _dot(s_qk.astype(BF16), vb) + w_inter * _dot_nt(qb, c_prev.astype(BF16))
        den = jnp.sum(s_qk, axis=1, keepdims=True) + w_inter * jnp.sum(q * n_prev, axis=1, keepdims=True)
        hout = num / jnp.maximum(jnp.abs(den), jnp.exp(-m_t))

        hn = hout * lax.rsqrt(jnp.mean(hout * hout, axis=1, keepdims=True) + NORM_EPS) * gm_ref[:, sl]
        o_ref[0, :, sl] = hn * jax.nn.sigmoid(om[:, sl])

        m_new = jnp.maximum(a_tot + m_prev, m_loc)
        s_prev = jnp.exp(a_tot + m_prev - m_new)
        s_loc = jnp.exp(m_loc - m_new)
        c_ref[h] = s_prev * c_prev + s_loc * c_loc
        n_ref[h:h + 1, :] = s_prev * n_prev + s_loc * n_loc
        m_ref[h:h + 1, :] = jnp.broadcast_to(m_new, (1, m_ref.shape[1]))


def _mlstm(qkm, vm, om, gates, conv_w, conv_b, b_gate, g_m):
    b, s, _ = qkm.shape
    L = MLSTM_CHUNK
    nc = s // L
    bg = jnp.pad(b_gate.reshape(1, -1), ((0, 0), (0, GATE_LANES - b_gate.shape[-1])))
    tok = lambda w: pl.BlockSpec((1, L, w), lambda bi, c: (bi, c, 0))
    full = lambda a: pl.BlockSpec(a.shape, lambda bi, c: (0,) * a.ndim)
    cb = conv_b.reshape(1, -1)
    gm = g_m.reshape(1, -1)
    return pl.pallas_call(
        _mlstm_kernel,
        grid=(b, nc),
        in_specs=[tok(2 * D_MLSTM), tok(D_MLSTM), tok(D_MLSTM), tok(GATE_LANES),
                  full(conv_w), full(cb), full(bg), full(gm)],
        out_specs=tok(D_MLSTM),
        out_shape=jax.ShapeDtypeStruct((b, s, D_MLSTM), F32),
        scratch_shapes=[pltpu.VMEM((SUBLANES + L, 2 * D_MLSTM), F32),
                        pltpu.VMEM((MLSTM_HEADS, MLSTM_HEAD_DIM, MLSTM_HEAD_DIM), F32),
                        pltpu.VMEM((SUBLANES, MLSTM_HEAD_DIM), F32),
                        pltpu.VMEM((SUBLANES, LANES), F32)],
        compiler_params=_params(("arbitrary", "arbitrary")),
        name="mlstm",
    )(qkm, vm, om, gates, conv_w, cb, bg, gm)


META_E0, META_E1, META_W0, META_W1, META_L0, META_L1 = range(6)
STAT_LEN, STAT_START = 0, 1


def _outproj_kernel(attn_ref, hm_ref, x_ref, mod_ref, g_ref, wout_ref, wr_hi_ref, wr_lo_ref, br_ref,
                    x1_ref, h2_ref, meta_ref, stat_ref):
    d = x_ref.shape[1]
    tm = x_ref.shape[0]

    mod = mod_ref[0]
    y = (_dot(attn_ref[...].astype(BF16), wout_ref[0:D_ATT, :])
         + _dot(hm_ref[...].astype(BF16), wout_ref[D_ATT:, :]))
    x1 = x_ref[...] + mod[:, 2 * d:3 * d] * y
    x1_ref[...] = x1
    h2 = _rms_modulate(x1, g_ref[...], mod[:, 3 * d:4 * d], mod[:, 4 * d:5 * d])
    h2_ref[...] = h2.astype(BF16)

    h_hi = h2.astype(BF16)
    h_lo = (h2 - h_hi.astype(F32)).astype(BF16)
    logit = (_dot(h_hi, wr_hi_ref[...]) + _dot(h_lo, wr_hi_ref[...]) + _dot(h_hi, wr_lo_ref[...])
             + br_ref[...])
    lane = lax.broadcasted_iota(jnp.int32, logit.shape, 1)
    big = jnp.int32(ROUTER_LANES)

    is_g = (lane >= GROUP_LANE0) & (lane < GROUP_LANE0 + N_GROUPS)
    gmax = jnp.max(jnp.where(is_g, logit, -jnp.inf), axis=1, keepdims=True)
    gsum = jnp.sum(jnp.where(is_g, jnp.exp(logit - gmax), 0.0), axis=1, keepdims=True)
    g_w = 1.0 / gsum
    g_idx = jnp.min(jnp.where(is_g & (logit == gmax), lane, big), axis=1, keepdims=True) - GROUP_LANE0

    in_grp = (lane < N_EXPERTS) & ((lane // EXPERTS_PER_GROUP) == g_idx)
    emax = jnp.max(jnp.where(in_grp, logit, -jnp.inf), axis=1, keepdims=True)
    esum = jnp.sum(jnp.where(in_grp, jnp.exp(logit - emax), 0.0), axis=1, keepdims=True)
    e0 = jnp.min(jnp.where(in_grp & (logit == emax), lane, big), axis=1, keepdims=True)
    rest = in_grp & (lane != e0)
    e2max = jnp.max(jnp.where(rest, logit, -jnp.inf), axis=1, keepdims=True)
    e1 = jnp.min(jnp.where(rest & (logit == e2max), lane, big), axis=1, keepdims=True)
    p0 = 1.0 / esum
    p1 = jnp.exp(e2max - emax) / esum
    w0 = g_w * p0 / (p0 + p1)
    w1 = g_w * p1 / (p0 + p1)

    memb = jnp.where((lane == e0) | (lane == e1), 1.0, 0.0)
    tpos = lax.broadcasted_iota(jnp.int32, (tm, tm), 0)
    spos = lax.broadcasted_iota(jnp.int32, (tm, tm), 1)
    before = jnp.where(spos < tpos, 1.0, 0.0).astype(BF16)
    earlier = _dot(before, memb.astype(BF16))
    count = jnp.sum(memb, axis=0, keepdims=True).astype(jnp.int32)
    run_len = (((count + (RUN_ALIGN - 1)) // RUN_ALIGN) * RUN_ALIGN).astype(F32)
    epos = lax.broadcasted_iota(jnp.int32, (ROUTER_LANES, ROUTER_LANES), 0)
    fpos = lax.broadcasted_iota(jnp.int32, (ROUTER_LANES, ROUTER_LANES), 1)
    lower_e = jnp.where(epos < fpos, 1.0, 0.0).astype(BF16)
    run_start = _dot(jnp.broadcast_to(run_len, (SUBLANES, ROUTER_LANES)).astype(BF16), lower_e)[0:1, :]
    row = run_start + earlier
    l0 = jnp.sum(jnp.where(lane == e0, row, 0.0), axis=1, keepdims=True)
    l1 = jnp.sum(jnp.where(lane == e1, row, 0.0), axis=1, keepdims=True)

    meta = jnp.zeros(logit.shape, F32)
    for slot, val in ((META_E0, e0.astype(F32)), (META_E1, e1.astype(F32)), (META_W0, w0), (META_W1, w1),
                      (META_L0, l0), (META_L1, l1)):
        meta = jnp.where(lane == slot, val, meta)
    meta_ref[...] = meta
    srow = lax.broadcasted_iota(jnp.int32, (SUBLANES, ROUTER_LANES), 0)
    stat_ref[0] = jnp.where(srow == STAT_LEN, run_len, jnp.where(srow == STAT_START, run_start, 0.0))


def _outproj(attn, hm, x2d, mod3, norm_g, w_out, w_rg, b_rg, w_re, b_re, seq):
    t, d = x2d.shape
    tm = MOE_TILE
    per_batch = seq // tm
    wr = jnp.pad(jnp.concatenate([w_re, w_rg], axis=1), ((0, 0), (0, ROUTER_LANES - N_EXPERTS - N_GROUPS)))
    br = jnp.pad(jnp.concatenate([b_re, b_rg]).reshape(1, -1), ((0, 0), (0, ROUTER_LANES - N_EXPERTS - N_GROUPS)))
    wr_hi = wr.astype(BF16)
    wr_lo = (wr - wr_hi.astype(F32)).astype(BF16)
    wout = w_out.astype(BF16)
    tok = lambda w: pl.BlockSpec((tm, w), lambda i: (i, 0))
    full = lambda a: pl.BlockSpec(a.shape, lambda i: (0,) * a.ndim)
    return pl.pallas_call(
        _outproj_kernel,
        grid=(t // tm,),
        in_specs=[tok(D_ATT), tok(D_MLSTM), tok(d),
                  pl.BlockSpec((1, 1, mod3.shape[2]), lambda i: (i // per_batch, 0, 0)),
                  full(norm_g), full(wout), full(wr_hi), full(wr_lo), full(br)],
        out_specs=[tok(d), tok(d), tok(ROUTER_LANES),
                   pl.BlockSpec((1, SUBLANES, ROUTER_LANES), lambda i: (i, 0, 0))],
        out_shape=[jax.ShapeDtypeStruct((t, d), F32), jax.ShapeDtypeStruct((t, d), BF16),
                   jax.ShapeDtypeStruct((t, ROUTER_LANES), F32),
                   jax.ShapeDtypeStruct((t // tm, SUBLANES, ROUTER_LANES), F32)],
        compiler_params=_params(("arbitrary",)),
        name="outproj",
    )(attn, hm, x2d, mod3, norm_g, wout, wr_hi, wr_lo, br)


def _for_each_piece(length, pieces, fn):
    pos = jnp.int32(0)
    for size in pieces:
        take = length & size

        @pl.when(take != 0)
        def _(pos=pos, size=size):
            fn(pl.multiple_of(pos, RUN_ALIGN), size)

        pos = pos + take


def _for_each_run(tile, len_ref, loc_ref, glob_ref, fn):
    def per_expert(e, carry):
        idx = tile * N_EXPERTS + e
        loc = loc_ref[idx]
        glob = glob_ref[idx]
        _for_each_piece(len_ref[idx], RUN_PIECES,
                        lambda pos, size: fn(pl.multiple_of(loc + pos, RUN_ALIGN),
                                             pl.multiple_of(glob + pos, RUN_ALIGN), size))
        return carry

    lax.fori_loop(0, N_EXPERTS, per_expert, 0)


def _dispatch_kernel(len_ref, loc_ref, glob_ref, tail_ref, tlen_ref, nu_ref, meta_ref, h2_ref, xs_ref,
                     perm_ref, zero_ref, sem, zsem):
    i = pl.program_id(0)
    rb = MOE_BLOCK_ROWS
    n_blocks = xs_ref.shape[0] // rb

    @pl.when(i == 0)
    def _():
        zero_ref[...] = jnp.zeros(zero_ref.shape, zero_ref.dtype)

        def zero_copy(row, size):
            return pltpu.make_async_copy(zero_ref.at[pl.ds(0, size)], xs_ref.at[pl.ds(row, size)], zsem)

        def fill(op):
            def per_expert(e, carry):
                start = tail_ref[e]
                _for_each_piece(tlen_ref[e], TAIL_PIECES,
                                lambda pos, size: op(zero_copy(pl.multiple_of(start + pos, RUN_ALIGN), size)))
                return carry

            lax.fori_loop(0, N_EXPERTS, per_expert, 0)

            def per_block(blk, carry):
                for part in range(rb // ZERO_ROWS):
                    op(zero_copy(pl.multiple_of(blk * rb + part * ZERO_ROWS, ZERO_ROWS), ZERO_ROWS))
                return carry

            lax.fori_loop(nu_ref[0], n_blocks, per_block, 0)

        fill(lambda cp: cp.start())
        fill(lambda cp: cp.wait())

    meta_t = meta_ref[...].T
    l0 = meta_t[META_L0:META_L0 + 1, :].astype(jnp.int32)
    l1 = meta_t[META_L1:META_L1 + 1, :].astype(jnp.int32)
    rpos = lax.broadcasted_iota(jnp.int32, (perm_ref.shape[1], meta_ref.shape[0]), 0)
    onehot = jnp.where((rpos == l0) | (rpos == l1), 1.0, 0.0).astype(BF16)
    slot = i % 2
    perm_ref[slot] = _dot(onehot, h2_ref[...])

    def run_copy(s, loc, glob, size):
        return pltpu.make_async_copy(perm_ref.at[s, pl.ds(loc, size)], xs_ref.at[pl.ds(glob, size)], sem.at[s])

    def wait_tile(tile, s):
        _for_each_run(tile, len_ref, loc_ref, glob_ref, lambda loc, glob, size: run_copy(s, loc, glob, size).wait())

    @pl.when(i > 0)
    def _():
        wait_tile(i - 1, 1 - slot)

    _for_each_run(i, len_ref, loc_ref, glob_ref, lambda loc, glob, size: run_copy(slot, loc, glob, size).start())

    @pl.when(i == pl.num_programs(0) - 1)
    def _():
        wait_tile(i, slot)


def _dispatch(plan, meta, h2):
    t, d = h2.shape
    tm = MOE_TILE
    grid_spec = pltpu.PrefetchScalarGridSpec(
        num_scalar_prefetch=6,
        grid=(t // tm,),
        in_specs=[pl.BlockSpec((tm, ROUTER_LANES), lambda i, *_: (i, 0)),
                  pl.BlockSpec((tm, d), lambda i, *_: (i, 0))],
        out_specs=pl.BlockSpec(memory_space=pl.ANY),
        scratch_shapes=[pltpu.VMEM((2, PERM_ROWS, d), F32), pltpu.VMEM((ZERO_ROWS, d), F32),
                        pltpu.SemaphoreType.DMA((2,)), pltpu.SemaphoreType.DMA(())],
    )
    return pl.pallas_call(
        _dispatch_kernel,
        grid_spec=grid_spec,
        out_shape=jax.ShapeDtypeStruct((plan["n_rows"], d), F32),
        compiler_params=_params(("arbitrary",)),
        name="dispatch",
    )(plan["run_len"], plan["run_loc"], plan["run_glob"], plan["tail_start"], plan["tail_len"], plan["n_used"],
      meta, h2)


def _expert_kernel(be_ref, nu_ref, xs_ref, w1_ref, w3_ref, w2_ref, y_ref, w1b_ref, w3b_ref, w2b_ref):
    i = pl.program_id(0)
    used = i < nu_ref[0]

    @pl.when(used & ((i == 0) | (be_ref[i] != be_ref[jnp.maximum(i, 1) - 1])))
    def _():
        w1b_ref[...] = w1_ref[0].astype(BF16)
        w3b_ref[...] = w3_ref[0].astype(BF16)
        w2b_ref[...] = w2_ref[0].astype(BF16)

    @pl.when(used)
    def _():
        x = xs_ref[...].astype(BF16)
        a = _dot(x, w1b_ref[...])
        b = _dot(x, w3b_ref[...])
        y_ref[...] = _dot((_silu(a) * b).astype(BF16), w2b_ref[...])

    @pl.when(jnp.logical_not(used))
    def _():
        y_ref[...] = jnp.zeros(y_ref.shape, F32)


def _experts(block_e, n_used, xs, w1, w3, w2):
    n_rows, d = xs.shape
    rb = MOE_BLOCK_ROWS
    de = w1.shape[2]
    used = lambda i, be, nu: jnp.minimum(i, nu[0] - 1)
    grid_spec = pltpu.PrefetchScalarGridSpec(
        num_scalar_prefetch=2,
        grid=(n_rows // rb,),
        in_specs=[pl.BlockSpec((rb, d), lambda i, be, nu: (used(i, be, nu), 0)),
                  pl.BlockSpec((1, d, de), lambda i, be, nu: (be[used(i, be, nu)], 0, 0)),
                  pl.BlockSpec((1, d, de), lambda i, be, nu: (be[used(i, be, nu)], 0, 0)),
                  pl.BlockSpec((1, de, d), lambda i, be, nu: (be[used(i, be, nu)], 0, 0))],
        out_specs=pl.BlockSpec((rb, d), lambda i, be, nu: (i, 0)),
        scratch_shapes=[pltpu.VMEM((d, de), BF16), pltpu.VMEM((d, de), BF16), pltpu.VMEM((de, d), BF16)],
    )
    return pl.pallas_call(
        _expert_kernel,
        grid_spec=grid_spec,
        out_shape=jax.ShapeDtypeStruct((n_rows, d), F32),
        compiler_params=_params(("arbitrary",)),
        name="experts",
    )(block_e, n_used, xs, w1, w3, w2)


def _combine_kernel(final_norm, len_ref, loc_ref, glob_ref, x1_ref, meta_ref, mod_ref, g_ref, y_ref, o_ref,
                    ybuf_ref, sem):
    tc, d = x1_ref.shape
    i = pl.program_id(0)

    slot = i % 2

    def run_copy(s, loc, glob, size):
        return pltpu.make_async_copy(y_ref.at[pl.ds(glob, size)], ybuf_ref.at[s, pl.ds(loc, size)], sem.at[s])

    def start_tile(tile, s):
        _for_each_run(tile, len_ref, loc_ref, glob_ref, lambda loc, glob, size: run_copy(s, loc, glob, size).start())

    @pl.when(i == 0)
    def _():
        ybuf_ref[...] = jnp.zeros(ybuf_ref.shape, ybuf_ref.dtype)
        start_tile(0, 0)

    @pl.when(i + 1 < pl.num_programs(0))
    def _():
        start_tile(i + 1, 1 - slot)

    _for_each_run(i, len_ref, loc_ref, glob_ref, lambda loc, glob, size: run_copy(slot, loc, glob, size).wait())

    meta = meta_ref[...]
    yb = ybuf_ref[slot].astype(BF16)
    rpos = lax.broadcasted_iota(jnp.int32, (tc, ybuf_ref.shape[1]), 1)
    sel = jnp.zeros(rpos.shape, F32)
    for l_lane, w_lane in ((META_L0, META_W0), (META_L1, META_W1)):
        sel = jnp.where(rpos == meta[:, l_lane:l_lane + 1].astype(jnp.int32), meta[:, w_lane:w_lane + 1], sel)
    moe = _dot(sel.astype(BF16), yb)
    x2 = x1_ref[...] + mod_ref[0][:, 5 * d:6 * d] * moe
    if final_norm:
        x2 = x2 * lax.rsqrt(jnp.mean(x2 * x2, axis=-1, keepdims=True) + NORM_EPS) * g_ref[...]
    o_ref[...] = x2


def _combine(plan, x1, meta, mod3, norm_f_g, y, seq, final_norm):
    t, d = x1.shape
    tc = MOE_TILE
    per_batch = seq // tc
    gf = norm_f_g.reshape(1, d)
    grid_spec = pltpu.PrefetchScalarGridSpec(
        num_scalar_prefetch=3,
        grid=(t // tc,),
        in_specs=[pl.BlockSpec((tc, d), lambda i, *_: (i, 0)),
                  pl.BlockSpec((tc, ROUTER_LANES), lambda i, *_: (i, 0)),
                  pl.BlockSpec((1, 1, mod3.shape[2]), lambda i, *_: (i // per_batch, 0, 0)),
                  pl.BlockSpec(gf.shape, lambda i, *_: (0, 0)),
                  pl.BlockSpec(memory_space=pl.ANY)],
        out_specs=pl.BlockSpec((tc, d), lambda i, *_: (i, 0)),
        scratch_shapes=[pltpu.VMEM((2, PERM_ROWS, d), F32), pltpu.SemaphoreType.DMA((2,))],
    )
    return pl.pallas_call(
        functools.partial(_combine_kernel, final_norm),
        grid_spec=grid_spec,
        out_shape=jax.ShapeDtypeStruct((t, d), F32),
        compiler_params=_params(("arbitrary",)),
        name="combine",
    )(plan["run_len"], plan["run_loc"], plan["run_glob"], x1, meta, mod3, gf, y)


def _routing_plan(stats, n_tokens):
    rb = MOE_BLOCK_ROWS
    run_len = stats[:, STAT_LEN, :N_EXPERTS].astype(jnp.int32)
    run_loc = stats[:, STAT_START, :N_EXPERTS].astype(jnp.int32)
    n_tiles = run_len.shape[0]
    total = jnp.sum(run_len, axis=0)
    padded = ((total + rb - 1) // rb) * rb
    pad_end = jnp.cumsum(padded)
    pad_start = pad_end - padded
    run_glob = pad_start[None, :] + jnp.cumsum(run_len, axis=0) - run_len
    n_rows = n_tokens * TOP_K_EXPERTS + n_tiles * N_EXPERTS * (RUN_ALIGN - 1) + N_EXPERTS * (rb - 1)
    n_rows = ((n_rows + rb - 1) // rb) * rb
    block_row = jnp.arange(n_rows // rb, dtype=jnp.int32) * rb
    block_e = jnp.minimum(jnp.sum((pad_end[None, :] <= block_row[:, None]).astype(jnp.int32), axis=1),
                          N_EXPERTS - 1)
    return dict(run_len=run_len.reshape(-1), run_loc=run_loc.reshape(-1), run_glob=run_glob.reshape(-1),
                tail_start=pad_start + total, tail_len=padded - total,
                n_used=(pad_end[-1:] // rb).astype(jnp.int32), block_e=block_e.astype(jnp.int32), n_rows=n_rows)


def kernel(x, c, positions, w_ada, b_ada, norm1_g, w_in, b_gate, conv_w, conv_b, attn_out_g, mlstm_out_g,
           w_out, norm2_g, w_rg, b_rg, w_re, b_re, w1, w3, w2, norm_f_g):
    b, s, d = x.shape
    depth = w_ada.shape[0]
    assert d == D_MODEL and s % MOE_TILE == 0 and s % (KV_CHUNK * MOBA_BLOCK) == 0
    x2d = x.reshape(b * s, d)
    for l in range(depth):
        mod3 = _mod(c, w_ada[l], b_ada[l]).reshape(b, 1, 6 * d)
        q, k, vt, qkm, vm, om, gates = _inproj(x2d.reshape(b, s, d), positions, mod3,
                                              norm1_g[l].reshape(1, d), w_in[l])
        attn = _moba(q, k, vt, attn_out_g[l].reshape(1, D_ATT))
        hm = _mlstm(qkm, vm, om, gates, conv_w[l], conv_b[l], b_gate[l], mlstm_out_g[l])
        x1, h2, meta, stats = _outproj(attn.reshape(b * s, D_ATT), hm.reshape(b * s, D_MLSTM), x2d, mod3,
                                       norm2_g[l].reshape(1, d), w_out[l], w_rg[l], b_rg[l], w_re[l],
                                       b_re[l], s)
        plan = _routing_plan(stats, b * s)
        xs = _dispatch(plan, meta, h2)
        y = _experts(plan["block_e"], plan["n_used"], xs, w1[l], w3[l], w2[l])
        x2d = _combine(plan, x1, meta, mod3, norm_f_g, y, s, final_norm=(l == depth - 1))
    return x2d.reshape(b, s, d)
```

```python
import functools

import jax
import jax.numpy as jnp
from jax import lax
from jax.experimental import pallas as pl
from jax.experimental.pallas import tpu as pltpu

F32 = jnp.float32
BF16 = jnp.bfloat16

D_MODEL = 1024
D_ATT = 512
ATT_HEADS = 8
ATT_HEAD_DIM = 64
D_MLSTM = 512
MLSTM_HEADS = 4
MLSTM_HEAD_DIM = 128
MOBA_BLOCK = 256
MOBA_TOPK = 3
ROPE_THETA = 10000.0
MLSTM_CHUNK = 128
CONV_WIDTH = 4
N_GROUPS = 4
EXPERTS_PER_GROUP = 8
N_EXPERTS = N_GROUPS * EXPERTS_PER_GROUP
TOP_K_EXPERTS = 2
D_EXPERT = 512
MOE_BLOCK_ROWS = 256
NORM_EPS = 1e-6
NEG_INF = -1e30
LOG2_E = 1.4426950408889634
KV_CHUNK = 4

LANES = 128
SUBLANES = 8
VMEM_LIMIT_BYTES = 56 * 1024 * 1024

MOE_TILE = 512
RUN_ALIGN = 2 * SUBLANES
PERM_ROWS = TOP_K_EXPERTS * MOE_TILE + N_EXPERTS * RUN_ALIGN
RUN_PIECES = tuple(RUN_ALIGN << p for p in reversed(range(6)))
TAIL_PIECES = tuple(RUN_ALIGN << p for p in reversed(range(4)))
ZERO_ROWS = TAIL_PIECES[0]
GATE_LANES = 2 * LANES
ROUTER_LANES = LANES
GROUP_LANE0 = N_EXPERTS


def _dot(a, b):
    return jnp.dot(a, b, preferred_element_type=F32)


def _dot_nt(a, b):
    return lax.dot_general(a, b, (((1,), (1,)), ((), ())), preferred_element_type=F32)


def _dot_tn(a, b):
    return lax.dot_general(a, b, (((0,), (0,)), ((), ())), preferred_element_type=F32)


def _split3(x):
    a = x.astype(BF16)
    r = x - a.astype(F32)
    b = r.astype(BF16)
    c = (r - b.astype(F32)).astype(BF16)
    return a, b, c


def _silu(x):
    return x * jax.nn.sigmoid(x)


def _log_sigmoid(x):
    return jnp.minimum(x, 0.0) - jnp.log1p(jnp.exp(-jnp.abs(x)))


def _params(semantics, vmem=VMEM_LIMIT_BYTES):
    return pltpu.CompilerParams(dimension_semantics=semantics, vmem_limit_bytes=vmem)


def _mod_kernel(c_ref, w_ref, b_ref, o_ref):
    sc = _silu(c_ref[...])
    o_ref[...] = jnp.dot(sc, w_ref[...], precision=lax.Precision.HIGHEST,
                         preferred_element_type=F32) + b_ref[...]


def _mod(c, w_ada, b_ada):
    b, d = c.shape
    n = w_ada.shape[1]
    tn = D_MODEL
    return pl.pallas_call(
        _mod_kernel,
        grid=(n // tn,),
        in_specs=[pl.BlockSpec((b, d), lambda i: (0, 0)),
                  pl.BlockSpec((d, tn), lambda i: (0, i)),
                  pl.BlockSpec((1, tn), lambda i: (0, i))],
        out_specs=pl.BlockSpec((b, tn), lambda i: (0, i)),
        out_shape=jax.ShapeDtypeStruct((b, n), F32),
        compiler_params=_params(("arbitrary",)),
        name="mod",
    )(c, w_ada, b_ada.reshape(1, n))


def _rms_modulate(x, g, shift, scale):
    y = x * lax.rsqrt(jnp.mean(x * x, axis=-1, keepdims=True) + NORM_EPS)
    return (y * g) * (1.0 + scale) + shift


def _rope(t, cos, sin, first_half):
    outs = []
    for c in range(t.shape[1] // LANES):
        tc = t[:, c * LANES:(c + 1) * LANES]
        rot = jnp.where(first_half, -pltpu.roll(tc, LANES - ATT_HEAD_DIM // 2, 1),
                        pltpu.roll(tc, ATT_HEAD_DIM // 2, 1))
        outs.append(tc * cos + rot * sin)
    return jnp.concatenate(outs, axis=1)


def _inproj_kernel(x_ref, pos_ref, mod_ref, g_ref, invf_ref, wq_ref, wk_ref, wvt_ref, wqk_ref,
                   wv_ref, wo_ref, wg_ref, q_ref, k_ref, vt_ref, qkm_ref, vm_ref, om_ref, gt_ref):
    d = x_ref.shape[2]
    x = x_ref[0]
    mod = mod_ref[0]
    h = _rms_modulate(x, g_ref[...], mod[:, 0:d], mod[:, d:2 * d])
    hb = h.astype(BF16)

    ang = pos_ref[0].astype(F32) * invf_ref[...]
    cos = jnp.cos(ang)
    sin = jnp.sin(ang)
    lane = lax.broadcasted_iota(jnp.int32, cos.shape, 1)
    first_half = (lane & (ATT_HEAD_DIM // 2)) == 0

    q = _rope(_dot(hb, wq_ref[...]), cos, sin, first_half)
    q_ref[0] = (q * (ATT_HEAD_DIM ** -0.5 * LOG2_E)).astype(BF16)
    k_ref[0] = _rope(_dot(hb, wk_ref[...]), cos, sin, first_half).astype(BF16)
    vt_ref[0, 0] = _dot_nt(wvt_ref[...], hb).astype(BF16)
    qkm_ref[0] = _dot(hb, wqk_ref[...])
    vm_ref[0] = _dot(hb, wv_ref[...])
    om_ref[0] = _dot(hb, wo_ref[...])
    gt_ref[0] = _dot(hb, wg_ref[...])


def _inproj(x, positions, mod3, norm_g, w_in):
    b, s, d = x.shape
    tm = MOBA_BLOCK
    nb = s // tm
    o = [0, D_ATT, 2 * D_ATT, 3 * D_ATT, 3 * D_ATT + 2 * D_MLSTM, 3 * D_ATT + 3 * D_MLSTM,
         3 * D_ATT + 4 * D_MLSTM, 3 * D_ATT + 4 * D_MLSTM + 2 * MLSTM_HEADS]
    wb = w_in.astype(BF16)
    wq, wk, wv_a, wqk, wv, wo, wg = (wb[:, o[i]:o[i + 1]] for i in range(7))
    wvt = wv_a.T
    lane_pad = ((0, 0), (0, LANES - MLSTM_HEADS))
    wg = jnp.concatenate([jnp.pad(wg[:, :MLSTM_HEADS], lane_pad), jnp.pad(wg[:, MLSTM_HEADS:], lane_pad)], axis=1)
    half = ATT_HEAD_DIM // 2
    inv_freq = ROPE_THETA ** (-jnp.arange(half, dtype=F32) / half)
    invf = jnp.tile(inv_freq, LANES // half).reshape(1, LANES)

    full = lambda a: pl.BlockSpec(a.shape, lambda bi, i: (0,) * a.ndim)
    tok = lambda w: pl.BlockSpec((1, tm, w), lambda bi, i: (bi, i, 0))
    out_shape = [jax.ShapeDtypeStruct((b, s, D_ATT), BF16),
                 jax.ShapeDtypeStruct((b, s, D_ATT), BF16),
                 jax.ShapeDtypeStruct((b, nb, D_ATT, tm), BF16),
                 jax.ShapeDtypeStruct((b, s, 2 * D_MLSTM), F32),
                 jax.ShapeDtypeStruct((b, s, D_MLSTM), F32),
                 jax.ShapeDtypeStruct((b, s, D_MLSTM), F32),
                 jax.ShapeDtypeStruct((b, s, GATE_LANES), F32)]
    return pl.pallas_call(
        _inproj_kernel,
        grid=(b, nb),
        in_specs=[tok(d), tok(1),
                  pl.BlockSpec((1, 1, mod3.shape[2]), lambda bi, i: (bi, 0, 0)),
                  full(norm_g), full(invf), full(wq), full(wk), full(wvt), full(wqk), full(wv),
                  full(wo), full(wg)],
        out_specs=[tok(D_ATT), tok(D_ATT),
                   pl.BlockSpec((1, 1, D_ATT, tm), lambda bi, i: (bi, i, 0, 0)),
                   tok(2 * D_MLSTM), tok(D_MLSTM), tok(D_MLSTM), tok(GATE_LANES)],
        out_shape=out_shape,
        compiler_params=_params(("arbitrary", "arbitrary")),
        name="inproj",
    )(x, positions.reshape(b, s, 1), mod3, norm_g, invf, wq, wk, wvt, wqk, wv, wo, wg)


def _moba_kernel(q_ref, k_ref, vt_ref, g_ref, o_ref, kmean_ref, sel_ref, st_ref):
    blk = MOBA_BLOCK
    nb = k_ref.shape[1] // blk
    j = pl.program_id(2)
    heads = LANES // ATT_HEAD_DIM

    @pl.when(j == 0)
    def _():
        for n in range(nb):
            kb = k_ref[0, n * blk:(n + 1) * blk, :].astype(F32)
            kmean_ref[n:n + 1, :] = jnp.mean(kb, axis=0, keepdims=True)

    q = q_ref[0]
    lane = lax.broadcasted_iota(jnp.int32, q.shape, 1)
    blk_id = lax.broadcasted_iota(jnp.int32, (nb, blk), 0)
    km_hi = kmean_ref[...].astype(BF16)
    km_lo = (kmean_ref[...] - km_hi.astype(F32)).astype(BF16)
    past = blk_id < j

    qh = []
    for hh in range(heads):
        in_head = (lane >= hh * ATT_HEAD_DIM) & (lane < (hh + 1) * ATT_HEAD_DIM)
        qh.append(jnp.where(in_head, q, jnp.zeros_like(q)))
        gate = _dot_nt(km_hi, qh[hh]) + _dot_nt(km_lo, qh[hh])
        g = jnp.where(past, gate, NEG_INF)
        sel = jnp.zeros(g.shape, F32)
        for _ in range(min(MOBA_TOPK, nb)):
            top = jnp.max(g, axis=0, keepdims=True)
            idx = jnp.min(jnp.where(g == top, blk_id, nb), axis=0, keepdims=True)
            pick = blk_id == idx
            sel = jnp.where(pick, 1.0, sel)
            g = jnp.where(pick, -jnp.inf, g)
        sel_ref[hh, 0:nb, :] = jnp.where((sel > 0.0) & past, 1.0, 0.0)
        sel_ref[hh, nb:nb + SUBLANES, :] = jnp.zeros((SUBLANES, blk), F32)

    kpos = lax.broadcasted_iota(jnp.int32, (blk, blk), 0)
    qpos = lax.broadcasted_iota(jnp.int32, (blk, blk), 1)
    causal_bias = jnp.where(kpos <= qpos, 0.0, NEG_INF)

    def slab_block(t, i):
        b_i = j - KV_CHUNK * t - (KV_CHUNK - 1 - i)
        return jnp.maximum(b_i, 0), jnp.where(b_i >= 0, b_i, nb)

    def score(t, slot, own_chunk):
        tops = [None] * heads
        for i in range(KV_CHUNK):
            src, row = slab_block(t, i)
            kb = k_ref[0, pl.ds(pl.multiple_of(src * blk, blk), blk), :]
            for hh in range(heads):
                st = _dot_nt(kb, qh[hh])
                if own_chunk and i == KV_CHUNK - 1:
                    st = st + causal_bias
                    smax = jnp.max(st, axis=0, keepdims=True)
                else:
                    smax = (jnp.max(st, axis=0, keepdims=True)
                            + (sel_ref[hh, pl.ds(row, 1), :] - 1.0) * (-NEG_INF))
                st_ref[slot, hh, i * blk:(i + 1) * blk, :] = st
                tops[hh] = smax if tops[hh] is None else jnp.maximum(tops[hh], smax)
        return tuple(tops)

    def accumulate(t, slot, state, tops):
        new = []
        for hh in range(heads):
            m, l, acc = state[3 * hh:3 * hh + 3]
            m_new = jnp.maximum(m, tops[hh])
            alpha = jnp.exp2(m - m_new)
            l = alpha * l
            acc = alpha * acc
            for i in range(KV_CHUNK):
                src, row = slab_block(t, i)
                keep = sel_ref[hh, pl.ds(row, 1), :]
                if i == KV_CHUNK - 1:
                    keep = jnp.where(t == 0, 1.0, keep)
                p = jnp.exp2(jnp.minimum(st_ref[slot, hh, i * blk:(i + 1) * blk, :] - m_new, 0.0))
                l = l + keep * jnp.sum(p, axis=0, keepdims=True)
                v_h = vt_ref[0, src][hh * ATT_HEAD_DIM:(hh + 1) * ATT_HEAD_DIM, :]
                acc = acc + keep * _dot(v_h, p.astype(BF16))
            new += [m_new, l, acc]
        return tuple(new)

    state = ()
    for hh in range(heads):
        state += (jnp.full((1, blk), NEG_INF, F32), jnp.zeros((1, blk), F32), jnp.zeros((ATT_HEAD_DIM, blk), F32))
    last = j // KV_CHUNK

    def stage_pair(t, slot, state, tops):
        nxt = score(t + 1, 1 - slot, False)
        return accumulate(t, slot, state, tops), nxt

    def body(t, carry):
        return lax.cond(t % 2 == 0, functools.partial(stage_pair, t, 0), functools.partial(stage_pair, t, 1),
                        *carry)

    state, tops = lax.fori_loop(0, last, body, (state, score(0, 0, True)))
    state = lax.cond(last % 2 == 0, functools.partial(accumulate, last, 0), functools.partial(accumulate, last, 1),
                     state, tops)

    outs = []
    for hh in range(heads):
        o_h = state[3 * hh + 2] / state[3 * hh + 1]
        ms = jnp.mean(o_h * o_h, axis=0, keepdims=True)
        outs.append(o_h * lax.rsqrt(ms + NORM_EPS))
    o_ref[0] = jnp.concatenate(outs, axis=0).T * g_ref[...]


def _moba(q, k, vt, g_att):
    b, s, w = q.shape
    blk = MOBA_BLOCK
    nb = s // blk
    pairs = w // LANES
    return pl.pallas_call(
        _moba_kernel,
        grid=(b, pairs, nb),
        in_specs=[pl.BlockSpec((1, blk, LANES), lambda bi, p, j: (bi, j, p)),
                  pl.BlockSpec((1, s, LANES), lambda bi, p, j: (bi, 0, p)),
                  pl.BlockSpec((1, nb, LANES, blk), lambda bi, p, j: (bi, 0, p, 0)),
                  pl.BlockSpec((1, LANES), lambda bi, p, j: (0, p))],
        out_specs=pl.BlockSpec((1, blk, LANES), lambda bi, p, j: (bi, j, p)),
        out_shape=jax.ShapeDtypeStruct((b, s, w), F32),
        scratch_shapes=[pltpu.VMEM((nb, LANES), F32),
                        pltpu.VMEM((LANES // ATT_HEAD_DIM, nb + SUBLANES, blk), F32),
                        pltpu.VMEM((2, LANES // ATT_HEAD_DIM, KV_CHUNK * blk, blk), F32)],
        compiler_params=_params(("arbitrary", "arbitrary", "arbitrary")),
        name="moba",
    )(q, k, vt, g_att)


def _mlstm_kernel(qkm_ref, vm_ref, om_ref, gt_ref, cw_ref, cb_ref, bg_ref, gm_ref, bsel_ref, o_ref,
                  uext_ref, c_ref, n_ref, m_ref):
    L = MLSTM_CHUNK
    dh = MLSTM_HEAD_DIM
    nh = MLSTM_HEADS
    assert L == dh == LANES and 2 * nh == SUBLANES
    c = pl.program_id(1)

    @pl.when(c == 0)
    def _():
        uext_ref[0:SUBLANES, :] = jnp.zeros((SUBLANES, uext_ref.shape[1]), F32)
        c_ref[...] = jnp.zeros(c_ref.shape, F32)
        n_ref[...] = jnp.zeros(n_ref.shape, F32)
        m_ref[...] = jnp.zeros(m_ref.shape, F32)

    u = qkm_ref[0]
    uext_ref[SUBLANES:SUBLANES + L, :] = u
    cw = cw_ref[...]
    conv = cb_ref[...] + cw[CONV_WIDTH - 1:CONV_WIDTH, :] * u
    for dlt in range(1, CONV_WIDTH):
        conv = conv + cw[CONV_WIDTH - 1 - dlt:CONV_WIDTH - dlt, :] * uext_ref[SUBLANES - dlt:SUBLANES - dlt + L, :]
    uext_ref[0:SUBLANES, :] = u[L - SUBLANES:L, :]
    act = _silu(conv)

    gates = gt_ref[0] + bg_ref[...]
    ig = gates[:, 0:LANES]
    tpos = lax.broadcasted_iota(jnp.int32, (L, L), 0)
    spos = lax.broadcasted_iota(jnp.int32, (L, L), 1)
    causal = spos <= tpos
    tril = jnp.where(causal, 1.0, 0.0).astype(BF16)
    cum = sum(_dot(tril, part) for part in _split3(_log_sigmoid(gates[:, LANES:])))
    a_tot = cum[L - 1:L, :]
    m_prev = m_ref[0:1, :]
    resid = ig - cum
    pmax = resid
    shift = 1
    while shift < L:
        pmax = jnp.maximum(pmax, jnp.where(tpos >= shift, pltpu.roll(pmax, shift, 0), -jnp.inf))
        shift *= 2
    inter = cum + m_prev
    m_t = jnp.maximum(inter, cum + pmax)
    g_loc = a_tot - cum + ig
    m_loc = jnp.max(g_loc, axis=0, keepdims=True)
    w_loc = jnp.exp(g_loc - m_loc)
    head_lane = spos < nh
    packed = jnp.where(head_lane, cum - m_t, 0.0)
    for i, qty in enumerate((inter - m_t, -m_t, g_loc - m_loc), start=1):
        packed = packed + pltpu.roll(jnp.where(head_lane, qty, 0.0), SUBLANES * i, 1)
    per_t_cols = jnp.concatenate(_split3(packed), axis=1)
    per_t_rows = (jnp.where(head_lane, resid, 0.0) + pltpu.roll(jnp.where(head_lane, w_loc, 0.0), SUBLANES, 1)).T

    m_new = jnp.maximum(a_tot + m_prev, m_loc)
    s_prev = jnp.exp(a_tot + m_prev - m_new)
    s_loc = jnp.exp(m_loc - m_new)
    m_ref[0:1, :] = m_new

    vm = vm_ref[0]
    om = om_ref[0]
    ones_b = jnp.ones((L, dh), BF16)
    c_all = [c_ref[h] for h in range(nh)]
    n_all = n_ref[...]
    new_state = []
    for h in range(nh):
        sl = slice(h * dh, (h + 1) * dh)
        q = act[:, sl]
        k = act[:, nh * dh + h * dh:nh * dh + (h + 1) * dh] * (dh ** -0.5)
        v = vm[:, sl]
        qb, kb, vb = q.astype(BF16), k.astype(BF16), v.astype(BF16)
        c_prev = c_all[h]
        n_prev = n_all[h:h + 1, :]
        bcast = _dot(per_t_cols, bsel_ref[h])
        decay_b = bcast[:, 0:dh]
        w_inter_b, floor_b, w_loc_b = (jnp.exp(bcast[:, i * dh:(i + 1) * dh]) for i in range(1, 4))

        s_qk = _dot_nt(qb, kb) * jnp.where(causal, jnp.exp(decay_b + per_t_rows[h:h + 1, :]), 0.0)
        intra = _dot(s_qk.astype(BF16), jnp.concatenate([vb, ones_b], axis=1))
        state_rows = jnp.concatenate([c_prev, jnp.broadcast_to(n_prev, (dh, dh))], axis=0).astype(BF16)
        carried = _dot_nt(qb, state_rows)
        num = intra[:, :dh] + w_inter_b * carried[:, :dh]
        den = intra[:, dh:] + w_inter_b * carried[:, dh:]
        hout = num / jnp.maximum(jnp.abs(den), floor_b)

        mean_sq = _dot((hout * hout).astype(BF16), ones_b) * (1.0 / dh)
        hn = hout * lax.rsqrt(mean_sq + NORM_EPS) * gm_ref[:, sl]
        o_ref[0, :, sl] = hn * jax.nn.sigmoid(om[:, sl])

        c_loc = _dot_tn((v * w_loc_b).astype(BF16), kb)
        w_row = per_t_rows[SUBLANES + h:SUBLANES + h + 1, :]
        n_loc = _dot(jnp.broadcast_to(w_row, (SUBLANES, L)).astype(BF16), kb)[0:1, :]
        new_state.append((s_prev[:, h:h + 1] * c_prev + s_loc[:, h:h + 1] * c_loc,
                          s_prev[:, h:h + 1] * n_prev + s_loc[:, h:h + 1] * n_loc))
    for h, (c_new, n_new) in enumerate(new_state):
        c_ref[h] = c_new
        n_ref[h:h + 1, :] = n_new


def _mlstm(qkm, vm, om, gates, conv_w, conv_b, b_gate, g_m):
    b, s, _ = qkm.shape
    L = MLSTM_CHUNK
    nc = s // L
    lane_pad = ((0, 0), (0, LANES - MLSTM_HEADS))
    bg = b_gate.reshape(1, -1)
    bg = jnp.concatenate([jnp.pad(bg[:, :MLSTM_HEADS], lane_pad), jnp.pad(bg[:, MLSTM_HEADS:], lane_pad)], axis=1)
    tok = lambda w: pl.BlockSpec((1, L, w), lambda bi, c: (bi, c, 0))
    full = lambda a: pl.BlockSpec(a.shape, lambda bi, c: (0,) * a.ndim)
    cb = conv_b.reshape(1, -1)
    gm = g_m.reshape(1, -1)
    n_qty = 4
    row = jnp.arange(3 * LANES, dtype=jnp.int32) % LANES
    col_qty = jnp.arange(n_qty * MLSTM_HEAD_DIM, dtype=jnp.int32) // MLSTM_HEAD_DIM
    bsel = jnp.stack([
        ((row // SUBLANES)[:, None] == col_qty[None, :]) & (row % SUBLANES == h)[:, None]
        for h in range(MLSTM_HEADS)]).astype(BF16)
    return pl.pallas_call(
        _mlstm_kernel,
        grid=(b, nc),
        in_specs=[tok(2 * D_MLSTM), tok(D_MLSTM), tok(D_MLSTM), tok(GATE_LANES),
                  full(conv_w), full(cb), full(bg), full(gm), full(bsel)],
        out_specs=tok(D_MLSTM),
        out_shape=jax.ShapeDtypeStruct((b, s, D_MLSTM), F32),
        scratch_shapes=[pltpu.VMEM((SUBLANES + L, 2 * D_MLSTM), F32),
                        pltpu.VMEM((MLSTM_HEADS, MLSTM_HEAD_DIM, MLSTM_HEAD_DIM), F32),
                        pltpu.VMEM((SUBLANES, MLSTM_HEAD_DIM), F32),
                        pltpu.VMEM((SUBLANES, LANES), F32)],
        compiler_params=_params(("arbitrary", "arbitrary")),
        name="mlstm",
    )(qkm, vm, om, gates, conv_w, cb, bg, gm, bsel)


META_E0, META_E1, META_W0, META_W1, META_L0, META_L1 = range(6)
STAT_LEN, STAT_START = 0, 1


def _outproj_kernel(attn_ref, hm_ref, x_ref, mod_ref, g_ref, wout_ref, wr_hi_ref, wr_lo_ref, br_ref,
                    x1_ref, h2_ref, meta_ref, stat_ref):
    d = x_ref.shape[1]
    tm = x_ref.shape[0]

    mod = mod_ref[0]
    y = (_dot(attn_ref[...].astype(BF16), wout_ref[0:D_ATT, :])
         + _dot(hm_ref[...].astype(BF16), wout_ref[D_ATT:, :]))
    x1 = x_ref[...] + mod[:, 2 * d:3 * d] * y
    x1_ref[...] = x1
    h2 = _rms_modulate(x1, g_ref[...], mod[:, 3 * d:4 * d], mod[:, 4 * d:5 * d])
    h2_ref[...] = h2.astype(BF16)

    h_hi = h2.astype(BF16)
    h_lo = (h2 - h_hi.astype(F32)).astype(BF16)
    logit = (_dot(h_hi, wr_hi_ref[...]) + _dot(h_lo, wr_hi_ref[...]) + _dot(h_hi, wr_lo_ref[...])
             + br_ref[...])
    lane = lax.broadcasted_iota(jnp.int32, logit.shape, 1)
    big = jnp.int32(ROUTER_LANES)

    is_g = (lane >= GROUP_LANE0) & (lane < GROUP_LANE0 + N_GROUPS)
    gmax = jnp.max(jnp.where(is_g, logit, -jnp.inf), axis=1, keepdims=True)
    gsum = jnp.sum(jnp.where(is_g, jnp.exp(logit - gmax), 0.0), axis=1, keepdims=True)
    g_w = 1.0 / gsum
    g_idx = jnp.min(jnp.where(is_g & (logit == gmax), lane, big), axis=1, keepdims=True) - GROUP_LANE0

    in_grp = (lane < N_EXPERTS) & ((lane // EXPERTS_PER_GROUP) == g_idx)
    emax = jnp.max(jnp.where(in_grp, logit, -jnp.inf), axis=1, keepdims=True)
    esum = jnp.sum(jnp.where(in_grp, jnp.exp(logit - emax), 0.0), axis=1, keepdims=True)
    e0 = jnp.min(jnp.where(in_grp & (logit == emax), lane, big), axis=1, keepdims=True)
    rest = in_grp & (lane != e0)
    e2max = jnp.max(jnp.where(rest, logit, -jnp.inf), axis=1, keepdims=True)
    e1 = jnp.min(jnp.where(rest & (logit == e2max), lane, big), axis=1, keepdims=True)
    p0 = 1.0 / esum
    p1 = jnp.exp(e2max - emax) / esum
    w0 = g_w * p0 / (p0 + p1)
    w1 = g_w * p1 / (p0 + p1)

    memb = jnp.where((lane == e0) | (lane == e1), 1.0, 0.0)
    tpos = lax.broadcasted_iota(jnp.int32, (tm, tm), 0)
    spos = lax.broadcasted_iota(jnp.int32, (tm, tm), 1)
    before = jnp.where(spos < tpos, 1.0, 0.0).astype(BF16)
    earlier = _dot(before, memb.astype(BF16))
    count = jnp.sum(memb, axis=0, keepdims=True).astype(jnp.int32)
    run_len = (((count + (RUN_ALIGN - 1)) // RUN_ALIGN) * RUN_ALIGN).astype(F32)
    epos = lax.broadcasted_iota(jnp.int32, (ROUTER_LANES, ROUTER_LANES), 0)
    fpos = lax.broadcasted_iota(jnp.int32, (ROUTER_LANES, ROUTER_LANES), 1)
    lower_e = jnp.where(epos < fpos, 1.0, 0.0).astype(BF16)
    run_start = _dot(jnp.broadcast_to(run_len, (SUBLANES, ROUTER_LANES)).astype(BF16), lower_e)[0:1, :]
    row = run_start + earlier
    l0 = jnp.sum(jnp.where(lane == e0, row, 0.0), axis=1, keepdims=True)
    l1 = jnp.sum(jnp.where(lane == e1, row, 0.0), axis=1, keepdims=True)

    meta = jnp.zeros(logit.shape, F32)
    for slot, val in ((META_E0, e0.astype(F32)), (META_E1, e1.astype(F32)), (META_W0, w0), (META_W1, w1),
                      (META_L0, l0), (META_L1, l1)):
        meta = jnp.where(lane == slot, val, meta)
    meta_ref[...] = meta
    srow = lax.broadcasted_iota(jnp.int32, (SUBLANES, ROUTER_LANES), 0)
    stat_ref[0] = jnp.where(srow == STAT_LEN, run_len, jnp.where(srow == STAT_START, run_start, 0.0))


def _outproj(attn, hm, x2d, mod3, norm_g, w_out, w_rg, b_rg, w_re, b_re, seq):
    t, d = x2d.shape
    tm = MOE_TILE
    per_batch = seq // tm
    wr = jnp.pad(jnp.concatenate([w_re, w_rg], axis=1), ((0, 0), (0, ROUTER_LANES - N_EXPERTS - N_GROUPS)))
    br = jnp.pad(jnp.concatenate([b_re, b_rg]).reshape(1, -1), ((0, 0), (0, ROUTER_LANES - N_EXPERTS - N_GROUPS)))
    wr_hi = wr.astype(BF16)
    wr_lo = (wr - wr_hi.astype(F32)).astype(BF16)
    wout = w_out.astype(BF16)
    tok = lambda w: pl.BlockSpec((tm, w), lambda i: (i, 0))
    full = lambda a: pl.BlockSpec(a.shape, lambda i: (0,) * a.ndim)
    return pl.pallas_call(
        _outproj_kernel,
        grid=(t // tm,),
        in_specs=[tok(D_ATT), tok(D_MLSTM), tok(d),
                  pl.BlockSpec((1, 1, mod3.shape[2]), lambda i: (i // per_batch, 0, 0)),
                  full(norm_g), full(wout), full(wr_hi), full(wr_lo), full(br)],
        out_specs=[tok(d), tok(d), tok(ROUTER_LANES),
                   pl.BlockSpec((1, SUBLANES, ROUTER_LANES), lambda i: (i, 0, 0))],
        out_shape=[jax.ShapeDtypeStruct((t, d), F32), jax.ShapeDtypeStruct((t, d), BF16),
                   jax.ShapeDtypeStruct((t, ROUTER_LANES), F32),
                   jax.ShapeDtypeStruct((t // tm, SUBLANES, ROUTER_LANES), F32)],
        compiler_params=_params(("arbitrary",)),
        name="outproj",
    )(attn, hm, x2d, mod3, norm_g, wout, wr_hi, wr_lo, br)


def _for_each_piece(length, pieces, fn):
    pos = jnp.int32(0)
    for size in pieces:
        take = length & size

        @pl.when(take != 0)
        def _(pos=pos, size=size):
            fn(pl.multiple_of(pos, RUN_ALIGN), size)

        pos = pos + take


def _for_each_run(tile, len_ref, loc_ref, glob_ref, fn):
    def per_expert(e, carry):
        idx = tile * N_EXPERTS + e
        loc = loc_ref[idx]
        glob = glob_ref[idx]
        _for_each_piece(len_ref[idx], RUN_PIECES,
                        lambda pos, size: fn(pl.multiple_of(loc + pos, RUN_ALIGN),
                                             pl.multiple_of(glob + pos, RUN_ALIGN), size))
        return carry

    lax.fori_loop(0, N_EXPERTS, per_expert, 0)


def _dispatch_kernel(len_ref, loc_ref, glob_ref, tail_ref, tlen_ref, nu_ref, meta_ref, h2_ref, xs_ref,
                     perm_ref, zero_ref, sem, zsem):
    i = pl.program_id(0)
    rb = MOE_BLOCK_ROWS
    n_blocks = xs_ref.shape[0] // rb

    @pl.when(i == 0)
    def _():
        zero_ref[...] = jnp.zeros(zero_ref.shape, zero_ref.dtype)

        def zero_copy(row, size):
            return pltpu.make_async_copy(zero_ref.at[pl.ds(0, size)], xs_ref.at[pl.ds(row, size)], zsem)

        def fill(op):
            def per_expert(e, carry):
                start = tail_ref[e]
                _for_each_piece(tlen_ref[e], TAIL_PIECES,
                                lambda pos, size: op(zero_copy(pl.multiple_of(start + pos, RUN_ALIGN), size)))
                return carry

            lax.fori_loop(0, N_EXPERTS, per_expert, 0)

            def per_block(blk, carry):
                for part in range(rb // ZERO_ROWS):
                    op(zero_copy(pl.multiple_of(blk * rb + part * ZERO_ROWS, ZERO_ROWS), ZERO_ROWS))
                return carry

            lax.fori_loop(nu_ref[0], n_blocks, per_block, 0)

        fill(lambda cp: cp.start())
        fill(lambda cp: cp.wait())

    meta_t = meta_ref[...].T
    l0 = meta_t[META_L0:META_L0 + 1, :].astype(jnp.int32)
    l1 = meta_t[META_L1:META_L1 + 1, :].astype(jnp.int32)
    rpos = lax.broadcasted_iota(jnp.int32, (perm_ref.shape[1], meta_ref.shape[0]), 0)
    onehot = jnp.where((rpos == l0) | (rpos == l1), 1.0, 0.0).astype(BF16)
    slot = i % 2
    perm_ref[slot] = _dot(onehot, h2_ref[...]).astype(BF16)

    def run_copy(s, loc, glob, size):
        return pltpu.make_async_copy(perm_ref.at[s, pl.ds(loc, size)], xs_ref.at[pl.ds(glob, size)], sem.at[s])

    def wait_tile(tile, s):
        _for_each_run(tile, len_ref, loc_ref, glob_ref, lambda loc, glob, size: run_copy(s, loc, glob, size).wait())

    @pl.when(i > 0)
    def _():
        wait_tile(i - 1, 1 - slot)

    _for_each_run(i, len_ref, loc_ref, glob_ref, lambda loc, glob, size: run_copy(slot, loc, glob, size).start())

    @pl.when(i == pl.num_programs(0) - 1)
    def _():
        wait_tile(i, slot)


def _dispatch(plan, meta, h2):
    t, d = h2.shape
    tm = MOE_TILE
    grid_spec = pltpu.PrefetchScalarGridSpec(
        num_scalar_prefetch=6,
        grid=(t // tm,),
        in_specs=[pl.BlockSpec((tm, ROUTER_LANES), lambda i, *_: (i, 0)),
                  pl.BlockSpec((tm, d), lambda i, *_: (i, 0))],
        out_specs=pl.BlockSpec(memory_space=pl.ANY),
        scratch_shapes=[pltpu.VMEM((2, PERM_ROWS, d), BF16), pltpu.VMEM((ZERO_ROWS, d), BF16),
                        pltpu.SemaphoreType.DMA((2,)), pltpu.SemaphoreType.DMA(())],
    )
    return pl.pallas_call(
        _dispatch_kernel,
        grid_spec=grid_spec,
        out_shape=jax.ShapeDtypeStruct((plan["n_rows"], d), BF16),
        compiler_params=_params(("arbitrary",)),
        name="dispatch",
    )(plan["run_len"], plan["run_loc"], plan["run_glob"], plan["tail_start"], plan["tail_len"], plan["n_used"],
      meta, h2)


def _expert_kernel(be_ref, nu_ref, xs_ref, w1_ref, w3_ref, w2_ref, y_ref, w1b_ref, w3b_ref, w2b_ref):
    i = pl.program_id(0)
    used = i < nu_ref[0]

    @pl.when(used & ((i == 0) | (be_ref[i] != be_ref[jnp.maximum(i, 1) - 1])))
    def _():
        w1b_ref[...] = w1_ref[0].astype(BF16)
        w3b_ref[...] = w3_ref[0].astype(BF16)
        w2b_ref[...] = w2_ref[0].astype(BF16)

    @pl.when(used)
    def _():
        x = xs_ref[...]
        a = _dot(x, w1b_ref[...])
        b = _dot(x, w3b_ref[...])
        y_ref[...] = _dot((_silu(a) * b).astype(BF16), w2b_ref[...]).astype(y_ref.dtype)

    @pl.when(jnp.logical_not(used))
    def _():
        y_ref[...] = jnp.zeros(y_ref.shape, y_ref.dtype)


def _experts(block_e, n_used, xs, w1, w3, w2):
    n_rows, d = xs.shape
    rb = MOE_BLOCK_ROWS
    de = w1.shape[2]
    used = lambda i, be, nu: jnp.minimum(i, nu[0] - 1)
    grid_spec = pltpu.PrefetchScalarGridSpec(
        num_scalar_prefetch=2,
        grid=(n_rows // rb,),
        in_specs=[pl.BlockSpec((rb, d), lambda i, be, nu: (used(i, be, nu), 0)),
                  pl.BlockSpec((1, d, de), lambda i, be, nu: (be[used(i, be, nu)], 0, 0)),
                  pl.BlockSpec((1, d, de), lambda i, be, nu: (be[used(i, be, nu)], 0, 0)),
                  pl.BlockSpec((1, de, d), lambda i, be, nu: (be[used(i, be, nu)], 0, 0))],
        out_specs=pl.BlockSpec((rb, d), lambda i, be, nu: (i, 0)),
        scratch_shapes=[pltpu.VMEM((d, de), BF16), pltpu.VMEM((d, de), BF16), pltpu.VMEM((de, d), BF16)],
    )
    return pl.pallas_call(
        _expert_kernel,
        grid_spec=grid_spec,
        out_shape=jax.ShapeDtypeStruct((n_rows, d), BF16),
        compiler_params=_params(("arbitrary",)),
        name="experts",
    )(block_e, n_used, xs, w1, w3, w2)


def _combine_kernel(final_norm, len_ref, loc_ref, glob_ref, x1_ref, meta_ref, mod_ref, g_ref, y_ref, o_ref,
                    ybuf_ref, sem):
    tc, d = x1_ref.shape
    i = pl.program_id(0)

    slot = i % 2

    def run_copy(s, loc, glob, size):
        return pltpu.make_async_copy(y_ref.at[pl.ds(glob, size)], ybuf_ref.at[s, pl.ds(loc, size)], sem.at[s])

    def start_tile(tile, s):
        _for_each_run(tile, len_ref, loc_ref, glob_ref, lambda loc, glob, size: run_copy(s, loc, glob, size).start())

    @pl.when(i == 0)
    def _():
        ybuf_ref[...] = jnp.zeros(ybuf_ref.shape, ybuf_ref.dtype)
        start_tile(0, 0)

    @pl.when(i + 1 < pl.num_programs(0))
    def _():
        start_tile(i + 1, 1 - slot)

    _for_each_run(i, len_ref, loc_ref, glob_ref, lambda loc, glob, size: run_copy(slot, loc, glob, size).wait())

    meta = meta_ref[...]
    yb = ybuf_ref[slot]
    rpos = lax.broadcasted_iota(jnp.int32, (tc, ybuf_ref.shape[1]), 1)
    sel = jnp.zeros(rpos.shape, F32)
    for l_lane, w_lane in ((META_L0, META_W0), (META_L1, META_W1)):
        sel = jnp.where(rpos == meta[:, l_lane:l_lane + 1].astype(jnp.int32), meta[:, w_lane:w_lane + 1], sel)
    moe = _dot(sel.astype(BF16), yb)
    x2 = x1_ref[...] + mod_ref[0][:, 5 * d:6 * d] * moe
    if final_norm:
        x2 = x2 * lax.rsqrt(jnp.mean(x2 * x2, axis=-1, keepdims=True) + NORM_EPS) * g_ref[...]
    o_ref[...] = x2


def _combine(plan, x1, meta, mod3, norm_f_g, y, seq, final_norm):
    t, d = x1.shape
    tc = MOE_TILE
    per_batch = seq // tc
    gf = norm_f_g.reshape(1, d)
    grid_spec = pltpu.PrefetchScalarGridSpec(
        num_scalar_prefetch=3,
        grid=(t // tc,),
        in_specs=[pl.BlockSpec((tc, d), lambda i, *_: (i, 0)),
                  pl.BlockSpec((tc, ROUTER_LANES), lambda i, *_: (i, 0)),
                  pl.BlockSpec((1, 1, mod3.shape[2]), lambda i, *_: (i // per_batch, 0, 0)),
                  pl.BlockSpec(gf.shape, lambda i, *_: (0, 0)),
                  pl.BlockSpec(memory_space=pl.ANY)],
        out_specs=pl.BlockSpec((tc, d), lambda i, *_: (i, 0)),
        scratch_shapes=[pltpu.VMEM((2, PERM_ROWS, d), BF16), pltpu.SemaphoreType.DMA((2,))],
    )
    return pl.pallas_call(
        functools.partial(_combine_kernel, final_norm),
        grid_spec=grid_spec,
        out_shape=jax.ShapeDtypeStruct((t, d), F32),
        compiler_params=_params(("arbitrary",)),
        name="combine",
    )(plan["run_len"], plan["run_loc"], plan["run_glob"], x1, meta, mod3, gf, y)


def _routing_plan(stats, n_tokens):
    rb = MOE_BLOCK_ROWS
    run_len = stats[:, STAT_LEN, :N_EXPERTS].astype(jnp.int32)
    run_loc = stats[:, STAT_START, :N_EXPERTS].astype(jnp.int32)
    n_tiles = run_len.shape[0]
    total = jnp.sum(run_len, axis=0)
    padded = ((total + rb - 1) // rb) * rb
    pad_end = jnp.cumsum(padded)
    pad_start = pad_end - padded
    run_glob = pad_start[None, :] + jnp.cumsum(run_len, axis=0) - run_len
    n_rows = n_tokens * TOP_K_EXPERTS + n_tiles * N_EXPERTS * (RUN_ALIGN - 1) + N_EXPERTS * (rb - 1)
    n_rows = ((n_rows + rb - 1) // rb) * rb
    block_row = jnp.arange(n_rows // rb, dtype=jnp.int32) * rb
    block_e = jnp.minimum(jnp.sum((pad_end[None, :] <= block_row[:, None]).astype(jnp.int32), axis=1),
                          N_EXPERTS - 1)
    return dict(run_len=run_len.reshape(-1), run_loc=run_loc.reshape(-1), run_glob=run_glob.reshape(-1),
                tail_start=pad_start + total, tail_len=padded - total,
                n_used=(pad_end[-1:] // rb).astype(jnp.int32), block_e=block_e.astype(jnp.int32), n_rows=n_rows)


def kernel(x, c, positions, w_ada, b_ada, norm1_g, w_in, b_gate, conv_w, conv_b, attn_out_g, mlstm_out_g,
           w_out, norm2_g, w_rg, b_rg, w_re, b_re, w1, w3, w2, norm_f_g):
    b, s, d = x.shape
    depth = w_ada.shape[0]
    assert d == D_MODEL and s % MOE_TILE == 0 and s % (KV_CHUNK * MOBA_BLOCK) == 0
    x2d = x.reshape(b * s, d)
    for l in range(depth):
        mod3 = _mod(c, w_ada[l], b_ada[l]).reshape(b, 1, 6 * d)
        q, k, vt, qkm, vm, om, gates = _inproj(x2d.reshape(b, s, d), positions, mod3,
                                              norm1_g[l].reshape(1, d), w_in[l])
        attn = _moba(q, k, vt, attn_out_g[l].reshape(1, D_ATT))
        hm = _mlstm(qkm, vm, om, gates, conv_w[l], conv_b[l], b_gate[l], mlstm_out_g[l])
        x1, h2, meta, stats = _outproj(attn.reshape(b * s, D_ATT), hm.reshape(b * s, D_MLSTM), x2d, mod3,
                                       norm2_g[l].reshape(1, d), w_out[l], w_rg[l], b_rg[l], w_re[l],
                                       b_re[l], s)
        plan = _routing_plan(stats, b * s)
        xs = _dispatch(plan, meta, h2)
        y = _experts(plan["block_e"], plan["n_used"], xs, w1[l], w3[l], w2[l])
        x2d = _combine(plan, x1, meta, mod3, norm_f_g, y, s, final_norm=(l == depth - 1))
    return x2d.reshape(b, s, d)
```

```python
import functools

import jax
import jax.numpy as jnp
from jax import lax
from jax.experimental import pallas as pl
from jax.experimental.pallas import tpu as pltpu

F32 = jnp.float32
BF16 = jnp.bfloat16

D_MODEL = 1024
D_ATT = 512
ATT_HEADS = 8
ATT_HEAD_DIM = 64
D_MLSTM = 512
MLSTM_HEADS = 4
MLSTM_HEAD_DIM = 128
MOBA_BLOCK = 256
MOBA_TOPK = 3
ROPE_THETA = 10000.0
MLSTM_CHUNK = 128
CONV_WIDTH = 4
N_GROUPS = 4
EXPERTS_PER_GROUP = 8
N_EXPERTS = N_GROUPS * EXPERTS_PER_GROUP
TOP_K_EXPERTS = 2
D_EXPERT = 512
MOE_BLOCK_ROWS = 256
NORM_EPS = 1e-6
NEG_INF = -1e30
LOG2_E = 1.4426950408889634
KV_CHUNK = 4
MOBA_GROUP_LANES = 256

LANES = 128
SUBLANES = 8
VMEM_LIMIT_BYTES = 56 * 1024 * 1024

MOE_TILE = 512
RUN_ALIGN = 2 * SUBLANES
PERM_ROWS = TOP_K_EXPERTS * MOE_TILE + N_EXPERTS * RUN_ALIGN
RUN_PIECES = tuple(RUN_ALIGN << p for p in reversed(range(6)))
TAIL_PIECES = tuple(RUN_ALIGN << p for p in reversed(range(4)))
TILE_PIECES = tuple(RUN_ALIGN << p for p in reversed(range(7)))
ZERO_ROWS = TAIL_PIECES[0]
GATE_LANES = 2 * LANES
ROUTER_LANES = LANES
GROUP_LANE0 = N_EXPERTS


def _dot(a, b):
    return jnp.dot(a, b, preferred_element_type=F32)


def _dot_nt(a, b):
    return lax.dot_general(a, b, (((1,), (1,)), ((), ())), preferred_element_type=F32)


def _dot_tn(a, b):
    return lax.dot_general(a, b, (((0,), (0,)), ((), ())), preferred_element_type=F32)


def _split3(x):
    a = x.astype(BF16)
    r = x - a.astype(F32)
    b = r.astype(BF16)
    c = (r - b.astype(F32)).astype(BF16)
    return a, b, c


def _silu(x):
    return x * jax.nn.sigmoid(x)


def _log_sigmoid(x):
    return jnp.minimum(x, 0.0) - jnp.log1p(jnp.exp(-jnp.abs(x)))


def _params(semantics, vmem=VMEM_LIMIT_BYTES):
    return pltpu.CompilerParams(dimension_semantics=semantics, vmem_limit_bytes=vmem)


def _mod_kernel(c_ref, w_ref, b_ref, o_ref):
    sc = _silu(c_ref[...])
    o_ref[...] = jnp.dot(sc, w_ref[...], precision=lax.Precision.HIGHEST,
                         preferred_element_type=F32) + b_ref[...]


def _mod(c, w_ada, b_ada):
    b, d = c.shape
    n = w_ada.shape[1]
    tn = D_MODEL
    return pl.pallas_call(
        _mod_kernel,
        grid=(n // tn,),
        in_specs=[pl.BlockSpec((b, d), lambda i: (0, 0)),
                  pl.BlockSpec((d, tn), lambda i: (0, i)),
                  pl.BlockSpec((1, tn), lambda i: (0, i))],
        out_specs=pl.BlockSpec((b, tn), lambda i: (0, i)),
        out_shape=jax.ShapeDtypeStruct((b, n), F32),
        compiler_params=_params(("arbitrary",)),
        name="mod",
    )(c, w_ada, b_ada.reshape(1, n))


def _rms_modulate(x, g, shift, scale):
    y = x * lax.rsqrt(jnp.mean(x * x, axis=-1, keepdims=True) + NORM_EPS)
    return (y * g) * (1.0 + scale) + shift


def _rope(t, cos, sin, first_half):
    outs = []
    for c in range(t.shape[1] // LANES):
        tc = t[:, c * LANES:(c + 1) * LANES]
        rot = jnp.where(first_half, -pltpu.roll(tc, LANES - ATT_HEAD_DIM // 2, 1),
                        pltpu.roll(tc, ATT_HEAD_DIM // 2, 1))
        outs.append(tc * cos + rot * sin)
    return jnp.concatenate(outs, axis=1)


def _inproj_kernel(x_ref, pos_ref, mod_ref, g_ref, invf_ref, wq_ref, wk_ref, wvt_ref, wqk_ref,
                   wv_ref, wo_ref, wg_ref, q_ref, k_ref, vt_ref, qkm_ref, vm_ref, om_ref, gt_ref):
    d = x_ref.shape[2]
    x = x_ref[0]
    mod = mod_ref[0]
    h = _rms_modulate(x, g_ref[...], mod[:, 0:d], mod[:, d:2 * d])
    hb = h.astype(BF16)

    ang = pos_ref[0].astype(F32) * invf_ref[...]
    cos = jnp.cos(ang)
    sin = jnp.sin(ang)
    lane = lax.broadcasted_iota(jnp.int32, cos.shape, 1)
    first_half = (lane & (ATT_HEAD_DIM // 2)) == 0

    q = _rope(_dot(hb, wq_ref[...]), cos, sin, first_half)
    q_ref[0] = (q * (ATT_HEAD_DIM ** -0.5 * LOG2_E)).astype(BF16)
    k_ref[0] = _rope(_dot(hb, wk_ref[...]), cos, sin, first_half).astype(BF16)
    vt_ref[0, 0] = _dot_nt(wvt_ref[...], hb).astype(BF16)
    qkm_ref[0] = _dot(hb, wqk_ref[...])
    vm_ref[0] = _dot(hb, wv_ref[...])
    om_ref[0] = _dot(hb, wo_ref[...])
    gt_ref[0] = _dot(hb, wg_ref[...])


def _inproj(x, positions, mod3, norm_g, w_in):
    b, s, d = x.shape
    tm = MOBA_BLOCK
    nb = s // tm
    o = [0, D_ATT, 2 * D_ATT, 3 * D_ATT, 3 * D_ATT + 2 * D_MLSTM, 3 * D_ATT + 3 * D_MLSTM,
         3 * D_ATT + 4 * D_MLSTM, 3 * D_ATT + 4 * D_MLSTM + 2 * MLSTM_HEADS]
    wb = w_in.astype(BF16)
    wq, wk, wv_a, wqk, wv, wo, wg = (wb[:, o[i]:o[i + 1]] for i in range(7))
    wvt = wv_a.T
    lane_pad = ((0, 0), (0, LANES - MLSTM_HEADS))
    wg = jnp.concatenate([jnp.pad(wg[:, :MLSTM_HEADS], lane_pad), jnp.pad(wg[:, MLSTM_HEADS:], lane_pad)], axis=1)
    half = ATT_HEAD_DIM // 2
    inv_freq = ROPE_THETA ** (-jnp.arange(half, dtype=F32) / half)
    invf = jnp.tile(inv_freq, LANES // half).reshape(1, LANES)

    full = lambda a: pl.BlockSpec(a.shape, lambda bi, i: (0,) * a.ndim)
    tok = lambda w: pl.BlockSpec((1, tm, w), lambda bi, i: (bi, i, 0))
    out_shape = [jax.ShapeDtypeStruct((b, s, D_ATT), BF16),
                 jax.ShapeDtypeStruct((b, s, D_ATT), BF16),
                 jax.ShapeDtypeStruct((b, nb, D_ATT, tm), BF16),
                 jax.ShapeDtypeStruct((b, s, 2 * D_MLSTM), F32),
                 jax.ShapeDtypeStruct((b, s, D_MLSTM), F32),
                 jax.ShapeDtypeStruct((b, s, D_MLSTM), F32),
                 jax.ShapeDtypeStruct((b, s, GATE_LANES), F32)]
    return pl.pallas_call(
        _inproj_kernel,
        grid=(b, nb),
        in_specs=[tok(d), tok(1),
                  pl.BlockSpec((1, 1, mod3.shape[2]), lambda bi, i: (bi, 0, 0)),
                  full(norm_g), full(invf), full(wq), full(wk), full(wvt), full(wqk), full(wv),
                  full(wo), full(wg)],
        out_specs=[tok(D_ATT), tok(D_ATT),
                   pl.BlockSpec((1, 1, D_ATT, tm), lambda bi, i: (bi, i, 0, 0)),
                   tok(2 * D_MLSTM), tok(D_MLSTM), tok(D_MLSTM), tok(GATE_LANES)],
        out_shape=out_shape,
        compiler_params=_params(("arbitrary", "arbitrary")),
        name="inproj",
    )(x, positions.reshape(b, s, 1), mod3, norm_g, invf, wq, wk, wvt, wqk, wv, wo, wg)


def _moba_kernel(q_ref, k_ref, vt_ref, g_ref, o_ref, kmean_ref, sel_ref, st_ref):
    blk = MOBA_BLOCK
    nb = k_ref.shape[1] // blk
    j = pl.program_id(2)
    heads = q_ref.shape[2] // ATT_HEAD_DIM
    hpl = LANES // ATT_HEAD_DIM

    @pl.when(j == 0)
    def _():
        for n in range(nb):
            kb = k_ref[0, n * blk:(n + 1) * blk, :].astype(F32)
            kmean_ref[n:n + 1, :] = jnp.mean(kb, axis=0, keepdims=True)

    lane = lax.broadcasted_iota(jnp.int32, (blk, LANES), 1)
    blk_id = lax.broadcasted_iota(jnp.int32, (nb, blk), 0)
    km_hi = kmean_ref[...].astype(BF16)
    km_lo = (kmean_ref[...] - km_hi.astype(F32)).astype(BF16)
    past = blk_id < j
    tile = lambda a, hh: a[:, (hh // hpl) * LANES:(hh // hpl + 1) * LANES]

    qh = []
    for hh in range(heads):
        q = tile(q_ref[0], hh)
        in_head = (lane >= (hh % hpl) * ATT_HEAD_DIM) & (lane < (hh % hpl + 1) * ATT_HEAD_DIM)
        qh.append(jnp.where(in_head, q, jnp.zeros_like(q)))
        gate = _dot_nt(tile(km_hi, hh), qh[hh]) + _dot_nt(tile(km_lo, hh), qh[hh])
        g = jnp.where(past, gate, NEG_INF)
        sel = jnp.zeros(g.shape, F32)
        for _ in range(min(MOBA_TOPK, nb)):
            top = jnp.max(g, axis=0, keepdims=True)
            idx = jnp.min(jnp.where(g == top, blk_id, nb), axis=0, keepdims=True)
            pick = blk_id == idx
            sel = jnp.where(pick, 1.0, sel)
            g = jnp.where(pick, -jnp.inf, g)
        sel_ref[hh, 0:nb, :] = jnp.where((sel > 0.0) & past, 1.0, 0.0)
        sel_ref[hh, nb:nb + SUBLANES, :] = jnp.zeros((SUBLANES, blk), F32)

    kpos = lax.broadcasted_iota(jnp.int32, (blk, blk), 0)
    qpos = lax.broadcasted_iota(jnp.int32, (blk, blk), 1)
    causal_bias = jnp.where(kpos <= qpos, 0.0, NEG_INF)

    def slab_block(t, i):
        b_i = j - KV_CHUNK * t - (KV_CHUNK - 1 - i)
        return jnp.maximum(b_i, 0), jnp.where(b_i >= 0, b_i, nb)

    def score(t, slot, own_chunk):
        tops = [None] * heads
        slab_max = [[None] * KV_CHUNK for _ in range(heads)]
        for i in range(KV_CHUNK):
            src, row = slab_block(t, i)
            kb = k_ref[0, pl.ds(pl.multiple_of(src * blk, blk), blk), :]
            for hh in range(heads):
                st = _dot_nt(tile(kb, hh), qh[hh])
                if own_chunk and i == KV_CHUNK - 1:
                    st = st + causal_bias
                    cmax = jnp.max(st, axis=0, keepdims=True)
                    smax = cmax
                else:
                    cmax = jnp.max(st, axis=0, keepdims=True)
                    smax = cmax + (sel_ref[hh, pl.ds(row, 1), :] - 1.0) * (-NEG_INF)
                st_ref[slot, hh, i * blk:(i + 1) * blk, :] = st
                slab_max[hh][i] = cmax
                tops[hh] = smax if tops[hh] is None else jnp.maximum(tops[hh], smax)
        return tuple(tops), tuple(tuple(r) for r in slab_max)

    def accumulate(t, slot, state, maxes):
        tops, slab_max = maxes
        new = []
        for hh in range(heads):
            m, l, acc = state[3 * hh:3 * hh + 3]
            m_new = jnp.maximum(m, tops[hh])
            alpha = jnp.exp2(m - m_new)
            l = alpha * l
            acc = alpha * acc
            for i in range(KV_CHUNK):
                src, row = slab_block(t, i)
                keep = sel_ref[hh, pl.ds(row, 1), :]
                if i == KV_CHUNK - 1:
                    keep = jnp.where(t == 0, 1.0, keep)
                p = jnp.exp2(st_ref[slot, hh, i * blk:(i + 1) * blk, :] - jnp.maximum(m_new, slab_max[hh][i]))
                l = l + keep * jnp.sum(p, axis=0, keepdims=True)
                v_h = vt_ref[0, src][hh * ATT_HEAD_DIM:(hh + 1) * ATT_HEAD_DIM, :]
                acc = acc + keep * _dot(v_h, p.astype(BF16))
            new += [m_new, l, acc]
        return tuple(new)

    state = ()
    for hh in range(heads):
        state += (jnp.full((1, blk), NEG_INF, F32), jnp.zeros((1, blk), F32), jnp.zeros((ATT_HEAD_DIM, blk), F32))
    last = j // KV_CHUNK

    def stage_pair(t, slot, state, maxes):
        nxt = score(t + 1, 1 - slot, False)
        return accumulate(t, slot, state, maxes), nxt

    def body(t, carry):
        return lax.cond(t % 2 == 0, functools.partial(stage_pair, t, 0), functools.partial(stage_pair, t, 1),
                        *carry)

    state, maxes = lax.fori_loop(0, last, body, (state, score(0, 0, True)))
    state = lax.cond(last % 2 == 0, functools.partial(accumulate, last, 0), functools.partial(accumulate, last, 1),
                     state, maxes)

    outs = []
    for hh in range(heads):
        o_h = state[3 * hh + 2] / state[3 * hh + 1]
        ms = jnp.mean(o_h * o_h, axis=0, keepdims=True)
        outs.append(o_h * lax.rsqrt(ms + NORM_EPS))
    o_ref[0] = jnp.concatenate(outs, axis=0).T * g_ref[...]


def _moba(q, k, vt, g_att):
    b, s, w = q.shape
    blk = MOBA_BLOCK
    nb = s // blk
    gl = MOBA_GROUP_LANES
    groups = w // gl
    heads = gl // ATT_HEAD_DIM
    return pl.pallas_call(
        _moba_kernel,
        grid=(b, groups, nb),
        in_specs=[pl.BlockSpec((1, blk, gl), lambda bi, p, j: (bi, j, p)),
                  pl.BlockSpec((1, s, gl), lambda bi, p, j: (bi, 0, p)),
                  pl.BlockSpec((1, nb, gl, blk), lambda bi, p, j: (bi, 0, p, 0)),
                  pl.BlockSpec((1, gl), lambda bi, p, j: (0, p))],
        out_specs=pl.BlockSpec((1, blk, gl), lambda bi, p, j: (bi, j, p)),
        out_shape=jax.ShapeDtypeStruct((b, s, w), F32),
        scratch_shapes=[pltpu.VMEM((nb, gl), F32),
                        pltpu.VMEM((heads, nb + SUBLANES, blk), F32),
                        pltpu.VMEM((2, heads, KV_CHUNK * blk, blk), F32)],
        compiler_params=_params(("arbitrary", "arbitrary", "arbitrary")),
        name="moba",
    )(q, k, vt, g_att)


def _mlstm_kernel(qkm_ref, vm_ref, om_ref, gt_ref, cw_ref, cb_ref, bg_ref, gm_ref, bsel_ref, o_ref,
                  uext_ref, c_ref, n_ref, m_ref):
    L = MLSTM_CHUNK
    dh = MLSTM_HEAD_DIM
    nh = MLSTM_HEADS
    assert L == dh == LANES and 2 * nh == SUBLANES
    c = pl.program_id(1)

    @pl.when(c == 0)
    def _():
        uext_ref[0:SUBLANES, :] = jnp.zeros((SUBLANES, uext_ref.shape[1]), F32)
        c_ref[...] = jnp.zeros(c_ref.shape, F32)
        n_ref[...] = jnp.zeros(n_ref.shape, F32)
        m_ref[...] = jnp.zeros(m_ref.shape, F32)

    u = qkm_ref[0]
    uext_ref[SUBLANES:SUBLANES + L, :] = u
    cw = cw_ref[...]
    conv = cb_ref[...] + cw[CONV_WIDTH - 1:CONV_WIDTH, :] * u
    for dlt in range(1, CONV_WIDTH):
        conv = conv + cw[CONV_WIDTH - 1 - dlt:CONV_WIDTH - dlt, :] * uext_ref[SUBLANES - dlt:SUBLANES - dlt + L, :]
    uext_ref[0:SUBLANES, :] = u[L - SUBLANES:L, :]
    act = _silu(conv)

    gates = gt_ref[0] + bg_ref[...]
    ig = gates[:, 0:LANES]
    tpos = lax.broadcasted_iota(jnp.int32, (L, L), 0)
    spos = lax.broadcasted_iota(jnp.int32, (L, L), 1)
    causal = spos <= tpos
    tril = jnp.where(causal, 1.0, 0.0).astype(BF16)
    cum = sum(_dot(tril, part) for part in _split3(_log_sigmoid(gates[:, LANES:])))
    a_tot = cum[L - 1:L, :]
    m_prev = m_ref[0:1, :]
    resid = ig - cum
    pmax = resid
    shift = 1
    while shift < L:
        pmax = jnp.maximum(pmax, jnp.where(tpos >= shift, pltpu.roll(pmax, shift, 0), -jnp.inf))
        shift *= 2
    inter = cum + m_prev
    m_t = jnp.maximum(inter, cum + pmax)
    g_loc = a_tot - cum + ig
    m_loc = jnp.max(g_loc, axis=0, keepdims=True)
    w_loc = jnp.exp(g_loc - m_loc)
    head_lane = spos < nh
    packed = jnp.where(head_lane, cum - m_t, 0.0)
    for i, qty in enumerate((inter - m_t, -m_t, g_loc - m_loc), start=1):
        packed = packed + pltpu.roll(jnp.where(head_lane, qty, 0.0), SUBLANES * i, 1)
    per_t_cols = jnp.concatenate(_split3(packed), axis=1)
    per_t_rows = (jnp.where(head_lane, resid, 0.0) + pltpu.roll(jnp.where(head_lane, w_loc, 0.0), SUBLANES, 1)).T

    m_new = jnp.maximum(a_tot + m_prev, m_loc)
    s_prev = jnp.exp(a_tot + m_prev - m_new)
    s_loc = jnp.exp(m_loc - m_new)
    m_ref[0:1, :] = m_new

    vm = vm_ref[0]
    om = om_ref[0]
    ones_b = jnp.ones((L, dh), BF16)
    c_all = [c_ref[h] for h in range(nh)]
    n_all = n_ref[...]
    new_state = []
    for h in range(nh):
        sl = slice(h * dh, (h + 1) * dh)
        q = act[:, sl]
        k = act[:, nh * dh + h * dh:nh * dh + (h + 1) * dh] * (dh ** -0.5)
        v = vm[:, sl]
        qb, kb, vb = q.astype(BF16), k.astype(BF16), v.astype(BF16)
        c_prev = c_all[h]
        n_prev = n_all[h:h + 1, :]
        bcast = _dot(per_t_cols, bsel_ref[h])
        decay_b = bcast[:, 0:dh]
        w_inter_b, floor_b, w_loc_b = (jnp.exp(bcast[:, i * dh:(i + 1) * dh]) for i in range(1, 4))

        s_qk = _dot_nt(qb, kb) * jnp.where(causal, jnp.exp(decay_b + per_t_rows[h:h + 1, :]), 0.0)
        intra = _dot(s_qk.astype(BF16), jnp.concatenate([vb, ones_b], axis=1))
        state_rows = jnp.concatenate([c_prev, jnp.broadcast_to(n_prev, (dh, dh))], axis=0).astype(BF16)
        carried = _dot_nt(qb, state_rows)
        num = intra[:, :dh] + w_inter_b * carried[:, :dh]
        den = intra[:, dh:] + w_inter_b * carried[:, dh:]
        hout = num / jnp.maximum(jnp.abs(den), floor_b)

        mean_sq = _dot((hout * hout).astype(BF16), ones_b) * (1.0 / dh)
        hn = hout * lax.rsqrt(mean_sq + NORM_EPS) * gm_ref[:, sl]
        o_ref[0, :, sl] = hn * jax.nn.sigmoid(om[:, sl])

        c_loc = _dot_tn((v * w_loc_b).astype(BF16), kb)
        w_row = per_t_rows[SUBLANES + h:SUBLANES + h + 1, :]
        n_loc = _dot(jnp.broadcast_to(w_row, (SUBLANES, L)).astype(BF16), kb)[0:1, :]
        new_state.append((s_prev[:, h:h + 1] * c_prev + s_loc[:, h:h + 1] * c_loc,
                          s_prev[:, h:h + 1] * n_prev + s_loc[:, h:h + 1] * n_loc))
    for h, (c_new, n_new) in enumerate(new_state):
        c_ref[h] = c_new
        n_ref[h:h + 1, :] = n_new


def _mlstm(qkm, vm, om, gates, conv_w, conv_b, b_gate, g_m):
    b, s, _ = qkm.shape
    L = MLSTM_CHUNK
    nc = s // L
    lane_pad = ((0, 0), (0, LANES - MLSTM_HEADS))
    bg = b_gate.reshape(1, -1)
    bg = jnp.concatenate([jnp.pad(bg[:, :MLSTM_HEADS], lane_pad), jnp.pad(bg[:, MLSTM_HEADS:], lane_pad)], axis=1)
    tok = lambda w: pl.BlockSpec((1, L, w), lambda bi, c: (bi, c, 0))
    full = lambda a: pl.BlockSpec(a.shape, lambda bi, c: (0,) * a.ndim)
    cb = conv_b.reshape(1, -1)
    gm = g_m.reshape(1, -1)
    n_qty = 4
    row = jnp.arange(3 * LANES, dtype=jnp.int32) % LANES
    col_qty = jnp.arange(n_qty * MLSTM_HEAD_DIM, dtype=jnp.int32) // MLSTM_HEAD_DIM
    bsel = jnp.stack([
        ((row // SUBLANES)[:, None] == col_qty[None, :]) & (row % SUBLANES == h)[:, None]
        for h in range(MLSTM_HEADS)]).astype(BF16)
    return pl.pallas_call(
        _mlstm_kernel,
        grid=(b, nc),
        in_specs=[tok(2 * D_MLSTM), tok(D_MLSTM), tok(D_MLSTM), tok(GATE_LANES),
                  full(conv_w), full(cb), full(bg), full(gm), full(bsel)],
        out_specs=tok(D_MLSTM),
        out_shape=jax.ShapeDtypeStruct((b, s, D_MLSTM), F32),
        scratch_shapes=[pltpu.VMEM((SUBLANES + L, 2 * D_MLSTM), F32),
                        pltpu.VMEM((MLSTM_HEADS, MLSTM_HEAD_DIM, MLSTM_HEAD_DIM), F32),
                        pltpu.VMEM((SUBLANES, MLSTM_HEAD_DIM), F32),
                        pltpu.VMEM((SUBLANES, LANES), F32)],
        compiler_params=_params(("arbitrary", "arbitrary")),
        name="mlstm",
    )(qkm, vm, om, gates, conv_w, cb, bg, gm, bsel)


META_E0, META_E1, META_W0, META_W1, META_L0, META_L1 = range(6)
STAT_LEN, STAT_START = 0, 1


def _outproj_kernel(attn_ref, hm_ref, x_ref, mod_ref, g_ref, wout_ref, wr_hi_ref, wr_lo_ref, br_ref,
                    x1_ref, h2_ref, meta_ref, stat_ref):
    d = x_ref.shape[1]
    tm = x_ref.shape[0]

    mod = mod_ref[0]
    y = (_dot(attn_ref[...].astype(BF16), wout_ref[0:D_ATT, :])
         + _dot(hm_ref[...].astype(BF16), wout_ref[D_ATT:, :]))
    x1 = x_ref[...] + mod[:, 2 * d:3 * d] * y
    x1_ref[...] = x1
    h2 = _rms_modulate(x1, g_ref[...], mod[:, 3 * d:4 * d], mod[:, 4 * d:5 * d])
    h2_ref[...] = h2.astype(BF16)

    h_hi = h2.astype(BF16)
    h_lo = (h2 - h_hi.astype(F32)).astype(BF16)
    logit = (_dot(h_hi, wr_hi_ref[...]) + _dot(h_lo, wr_hi_ref[...]) + _dot(h_hi, wr_lo_ref[...])
             + br_ref[...])
    lane = lax.broadcasted_iota(jnp.int32, logit.shape, 1)
    big = jnp.int32(ROUTER_LANES)

    is_g = (lane >= GROUP_LANE0) & (lane < GROUP_LANE0 + N_GROUPS)
    gmax = jnp.max(jnp.where(is_g, logit, -jnp.inf), axis=1, keepdims=True)
    gsum = jnp.sum(jnp.where(is_g, jnp.exp(logit - gmax), 0.0), axis=1, keepdims=True)
    g_w = 1.0 / gsum
    g_idx = jnp.min(jnp.where(is_g & (logit == gmax), lane, big), axis=1, keepdims=True) - GROUP_LANE0

    in_grp = (lane < N_EXPERTS) & ((lane // EXPERTS_PER_GROUP) == g_idx)
    emax = jnp.max(jnp.where(in_grp, logit, -jnp.inf), axis=1, keepdims=True)
    esum = jnp.sum(jnp.where(in_grp, jnp.exp(logit - emax), 0.0), axis=1, keepdims=True)
    e0 = jnp.min(jnp.where(in_grp & (logit == emax), lane, big), axis=1, keepdims=True)
    rest = in_grp & (lane != e0)
    e2max = jnp.max(jnp.where(rest, logit, -jnp.inf), axis=1, keepdims=True)
    e1 = jnp.min(jnp.where(rest & (logit == e2max), lane, big), axis=1, keepdims=True)
    p0 = 1.0 / esum
    p1 = jnp.exp(e2max - emax) / esum
    w0 = g_w * p0 / (p0 + p1)
    w1 = g_w * p1 / (p0 + p1)

    memb = jnp.where((lane == e0) | (lane == e1), 1.0, 0.0)
    tpos = lax.broadcasted_iota(jnp.int32, (tm, tm), 0)
    spos = lax.broadcasted_iota(jnp.int32, (tm, tm), 1)
    before = jnp.where(spos < tpos, 1.0, 0.0).astype(BF16)
    earlier = _dot(before, memb.astype(BF16))
    count = jnp.sum(memb, axis=0, keepdims=True).astype(jnp.int32)
    run_len = (((count + (RUN_ALIGN - 1)) // RUN_ALIGN) * RUN_ALIGN).astype(F32)
    epos = lax.broadcasted_iota(jnp.int32, (ROUTER_LANES, ROUTER_LANES), 0)
    fpos = lax.broadcasted_iota(jnp.int32, (ROUTER_LANES, ROUTER_LANES), 1)
    lower_e = jnp.where(epos < fpos, 1.0, 0.0).astype(BF16)
    run_start = _dot(jnp.broadcast_to(run_len, (SUBLANES, ROUTER_LANES)).astype(BF16), lower_e)[0:1, :]
    row = run_start + earlier
    l0 = jnp.sum(jnp.where(lane == e0, row, 0.0), axis=1, keepdims=True)
    l1 = jnp.sum(jnp.where(lane == e1, row, 0.0), axis=1, keepdims=True)

    meta = jnp.zeros(logit.shape, F32)
    for slot, val in ((META_E0, e0.astype(F32)), (META_E1, e1.astype(F32)), (META_W0, w0), (META_W1, w1),
                      (META_L0, l0), (META_L1, l1)):
        meta = jnp.where(lane == slot, val, meta)
    meta_ref[...] = meta
    srow = lax.broadcasted_iota(jnp.int32, (SUBLANES, ROUTER_LANES), 0)
    stat_ref[0] = jnp.where(srow == STAT_LEN, run_len, jnp.where(srow == STAT_START, run_start, 0.0))


def _outproj(attn, hm, x2d, mod3, norm_g, w_out, w_rg, b_rg, w_re, b_re, seq):
    t, d = x2d.shape
    tm = MOE_TILE
    per_batch = seq // tm
    wr = jnp.pad(jnp.concatenate([w_re, w_rg], axis=1), ((0, 0), (0, ROUTER_LANES - N_EXPERTS - N_GROUPS)))
    br = jnp.pad(jnp.concatenate([b_re, b_rg]).reshape(1, -1), ((0, 0), (0, ROUTER_LANES - N_EXPERTS - N_GROUPS)))
    wr_hi = wr.astype(BF16)
    wr_lo = (wr - wr_hi.astype(F32)).astype(BF16)
    wout = w_out.astype(BF16)
    tok = lambda w: pl.BlockSpec((tm, w), lambda i: (i, 0))
    full = lambda a: pl.BlockSpec(a.shape, lambda i: (0,) * a.ndim)
    return pl.pallas_call(
        _outproj_kernel,
        grid=(t // tm,),
        in_specs=[tok(D_ATT), tok(D_MLSTM), tok(d),
                  pl.BlockSpec((1, 1, mod3.shape[2]), lambda i: (i // per_batch, 0, 0)),
                  full(norm_g), full(wout), full(wr_hi), full(wr_lo), full(br)],
        out_specs=[tok(d), tok(d), tok(ROUTER_LANES),
                   pl.BlockSpec((1, SUBLANES, ROUTER_LANES), lambda i: (i, 0, 0))],
        out_shape=[jax.ShapeDtypeStruct((t, d), F32), jax.ShapeDtypeStruct((t, d), BF16),
                   jax.ShapeDtypeStruct((t, ROUTER_LANES), F32),
                   jax.ShapeDtypeStruct((t // tm, SUBLANES, ROUTER_LANES), F32)],
        compiler_params=_params(("arbitrary",)),
        name="outproj",
    )(attn, hm, x2d, mod3, norm_g, wout, wr_hi, wr_lo, br)


def _for_each_piece(length, pieces, fn):
    pos = jnp.int32(0)
    for size in pieces:
        take = length & size

        @pl.when(take != 0)
        def _(pos=pos, size=size):
            fn(pl.multiple_of(pos, RUN_ALIGN), size)

        pos = pos + take


def _for_each_run(tile, len_ref, loc_ref, glob_ref, fn):
    def per_expert(e, carry):
        idx = tile * N_EXPERTS + e
        loc = loc_ref[idx]
        glob = glob_ref[idx]
        _for_each_piece(len_ref[idx], RUN_PIECES,
                        lambda pos, size: fn(pl.multiple_of(loc + pos, RUN_ALIGN),
                                             pl.multiple_of(glob + pos, RUN_ALIGN), size))
        return carry

    lax.fori_loop(0, N_EXPERTS, per_expert, 0)


def _dispatch_kernel(len_ref, loc_ref, glob_ref, rows_ref, tail_ref, tlen_ref, nu_ref, meta_ref, h2_ref, xs_ref,
                     perm_ref, zero_ref, sem, zsem):
    i = pl.program_id(0)
    rb = MOE_BLOCK_ROWS
    n_blocks = xs_ref.shape[0] // rb

    @pl.when(i == 0)
    def _():
        zero_ref[...] = jnp.zeros(zero_ref.shape, zero_ref.dtype)

        def zero_copy(row, size):
            return pltpu.make_async_copy(zero_ref.at[pl.ds(0, size)], xs_ref.at[pl.ds(row, size)], zsem)

        def fill(op):
            def per_expert(e, carry):
                start = tail_ref[e]
                _for_each_piece(tlen_ref[e], TAIL_PIECES,
                                lambda pos, size: op(zero_copy(pl.multiple_of(start + pos, RUN_ALIGN), size)))
                return carry

            lax.fori_loop(0, N_EXPERTS, per_expert, 0)

            def per_block(blk, carry):
                for part in range(rb // ZERO_ROWS):
                    op(zero_copy(pl.multiple_of(blk * rb + part * ZERO_ROWS, ZERO_ROWS), ZERO_ROWS))
                return carry

            lax.fori_loop(nu_ref[0], n_blocks, per_block, 0)

        fill(lambda cp: cp.start())
        fill(lambda cp: cp.wait())

    meta_t = meta_ref[...].T
    l0 = meta_t[META_L0:META_L0 + 1, :].astype(jnp.int32)
    l1 = meta_t[META_L1:META_L1 + 1, :].astype(jnp.int32)
    rpos = lax.broadcasted_iota(jnp.int32, (perm_ref.shape[1], meta_ref.shape[0]), 0)
    onehot = jnp.where((rpos == l0) | (rpos == l1), 1.0, 0.0).astype(BF16)
    slot = i % 2
    perm_ref[slot] = _dot(onehot, h2_ref[...]).astype(BF16)

    def run_copy(s, loc, glob, size):
        return pltpu.make_async_copy(perm_ref.at[s, pl.ds(loc, size)], xs_ref.at[pl.ds(glob, size)], sem.at[s])

    def wait_tile(tile, s):
        _for_each_piece(rows_ref[tile], TILE_PIECES, lambda pos, size: run_copy(s, 0, 0, size).wait())

    @pl.when(i > 0)
    def _():
        wait_tile(i - 1, 1 - slot)

    _for_each_run(i, len_ref, loc_ref, glob_ref, lambda loc, glob, size: run_copy(slot, loc, glob, size).start())

    @pl.when(i == pl.num_programs(0) - 1)
    def _():
        wait_tile(i, slot)


def _dispatch(plan, meta, h2):
    t, d = h2.shape
    tm = MOE_TILE
    grid_spec = pltpu.PrefetchScalarGridSpec(
        num_scalar_prefetch=7,
        grid=(t // tm,),
        in_specs=[pl.BlockSpec((tm, ROUTER_LANES), lambda i, *_: (i, 0)),
                  pl.BlockSpec((tm, d), lambda i, *_: (i, 0))],
        out_specs=pl.BlockSpec(memory_space=pl.ANY),
        scratch_shapes=[pltpu.VMEM((2, PERM_ROWS, d), BF16), pltpu.VMEM((ZERO_ROWS, d), BF16),
                        pltpu.SemaphoreType.DMA((2,)), pltpu.SemaphoreType.DMA(())],
    )
    return pl.pallas_call(
        _dispatch_kernel,
        grid_spec=grid_spec,
        out_shape=jax.ShapeDtypeStruct((plan["n_rows"], d), BF16),
        compiler_params=_params(("arbitrary",)),
        name="dispatch",
    )(plan["run_len"], plan["run_loc"], plan["run_glob"], plan["tile_rows"], plan["tail_start"], plan["tail_len"],
      plan["n_used"], meta, h2)


def _expert_kernel(be_ref, nu_ref, xs_ref, w1_ref, w3_ref, w2_ref, y_ref, w1b_ref, w3b_ref, w2b_ref):
    i = pl.program_id(0)
    used = i < nu_ref[0]

    @pl.when(used & ((i == 0) | (be_ref[i] != be_ref[jnp.maximum(i, 1) - 1])))
    def _():
        w1b_ref[...] = w1_ref[0].astype(BF16)
        w3b_ref[...] = w3_ref[0].astype(BF16)
        w2b_ref[...] = w2_ref[0].astype(BF16)

    @pl.when(used)
    def _():
        x = xs_ref[...]
        a = _dot(x, w1b_ref[...])
        b = _dot(x, w3b_ref[...])
        y_ref[...] = _dot((_silu(a) * b).astype(BF16), w2b_ref[...]).astype(y_ref.dtype)

    @pl.when(jnp.logical_not(used))
    def _():
        y_ref[...] = jnp.zeros(y_ref.shape, y_ref.dtype)


def _experts(block_e, n_used, xs, w1, w3, w2):
    n_rows, d = xs.shape
    rb = MOE_BLOCK_ROWS
    de = w1.shape[2]
    used = lambda i, be, nu: jnp.minimum(i, nu[0] - 1)
    grid_spec = pltpu.PrefetchScalarGridSpec(
        num_scalar_prefetch=2,
        grid=(n_rows // rb,),
        in_specs=[pl.BlockSpec((rb, d), lambda i, be, nu: (used(i, be, nu), 0)),
                  pl.BlockSpec((1, d, de), lambda i, be, nu: (be[used(i, be, nu)], 0, 0)),
                  pl.BlockSpec((1, d, de), lambda i, be, nu: (be[used(i, be, nu)], 0, 0)),
                  pl.BlockSpec((1, de, d), lambda i, be, nu: (be[used(i, be, nu)], 0, 0))],
        out_specs=pl.BlockSpec((rb, d), lambda i, be, nu: (i, 0)),
        scratch_shapes=[pltpu.VMEM((d, de), BF16), pltpu.VMEM((d, de), BF16), pltpu.VMEM((de, d), BF16)],
    )
    return pl.pallas_call(
        _expert_kernel,
        grid_spec=grid_spec,
        out_shape=jax.ShapeDtypeStruct((n_rows, d), BF16),
        compiler_params=_params(("arbitrary",)),
        name="experts",
    )(block_e, n_used, xs, w1, w3, w2)


def _combine_kernel(final_norm, len_ref, loc_ref, glob_ref, rows_ref, x1_ref, meta_ref, mod_ref, g_ref, y_ref,
                    o_ref, ybuf_ref, sem):
    tc, d = x1_ref.shape
    i = pl.program_id(0)

    slot = i % 2

    def run_copy(s, loc, glob, size):
        return pltpu.make_async_copy(y_ref.at[pl.ds(glob, size)], ybuf_ref.at[s, pl.ds(loc, size)], sem.at[s])

    def start_tile(tile, s):
        _for_each_run(tile, len_ref, loc_ref, glob_ref, lambda loc, glob, size: run_copy(s, loc, glob, size).start())

    @pl.when(i == 0)
    def _():
        ybuf_ref[...] = jnp.zeros(ybuf_ref.shape, ybuf_ref.dtype)
        start_tile(0, 0)

    @pl.when(i + 1 < pl.num_programs(0))
    def _():
        start_tile(i + 1, 1 - slot)

    _for_each_piece(rows_ref[i], TILE_PIECES, lambda pos, size: run_copy(slot, 0, 0, size).wait())

    meta = meta_ref[...]
    yb = ybuf_ref[slot]
    rpos = lax.broadcasted_iota(jnp.int32, (tc, ybuf_ref.shape[1]), 1)
    sel = jnp.zeros(rpos.shape, F32)
    for l_lane, w_lane in ((META_L0, META_W0), (META_L1, META_W1)):
        sel = jnp.where(rpos == meta[:, l_lane:l_lane + 1].astype(jnp.int32), meta[:, w_lane:w_lane + 1], sel)
    moe = _dot(sel.astype(BF16), yb)
    x2 = x1_ref[...] + mod_ref[0][:, 5 * d:6 * d] * moe
    if final_norm:
        x2 = x2 * lax.rsqrt(jnp.mean(x2 * x2, axis=-1, keepdims=True) + NORM_EPS) * g_ref[...]
    o_ref[...] = x2


def _combine(plan, x1, meta, mod3, norm_f_g, y, seq, final_norm):
    t, d = x1.shape
    tc = MOE_TILE
    per_batch = seq // tc
    gf = norm_f_g.reshape(1, d)
    grid_spec = pltpu.PrefetchScalarGridSpec(
        num_scalar_prefetch=4,
        grid=(t // tc,),
        in_specs=[pl.BlockSpec((tc, d), lambda i, *_: (i, 0)),
                  pl.BlockSpec((tc, ROUTER_LANES), lambda i, *_: (i, 0)),
                  pl.BlockSpec((1, 1, mod3.shape[2]), lambda i, *_: (i // per_batch, 0, 0)),
                  pl.BlockSpec(gf.shape, lambda i, *_: (0, 0)),
                  pl.BlockSpec(memory_space=pl.ANY)],
        out_specs=pl.BlockSpec((tc, d), lambda i, *_: (i, 0)),
        scratch_shapes=[pltpu.VMEM((2, PERM_ROWS, d), BF16), pltpu.SemaphoreType.DMA((2,))],
    )
    return pl.pallas_call(
        functools.partial(_combine_kernel, final_norm),
        grid_spec=grid_spec,
        out_shape=jax.ShapeDtypeStruct((t, d), F32),
        compiler_params=_params(("arbitrary",)),
        name="combine",
    )(plan["run_len"], plan["run_loc"], plan["run_glob"], plan["tile_rows"], x1, meta, mod3, gf, y)


def _routing_plan(stats, n_tokens):
    rb = MOE_BLOCK_ROWS
    run_len = stats[:, STAT_LEN, :N_EXPERTS].astype(jnp.int32)
    run_loc = stats[:, STAT_START, :N_EXPERTS].astype(jnp.int32)
    n_tiles = run_len.shape[0]
    total = jnp.sum(run_len, axis=0)
    padded = ((total + rb - 1) // rb) * rb
    pad_end = jnp.cumsum(padded)
    pad_start = pad_end - padded
    run_glob = pad_start[None, :] + jnp.cumsum(run_len, axis=0) - run_len
    n_rows = n_tokens * TOP_K_EXPERTS + n_tiles * N_EXPERTS * (RUN_ALIGN - 1) + N_EXPERTS * (rb - 1)
    n_rows = ((n_rows + rb - 1) // rb) * rb
    block_row = jnp.arange(n_rows // rb, dtype=jnp.int32) * rb
    block_e = jnp.minimum(jnp.sum((pad_end[None, :] <= block_row[:, None]).astype(jnp.int32), axis=1),
                          N_EXPERTS - 1)
    return dict(run_len=run_len.reshape(-1), run_loc=run_loc.reshape(-1), run_glob=run_glob.reshape(-1),
                tile_rows=jnp.sum(run_len, axis=1),
                tail_start=pad_start + total, tail_len=padded - total,
                n_used=(pad_end[-1:] // rb).astype(jnp.int32), block_e=block_e.astype(jnp.int32), n_rows=n_rows)


def kernel(x, c, positions, w_ada, b_ada, norm1_g, w_in, b_gate, conv_w, conv_b, attn_out_g, mlstm_out_g,
           w_out, norm2_g, w_rg, b_rg, w_re, b_re, w1, w3, w2, norm_f_g):
    b, s, d = x.shape
    depth = w_ada.shape[0]
    assert d == D_MODEL and s % MOE_TILE == 0 and s % (KV_CHUNK * MOBA_BLOCK) == 0
    x2d = x.reshape(b * s, d)
    for l in range(depth):
        mod3 = _mod(c, w_ada[l], b_ada[l]).reshape(b, 1, 6 * d)
        q, k, vt, qkm, vm, om, gates = _inproj(x2d.reshape(b, s, d), positions, mod3,
                                              norm1_g[l].reshape(1, d), w_in[l])
        attn = _moba(q, k, vt, attn_out_g[l].reshape(1, D_ATT))
        hm = _mlstm(qkm, vm, om, gates, conv_w[l], conv_b[l], b_gate[l], mlstm_out_g[l])
        x1, h2, meta, stats = _outproj(attn.reshape(b * s, D_ATT), hm.reshape(b * s, D_MLSTM), x2d, mod3,
                                       norm2_g[l].reshape(1, d), w_out[l], w_rg[l], b_rg[l], w_re[l],
                                       b_re[l], s)
        plan = _routing_plan(stats, b * s)
        xs = _dispatch(plan, meta, h2)
        y = _experts(plan["block_e"], plan["n_used"], xs, w1[l], w3[l], w2[l])
        x2d = _combine(plan, x1, meta, mod3, norm_f_g, y, s, final_norm=(l == depth - 1))
    return x2d.reshape(b, s, d)
```

```python
import functools

import jax
import jax.numpy as jnp
from jax import lax
from jax.experimental import pallas as pl
from jax.experimental.pallas import tpu as pltpu

F32 = jnp.float32
BF16 = jnp.bfloat16

D_MODEL = 1024
D_ATT = 512
ATT_HEADS = 8
ATT_HEAD_DIM = 64
D_MLSTM = 512
MLSTM_HEADS = 4
MLSTM_HEAD_DIM = 128
MOBA_BLOCK = 256
MOBA_TOPK = 3
ROPE_THETA = 10000.0
MLSTM_CHUNK = 128
CONV_WIDTH = 4
N_GROUPS = 4
EXPERTS_PER_GROUP = 8
N_EXPERTS = N_GROUPS * EXPERTS_PER_GROUP
TOP_K_EXPERTS = 2
D_EXPERT = 512
MOE_BLOCK_ROWS = 256
NORM_EPS = 1e-6
NEG_INF = -1e30
LOG2_E = 1.4426950408889634
KV_CHUNK = 4
MOBA_GROUP_LANES = 512

LANES = 128
SUBLANES = 8
VMEM_LIMIT_BYTES = 56 * 1024 * 1024

MOE_TILE = 512
RUN_ALIGN = 2 * SUBLANES
PERM_ROWS = TOP_K_EXPERTS * MOE_TILE + N_EXPERTS * RUN_ALIGN
CHUNKS_PER_TILE = PERM_ROWS // RUN_ALIGN
TAIL_PIECES = tuple(RUN_ALIGN << p for p in reversed(range(4)))
TILE_PIECES = tuple(RUN_ALIGN << p for p in reversed(range(7)))
ZERO_ROWS = TAIL_PIECES[0]
GATE_LANES = 2 * LANES
ROUTER_LANES = LANES
GROUP_LANE0 = N_EXPERTS


def _dot(a, b):
    return jnp.dot(a, b, preferred_element_type=F32)


def _dot_nt(a, b):
    return lax.dot_general(a, b, (((1,), (1,)), ((), ())), preferred_element_type=F32)


def _dot_tn(a, b):
    return lax.dot_general(a, b, (((0,), (0,)), ((), ())), preferred_element_type=F32)


def _split3(x):
    a = x.astype(BF16)
    r = x - a.astype(F32)
    b = r.astype(BF16)
    c = (r - b.astype(F32)).astype(BF16)
    return a, b, c


def _silu(x):
    return x * jax.nn.sigmoid(x)


def _log_sigmoid(x):
    return jnp.minimum(x, 0.0) - jnp.log1p(jnp.exp(-jnp.abs(x)))


def _params(semantics, vmem=VMEM_LIMIT_BYTES):
    return pltpu.CompilerParams(dimension_semantics=semantics, vmem_limit_bytes=vmem)


def _mod_kernel(c_ref, w_ref, b_ref, o_ref):
    sc = _silu(c_ref[...])
    o_ref[...] = jnp.dot(sc, w_ref[...], precision=lax.Precision.HIGHEST,
                         preferred_element_type=F32) + b_ref[...]


def _mod(c, w_ada, b_ada):
    b, d = c.shape
    n = w_ada.shape[1]
    tn = D_MODEL
    return pl.pallas_call(
        _mod_kernel,
        grid=(n // tn,),
        in_specs=[pl.BlockSpec((b, d), lambda i: (0, 0)),
                  pl.BlockSpec((d, tn), lambda i: (0, i)),
                  pl.BlockSpec((1, tn), lambda i: (0, i))],
        out_specs=pl.BlockSpec((b, tn), lambda i: (0, i)),
        out_shape=jax.ShapeDtypeStruct((b, n), F32),
        compiler_params=_params(("arbitrary",)),
        name="mod",
    )(c, w_ada, b_ada.reshape(1, n))


def _rms_modulate(x, g, shift, scale):
    y = x * lax.rsqrt(jnp.mean(x * x, axis=-1, keepdims=True) + NORM_EPS)
    return (y * g) * (1.0 + scale) + shift


def _rope(t, cos, sin, first_half):
    outs = []
    for c in range(t.shape[1] // LANES):
        tc = t[:, c * LANES:(c + 1) * LANES]
        rot = jnp.where(first_half, -pltpu.roll(tc, LANES - ATT_HEAD_DIM // 2, 1),
                        pltpu.roll(tc, ATT_HEAD_DIM // 2, 1))
        outs.append(tc * cos + rot * sin)
    return jnp.concatenate(outs, axis=1)


def _inproj_kernel(x_ref, pos_ref, mod_ref, g_ref, invf_ref, wq_ref, wk_ref, wvt_ref, wqk_ref,
                   wv_ref, wo_ref, wg_ref, q_ref, k_ref, vt_ref, qkm_ref, vm_ref, om_ref, gt_ref):
    d = x_ref.shape[2]
    x = x_ref[0]
    mod = mod_ref[0]
    h = _rms_modulate(x, g_ref[...], mod[:, 0:d], mod[:, d:2 * d])
    hb = h.astype(BF16)

    ang = pos_ref[0].astype(F32) * invf_ref[...]
    cos = jnp.cos(ang)
    sin = jnp.sin(ang)
    lane = lax.broadcasted_iota(jnp.int32, cos.shape, 1)
    first_half = (lane & (ATT_HEAD_DIM // 2)) == 0

    q = _rope(_dot(hb, wq_ref[...]), cos, sin, first_half)
    q_ref[0] = (q * (ATT_HEAD_DIM ** -0.5 * LOG2_E)).astype(BF16)
    k_ref[0] = _rope(_dot(hb, wk_ref[...]), cos, sin, first_half).astype(BF16)
    vt_ref[0, 0] = _dot_nt(wvt_ref[...], hb).astype(BF16)
    qkm_ref[0] = _dot(hb, wqk_ref[...])
    vm_ref[0] = _dot(hb, wv_ref[...])
    om_ref[0] = _dot(hb, wo_ref[...])
    gt_ref[0] = _dot(hb, wg_ref[...])


def _inproj(x, positions, mod3, norm_g, w_in):
    b, s, d = x.shape
    tm = MOBA_BLOCK
    nb = s // tm
    o = [0, D_ATT, 2 * D_ATT, 3 * D_ATT, 3 * D_ATT + 2 * D_MLSTM, 3 * D_ATT + 3 * D_MLSTM,
         3 * D_ATT + 4 * D_MLSTM, 3 * D_ATT + 4 * D_MLSTM + 2 * MLSTM_HEADS]
    wb = w_in.astype(BF16)
    wq, wk, wv_a, wqk, wv, wo, wg = (wb[:, o[i]:o[i + 1]] for i in range(7))
    wvt = wv_a.T
    lane_pad = ((0, 0), (0, LANES - MLSTM_HEADS))
    wg = jnp.concatenate([jnp.pad(wg[:, :MLSTM_HEADS], lane_pad), jnp.pad(wg[:, MLSTM_HEADS:], lane_pad)], axis=1)
    half = ATT_HEAD_DIM // 2
    inv_freq = ROPE_THETA ** (-jnp.arange(half, dtype=F32) / half)
    invf = jnp.tile(inv_freq, LANES // half).reshape(1, LANES)

    full = lambda a: pl.BlockSpec(a.shape, lambda bi, i: (0,) * a.ndim)
    tok = lambda w: pl.BlockSpec((1, tm, w), lambda bi, i: (bi, i, 0))
    out_shape = [jax.ShapeDtypeStruct((b, s, D_ATT), BF16),
                 jax.ShapeDtypeStruct((b, s, D_ATT), BF16),
                 jax.ShapeDtypeStruct((b, nb, D_ATT, tm), BF16),
                 jax.ShapeDtypeStruct((b, s, 2 * D_MLSTM), F32),
                 jax.ShapeDtypeStruct((b, s, D_MLSTM), F32),
                 jax.ShapeDtypeStruct((b, s, D_MLSTM), F32),
                 jax.ShapeDtypeStruct((b, s, GATE_LANES), F32)]
    return pl.pallas_call(
        _inproj_kernel,
        grid=(b, nb),
        in_specs=[tok(d), tok(1),
                  pl.BlockSpec((1, 1, mod3.shape[2]), lambda bi, i: (bi, 0, 0)),
                  full(norm_g), full(invf), full(wq), full(wk), full(wvt), full(wqk), full(wv),
                  full(wo), full(wg)],
        out_specs=[tok(D_ATT), tok(D_ATT),
                   pl.BlockSpec((1, 1, D_ATT, tm), lambda bi, i: (bi, i, 0, 0)),
                   tok(2 * D_MLSTM), tok(D_MLSTM), tok(D_MLSTM), tok(GATE_LANES)],
        out_shape=out_shape,
        compiler_params=_params(("arbitrary", "arbitrary")),
        name="inproj",
    )(x, positions.reshape(b, s, 1), mod3, norm_g, invf, wq, wk, wvt, wqk, wv, wo, wg)


def _moba_kernel(q_ref, k_ref, vt_ref, g_ref, o_ref, kmean_ref, sel_ref, st_ref, qh_ref, acc_ref):
    blk = MOBA_BLOCK
    nb = k_ref.shape[1] // blk
    j = pl.program_id(2)
    heads = q_ref.shape[2] // ATT_HEAD_DIM
    hpl = LANES // ATT_HEAD_DIM

    @pl.when(j == 0)
    def _():
        for n in range(nb):
            kb = k_ref[0, n * blk:(n + 1) * blk, :].astype(F32)
            kmean_ref[n:n + 1, :] = jnp.mean(kb, axis=0, keepdims=True)

    lane = lax.broadcasted_iota(jnp.int32, (blk, LANES), 1)
    blk_id = lax.broadcasted_iota(jnp.int32, (nb, blk), 0)
    km_hi = kmean_ref[...].astype(BF16)
    km_lo = (kmean_ref[...] - km_hi.astype(F32)).astype(BF16)
    past = blk_id < j
    tile = lambda a, hh: a[:, (hh // hpl) * LANES:(hh // hpl + 1) * LANES]

    for hh in range(heads):
        q = tile(q_ref[0], hh)
        in_head = (lane >= (hh % hpl) * ATT_HEAD_DIM) & (lane < (hh % hpl + 1) * ATT_HEAD_DIM)
        qh_ref[hh] = jnp.where(in_head, q, jnp.zeros_like(q))
        gate = _dot_nt(tile(km_hi, hh), qh_ref[hh]) + _dot_nt(tile(km_lo, hh), qh_ref[hh])
        g = jnp.where(past, gate, NEG_INF)
        sel = jnp.zeros(g.shape, F32)
        for _ in range(min(MOBA_TOPK, nb)):
            top = jnp.max(g, axis=0, keepdims=True)
            idx = jnp.min(jnp.where(g == top, blk_id, nb), axis=0, keepdims=True)
            pick = blk_id == idx
            sel = jnp.where(pick, 1.0, sel)
            g = jnp.where(pick, -jnp.inf, g)
        sel_ref[hh, 0:nb, :] = jnp.where((sel > 0.0) & past, 1.0, 0.0)
        sel_ref[hh, nb:nb + SUBLANES, :] = jnp.zeros((SUBLANES, blk), F32)

    kpos = lax.broadcasted_iota(jnp.int32, (blk, blk), 0)
    qpos = lax.broadcasted_iota(jnp.int32, (blk, blk), 1)
    causal_bias = jnp.where(kpos <= qpos, 0.0, NEG_INF)

    def slab_block(t, i):
        b_i = j - KV_CHUNK * t - (KV_CHUNK - 1 - i)
        return jnp.maximum(b_i, 0), jnp.where(b_i >= 0, b_i, nb)

    def score(t, slot, own_chunk):
        tops = [None] * heads
        slab_max = [[None] * KV_CHUNK for _ in range(heads)]
        for i in range(KV_CHUNK):
            src, row = slab_block(t, i)
            kb = k_ref[0, pl.ds(pl.multiple_of(src * blk, blk), blk), :]
            for hh in range(heads):
                st = _dot_nt(tile(kb, hh), qh_ref[hh])
                if own_chunk and i == KV_CHUNK - 1:
                    st = st + causal_bias
                    cmax = jnp.max(st, axis=0, keepdims=True)
                    smax = cmax
                else:
                    cmax = jnp.max(st, axis=0, keepdims=True)
                    smax = cmax + (sel_ref[hh, pl.ds(row, 1), :] - 1.0) * (-NEG_INF)
                st_ref[slot, hh, i * blk:(i + 1) * blk, :] = st
                slab_max[hh][i] = cmax
                tops[hh] = smax if tops[hh] is None else jnp.maximum(tops[hh], smax)
        return tuple(tops), tuple(tuple(r) for r in slab_max)

    def accumulate(t, slot, state, maxes):
        tops, slab_max = maxes
        new = []
        for hh in range(heads):
            m, l = state[2 * hh:2 * hh + 2]
            m_new = jnp.maximum(m, tops[hh])
            alpha = jnp.exp2(m - m_new)
            l = alpha * l
            acc = alpha * acc_ref[hh]
            for i in range(KV_CHUNK):
                src, row = slab_block(t, i)
                keep = sel_ref[hh, pl.ds(row, 1), :]
                if i == KV_CHUNK - 1:
                    keep = jnp.where(t == 0, 1.0, keep)
                p = jnp.exp2(st_ref[slot, hh, i * blk:(i + 1) * blk, :] - jnp.maximum(m_new, slab_max[hh][i]))
                l = l + keep * jnp.sum(p, axis=0, keepdims=True)
                v_h = vt_ref[0, src][hh * ATT_HEAD_DIM:(hh + 1) * ATT_HEAD_DIM, :]
                acc = acc + keep * _dot(v_h, p.astype(BF16))
            acc_ref[hh] = acc
            new += [m_new, l]
        return tuple(new)

    state = ()
    for hh in range(heads):
        state += (jnp.full((1, blk), NEG_INF, F32), jnp.zeros((1, blk), F32))
        acc_ref[hh] = jnp.zeros((ATT_HEAD_DIM, blk), F32)
    last = j // KV_CHUNK

    def stage_pair(t, slot, state, maxes):
        nxt = score(t + 1, 1 - slot, False)
        return accumulate(t, slot, state, maxes), nxt

    def body(t, carry):
        return lax.cond(t % 2 == 0, functools.partial(stage_pair, t, 0), functools.partial(stage_pair, t, 1),
                        *carry)

    state, maxes = lax.fori_loop(0, last, body, (state, score(0, 0, True)))
    state = lax.cond(last % 2 == 0, functools.partial(accumulate, last, 0), functools.partial(accumulate, last, 1),
                     state, maxes)

    outs = []
    for hh in range(heads):
        o_h = acc_ref[hh] / state[2 * hh + 1]
        ms = jnp.mean(o_h * o_h, axis=0, keepdims=True)
        outs.append(o_h * lax.rsqrt(ms + NORM_EPS))
    o_ref[0] = jnp.concatenate(outs, axis=0).T * g_ref[...]


def _moba(q, k, vt, g_att):
    b, s, w = q.shape
    blk = MOBA_BLOCK
    nb = s // blk
    gl = MOBA_GROUP_LANES
    groups = w // gl
    heads = gl // ATT_HEAD_DIM
    return pl.pallas_call(
        _moba_kernel,
        grid=(b, groups, nb),
        in_specs=[pl.BlockSpec((1, blk, gl), lambda bi, p, j: (bi, j, p)),
                  pl.BlockSpec((1, s, gl), lambda bi, p, j: (bi, 0, p)),
                  pl.BlockSpec((1, nb, gl, blk), lambda bi, p, j: (bi, 0, p, 0)),
                  pl.BlockSpec((1, gl), lambda bi, p, j: (0, p))],
        out_specs=pl.BlockSpec((1, blk, gl), lambda bi, p, j: (bi, j, p)),
        out_shape=jax.ShapeDtypeStruct((b, s, w), F32),
        scratch_shapes=[pltpu.VMEM((nb, gl), F32),
                        pltpu.VMEM((heads, nb + SUBLANES, blk), F32),
                        pltpu.VMEM((2, heads, KV_CHUNK * blk, blk), F32),
                        pltpu.VMEM((heads, blk, LANES), BF16),
                        pltpu.VMEM((heads, ATT_HEAD_DIM, blk), F32)],
        compiler_params=_params(("arbitrary", "arbitrary", "arbitrary")),
        name="moba",
    )(q, k, vt, g_att)


def _mlstm_kernel(qkm_ref, vm_ref, om_ref, gt_ref, cw_ref, cb_ref, bg_ref, gm_ref, bsel_ref, o_ref,
                  uext_ref, c_ref, n_ref, m_ref):
    L = MLSTM_CHUNK
    dh = MLSTM_HEAD_DIM
    nh = MLSTM_HEADS
    assert L == dh == LANES and 2 * nh == SUBLANES
    c = pl.program_id(1)

    @pl.when(c == 0)
    def _():
        uext_ref[0:SUBLANES, :] = jnp.zeros((SUBLANES, uext_ref.shape[1]), F32)
        c_ref[...] = jnp.zeros(c_ref.shape, F32)
        n_ref[...] = jnp.zeros(n_ref.shape, F32)
        m_ref[...] = jnp.zeros(m_ref.shape, F32)

    u = qkm_ref[0]
    uext_ref[SUBLANES:SUBLANES + L, :] = u
    cw = cw_ref[...]
    conv = cb_ref[...] + cw[CONV_WIDTH - 1:CONV_WIDTH, :] * u
    for dlt in range(1, CONV_WIDTH):
        conv = conv + cw[CONV_WIDTH - 1 - dlt:CONV_WIDTH - dlt, :] * uext_ref[SUBLANES - dlt:SUBLANES - dlt + L, :]
    uext_ref[0:SUBLANES, :] = u[L - SUBLANES:L, :]
    act = _silu(conv)

    gates = gt_ref[0] + bg_ref[...]
    ig = gates[:, 0:LANES]
    tpos = lax.broadcasted_iota(jnp.int32, (L, L), 0)
    spos = lax.broadcasted_iota(jnp.int32, (L, L), 1)
    causal = spos <= tpos
    tril = jnp.where(causal, 1.0, 0.0).astype(BF16)
    cum = sum(_dot(tril, part) for part in _split3(_log_sigmoid(gates[:, LANES:])))
    a_tot = cum[L - 1:L, :]
    m_prev = m_ref[0:1, :]
    resid = ig - cum
    pmax = resid
    shift = 1
    while shift < L:
        pmax = jnp.maximum(pmax, jnp.where(tpos >= shift, pltpu.roll(pmax, shift, 0), -jnp.inf))
        shift *= 2
    inter = cum + m_prev
    m_t = jnp.maximum(inter, cum + pmax)
    g_loc = a_tot - cum + ig
    m_loc = jnp.max(g_loc, axis=0, keepdims=True)
    w_loc = jnp.exp(g_loc - m_loc)
    head_lane = spos < nh
    packed = jnp.where(head_lane, cum - m_t, 0.0)
    for i, qty in enumerate((inter - m_t, -m_t, g_loc - m_loc), start=1):
        packed = packed + pltpu.roll(jnp.where(head_lane, qty, 0.0), SUBLANES * i, 1)
    per_t_cols = jnp.concatenate(_split3(packed), axis=1)
    per_t_rows = (jnp.where(head_lane, resid, 0.0) + pltpu.roll(jnp.where(head_lane, w_loc, 0.0), SUBLANES, 1)).T

    m_new = jnp.maximum(a_tot + m_prev, m_loc)
    s_prev = jnp.exp(a_tot + m_prev - m_new)
    s_loc = jnp.exp(m_loc - m_new)
    m_ref[0:1, :] = m_new

    vm = vm_ref[0]
    om = om_ref[0]
    ones_b = jnp.ones((L, dh), BF16)
    c_all = [c_ref[h] for h in range(nh)]
    n_all = n_ref[...]
    new_state = []
    for h in range(nh):
        sl = slice(h * dh, (h + 1) * dh)
        q = act[:, sl]
        k = act[:, nh * dh + h * dh:nh * dh + (h + 1) * dh] * (dh ** -0.5)
        v = vm[:, sl]
        qb, kb, vb = q.astype(BF16), k.astype(BF16), v.astype(BF16)
        c_prev = c_all[h]
        n_prev = n_all[h:h + 1, :]
        bcast = _dot(per_t_cols, bsel_ref[h])
        decay_b = bcast[:, 0:dh]
        w_inter_b, floor_b, w_loc_b = (jnp.exp(bcast[:, i * dh:(i + 1) * dh]) for i in range(1, 4))

        s_qk = _dot_nt(qb, kb) * jnp.where(causal, jnp.exp(decay_b + per_t_rows[h:h + 1, :]), 0.0)
        intra = _dot(s_qk.astype(BF16), jnp.concatenate([vb, ones_b], axis=1))
        state_rows = jnp.concatenate([c_prev, jnp.broadcast_to(n_prev, (dh, dh))], axis=0).astype(BF16)
        carried = _dot_nt(qb, state_rows)
        num = intra[:, :dh] + w_inter_b * carried[:, :dh]
        den = intra[:, dh:] + w_inter_b * carried[:, dh:]
        hout = num / jnp.maximum(jnp.abs(den), floor_b)

        mean_sq = _dot((hout * hout).astype(BF16), ones_b) * (1.0 / dh)
        hn = hout * lax.rsqrt(mean_sq + NORM_EPS) * gm_ref[:, sl]
        o_ref[0, :, sl] = hn * jax.nn.sigmoid(om[:, sl])

        c_loc = _dot_tn((v * w_loc_b).astype(BF16), kb)
        w_row = per_t_rows[SUBLANES + h:SUBLANES + h + 1, :]
        n_loc = _dot(jnp.broadcast_to(w_row, (SUBLANES, L)).astype(BF16), kb)[0:1, :]
        new_state.append((s_prev[:, h:h + 1] * c_prev + s_loc[:, h:h + 1] * c_loc,
                          s_prev[:, h:h + 1] * n_prev + s_loc[:, h:h + 1] * n_loc))
    for h, (c_new, n_new) in enumerate(new_state):
        c_ref[h] = c_new
        n_ref[h:h + 1, :] = n_new


def _mlstm(qkm, vm, om, gates, conv_w, conv_b, b_gate, g_m):
    b, s, _ = qkm.shape
    L = MLSTM_CHUNK
    nc = s // L
    lane_pad = ((0, 0), (0, LANES - MLSTM_HEADS))
    bg = b_gate.reshape(1, -1)
    bg = jnp.concatenate([jnp.pad(bg[:, :MLSTM_HEADS], lane_pad), jnp.pad(bg[:, MLSTM_HEADS:], lane_pad)], axis=1)
    tok = lambda w: pl.BlockSpec((1, L, w), lambda bi, c: (bi, c, 0))
    full = lambda a: pl.BlockSpec(a.shape, lambda bi, c: (0,) * a.ndim)
    cb = conv_b.reshape(1, -1)
    gm = g_m.reshape(1, -1)
    n_qty = 4
    row = jnp.arange(3 * LANES, dtype=jnp.int32) % LANES
    col_qty = jnp.arange(n_qty * MLSTM_HEAD_DIM, dtype=jnp.int32) // MLSTM_HEAD_DIM
    bsel = jnp.stack([
        ((row // SUBLANES)[:, None] == col_qty[None, :]) & (row % SUBLANES == h)[:, None]
        for h in range(MLSTM_HEADS)]).astype(BF16)
    return pl.pallas_call(
        _mlstm_kernel,
        grid=(b, nc),
        in_specs=[tok(2 * D_MLSTM), tok(D_MLSTM), tok(D_MLSTM), tok(GATE_LANES),
                  full(conv_w), full(cb), full(bg), full(gm), full(bsel)],
        out_specs=tok(D_MLSTM),
        out_shape=jax.ShapeDtypeStruct((b, s, D_MLSTM), F32),
        scratch_shapes=[pltpu.VMEM((SUBLANES + L, 2 * D_MLSTM), F32),
                        pltpu.VMEM((MLSTM_HEADS, MLSTM_HEAD_DIM, MLSTM_HEAD_DIM), F32),
                        pltpu.VMEM((SUBLANES, MLSTM_HEAD_DIM), F32),
                        pltpu.VMEM((SUBLANES, LANES), F32)],
        compiler_params=_params(("arbitrary", "arbitrary")),
        name="mlstm",
    )(qkm, vm, om, gates, conv_w, cb, bg, gm, bsel)


META_E0, META_E1, META_W0, META_W1, META_L0, META_L1 = range(6)
STAT_LEN, STAT_START = 0, 1


def _outproj_kernel(attn_ref, hm_ref, x_ref, mod_ref, g_ref, wout_ref, wr_hi_ref, wr_lo_ref, br_ref,
                    x1_ref, h2_ref, meta_ref, stat_ref):
    d = x_ref.shape[1]
    tm = x_ref.shape[0]

    mod = mod_ref[0]
    y = (_dot(attn_ref[...].astype(BF16), wout_ref[0:D_ATT, :])
         + _dot(hm_ref[...].astype(BF16), wout_ref[D_ATT:, :]))
    x1 = x_ref[...] + mod[:, 2 * d:3 * d] * y
    x1_ref[...] = x1
    h2 = _rms_modulate(x1, g_ref[...], mod[:, 3 * d:4 * d], mod[:, 4 * d:5 * d])
    h2_ref[...] = h2.astype(BF16)

    h_hi = h2.astype(BF16)
    h_lo = (h2 - h_hi.astype(F32)).astype(BF16)
    logit = (_dot(h_hi, wr_hi_ref[...]) + _dot(h_lo, wr_hi_ref[...]) + _dot(h_hi, wr_lo_ref[...])
             + br_ref[...])
    lane = lax.broadcasted_iota(jnp.int32, logit.shape, 1)
    big = jnp.int32(ROUTER_LANES)

    is_g = (lane >= GROUP_LANE0) & (lane < GROUP_LANE0 + N_GROUPS)
    gmax = jnp.max(jnp.where(is_g, logit, -jnp.inf), axis=1, keepdims=True)
    gsum = jnp.sum(jnp.where(is_g, jnp.exp(logit - gmax), 0.0), axis=1, keepdims=True)
    g_w = 1.0 / gsum
    g_idx = jnp.min(jnp.where(is_g & (logit == gmax), lane, big), axis=1, keepdims=True) - GROUP_LANE0

    in_grp = (lane < N_EXPERTS) & ((lane // EXPERTS_PER_GROUP) == g_idx)
    emax = jnp.max(jnp.where(in_grp, logit, -jnp.inf), axis=1, keepdims=True)
    esum = jnp.sum(jnp.where(in_grp, jnp.exp(logit - emax), 0.0), axis=1, keepdims=True)
    e0 = jnp.min(jnp.where(in_grp & (logit == emax), lane, big), axis=1, keepdims=True)
    rest = in_grp & (lane != e0)
    e2max = jnp.max(jnp.where(rest, logit, -jnp.inf), axis=1, keepdims=True)
    e1 = jnp.min(jnp.where(rest & (logit == e2max), lane, big), axis=1, keepdims=True)
    p0 = 1.0 / esum
    p1 = jnp.exp(e2max - emax) / esum
    w0 = g_w * p0 / (p0 + p1)
    w1 = g_w * p1 / (p0 + p1)

    memb = jnp.where((lane == e0) | (lane == e1), 1.0, 0.0)
    tpos = lax.broadcasted_iota(jnp.int32, (tm, tm), 0)
    spos = lax.broadcasted_iota(jnp.int32, (tm, tm), 1)
    before = jnp.where(spos < tpos, 1.0, 0.0).astype(BF16)
    earlier = _dot(before, memb.astype(BF16))
    count = jnp.sum(memb, axis=0, keepdims=True).astype(jnp.int32)
    run_len = (((count + (RUN_ALIGN - 1)) // RUN_ALIGN) * RUN_ALIGN).astype(F32)
    epos = lax.broadcasted_iota(jnp.int32, (ROUTER_LANES, ROUTER_LANES), 0)
    fpos = lax.broadcasted_iota(jnp.int32, (ROUTER_LANES, ROUTER_LANES), 1)
    lower_e = jnp.where(epos < fpos, 1.0, 0.0).astype(BF16)
    run_start = _dot(jnp.broadcast_to(run_len, (SUBLANES, ROUTER_LANES)).astype(BF16), lower_e)[0:1, :]
    row = run_start + earlier
    l0 = jnp.sum(jnp.where(lane == e0, row, 0.0), axis=1, keepdims=True)
    l1 = jnp.sum(jnp.where(lane == e1, row, 0.0), axis=1, keepdims=True)

    meta = jnp.zeros(logit.shape, F32)
    for slot, val in ((META_E0, e0.astype(F32)), (META_E1, e1.astype(F32)), (META_W0, w0), (META_W1, w1),
                      (META_L0, l0), (META_L1, l1)):
        meta = jnp.where(lane == slot, val, meta)
    meta_ref[...] = meta
    srow = lax.broadcasted_iota(jnp.int32, (SUBLANES, ROUTER_LANES), 0)
    stat_ref[0] = jnp.where(srow == STAT_LEN, run_len, jnp.where(srow == STAT_START, run_start, 0.0))


def _outproj(attn, hm, x2d, mod3, norm_g, w_out, w_rg, b_rg, w_re, b_re, seq):
    t, d = x2d.shape
    tm = MOE_TILE
    per_batch = seq // tm
    wr = jnp.pad(jnp.concatenate([w_re, w_rg], axis=1), ((0, 0), (0, ROUTER_LANES - N_EXPERTS - N_GROUPS)))
    br = jnp.pad(jnp.concatenate([b_re, b_rg]).reshape(1, -1), ((0, 0), (0, ROUTER_LANES - N_EXPERTS - N_GROUPS)))
    wr_hi = wr.astype(BF16)
    wr_lo = (wr - wr_hi.astype(F32)).astype(BF16)
    wout = w_out.astype(BF16)
    tok = lambda w: pl.BlockSpec((tm, w), lambda i: (i, 0))
    full = lambda a: pl.BlockSpec(a.shape, lambda i: (0,) * a.ndim)
    return pl.pallas_call(
        _outproj_kernel,
        grid=(t // tm,),
        in_specs=[tok(D_ATT), tok(D_MLSTM), tok(d),
                  pl.BlockSpec((1, 1, mod3.shape[2]), lambda i: (i // per_batch, 0, 0)),
                  full(norm_g), full(wout), full(wr_hi), full(wr_lo), full(br)],
        out_specs=[tok(d), tok(d), tok(ROUTER_LANES),
                   pl.BlockSpec((1, SUBLANES, ROUTER_LANES), lambda i: (i, 0, 0))],
        out_shape=[jax.ShapeDtypeStruct((t, d), F32), jax.ShapeDtypeStruct((t, d), BF16),
                   jax.ShapeDtypeStruct((t, ROUTER_LANES), F32),
                   jax.ShapeDtypeStruct((t // tm, SUBLANES, ROUTER_LANES), F32)],
        compiler_params=_params(("arbitrary",)),
        name="outproj",
    )(attn, hm, x2d, mod3, norm_g, wout, wr_hi, wr_lo, br)


def _for_each_piece(length, pieces, fn):
    pos = jnp.int32(0)
    for size in pieces:
        take = length & size

        @pl.when(take != 0)
        def _(pos=pos, size=size):
            fn(pl.multiple_of(pos, RUN_ALIGN), size)

        pos = pos + take


def _for_each_chunk(tile, rows_ref, cdst_ref, fn):
    def body(c, carry):
        fn(pl.multiple_of(c * RUN_ALIGN, RUN_ALIGN),
           pl.multiple_of(cdst_ref[tile * CHUNKS_PER_TILE + c], RUN_ALIGN))
        return carry

    lax.fori_loop(0, rows_ref[tile] // RUN_ALIGN, body, 0)


def _dispatch_kernel(cdst_ref, rows_ref, tail_ref, tlen_ref, nu_ref, meta_ref, h2_ref, xs_ref,
                     perm_ref, zero_ref, sem, zsem):
    i = pl.program_id(0)
    rb = MOE_BLOCK_ROWS
    n_blocks = xs_ref.shape[0] // rb

    @pl.when(i == 0)
    def _():
        zero_ref[...] = jnp.zeros(zero_ref.shape, zero_ref.dtype)

        def zero_copy(row, size):
            return pltpu.make_async_copy(zero_ref.at[pl.ds(0, size)], xs_ref.at[pl.ds(row, size)], zsem)

        def fill(op):
            def per_expert(e, carry):
                start = tail_ref[e]
                _for_each_piece(tlen_ref[e], TAIL_PIECES,
                                lambda pos, size: op(zero_copy(pl.multiple_of(start + pos, RUN_ALIGN), size)))
                return carry

            lax.fori_loop(0, N_EXPERTS, per_expert, 0)

            def per_block(blk, carry):
                for part in range(rb // ZERO_ROWS):
                    op(zero_copy(pl.multiple_of(blk * rb + part * ZERO_ROWS, ZERO_ROWS), ZERO_ROWS))
                return carry

            lax.fori_loop(nu_ref[0], n_blocks, per_block, 0)

        fill(lambda cp: cp.start())
        fill(lambda cp: cp.wait())

    meta_t = meta_ref[...].T
    l0 = meta_t[META_L0:META_L0 + 1, :].astype(jnp.int32)
    l1 = meta_t[META_L1:META_L1 + 1, :].astype(jnp.int32)
    rpos = lax.broadcasted_iota(jnp.int32, (perm_ref.shape[1], meta_ref.shape[0]), 0)
    onehot = jnp.where((rpos == l0) | (rpos == l1), 1.0, 0.0).astype(BF16)
    slot = i % 2
    perm_ref[slot] = _dot(onehot, h2_ref[...]).astype(BF16)

    def run_copy(s, loc, glob, size):
        return pltpu.make_async_copy(perm_ref.at[s, pl.ds(loc, size)], xs_ref.at[pl.ds(glob, size)], sem.at[s])

    def wait_tile(tile, s):
        _for_each_piece(rows_ref[tile], TILE_PIECES, lambda pos, size: run_copy(s, 0, 0, size).wait())

    @pl.when(i > 0)
    def _():
        wait_tile(i - 1, 1 - slot)

    _for_each_chunk(i, rows_ref, cdst_ref, lambda loc, glob: run_copy(slot, loc, glob, RUN_ALIGN).start())

    @pl.when(i == pl.num_programs(0) - 1)
    def _():
        wait_tile(i, slot)


def _dispatch(plan, meta, h2):
    t, d = h2.shape
    tm = MOE_TILE
    grid_spec = pltpu.PrefetchScalarGridSpec(
        num_scalar_prefetch=5,
        grid=(t // tm,),
        in_specs=[pl.BlockSpec((tm, ROUTER_LANES), lambda i, *_: (i, 0)),
                  pl.BlockSpec((tm, d), lambda i, *_: (i, 0))],
        out_specs=pl.BlockSpec(memory_space=pl.ANY),
        scratch_shapes=[pltpu.VMEM((2, PERM_ROWS, d), BF16), pltpu.VMEM((ZERO_ROWS, d), BF16),
                        pltpu.SemaphoreType.DMA((2,)), pltpu.SemaphoreType.DMA(())],
    )
    return pl.pallas_call(
        _dispatch_kernel,
        grid_spec=grid_spec,
        out_shape=jax.ShapeDtypeStruct((plan["n_rows"], d), BF16),
        compiler_params=_params(("arbitrary",)),
        name="dispatch",
    )(plan["chunk_dst"], plan["tile_rows"], plan["tail_start"], plan["tail_len"], plan["n_used"], meta, h2)


def _expert_kernel(be_ref, nu_ref, xs_ref, w1_ref, w3_ref, w2_ref, y_ref, w1b_ref, w3b_ref, w2b_ref):
    i = pl.program_id(0)
    used = i < nu_ref[0]

    @pl.when(used & ((i == 0) | (be_ref[i] != be_ref[jnp.maximum(i, 1) - 1])))
    def _():
        w1b_ref[...] = w1_ref[0].astype(BF16)
        w3b_ref[...] = w3_ref[0].astype(BF16)
        w2b_ref[...] = w2_ref[0].astype(BF16)

    @pl.when(used)
    def _():
        x = xs_ref[...]
        a = _dot(x, w1b_ref[...])
        b = _dot(x, w3b_ref[...])
        y_ref[...] = _dot((_silu(a) * b).astype(BF16), w2b_ref[...]).astype(y_ref.dtype)

    @pl.when(jnp.logical_not(used))
    def _():
        y_ref[...] = jnp.zeros(y_ref.shape, y_ref.dtype)


def _experts(block_e, n_used, xs, w1, w3, w2):
    n_rows, d = xs.shape
    rb = MOE_BLOCK_ROWS
    de = w1.shape[2]
    used = lambda i, be, nu: jnp.minimum(i, nu[0] - 1)
    grid_spec = pltpu.PrefetchScalarGridSpec(
        num_scalar_prefetch=2,
        grid=(n_rows // rb,),
        in_specs=[pl.BlockSpec((rb, d), lambda i, be, nu: (used(i, be, nu), 0)),
                  pl.BlockSpec((1, d, de), lambda i, be, nu: (be[used(i, be, nu)], 0, 0)),
                  pl.BlockSpec((1, d, de), lambda i, be, nu: (be[used(i, be, nu)], 0, 0)),
                  pl.BlockSpec((1, de, d), lambda i, be, nu: (be[used(i, be, nu)], 0, 0))],
        out_specs=pl.BlockSpec((rb, d), lambda i, be, nu: (i, 0)),
        scratch_shapes=[pltpu.VMEM((d, de), BF16), pltpu.VMEM((d, de), BF16), pltpu.VMEM((de, d), BF16)],
    )
    return pl.pallas_call(
        _expert_kernel,
        grid_spec=grid_spec,
        out_shape=jax.ShapeDtypeStruct((n_rows, d), BF16),
        compiler_params=_params(("arbitrary",)),
        name="experts",
    )(block_e, n_used, xs, w1, w3, w2)


def _combine_kernel(final_norm, cdst_ref, rows_ref, x1_ref, meta_ref, mod_ref, g_ref, y_ref, o_ref, ybuf_ref, sem):
    tc, d = x1_ref.shape
    i = pl.program_id(0)

    slot = i % 2

    def run_copy(s, loc, glob, size):
        return pltpu.make_async_copy(y_ref.at[pl.ds(glob, size)], ybuf_ref.at[s, pl.ds(loc, size)], sem.at[s])

    def start_tile(tile, s):
        _for_each_chunk(tile, rows_ref, cdst_ref, lambda loc, glob: run_copy(s, loc, glob, RUN_ALIGN).start())

    @pl.when(i == 0)
    def _():
        ybuf_ref[...] = jnp.zeros(ybuf_ref.shape, ybuf_ref.dtype)
        start_tile(0, 0)

    @pl.when(i + 1 < pl.num_programs(0))
    def _():
        start_tile(i + 1, 1 - slot)

    _for_each_piece(rows_ref[i], TILE_PIECES, lambda pos, size: run_copy(slot, 0, 0, size).wait())

    meta = meta_ref[...]
    yb = ybuf_ref[slot]
    rpos = lax.broadcasted_iota(jnp.int32, (tc, ybuf_ref.shape[1]), 1)
    sel = jnp.zeros(rpos.shape, F32)
    for l_lane, w_lane in ((META_L0, META_W0), (META_L1, META_W1)):
        sel = jnp.where(rpos == meta[:, l_lane:l_lane + 1].astype(jnp.int32), meta[:, w_lane:w_lane + 1], sel)
    moe = _dot(sel.astype(BF16), yb)
    x2 = x1_ref[...] + mod_ref[0][:, 5 * d:6 * d] * moe
    if final_norm:
        x2 = x2 * lax.rsqrt(jnp.mean(x2 * x2, axis=-1, keepdims=True) + NORM_EPS) * g_ref[...]
    o_ref[...] = x2


def _combine(plan, x1, meta, mod3, norm_f_g, y, seq, final_norm):
    t, d = x1.shape
    tc = MOE_TILE
    per_batch = seq // tc
    gf = norm_f_g.reshape(1, d)
    grid_spec = pltpu.PrefetchScalarGridSpec(
        num_scalar_prefetch=2,
        grid=(t // tc,),
        in_specs=[pl.BlockSpec((tc, d), lambda i, *_: (i, 0)),
                  pl.BlockSpec((tc, ROUTER_LANES), lambda i, *_: (i, 0)),
                  pl.BlockSpec((1, 1, mod3.shape[2]), lambda i, *_: (i // per_batch, 0, 0)),
                  pl.BlockSpec(gf.shape, lambda i, *_: (0, 0)),
                  pl.BlockSpec(memory_space=pl.ANY)],
        out_specs=pl.BlockSpec((tc, d), lambda i, *_: (i, 0)),
        scratch_shapes=[pltpu.VMEM((2, PERM_ROWS, d), BF16), pltpu.SemaphoreType.DMA((2,))],
    )
    return pl.pallas_call(
        functools.partial(_combine_kernel, final_norm),
        grid_spec=grid_spec,
        out_shape=jax.ShapeDtypeStruct((t, d), F32),
        compiler_params=_params(("arbitrary",)),
        name="combine",
    )(plan["chunk_dst"], plan["tile_rows"], x1, meta, mod3, gf, y)


def _routing_plan(stats, n_tokens):
    rb = MOE_BLOCK_ROWS
    run_len = stats[:, STAT_LEN, :N_EXPERTS].astype(jnp.int32)
    run_loc = stats[:, STAT_START, :N_EXPERTS].astype(jnp.int32)
    n_tiles = run_len.shape[0]
    total = jnp.sum(run_len, axis=0)
    padded = ((total + rb - 1) // rb) * rb
    pad_end = jnp.cumsum(padded)
    pad_start = pad_end - padded
    run_glob = pad_start[None, :] + jnp.cumsum(run_len, axis=0) - run_len
    n_rows = n_tokens * TOP_K_EXPERTS + n_tiles * N_EXPERTS * (RUN_ALIGN - 1) + N_EXPERTS * (rb - 1)
    n_rows = ((n_rows + rb - 1) // rb) * rb
    block_row = jnp.arange(n_rows // rb, dtype=jnp.int32) * rb
    block_e = jnp.minimum(jnp.sum((pad_end[None, :] <= block_row[:, None]).astype(jnp.int32), axis=1),
                          N_EXPERTS - 1)
    local_row = jnp.arange(CHUNKS_PER_TILE, dtype=jnp.int32) * RUN_ALIGN
    chunk_e = jnp.minimum(jnp.sum(((run_loc + run_len)[:, None, :] <= local_row[None, :, None]).astype(jnp.int32),
                                  axis=-1), N_EXPERTS - 1)
    shift = jnp.sum(jnp.where(chunk_e[..., None] == jnp.arange(N_EXPERTS, dtype=jnp.int32),
                              (run_glob - run_loc)[:, None, :], 0), axis=-1)
    chunk_dst = shift + local_row[None, :]
    return dict(chunk_dst=chunk_dst.reshape(-1), tile_rows=jnp.sum(run_len, axis=1),
                tail_start=pad_start + total, tail_len=padded - total,
                n_used=(pad_end[-1:] // rb).astype(jnp.int32), block_e=block_e.astype(jnp.int32), n_rows=n_rows)


def kernel(x, c, positions, w_ada, b_ada, norm1_g, w_in, b_gate, conv_w, conv_b, attn_out_g, mlstm_out_g,
           w_out, norm2_g, w_rg, b_rg, w_re, b_re, w1, w3, w2, norm_f_g):
    b, s, d = x.shape
    depth = w_ada.shape[0]
    assert d == D_MODEL and s % MOE_TILE == 0 and s % (KV_CHUNK * MOBA_BLOCK) == 0
    x2d = x.reshape(b * s, d)
    for l in range(depth):
        mod3 = _mod(c, w_ada[l], b_ada[l]).reshape(b, 1, 6 * d)
        q, k, vt, qkm, vm, om, gates = _inproj(x2d.reshape(b, s, d), positions, mod3,
                                              norm1_g[l].reshape(1, d), w_in[l])
        attn = _moba(q, k, vt, attn_out_g[l].reshape(1, D_ATT))
        hm = _mlstm(qkm, vm, om, gates, conv_w[l], conv_b[l], b_gate[l], mlstm_out_g[l])
        x1, h2, meta, stats = _outproj(attn.reshape(b * s, D_ATT), hm.reshape(b * s, D_MLSTM), x2d, mod3,
                                       norm2_g[l].reshape(1, d), w_out[l], w_rg[l], b_rg[l], w_re[l],
                                       b_re[l], s)
        plan = _routing_plan(stats, b * s)
        xs = _dispatch(plan, meta, h2)
        y = _experts(plan["block_e"], plan["n_used"], xs, w1[l], w3[l], w2[l])
        x2d = _combine(plan, x1, meta, mod3, norm_f_g, y, s, final_norm=(l == depth - 1))
    return x2d.reshape(b, s, d)
```

```python
import functools

import jax
import jax.numpy as jnp
from jax import lax
from jax.experimental import pallas as pl
from jax.experimental.pallas import tpu as pltpu

F32 = jnp.float32
BF16 = jnp.bfloat16

D_MODEL = 1024
D_ATT = 512
ATT_HEADS = 8
ATT_HEAD_DIM = 64
D_MLSTM = 512
MLSTM_HEADS = 4
MLSTM_HEAD_DIM = 128
MOBA_BLOCK = 256
MOBA_TOPK = 3
ROPE_THETA = 10000.0
MLSTM_CHUNK = 128
MLSTM_CHUNKS_PER_STEP = 4
CONV_WIDTH = 4
N_GROUPS = 4
EXPERTS_PER_GROUP = 8
N_EXPERTS = N_GROUPS * EXPERTS_PER_GROUP
TOP_K_EXPERTS = 2
D_EXPERT = 512
MOE_BLOCK_ROWS = 512
NORM_EPS = 1e-6
NEG_INF = -1e30
LOG2_E = 1.4426950408889634
KV_CHUNK = 4
MOBA_GROUP_LANES = 512

LANES = 128
SUBLANES = 8
VMEM_LIMIT_BYTES = 56 * 1024 * 1024

INPROJ_TILE = 512
MOE_TILE = 512
RUN_ALIGN = 2 * SUBLANES
PERM_ROWS = TOP_K_EXPERTS * MOE_TILE + N_EXPERTS * RUN_ALIGN
CHUNKS_PER_TILE = PERM_ROWS // RUN_ALIGN
TAIL_PIECES = tuple(RUN_ALIGN << p for p in reversed(range(5)))
assert TAIL_PIECES[0] * 2 == MOE_BLOCK_ROWS
TILE_PIECES = tuple(RUN_ALIGN << p for p in reversed(range(7)))
ZERO_ROWS = TAIL_PIECES[0]
GATE_LANES = 2 * LANES
ROUTER_LANES = LANES
GROUP_LANE0 = N_EXPERTS


def _dot(a, b):
    return jnp.dot(a, b, preferred_element_type=F32)


def _dot_nt(a, b):
    return lax.dot_general(a, b, (((1,), (1,)), ((), ())), preferred_element_type=F32)


def _dot_tn(a, b):
    return lax.dot_general(a, b, (((0,), (0,)), ((), ())), preferred_element_type=F32)


def _split3(x):
    a = x.astype(BF16)
    r = x - a.astype(F32)
    b = r.astype(BF16)
    c = (r - b.astype(F32)).astype(BF16)
    return a, b, c


def _silu(x):
    return x * jax.nn.sigmoid(x)


def _log_sigmoid(x):
    return jnp.minimum(x, 0.0) - jnp.log1p(jnp.exp(-jnp.abs(x)))


def _params(semantics, vmem=VMEM_LIMIT_BYTES):
    return pltpu.CompilerParams(dimension_semantics=semantics, vmem_limit_bytes=vmem)


def _mod_kernel(c_ref, w_ref, b_ref, o_ref):
    sc = _silu(c_ref[...])
    o_ref[...] = jnp.dot(sc, w_ref[...], precision=lax.Precision.HIGHEST,
                         preferred_element_type=F32) + b_ref[...]


def _mod(c, w_ada, b_ada):
    b, d = c.shape
    n = w_ada.shape[1]
    tn = D_MODEL
    return pl.pallas_call(
        _mod_kernel,
        grid=(n // tn,),
        in_specs=[pl.BlockSpec((b, d), lambda i: (0, 0)),
                  pl.BlockSpec((d, tn), lambda i: (0, i)),
                  pl.BlockSpec((1, tn), lambda i: (0, i))],
        out_specs=pl.BlockSpec((b, tn), lambda i: (0, i)),
        out_shape=jax.ShapeDtypeStruct((b, n), F32),
        compiler_params=_params(("arbitrary",)),
        name="mod",
    )(c, w_ada, b_ada.reshape(1, n))


def _rms_modulate(x, g, shift, scale):
    y = x * lax.rsqrt(jnp.mean(x * x, axis=-1, keepdims=True) + NORM_EPS)
    return (y * g) * (1.0 + scale) + shift


def _rope(t, cos, sin, first_half):
    outs = []
    for c in range(t.shape[1] // LANES):
        tc = t[:, c * LANES:(c + 1) * LANES]
        rot = jnp.where(first_half, -pltpu.roll(tc, LANES - ATT_HEAD_DIM // 2, 1),
                        pltpu.roll(tc, ATT_HEAD_DIM // 2, 1))
        outs.append(tc * cos + rot * sin)
    return jnp.concatenate(outs, axis=1)


def _inproj_kernel(x_ref, pos_ref, mod_ref, g_ref, invf_ref, wq_ref, wk_ref, wvt_ref, wqk_ref,
                   wv_ref, wo_ref, wg_ref, q_ref, k_ref, vt_ref, qkm_ref, vm_ref, om_ref, gt_ref):
    d = x_ref.shape[2]
    x = x_ref[0]
    mod = mod_ref[0]
    h = _rms_modulate(x, g_ref[...], mod[:, 0:d], mod[:, d:2 * d])
    hb = h.astype(BF16)

    ang = pos_ref[0].astype(F32) * invf_ref[...]
    cos = jnp.cos(ang)
    sin = jnp.sin(ang)
    lane = lax.broadcasted_iota(jnp.int32, cos.shape, 1)
    first_half = (lane & (ATT_HEAD_DIM // 2)) == 0

    q = _rope(_dot(hb, wq_ref[...]), cos, sin, first_half)
    q_ref[0] = (q * (ATT_HEAD_DIM ** -0.5 * LOG2_E)).astype(BF16)
    k_ref[0] = _rope(_dot(hb, wk_ref[...]), cos, sin, first_half).astype(BF16)
    vt = _dot_nt(wvt_ref[...], hb).astype(BF16)
    for blk_i in range(vt_ref.shape[1]):
        vt_ref[0, blk_i] = vt[:, blk_i * MOBA_BLOCK:(blk_i + 1) * MOBA_BLOCK]
    qkm_ref[0] = _dot(hb, wqk_ref[...])
    vm_ref[0] = _dot(hb, wv_ref[...])
    om_ref[0] = _dot(hb, wo_ref[...])
    gt_ref[0] = _dot(hb, wg_ref[...])


def _inproj(x, positions, mod3, norm_g, w_in):
    b, s, d = x.shape
    tm = INPROJ_TILE
    per_tile = tm // MOBA_BLOCK
    nb = s // MOBA_BLOCK
    o = [0, D_ATT, 2 * D_ATT, 3 * D_ATT, 3 * D_ATT + 2 * D_MLSTM, 3 * D_ATT + 3 * D_MLSTM,
         3 * D_ATT + 4 * D_MLSTM, 3 * D_ATT + 4 * D_MLSTM + 2 * MLSTM_HEADS]
    wb = w_in.astype(BF16)
    wq, wk, wv_a, wqk, wv, wo, wg = (wb[:, o[i]:o[i + 1]] for i in range(7))
    wvt = wv_a.T
    lane_pad = ((0, 0), (0, LANES - MLSTM_HEADS))
    wg = jnp.concatenate([jnp.pad(wg[:, :MLSTM_HEADS], lane_pad), jnp.pad(wg[:, MLSTM_HEADS:], lane_pad)], axis=1)
    half = ATT_HEAD_DIM // 2
    inv_freq = ROPE_THETA ** (-jnp.arange(half, dtype=F32) / half)
    invf = jnp.tile(inv_freq, LANES // half).reshape(1, LANES)

    full = lambda a: pl.BlockSpec(a.shape, lambda bi, i: (0,) * a.ndim)
    tok = lambda w: pl.BlockSpec((1, tm, w), lambda bi, i: (bi, i, 0))
    out_shape = [jax.ShapeDtypeStruct((b, s, D_ATT), BF16),
                 jax.ShapeDtypeStruct((b, s, D_ATT), BF16),
                 jax.ShapeDtypeStruct((b, nb, D_ATT, MOBA_BLOCK), BF16),
                 jax.ShapeDtypeStruct((b, s, 2 * D_MLSTM), F32),
                 jax.ShapeDtypeStruct((b, s, D_MLSTM), F32),
                 jax.ShapeDtypeStruct((b, s, D_MLSTM), F32),
                 jax.ShapeDtypeStruct((b, s, GATE_LANES), F32)]
    return pl.pallas_call(
        _inproj_kernel,
        grid=(b, s // tm),
        in_specs=[tok(d), tok(1),
                  pl.BlockSpec((1, 1, mod3.shape[2]), lambda bi, i: (bi, 0, 0)),
                  full(norm_g), full(invf), full(wq), full(wk), full(wvt), full(wqk), full(wv),
                  full(wo), full(wg)],
        out_specs=[tok(D_ATT), tok(D_ATT),
                   pl.BlockSpec((1, per_tile, D_ATT, MOBA_BLOCK), lambda bi, i: (bi, i, 0, 0)),
                   tok(2 * D_MLSTM), tok(D_MLSTM), tok(D_MLSTM), tok(GATE_LANES)],
        out_shape=out_shape,
        compiler_params=_params(("arbitrary", "arbitrary")),
        name="inproj",
    )(x, positions.reshape(b, s, 1), mod3, norm_g, invf, wq, wk, wvt, wqk, wv, wo, wg)


def _moba_kernel(q_ref, k_ref, vt_ref, g_ref, o_ref, kmean_ref, sel_ref, st_ref, qh_ref, acc_ref):
    blk = MOBA_BLOCK
    nb = k_ref.shape[1] // blk
    j = pl.program_id(2)
    heads = q_ref.shape[2] // ATT_HEAD_DIM
    hpl = LANES // ATT_HEAD_DIM

    @pl.when(j == 0)
    def _():
        for n in range(nb):
            kb = k_ref[0, n * blk:(n + 1) * blk, :].astype(F32)
            kmean_ref[n:n + 1, :] = jnp.mean(kb, axis=0, keepdims=True)

    lane = lax.broadcasted_iota(jnp.int32, (blk, LANES), 1)
    blk_id = lax.broadcasted_iota(jnp.int32, (nb, blk), 0)
    km_hi = kmean_ref[...].astype(BF16)
    km_lo = (kmean_ref[...] - km_hi.astype(F32)).astype(BF16)
    past = blk_id < j
    tile = lambda a, hh: a[:, (hh // hpl) * LANES:(hh // hpl + 1) * LANES]

    for hh in range(heads):
        q = tile(q_ref[0], hh)
        in_head = (lane >= (hh % hpl) * ATT_HEAD_DIM) & (lane < (hh % hpl + 1) * ATT_HEAD_DIM)
        qh_ref[hh] = jnp.where(in_head, q, jnp.zeros_like(q))
        gate = _dot_nt(tile(km_hi, hh), qh_ref[hh]) + _dot_nt(tile(km_lo, hh), qh_ref[hh])
        g = jnp.where(past, gate, NEG_INF)
        sel = jnp.zeros(g.shape, F32)
        for _ in range(min(MOBA_TOPK, nb)):
            top = jnp.max(g, axis=0, keepdims=True)
            idx = jnp.min(jnp.where(g == top, blk_id, nb), axis=0, keepdims=True)
            pick = blk_id == idx
            sel = jnp.where(pick, 1.0, sel)
            g = jnp.where(pick, -jnp.inf, g)
        sel_ref[hh, 0:nb, :] = jnp.where((sel > 0.0) & past, 1.0, 0.0)
        sel_ref[hh, nb:nb + SUBLANES, :] = jnp.zeros((SUBLANES, blk), F32)

    kpos = lax.broadcasted_iota(jnp.int32, (blk, blk), 0)
    qpos = lax.broadcasted_iota(jnp.int32, (blk, blk), 1)
    causal_bias = jnp.where(kpos <= qpos, 0.0, NEG_INF)

    def slab_block(t, i):
        b_i = j - KV_CHUNK * t - (KV_CHUNK - 1 - i)
        return jnp.maximum(b_i, 0), jnp.where(b_i >= 0, b_i, nb)

    def score(t, slot, own_chunk):
        tops = [None] * heads
        slab_max = [[None] * KV_CHUNK for _ in range(heads)]
        for i in range(KV_CHUNK):
            src, row = slab_block(t, i)
            kb = k_ref[0, pl.ds(pl.multiple_of(src * blk, blk), blk), :]
            for hh in range(heads):
                st = _dot_nt(tile(kb, hh), qh_ref[hh])
                if own_chunk and i == KV_CHUNK - 1:
                    st = st + causal_bias
                    cmax = jnp.max(st, axis=0, keepdims=True)
                    smax = cmax
                else:
                    cmax = jnp.max(st, axis=0, keepdims=True)
                    smax = cmax + (sel_ref[hh, pl.ds(row, 1), :] - 1.0) * (-NEG_INF)
                st_ref[slot, hh, i * blk:(i + 1) * blk, :] = st
                slab_max[hh][i] = cmax
                tops[hh] = smax if tops[hh] is None else jnp.maximum(tops[hh], smax)
        return tuple(tops), tuple(tuple(r) for r in slab_max)

    def accumulate(t, slot, state, maxes):
        tops, slab_max = maxes
        new = []
        for hh in range(heads):
            m, l = state[2 * hh:2 * hh + 2]
            m_new = jnp.maximum(m, tops[hh])
            alpha = jnp.exp2(m - m_new)
            l = alpha * l
            acc = alpha * acc_ref[hh]
            for i in range(KV_CHUNK):
                src, row = slab_block(t, i)
                keep = sel_ref[hh, pl.ds(row, 1), :]
                if i == KV_CHUNK - 1:
                    keep = jnp.where(t == 0, 1.0, keep)
                p = jnp.exp2(st_ref[slot, hh, i * blk:(i + 1) * blk, :] - jnp.maximum(m_new, slab_max[hh][i]))
                l = l + keep * jnp.sum(p, axis=0, keepdims=True)
                v_h = vt_ref[0, src][hh * ATT_HEAD_DIM:(hh + 1) * ATT_HEAD_DIM, :]
                acc = acc + keep * _dot(v_h, p.astype(BF16))
            acc_ref[hh] = acc
            new += [m_new, l]
        return tuple(new)

    state = ()
    for hh in range(heads):
        state += (jnp.full((1, blk), NEG_INF, F32), jnp.zeros((1, blk), F32))
        acc_ref[hh] = jnp.zeros((ATT_HEAD_DIM, blk), F32)
    last = j // KV_CHUNK

    def stage_pair(t, slot, state, maxes):
        nxt = score(t + 1, 1 - slot, False)
        return accumulate(t, slot, state, maxes), nxt

    def body(t, carry):
        return lax.cond(t % 2 == 0, functools.partial(stage_pair, t, 0), functools.partial(stage_pair, t, 1),
                        *carry)

    state, maxes = lax.fori_loop(0, last, body, (state, score(0, 0, True)))
    state = lax.cond(last % 2 == 0, functools.partial(accumulate, last, 0), functools.partial(accumulate, last, 1),
                     state, maxes)

    outs = []
    for hh in range(heads):
        o_h = acc_ref[hh] / state[2 * hh + 1]
        ms = jnp.mean(o_h * o_h, axis=0, keepdims=True)
        outs.append(o_h * lax.rsqrt(ms + NORM_EPS))
    o_ref[0] = jnp.concatenate(outs, axis=0).T * g_ref[...]


def _moba(q, k, vt, g_att):
    b, s, w = q.shape
    blk = MOBA_BLOCK
    nb = s // blk
    gl = MOBA_GROUP_LANES
    groups = w // gl
    heads = gl // ATT_HEAD_DIM
    return pl.pallas_call(
        _moba_kernel,
        grid=(b, groups, nb),
        in_specs=[pl.BlockSpec((1, blk, gl), lambda bi, p, j: (bi, j, p)),
                  pl.BlockSpec((1, s, gl), lambda bi, p, j: (bi, 0, p)),
                  pl.BlockSpec((1, nb, gl, blk), lambda bi, p, j: (bi, 0, p, 0)),
                  pl.BlockSpec((1, gl), lambda bi, p, j: (0, p))],
        out_specs=pl.BlockSpec((1, blk, gl), lambda bi, p, j: (bi, j, p)),
        out_shape=jax.ShapeDtypeStruct((b, s, w), F32),
        scratch_shapes=[pltpu.VMEM((nb, gl), F32),
                        pltpu.VMEM((heads, nb + SUBLANES, blk), F32),
                        pltpu.VMEM((2, heads, KV_CHUNK * blk, blk), F32),
                        pltpu.VMEM((heads, blk, LANES), BF16),
                        pltpu.VMEM((heads, ATT_HEAD_DIM, blk), F32)],
        compiler_params=_params(("arbitrary", "arbitrary", "arbitrary")),
        name="moba",
    )(q, k, vt, g_att)


def _mlstm_kernel(qkm_ref, vm_ref, om_ref, gt_ref, cw_ref, cb_ref, bg_ref, gm_ref, bsel_ref, o_ref,
                  uext_ref, c_ref, n_ref, m_ref):
    L = MLSTM_CHUNK
    dh = MLSTM_HEAD_DIM
    nh = MLSTM_HEADS
    assert L == dh == LANES and 2 * nh == SUBLANES
    c = pl.program_id(1)

    @pl.when(c == 0)
    def _():
        uext_ref[0:SUBLANES, :] = jnp.zeros((SUBLANES, uext_ref.shape[1]), F32)
        c_ref[...] = jnp.zeros(c_ref.shape, F32)
        n_ref[...] = jnp.zeros(n_ref.shape, F32)
        m_ref[...] = jnp.zeros(m_ref.shape, F32)

    for sub in range(qkm_ref.shape[1] // L):
        _mlstm_chunk(slice(sub * L, (sub + 1) * L), qkm_ref, vm_ref, om_ref, gt_ref, cw_ref, cb_ref, bg_ref, gm_ref,
                     bsel_ref, o_ref, uext_ref, c_ref, n_ref, m_ref)


def _mlstm_chunk(rows, qkm_ref, vm_ref, om_ref, gt_ref, cw_ref, cb_ref, bg_ref, gm_ref, bsel_ref, o_ref,
                 uext_ref, c_ref, n_ref, m_ref):
    L = MLSTM_CHUNK
    dh = MLSTM_HEAD_DIM
    nh = MLSTM_HEADS

    u = qkm_ref[0, rows, :]
    uext_ref[SUBLANES:SUBLANES + L, :] = u
    cw = cw_ref[...]
    conv = cb_ref[...] + cw[CONV_WIDTH - 1:CONV_WIDTH, :] * u
    for dlt in range(1, CONV_WIDTH):
        conv = conv + cw[CONV_WIDTH - 1 - dlt:CONV_WIDTH - dlt, :] * uext_ref[SUBLANES - dlt:SUBLANES - dlt + L, :]
    uext_ref[0:SUBLANES, :] = u[L - SUBLANES:L, :]
    act = _silu(conv)

    gates = gt_ref[0, rows, :] + bg_ref[...]
    ig = gates[:, 0:LANES]
    tpos = lax.broadcasted_iota(jnp.int32, (L, L), 0)
    spos = lax.broadcasted_iota(jnp.int32, (L, L), 1)
    causal = spos <= tpos
    tril = jnp.where(causal, 1.0, 0.0).astype(BF16)
    cum = sum(_dot(tril, part) for part in _split3(_log_sigmoid(gates[:, LANES:])))
    a_tot = cum[L - 1:L, :]
    m_prev = m_ref[0:1, :]
    resid = ig - cum
    pmax = resid
    shift = 1
    while shift < L:
        pmax = jnp.maximum(pmax, jnp.where(tpos >= shift, pltpu.roll(pmax, shift, 0), -jnp.inf))
        shift *= 2
    inter = cum + m_prev
    m_t = jnp.maximum(inter, cum + pmax)
    g_loc = a_tot - cum + ig
    m_loc = jnp.max(g_loc, axis=0, keepdims=True)
    w_loc = jnp.exp(g_loc - m_loc)
    head_lane = spos < nh
    packed = jnp.where(head_lane, cum - m_t, 0.0)
    for i, qty in enumerate((inter - m_t, -m_t, g_loc - m_loc), start=1):
        packed = packed + pltpu.roll(jnp.where(head_lane, qty, 0.0), SUBLANES * i, 1)
    per_t_cols = jnp.concatenate(_split3(packed), axis=1)
    per_t_rows = (jnp.where(head_lane, resid, 0.0) + pltpu.roll(jnp.where(head_lane, w_loc, 0.0), SUBLANES, 1)).T

    m_new = jnp.maximum(a_tot + m_prev, m_loc)
    s_prev = jnp.exp(a_tot + m_prev - m_new)
    s_loc = jnp.exp(m_loc - m_new)
    m_ref[0:1, :] = m_new

    vm = vm_ref[0, rows, :]
    om = om_ref[0, rows, :]
    ones_b = jnp.ones((L, dh), BF16)
    c_all = [c_ref[h] for h in range(nh)]
    n_all = n_ref[...]
    new_state = []
    for h in range(nh):
        sl = slice(h * dh, (h + 1) * dh)
        q = act[:, sl]
        k = act[:, nh * dh + h * dh:nh * dh + (h + 1) * dh] * (dh ** -0.5)
        v = vm[:, sl]
        qb, kb, vb = q.astype(BF16), k.astype(BF16), v.astype(BF16)
        c_prev = c_all[h]
        n_prev = n_all[h:h + 1, :]
        bcast = _dot(per_t_cols, bsel_ref[h])
        decay_b = bcast[:, 0:dh]
        w_inter_b, floor_b, w_loc_b = (jnp.exp(bcast[:, i * dh:(i + 1) * dh]) for i in range(1, 4))

        s_qk = _dot_nt(qb, kb) * jnp.where(causal, jnp.exp(decay_b + per_t_rows[h:h + 1, :]), 0.0)
        intra = _dot(s_qk.astype(BF16), jnp.concatenate([vb, ones_b], axis=1))
        state_rows = jnp.concatenate([c_prev, jnp.broadcast_to(n_prev, (dh, dh))], axis=0).astype(BF16)
        carried = _dot_nt(qb, state_rows)
        num = intra[:, :dh] + w_inter_b * carried[:, :dh]
        den = intra[:, dh:] + w_inter_b * carried[:, dh:]
        hout = num / jnp.maximum(jnp.abs(den), floor_b)

        mean_sq = _dot((hout * hout).astype(BF16), ones_b) * (1.0 / dh)
        hn = hout * lax.rsqrt(mean_sq + NORM_EPS) * gm_ref[:, sl]
        o_ref[0, rows, sl] = hn * jax.nn.sigmoid(om[:, sl])

        c_loc = _dot_tn((v * w_loc_b).astype(BF16), kb)
        w_row = per_t_rows[SUBLANES + h:SUBLANES + h + 1, :]
        n_loc = _dot(jnp.broadcast_to(w_row, (SUBLANES, L)).astype(BF16), kb)[0:1, :]
        new_state.append((s_prev[:, h:h + 1] * c_prev + s_loc[:, h:h + 1] * c_loc,
                          s_prev[:, h:h + 1] * n_prev + s_loc[:, h:h + 1] * n_loc))
    for h, (c_new, n_new) in enumerate(new_state):
        c_ref[h] = c_new
        n_ref[h:h + 1, :] = n_new


def _mlstm(qkm, vm, om, gates, conv_w, conv_b, b_gate, g_m):
    b, s, _ = qkm.shape
    L = MLSTM_CHUNK
    step_rows = MLSTM_CHUNKS_PER_STEP * L
    nc = s // step_rows
    lane_pad = ((0, 0), (0, LANES - MLSTM_HEADS))
    bg = b_gate.reshape(1, -1)
    bg = jnp.concatenate([jnp.pad(bg[:, :MLSTM_HEADS], lane_pad), jnp.pad(bg[:, MLSTM_HEADS:], lane_pad)], axis=1)
    tok = lambda w: pl.BlockSpec((1, step_rows, w), lambda bi, c: (bi, c, 0))
    full = lambda a: pl.BlockSpec(a.shape, lambda bi, c: (0,) * a.ndim)
    cb = conv_b.reshape(1, -1)
    gm = g_m.reshape(1, -1)
    n_qty = 4
    row = jnp.arange(3 * LANES, dtype=jnp.int32) % LANES
    col_qty = jnp.arange(n_qty * MLSTM_HEAD_DIM, dtype=jnp.int32) // MLSTM_HEAD_DIM
    bsel = jnp.stack([
        ((row // SUBLANES)[:, None] == col_qty[None, :]) & (row % SUBLANES == h)[:, None]
        for h in range(MLSTM_HEADS)]).astype(BF16)
    return pl.pallas_call(
        _mlstm_kernel,
        grid=(b, nc),
        in_specs=[tok(2 * D_MLSTM), tok(D_MLSTM), tok(D_MLSTM), tok(GATE_LANES),
                  full(conv_w), full(cb), full(bg), full(gm), full(bsel)],
        out_specs=tok(D_MLSTM),
        out_shape=jax.ShapeDtypeStruct((b, s, D_MLSTM), F32),
        scratch_shapes=[pltpu.VMEM((SUBLANES + L, 2 * D_MLSTM), F32),
                        pltpu.VMEM((MLSTM_HEADS, MLSTM_HEAD_DIM, MLSTM_HEAD_DIM), F32),
                        pltpu.VMEM((SUBLANES, MLSTM_HEAD_DIM), F32),
                        pltpu.VMEM((SUBLANES, LANES), F32)],
        compiler_params=_params(("arbitrary", "arbitrary")),
        name="mlstm",
    )(qkm, vm, om, gates, conv_w, cb, bg, gm, bsel)


META_E0, META_E1, META_W0, META_W1, META_L0, META_L1 = range(6)
STAT_LEN, STAT_START = 0, 1


def _outproj_kernel(attn_ref, hm_ref, x_ref, mod_ref, g_ref, wout_ref, wr_hi_ref, wr_lo_ref, br_ref,
                    x1_ref, h2_ref, meta_ref, stat_ref):
    d = x_ref.shape[1]
    tm = x_ref.shape[0]

    mod = mod_ref[0]
    y = (_dot(attn_ref[...].astype(BF16), wout_ref[0:D_ATT, :])
         + _dot(hm_ref[...].astype(BF16), wout_ref[D_ATT:, :]))
    x1 = x_ref[...] + mod[:, 2 * d:3 * d] * y
    x1_ref[...] = x1
    h2 = _rms_modulate(x1, g_ref[...], mod[:, 3 * d:4 * d], mod[:, 4 * d:5 * d])
    h2_ref[...] = h2.astype(BF16)

    h_hi = h2.astype(BF16)
    h_lo = (h2 - h_hi.astype(F32)).astype(BF16)
    logit = (_dot(h_hi, wr_hi_ref[...]) + _dot(h_lo, wr_hi_ref[...]) + _dot(h_hi, wr_lo_ref[...])
             + br_ref[...])
    lane = lax.broadcasted_iota(jnp.int32, logit.shape, 1)
    big = jnp.int32(ROUTER_LANES)

    is_g = (lane >= GROUP_LANE0) & (lane < GROUP_LANE0 + N_GROUPS)
    gmax = jnp.max(jnp.where(is_g, logit, -jnp.inf), axis=1, keepdims=True)
    gsum = jnp.sum(jnp.where(is_g, jnp.exp(logit - gmax), 0.0), axis=1, keepdims=True)
    g_w = 1.0 / gsum
    g_idx = jnp.min(jnp.where(is_g & (logit == gmax), lane, big), axis=1, keepdims=True) - GROUP_LANE0

    in_grp = (lane < N_EXPERTS) & ((lane // EXPERTS_PER_GROUP) == g_idx)
    emax = jnp.max(jnp.where(in_grp, logit, -jnp.inf), axis=1, keepdims=True)
    esum = jnp.sum(jnp.where(in_grp, jnp.exp(logit - emax), 0.0), axis=1, keepdims=True)
    e0 = jnp.min(jnp.where(in_grp & (logit == emax), lane, big), axis=1, keepdims=True)
    rest = in_grp & (lane != e0)
    e2max = jnp.max(jnp.where(rest, logit, -jnp.inf), axis=1, keepdims=True)
    e1 = jnp.min(jnp.where(rest & (logit == e2max), lane, big), axis=1, keepdims=True)
    p0 = 1.0 / esum
    p1 = jnp.exp(e2max - emax) / esum
    w0 = g_w * p0 / (p0 + p1)
    w1 = g_w * p1 / (p0 + p1)

    memb = jnp.where((lane == e0) | (lane == e1), 1.0, 0.0)
    tpos = lax.broadcasted_iota(jnp.int32, (tm, tm), 0)
    spos = lax.broadcasted_iota(jnp.int32, (tm, tm), 1)
    before = jnp.where(spos < tpos, 1.0, 0.0).astype(BF16)
    earlier = _dot(before, memb.astype(BF16))
    count = jnp.sum(memb, axis=0, keepdims=True).astype(jnp.int32)
    run_len = (((count + (RUN_ALIGN - 1)) // RUN_ALIGN) * RUN_ALIGN).astype(F32)
    epos = lax.broadcasted_iota(jnp.int32, (ROUTER_LANES, ROUTER_LANES), 0)
    fpos = lax.broadcasted_iota(jnp.int32, (ROUTER_LANES, ROUTER_LANES), 1)
    lower_e = jnp.where(epos < fpos, 1.0, 0.0).astype(BF16)
    run_start = _dot(jnp.broadcast_to(run_len, (SUBLANES, ROUTER_LANES)).astype(BF16), lower_e)[0:1, :]
    row = run_start + earlier
    l0 = jnp.sum(jnp.where(lane == e0, row, 0.0), axis=1, keepdims=True)
    l1 = jnp.sum(jnp.where(lane == e1, row, 0.0), axis=1, keepdims=True)

    meta = jnp.zeros(logit.shape, F32)
    for slot, val in ((META_E0, e0.astype(F32)), (META_E1, e1.astype(F32)), (META_W0, w0), (META_W1, w1),
                      (META_L0, l0), (META_L1, l1)):
        meta = jnp.where(lane == slot, val, meta)
    meta_ref[...] = meta
    srow = lax.broadcasted_iota(jnp.int32, (SUBLANES, ROUTER_LANES), 0)
    stat_ref[0] = jnp.where(srow == STAT_LEN, run_len, jnp.where(srow == STAT_START, run_start, 0.0))


def _outproj(attn, hm, x2d, mod3, norm_g, w_out, w_rg, b_rg, w_re, b_re, seq):
    t, d = x2d.shape
    tm = MOE_TILE
    per_batch = seq // tm
    wr = jnp.pad(jnp.concatenate([w_re, w_rg], axis=1), ((0, 0), (0, ROUTER_LANES - N_EXPERTS - N_GROUPS)))
    br = jnp.pad(jnp.concatenate([b_re, b_rg]).reshape(1, -1), ((0, 0), (0, ROUTER_LANES - N_EXPERTS - N_GROUPS)))
    wr_hi = wr.astype(BF16)
    wr_lo = (wr - wr_hi.astype(F32)).astype(BF16)
    wout = w_out.astype(BF16)
    tok = lambda w: pl.BlockSpec((tm, w), lambda i: (i, 0))
    full = lambda a: pl.BlockSpec(a.shape, lambda i: (0,) * a.ndim)
    return pl.pallas_call(
        _outproj_kernel,
        grid=(t // tm,),
        in_specs=[tok(D_ATT), tok(D_MLSTM), tok(d),
                  pl.BlockSpec((1, 1, mod3.shape[2]), lambda i: (i // per_batch, 0, 0)),
                  full(norm_g), full(wout), full(wr_hi), full(wr_lo), full(br)],
        out_specs=[tok(d), tok(d), tok(ROUTER_LANES),
                   pl.BlockSpec((1, SUBLANES, ROUTER_LANES), lambda i: (i, 0, 0))],
        out_shape=[jax.ShapeDtypeStruct((t, d), F32), jax.ShapeDtypeStruct((t, d), BF16),
                   jax.ShapeDtypeStruct((t, ROUTER_LANES), F32),
                   jax.ShapeDtypeStruct((t // tm, SUBLANES, ROUTER_LANES), F32)],
        compiler_params=_params(("arbitrary",)),
        name="outproj",
    )(attn, hm, x2d, mod3, norm_g, wout, wr_hi, wr_lo, br)


def _for_each_piece(length, pieces, fn):
    pos = jnp.int32(0)
    for size in pieces:
        take = length & size

        @pl.when(take != 0)
        def _(pos=pos, size=size):
            fn(pl.multiple_of(pos, RUN_ALIGN), size)

        pos = pos + take


def _for_each_chunk(tile, rows_ref, cdst_ref, fn):
    def body(c, carry):
        fn(pl.multiple_of(c * RUN_ALIGN, RUN_ALIGN),
           pl.multiple_of(cdst_ref[tile * CHUNKS_PER_TILE + c], RUN_ALIGN))
        return carry

    lax.fori_loop(0, rows_ref[tile] // RUN_ALIGN, body, 0)


def _dispatch_kernel(cdst_ref, rows_ref, tail_ref, tlen_ref, nu_ref, meta_ref, h2_ref, xs_ref,
                     perm_ref, zero_ref, sem, zsem):
    i = pl.program_id(0)
    rb = MOE_BLOCK_ROWS
    n_blocks = xs_ref.shape[0] // rb

    @pl.when(i == 0)
    def _():
        zero_ref[...] = jnp.zeros(zero_ref.shape, zero_ref.dtype)

        def zero_copy(row, size):
            return pltpu.make_async_copy(zero_ref.at[pl.ds(0, size)], xs_ref.at[pl.ds(row, size)], zsem)

        def fill(op):
            def per_expert(e, carry):
                start = tail_ref[e]
                _for_each_piece(tlen_ref[e], TAIL_PIECES,
                                lambda pos, size: op(zero_copy(pl.multiple_of(start + pos, RUN_ALIGN), size)))
                return carry

            lax.fori_loop(0, N_EXPERTS, per_expert, 0)

            def per_block(blk, carry):
                for part in range(rb // ZERO_ROWS):
                    op(zero_copy(pl.multiple_of(blk * rb + part * ZERO_ROWS, ZERO_ROWS), ZERO_ROWS))
                return carry

            lax.fori_loop(nu_ref[0], n_blocks, per_block, 0)

        fill(lambda cp: cp.start())
        fill(lambda cp: cp.wait())

    meta_t = meta_ref[...].T
    l0 = meta_t[META_L0:META_L0 + 1, :].astype(jnp.int32)
    l1 = meta_t[META_L1:META_L1 + 1, :].astype(jnp.int32)
    rpos = lax.broadcasted_iota(jnp.int32, (perm_ref.shape[1], meta_ref.shape[0]), 0)
    onehot = jnp.where((rpos == l0) | (rpos == l1), 1.0, 0.0).astype(BF16)
    slot = i % 2
    perm_ref[slot] = _dot(onehot, h2_ref[...]).astype(BF16)

    def run_copy(s, loc, glob, size):
        return pltpu.make_async_copy(perm_ref.at[s, pl.ds(loc, size)], xs_ref.at[pl.ds(glob, size)], sem.at[s])

    def wait_tile(tile, s):
        _for_each_piece(rows_ref[tile], TILE_PIECES, lambda pos, size: run_copy(s, 0, 0, size).wait())

    @pl.when(i > 0)
    def _():
        wait_tile(i - 1, 1 - slot)

    _for_each_chunk(i, rows_ref, cdst_ref, lambda loc, glob: run_copy(slot, loc, glob, RUN_ALIGN).start())

    @pl.when(i == pl.num_programs(0) - 1)
    def _():
        wait_tile(i, slot)


def _dispatch(plan, meta, h2):
    t, d = h2.shape
    tm = MOE_TILE
    grid_spec = pltpu.PrefetchScalarGridSpec(
        num_scalar_prefetch=5,
        grid=(t // tm,),
        in_specs=[pl.BlockSpec((tm, ROUTER_LANES), lambda i, *_: (i, 0)),
                  pl.BlockSpec((tm, d), lambda i, *_: (i, 0))],
        out_specs=pl.BlockSpec(memory_space=pl.ANY),
        scratch_shapes=[pltpu.VMEM((2, PERM_ROWS, d), BF16), pltpu.VMEM((ZERO_ROWS, d), BF16),
                        pltpu.SemaphoreType.DMA((2,)), pltpu.SemaphoreType.DMA(())],
    )
    return pl.pallas_call(
        _dispatch_kernel,
        grid_spec=grid_spec,
        out_shape=jax.ShapeDtypeStruct((plan["n_rows"], d), BF16),
        compiler_params=_params(("arbitrary",)),
        name="dispatch",
    )(plan["chunk_dst"], plan["tile_rows"], plan["tail_start"], plan["tail_len"], plan["n_used"], meta, h2)


def _expert_kernel(be_ref, nu_ref, xs_ref, w1_ref, w3_ref, w2_ref, y_ref, w1b_ref, w3b_ref, w2b_ref):
    i = pl.program_id(0)
    used = i < nu_ref[0]

    @pl.when(used & ((i == 0) | (be_ref[i] != be_ref[jnp.maximum(i, 1) - 1])))
    def _():
        w1b_ref[...] = w1_ref[0].astype(BF16)
        w3b_ref[...] = w3_ref[0].astype(BF16)
        w2b_ref[...] = w2_ref[0].astype(BF16)

    @pl.when(used)
    def _():
        x = xs_ref[...]
        a = _dot(x, w1b_ref[...])
        b = _dot(x, w3b_ref[...])
        y_ref[...] = _dot((_silu(a) * b).astype(BF16), w2b_ref[...]).astype(y_ref.dtype)

    @pl.when(jnp.logical_not(used))
    def _():
        y_ref[...] = jnp.zeros(y_ref.shape, y_ref.dtype)


def _experts(block_e, n_used, xs, w1, w3, w2):
    n_rows, d = xs.shape
    rb = MOE_BLOCK_ROWS
    de = w1.shape[2]
    used = lambda i, be, nu: jnp.minimum(i, nu[0] - 1)
    grid_spec = pltpu.PrefetchScalarGridSpec(
        num_scalar_prefetch=2,
        grid=(n_rows // rb,),
        in_specs=[pl.BlockSpec((rb, d), lambda i, be, nu: (used(i, be, nu), 0)),
                  pl.BlockSpec((1, d, de), lambda i, be, nu: (be[used(i, be, nu)], 0, 0)),
                  pl.BlockSpec((1, d, de), lambda i, be, nu: (be[used(i, be, nu)], 0, 0)),
                  pl.BlockSpec((1, de, d), lambda i, be, nu: (be[used(i, be, nu)], 0, 0))],
        out_specs=pl.BlockSpec((rb, d), lambda i, be, nu: (i, 0)),
        scratch_shapes=[pltpu.VMEM((d, de), BF16), pltpu.VMEM((d, de), BF16), pltpu.VMEM((de, d), BF16)],
    )
    return pl.pallas_call(
        _expert_kernel,
        grid_spec=grid_spec,
        out_shape=jax.ShapeDtypeStruct((n_rows, d), BF16),
        compiler_params=_params(("arbitrary",)),
        name="experts",
    )(block_e, n_used, xs, w1, w3, w2)


def _combine_kernel(final_norm, cdst_ref, rows_ref, x1_ref, meta_ref, mod_ref, g_ref, y_ref, o_ref, ybuf_ref, sem):
    tc, d = x1_ref.shape
    i = pl.program_id(0)

    slot = i % 2

    def run_copy(s, loc, glob, size):
        return pltpu.make_async_copy(y_ref.at[pl.ds(glob, size)], ybuf_ref.at[s, pl.ds(loc, size)], sem.at[s])

    def start_tile(tile, s):
        _for_each_chunk(tile, rows_ref, cdst_ref, lambda loc, glob: run_copy(s, loc, glob, RUN_ALIGN).start())

    @pl.when(i == 0)
    def _():
        ybuf_ref[...] = jnp.zeros(ybuf_ref.shape, ybuf_ref.dtype)
        start_tile(0, 0)

    @pl.when(i + 1 < pl.num_programs(0))
    def _():
        start_tile(i + 1, 1 - slot)

    _for_each_piece(rows_ref[i], TILE_PIECES, lambda pos, size: run_copy(slot, 0, 0, size).wait())

    meta = meta_ref[...]
    yb = ybuf_ref[slot]
    rpos = lax.broadcasted_iota(jnp.int32, (tc, ybuf_ref.shape[1]), 1)
    sel = jnp.zeros(rpos.shape, F32)
    for l_lane, w_lane in ((META_L0, META_W0), (META_L1, META_W1)):
        sel = jnp.where(rpos == meta[:, l_lane:l_lane + 1].astype(jnp.int32), meta[:, w_lane:w_lane + 1], sel)
    moe = _dot(sel.astype(BF16), yb)
    x2 = x1_ref[...] + mod_ref[0][:, 5 * d:6 * d] * moe
    if final_norm:
        x2 = x2 * lax.rsqrt(jnp.mean(x2 * x2, axis=-1, keepdims=True) + NORM_EPS) * g_ref[...]
    o_ref[...] = x2


def _combine(plan, x1, meta, mod3, norm_f_g, y, seq, final_norm):
    t, d = x1.shape
    tc = MOE_TILE
    per_batch = seq // tc
    gf = norm_f_g.reshape(1, d)
    grid_spec = pltpu.PrefetchScalarGridSpec(
        num_scalar_prefetch=2,
        grid=(t // tc,),
        in_specs=[pl.BlockSpec((tc, d), lambda i, *_: (i, 0)),
                  pl.BlockSpec((tc, ROUTER_LANES), lambda i, *_: (i, 0)),
                  pl.BlockSpec((1, 1, mod3.shape[2]), lambda i, *_: (i // per_batch, 0, 0)),
                  pl.BlockSpec(gf.shape, lambda i, *_: (0, 0)),
                  pl.BlockSpec(memory_space=pl.ANY)],
        out_specs=pl.BlockSpec((tc, d), lambda i, *_: (i, 0)),
        scratch_shapes=[pltpu.VMEM((2, PERM_ROWS, d), BF16), pltpu.SemaphoreType.DMA((2,))],
    )
    return pl.pallas_call(
        functools.partial(_combine_kernel, final_norm),
        grid_spec=grid_spec,
        out_shape=jax.ShapeDtypeStruct((t, d), F32),
        compiler_params=_params(("arbitrary",)),
        name="combine",
    )(plan["chunk_dst"], plan["tile_rows"], x1, meta, mod3, gf, y)


def _routing_plan(stats, n_tokens):
    rb = MOE_BLOCK_ROWS
    run_len = stats[:, STAT_LEN, :N_EXPERTS].astype(jnp.int32)
    run_loc = stats[:, STAT_START, :N_EXPERTS].astype(jnp.int32)
    n_tiles = run_len.shape[0]
    total = jnp.sum(run_len, axis=0)
    padded = ((total + rb - 1) // rb) * rb
    pad_end = jnp.cumsum(padded)
    pad_start = pad_end - padded
    run_glob = pad_start[None, :] + jnp.cumsum(run_len, axis=0) - run_len
    n_rows = n_tokens * TOP_K_EXPERTS + n_tiles * N_EXPERTS * (RUN_ALIGN - 1) + N_EXPERTS * (rb - 1)
    n_rows = ((n_rows + rb - 1) // rb) * rb
    block_row = jnp.arange(n_rows // rb, dtype=jnp.int32) * rb
    block_e = jnp.minimum(jnp.sum((pad_end[None, :] <= block_row[:, None]).astype(jnp.int32), axis=1),
                          N_EXPERTS - 1)
    local_row = jnp.arange(CHUNKS_PER_TILE, dtype=jnp.int32) * RUN_ALIGN
    chunk_e = jnp.minimum(jnp.sum(((run_loc + run_len)[:, None, :] <= local_row[None, :, None]).astype(jnp.int32),
                                  axis=-1), N_EXPERTS - 1)
    shift = jnp.sum(jnp.where(chunk_e[..., None] == jnp.arange(N_EXPERTS, dtype=jnp.int32),
                              (run_glob - run_loc)[:, None, :], 0), axis=-1)
    chunk_dst = shift + local_row[None, :]
    return dict(chunk_dst=chunk_dst.reshape(-1), tile_rows=jnp.sum(run_len, axis=1),
                tail_start=pad_start + total, tail_len=padded - total,
                n_used=(pad_end[-1:] // rb).astype(jnp.int32), block_e=block_e.astype(jnp.int32), n_rows=n_rows)


def kernel(x, c, positions, w_ada, b_ada, norm1_g, w_in, b_gate, conv_w, conv_b, attn_out_g, mlstm_out_g,
           w_out, norm2_g, w_rg, b_rg, w_re, b_re, w1, w3, w2, norm_f_g):
    b, s, d = x.shape
    depth = w_ada.shape[0]
    assert d == D_MODEL and s % MOE_TILE == 0 and s % (KV_CHUNK * MOBA_BLOCK) == 0
    x2d = x.reshape(b * s, d)
    for l in range(depth):
        mod3 = _mod(c, w_ada[l], b_ada[l]).reshape(b, 1, 6 * d)
        q, k, vt, qkm, vm, om, gates = _inproj(x2d.reshape(b, s, d), positions, mod3,
                                              norm1_g[l].reshape(1, d), w_in[l])
        attn = _moba(q, k, vt, attn_out_g[l].reshape(1, D_ATT))
        hm = _mlstm(qkm, vm, om, gates, conv_w[l], conv_b[l], b_gate[l], mlstm_out_g[l])
        x1, h2, meta, stats = _outproj(attn.reshape(b * s, D_ATT), hm.reshape(b * s, D_MLSTM), x2d, mod3,
                                       norm2_g[l].reshape(1, d), w_out[l], w_rg[l], b_rg[l], w_re[l],
                                       b_re[l], s)
        plan = _routing_plan(stats, b * s)
        xs = _dispatch(plan, meta, h2)
        y = _experts(plan["block_e"], plan["n_used"], xs, w1[l], w3[l], w2[l])
        x2d = _combine(plan, x1, meta, mod3, norm_f_g, y, s, final_norm=(l == depth - 1))
    return x2d.reshape(b, s, d)
```

```python
import functools

import jax
import jax.numpy as jnp
from jax import lax
from jax.experimental import pallas as pl
from jax.experimental.pallas import tpu as pltpu

F32 = jnp.float32
BF16 = jnp.bfloat16

D_MODEL = 1024
D_ATT = 512
ATT_HEADS = 8
ATT_HEAD_DIM = 64
D_MLSTM = 512
MLSTM_HEADS = 4
MLSTM_HEAD_DIM = 128
MOBA_BLOCK = 256
MOBA_TOPK = 3
ROPE_THETA = 10000.0
MLSTM_CHUNK = 128
MLSTM_CHUNKS_PER_STEP = 4
CONV_WIDTH = 4
N_GROUPS = 4
EXPERTS_PER_GROUP = 8
N_EXPERTS = N_GROUPS * EXPERTS_PER_GROUP
TOP_K_EXPERTS = 2
D_EXPERT = 512
MOE_BLOCK_ROWS = 512
NORM_EPS = 1e-6
NEG_INF = -1e30
LOG2_E = 1.4426950408889634
KV_CHUNK = 4
MOBA_GROUP_LANES = 512

LANES = 128
SUBLANES = 8
VMEM_LIMIT_BYTES = 56 * 1024 * 1024

INPROJ_TILE = 512
MOE_TILE = 512
RUN_ALIGN = 2 * SUBLANES
PERM_ROWS = TOP_K_EXPERTS * MOE_TILE + N_EXPERTS * RUN_ALIGN
CHUNKS_PER_TILE = PERM_ROWS // RUN_ALIGN
TAIL_PIECES = tuple(RUN_ALIGN << p for p in reversed(range(5)))
assert TAIL_PIECES[0] * 2 == MOE_BLOCK_ROWS
TILE_PIECES = tuple(RUN_ALIGN << p for p in reversed(range(7)))
ZERO_ROWS = TAIL_PIECES[0]
GATE_LANES = 2 * LANES
ROUTER_LANES = LANES
GROUP_LANE0 = N_EXPERTS


def _dot(a, b):
    return jnp.dot(a, b, preferred_element_type=F32)


def _dot_nt(a, b):
    return lax.dot_general(a, b, (((1,), (1,)), ((), ())), preferred_element_type=F32)


def _dot_tn(a, b):
    return lax.dot_general(a, b, (((0,), (0,)), ((), ())), preferred_element_type=F32)


def _split3(x):
    a = x.astype(BF16)
    r = x - a.astype(F32)
    b = r.astype(BF16)
    c = (r - b.astype(F32)).astype(BF16)
    return a, b, c


def _silu(x):
    return x * jax.nn.sigmoid(x)


def _log_sigmoid(x):
    return jnp.minimum(x, 0.0) - jnp.log1p(jnp.exp(-jnp.abs(x)))


def _params(semantics, vmem=VMEM_LIMIT_BYTES):
    return pltpu.CompilerParams(dimension_semantics=semantics, vmem_limit_bytes=vmem)


def _mod_kernel(c_ref, w_ref, b_ref, o_ref):
    sc = _silu(c_ref[...])
    o_ref[...] = jnp.dot(sc, w_ref[...], precision=lax.Precision.HIGHEST,
                         preferred_element_type=F32) + b_ref[...]


def _mod(c, w_ada, b_ada):
    b, d = c.shape
    n = w_ada.shape[1]
    tn = D_MODEL
    return pl.pallas_call(
        _mod_kernel,
        grid=(n // tn,),
        in_specs=[pl.BlockSpec((b, d), lambda i: (0, 0)),
                  pl.BlockSpec((d, tn), lambda i: (0, i)),
                  pl.BlockSpec((1, tn), lambda i: (0, i))],
        out_specs=pl.BlockSpec((b, tn), lambda i: (0, i)),
        out_shape=jax.ShapeDtypeStruct((b, n), F32),
        compiler_params=_params(("arbitrary",)),
        name="mod",
    )(c, w_ada, b_ada.reshape(1, n))


def _rms_modulate(x, g, shift, scale):
    y = x * lax.rsqrt(jnp.mean(x * x, axis=-1, keepdims=True) + NORM_EPS)
    return (y * g) * (1.0 + scale) + shift


def _rope(t, cos, sin, first_half):
    outs = []
    for c in range(t.shape[1] // LANES):
        tc = t[:, c * LANES:(c + 1) * LANES]
        rot = jnp.where(first_half, -pltpu.roll(tc, LANES - ATT_HEAD_DIM // 2, 1),
                        pltpu.roll(tc, ATT_HEAD_DIM // 2, 1))
        outs.append(tc * cos + rot * sin)
    return jnp.concatenate(outs, axis=1)


def _inproj_kernel(x_ref, pos_ref, mod_ref, g_ref, invf_ref, wq_ref, wk_ref, wvt_ref, wqk_ref,
                   wv_ref, wo_ref, wg_ref, q_ref, k_ref, vt_ref, qkm_ref, vm_ref, om_ref, gt_ref):
    d = x_ref.shape[2]
    x = x_ref[0]
    mod = mod_ref[0]
    h = _rms_modulate(x, g_ref[...], mod[:, 0:d], mod[:, d:2 * d])
    hb = h.astype(BF16)

    ang = pos_ref[0].astype(F32) * invf_ref[...]
    cos = jnp.cos(ang)
    sin = jnp.sin(ang)
    lane = lax.broadcasted_iota(jnp.int32, cos.shape, 1)
    first_half = (lane & (ATT_HEAD_DIM // 2)) == 0

    q = _rope(_dot(hb, wq_ref[...]), cos, sin, first_half)
    q_ref[0] = (q * (ATT_HEAD_DIM ** -0.5 * LOG2_E)).astype(BF16)
    k_ref[0] = _rope(_dot(hb, wk_ref[...]), cos, sin, first_half).astype(BF16)
    vt = _dot_nt(wvt_ref[...], hb).astype(BF16)
    for blk_i in range(vt_ref.shape[1]):
        vt_ref[0, blk_i] = vt[:, blk_i * MOBA_BLOCK:(blk_i + 1) * MOBA_BLOCK]
    qkm_ref[0] = _dot(hb, wqk_ref[...])
    vm_ref[0] = _dot(hb, wv_ref[...])
    om_ref[0] = _dot(hb, wo_ref[...])
    gt_ref[0] = _dot(hb, wg_ref[...])


def _inproj(x, positions, mod3, norm_g, w_in):
    b, s, d = x.shape
    tm = INPROJ_TILE
    per_tile = tm // MOBA_BLOCK
    nb = s // MOBA_BLOCK
    o = [0, D_ATT, 2 * D_ATT, 3 * D_ATT, 3 * D_ATT + 2 * D_MLSTM, 3 * D_ATT + 3 * D_MLSTM,
         3 * D_ATT + 4 * D_MLSTM, 3 * D_ATT + 4 * D_MLSTM + 2 * MLSTM_HEADS]
    wb = w_in.astype(BF16)
    wq, wk, wv_a, wqk, wv, wo, wg = (wb[:, o[i]:o[i + 1]] for i in range(7))
    wvt = wv_a.T
    lane_pad = ((0, 0), (0, LANES - MLSTM_HEADS))
    wg = jnp.concatenate([jnp.pad(wg[:, :MLSTM_HEADS], lane_pad), jnp.pad(wg[:, MLSTM_HEADS:], lane_pad)], axis=1)
    half = ATT_HEAD_DIM // 2
    inv_freq = ROPE_THETA ** (-jnp.arange(half, dtype=F32) / half)
    invf = jnp.tile(inv_freq, LANES // half).reshape(1, LANES)

    full = lambda a: pl.BlockSpec(a.shape, lambda bi, i: (0,) * a.ndim)
    tok = lambda w: pl.BlockSpec((1, tm, w), lambda bi, i: (bi, i, 0))
    out_shape = [jax.ShapeDtypeStruct((b, s, D_ATT), BF16),
                 jax.ShapeDtypeStruct((b, s, D_ATT), BF16),
                 jax.ShapeDtypeStruct((b, nb, D_ATT, MOBA_BLOCK), BF16),
                 jax.ShapeDtypeStruct((b, s, 2 * D_MLSTM), F32),
                 jax.ShapeDtypeStruct((b, s, D_MLSTM), F32),
                 jax.ShapeDtypeStruct((b, s, D_MLSTM), F32),
                 jax.ShapeDtypeStruct((b, s, GATE_LANES), F32)]
    return pl.pallas_call(
        _inproj_kernel,
        grid=(b, s // tm),
        in_specs=[tok(d), tok(1),
                  pl.BlockSpec((1, 1, mod3.shape[2]), lambda bi, i: (bi, 0, 0)),
                  full(norm_g), full(invf), full(wq), full(wk), full(wvt), full(wqk), full(wv),
                  full(wo), full(wg)],
        out_specs=[tok(D_ATT), tok(D_ATT),
                   pl.BlockSpec((1, per_tile, D_ATT, MOBA_BLOCK), lambda bi, i: (bi, i, 0, 0)),
                   tok(2 * D_MLSTM), tok(D_MLSTM), tok(D_MLSTM), tok(GATE_LANES)],
        out_shape=out_shape,
        compiler_params=_params(("arbitrary", "arbitrary")),
        name="inproj",
    )(x, positions.reshape(b, s, 1), mod3, norm_g, invf, wq, wk, wvt, wqk, wv, wo, wg)


def _moba_kernel(q_ref, k_ref, vt_ref, g_ref, o_ref, kmean_ref, sel_ref, st_ref, qh_ref, acc_ref):
    blk = MOBA_BLOCK
    nb = k_ref.shape[1] // blk
    j = pl.program_id(2)
    heads = q_ref.shape[2] // ATT_HEAD_DIM
    hpl = LANES // ATT_HEAD_DIM

    @pl.when(j == 0)
    def _():
        for n in range(nb):
            kb = k_ref[0, n * blk:(n + 1) * blk, :].astype(F32)
            kmean_ref[n:n + 1, :] = jnp.mean(kb, axis=0, keepdims=True)

    lane = lax.broadcasted_iota(jnp.int32, (blk, LANES), 1)
    blk_id = lax.broadcasted_iota(jnp.int32, (nb, blk), 0)
    km_hi = kmean_ref[...].astype(BF16)
    km_lo = (kmean_ref[...] - km_hi.astype(F32)).astype(BF16)
    past = blk_id < j
    tile = lambda a, hh: a[:, (hh // hpl) * LANES:(hh // hpl + 1) * LANES]

    for hh in range(heads):
        q = tile(q_ref[0], hh)
        in_head = (lane >= (hh % hpl) * ATT_HEAD_DIM) & (lane < (hh % hpl + 1) * ATT_HEAD_DIM)
        qh_ref[hh] = jnp.where(in_head, q, jnp.zeros_like(q))
        gate = _dot_nt(tile(km_hi, hh), qh_ref[hh]) + _dot_nt(tile(km_lo, hh), qh_ref[hh])
        g = jnp.where(past, gate, NEG_INF)
        sel = jnp.zeros(g.shape, F32)
        for _ in range(min(MOBA_TOPK, nb)):
            top = jnp.max(g, axis=0, keepdims=True)
            idx = jnp.min(jnp.where(g == top, blk_id, nb), axis=0, keepdims=True)
            pick = blk_id == idx
            sel = jnp.where(pick, 1.0, sel)
            g = jnp.where(pick, -jnp.inf, g)
        sel_ref[hh, 0:nb, :] = jnp.where((sel > 0.0) & past, 1.0, 0.0)
        sel_ref[hh, nb:nb + SUBLANES, :] = jnp.zeros((SUBLANES, blk), F32)

    kpos = lax.broadcasted_iota(jnp.int32, (blk, blk), 0)
    qpos = lax.broadcasted_iota(jnp.int32, (blk, blk), 1)
    causal_bias = jnp.where(kpos <= qpos, 0.0, NEG_INF)

    def slab_block(t, i):
        b_i = j - KV_CHUNK * t - (KV_CHUNK - 1 - i)
        return jnp.maximum(b_i, 0), jnp.where(b_i >= 0, b_i, nb)

    def score(t, slot, own_chunk):
        tops = [None] * heads
        slab_max = [[None] * KV_CHUNK for _ in range(heads)]
        for i in range(KV_CHUNK):
            src, row = slab_block(t, i)
            kb = k_ref[0, pl.ds(pl.multiple_of(src * blk, blk), blk), :]
            for hh in range(heads):
                st = _dot_nt(tile(kb, hh), qh_ref[hh])
                if own_chunk and i == KV_CHUNK - 1:
                    st = st + causal_bias
                    cmax = jnp.max(st, axis=0, keepdims=True)
                    smax = cmax
                else:
                    cmax = jnp.max(st, axis=0, keepdims=True)
                    smax = cmax + (sel_ref[hh, pl.ds(row, 1), :] - 1.0) * (-NEG_INF)
                st_ref[slot, hh, i * blk:(i + 1) * blk, :] = st
                slab_max[hh][i] = cmax
                tops[hh] = smax if tops[hh] is None else jnp.maximum(tops[hh], smax)
        return tuple(tops), tuple(tuple(r) for r in slab_max)

    def accumulate(t, slot, state, maxes):
        tops, slab_max = maxes
        new = []
        for hh in range(heads):
            m, l = state[2 * hh:2 * hh + 2]
            m_new = jnp.maximum(m, tops[hh])
            alpha = jnp.exp2(m - m_new)
            l = alpha * l
            acc = alpha * acc_ref[hh]
            for i in range(KV_CHUNK):
                src, row = slab_block(t, i)
                keep = sel_ref[hh, pl.ds(row, 1), :]
                if i == KV_CHUNK - 1:
                    keep = jnp.where(t == 0, 1.0, keep)
                p = jnp.exp2(st_ref[slot, hh, i * blk:(i + 1) * blk, :] - jnp.maximum(m_new, slab_max[hh][i]))
                l = l + keep * jnp.sum(p, axis=0, keepdims=True)
                v_h = vt_ref[0, src][hh * ATT_HEAD_DIM:(hh + 1) * ATT_HEAD_DIM, :]
                acc = acc + keep * _dot(v_h, p.astype(BF16))
            acc_ref[hh] = acc
            new += [m_new, l]
        return tuple(new)

    state = ()
    for hh in range(heads):
        state += (jnp.full((1, blk), NEG_INF, F32), jnp.zeros((1, blk), F32))
        acc_ref[hh] = jnp.zeros((ATT_HEAD_DIM, blk), F32)
    last = j // KV_CHUNK

    def stage_pair(t, slot, state, maxes):
        nxt = score(t + 1, 1 - slot, False)
        return accumulate(t, slot, state, maxes), nxt

    def body(t, carry):
        return lax.cond(t % 2 == 0, functools.partial(stage_pair, t, 0), functools.partial(stage_pair, t, 1),
                        *carry)

    state, maxes = lax.fori_loop(0, last, body, (state, score(0, 0, True)))
    state = lax.cond(last % 2 == 0, functools.partial(accumulate, last, 0), functools.partial(accumulate, last, 1),
                     state, maxes)

    outs = []
    for hh in range(heads):
        o_h = acc_ref[hh] / state[2 * hh + 1]
        ms = jnp.mean(o_h * o_h, axis=0, keepdims=True)
        outs.append(o_h * lax.rsqrt(ms + NORM_EPS))
    o_ref[0] = jnp.concatenate(outs, axis=0).T * g_ref[...]


def _moba(q, k, vt, g_att):
    b, s, w = q.shape
    blk = MOBA_BLOCK
    nb = s // blk
    gl = MOBA_GROUP_LANES
    groups = w // gl
    heads = gl // ATT_HEAD_DIM
    return pl.pallas_call(
        _moba_kernel,
        grid=(b, groups, nb),
        in_specs=[pl.BlockSpec((1, blk, gl), lambda bi, p, j: (bi, j, p)),
                  pl.BlockSpec((1, s, gl), lambda bi, p, j: (bi, 0, p)),
                  pl.BlockSpec((1, nb, gl, blk), lambda bi, p, j: (bi, 0, p, 0)),
                  pl.BlockSpec((1, gl), lambda bi, p, j: (0, p))],
        out_specs=pl.BlockSpec((1, blk, gl), lambda bi, p, j: (bi, j, p)),
        out_shape=jax.ShapeDtypeStruct((b, s, w), F32),
        scratch_shapes=[pltpu.VMEM((nb, gl), F32),
                        pltpu.VMEM((heads, nb + SUBLANES, blk), F32),
                        pltpu.VMEM((2, heads, KV_CHUNK * blk, blk), F32),
                        pltpu.VMEM((heads, blk, LANES), BF16),
                        pltpu.VMEM((heads, ATT_HEAD_DIM, blk), F32)],
        compiler_params=_params(("arbitrary", "arbitrary", "arbitrary")),
        name="moba",
    )(q, k, vt, g_att)


def _mlstm_kernel(qkm_ref, vm_ref, om_ref, gt_ref, cw_ref, cb_ref, bg_ref, gm_ref, bsel_ref, o_ref,
                  uext_ref, c_ref, n_ref, m_ref):
    L = MLSTM_CHUNK
    dh = MLSTM_HEAD_DIM
    nh = MLSTM_HEADS
    assert L == dh == LANES and 2 * nh == SUBLANES
    c = pl.program_id(1)

    @pl.when(c == 0)
    def _():
        uext_ref[0:SUBLANES, :] = jnp.zeros((SUBLANES, uext_ref.shape[1]), F32)
        c_ref[...] = jnp.zeros(c_ref.shape, F32)
        n_ref[...] = jnp.zeros(n_ref.shape, F32)
        m_ref[...] = jnp.zeros(m_ref.shape, F32)

    for sub in range(qkm_ref.shape[1] // L):
        _mlstm_chunk(slice(sub * L, (sub + 1) * L), qkm_ref, vm_ref, om_ref, gt_ref, cw_ref, cb_ref, bg_ref, gm_ref,
                     bsel_ref, o_ref, uext_ref, c_ref, n_ref, m_ref)


def _mlstm_chunk(rows, qkm_ref, vm_ref, om_ref, gt_ref, cw_ref, cb_ref, bg_ref, gm_ref, bsel_ref, o_ref,
                 uext_ref, c_ref, n_ref, m_ref):
    L = MLSTM_CHUNK
    dh = MLSTM_HEAD_DIM
    nh = MLSTM_HEADS

    u = qkm_ref[0, rows, :]
    uext_ref[SUBLANES:SUBLANES + L, :] = u
    cw = cw_ref[...]
    conv = cb_ref[...] + cw[CONV_WIDTH - 1:CONV_WIDTH, :] * u
    for dlt in range(1, CONV_WIDTH):
        conv = conv + cw[CONV_WIDTH - 1 - dlt:CONV_WIDTH - dlt, :] * uext_ref[SUBLANES - dlt:SUBLANES - dlt + L, :]
    uext_ref[0:SUBLANES, :] = u[L - SUBLANES:L, :]
    act = _silu(conv)

    gates = gt_ref[0, rows, :] + bg_ref[...]
    ig = gates[:, 0:LANES]
    tpos = lax.broadcasted_iota(jnp.int32, (L, L), 0)
    spos = lax.broadcasted_iota(jnp.int32, (L, L), 1)
    causal = spos <= tpos
    tril = jnp.where(causal, 1.0, 0.0).astype(BF16)
    cum = sum(_dot(tril, part) for part in _split3(_log_sigmoid(gates[:, LANES:])))
    a_tot = cum[L - 1:L, :]
    m_prev = m_ref[0:1, :]
    resid = ig - cum
    pmax = resid
    shift = 1
    while shift < L:
        pmax = jnp.maximum(pmax, jnp.where(tpos >= shift, pltpu.roll(pmax, shift, 0), -jnp.inf))
        shift *= 2
    inter = cum + m_prev
    m_t = jnp.maximum(inter, cum + pmax)
    g_loc = a_tot - cum + ig
    m_loc = jnp.max(g_loc, axis=0, keepdims=True)
    w_loc = jnp.exp(g_loc - m_loc)
    head_lane = spos < nh
    packed = jnp.where(head_lane, cum - m_t, 0.0)
    for i, qty in enumerate((inter - m_t, -m_t, g_loc - m_loc), start=1):
        packed = packed + pltpu.roll(jnp.where(head_lane, qty, 0.0), SUBLANES * i, 1)
    per_t_cols = jnp.concatenate(_split3(packed), axis=1)
    per_t_rows = (jnp.where(head_lane, resid, 0.0) + pltpu.roll(jnp.where(head_lane, w_loc, 0.0), SUBLANES, 1)).T

    m_new = jnp.maximum(a_tot + m_prev, m_loc)
    s_prev = jnp.exp(a_tot + m_prev - m_new)
    s_loc = jnp.exp(m_loc - m_new)
    m_ref[0:1, :] = m_new

    vm = vm_ref[0, rows, :]
    om = om_ref[0, rows, :]
    heads = lambda a, off=0: jnp.stack([a[:, off + h * dh:off + (h + 1) * dh] for h in range(nh)])
    bdot = lambda a, b, ca, cb: lax.dot_general(a, b, (((ca,), (cb,)), ((0,), (0,))), preferred_element_type=F32)
    q4 = heads(act)
    k4 = heads(act, nh * dh) * (dh ** -0.5)
    v4 = heads(vm)
    qb, kb, vb = q4.astype(BF16), k4.astype(BF16), v4.astype(BF16)
    c_prev = c_ref[...]
    n_all = n_ref[...]
    n_prev = jnp.stack([n_all[h:h + 1, :] for h in range(nh)])
    bcast = _dot(per_t_cols, bsel_ref[...])
    qty = lambda i: jnp.stack([bcast[:, (4 * h + i) * dh:(4 * h + i + 1) * dh] for h in range(nh)])
    decay_b = qty(0)
    w_inter_b, floor_b, w_loc_b = jnp.exp(qty(1)), jnp.exp(qty(2)), jnp.exp(qty(3))
    resid_rows = jnp.stack([per_t_rows[h:h + 1, :] for h in range(nh)])
    w_rows = jnp.stack([per_t_rows[SUBLANES + h:SUBLANES + h + 1, :] for h in range(nh)])

    s_qk = bdot(qb, kb, 2, 2) * jnp.where(causal[None], jnp.exp(decay_b + resid_rows), 0.0)
    intra = bdot(s_qk.astype(BF16), jnp.concatenate([vb, jnp.ones((nh, L, dh), BF16)], axis=2), 2, 1)
    state_rows = jnp.concatenate([c_prev, jnp.broadcast_to(n_prev, (nh, dh, dh))], axis=1).astype(BF16)
    carried = bdot(qb, state_rows, 2, 2)
    num = intra[:, :, :dh] + w_inter_b * carried[:, :, :dh]
    den = intra[:, :, dh:] + w_inter_b * carried[:, :, dh:]
    hout = num / jnp.maximum(jnp.abs(den), floor_b)

    sq = (hout * hout).astype(BF16).reshape(nh * L, dh)
    mean_sq = (_dot(sq, jnp.ones((dh, dh), BF16)) * (1.0 / dh)).reshape(nh, L, dh)
    gm = jnp.stack([gm_ref[:, h * dh:(h + 1) * dh] for h in range(nh)])
    out = hout * lax.rsqrt(mean_sq + NORM_EPS) * gm * jax.nn.sigmoid(heads(om))
    for h in range(nh):
        o_ref[0, rows, h * dh:(h + 1) * dh] = out[h]

    vw = (v4 * w_loc_b).astype(BF16)
    for h in range(nh):
        c_loc = _dot_tn(vw[h], kb[h])
        n_loc = _dot(jnp.broadcast_to(w_rows[h], (SUBLANES, L)).astype(BF16), kb[h])[0:1, :]
        c_ref[h] = s_prev[:, h:h + 1] * c_prev[h] + s_loc[:, h:h + 1] * c_loc
        n_ref[h:h + 1, :] = s_prev[:, h:h + 1] * n_all[h:h + 1, :] + s_loc[:, h:h + 1] * n_loc


def _mlstm(qkm, vm, om, gates, conv_w, conv_b, b_gate, g_m):
    b, s, _ = qkm.shape
    L = MLSTM_CHUNK
    step_rows = MLSTM_CHUNKS_PER_STEP * L
    nc = s // step_rows
    lane_pad = ((0, 0), (0, LANES - MLSTM_HEADS))
    bg = b_gate.reshape(1, -1)
    bg = jnp.concatenate([jnp.pad(bg[:, :MLSTM_HEADS], lane_pad), jnp.pad(bg[:, MLSTM_HEADS:], lane_pad)], axis=1)
    tok = lambda w: pl.BlockSpec((1, step_rows, w), lambda bi, c: (bi, c, 0))
    full = lambda a: pl.BlockSpec(a.shape, lambda bi, c: (0,) * a.ndim)
    cb = conv_b.reshape(1, -1)
    gm = g_m.reshape(1, -1)
    n_qty = 4
    row = jnp.arange(3 * LANES, dtype=jnp.int32) % LANES
    col_blk = jnp.arange(MLSTM_HEADS * n_qty * MLSTM_HEAD_DIM, dtype=jnp.int32) // MLSTM_HEAD_DIM
    bsel = (((row // SUBLANES)[:, None] == (col_blk % n_qty)[None, :])
            & ((row % SUBLANES)[:, None] == (col_blk // n_qty)[None, :])).astype(BF16)
    return pl.pallas_call(
        _mlstm_kernel,
        grid=(b, nc),
        in_specs=[tok(2 * D_MLSTM), tok(D_MLSTM), tok(D_MLSTM), tok(GATE_LANES),
                  full(conv_w), full(cb), full(bg), full(gm), full(bsel)],
        out_specs=tok(D_MLSTM),
        out_shape=jax.ShapeDtypeStruct((b, s, D_MLSTM), F32),
        scratch_shapes=[pltpu.VMEM((SUBLANES + L, 2 * D_MLSTM), F32),
                        pltpu.VMEM((MLSTM_HEADS, MLSTM_HEAD_DIM, MLSTM_HEAD_DIM), F32),
                        pltpu.VMEM((SUBLANES, MLSTM_HEAD_DIM), F32),
                        pltpu.VMEM((SUBLANES, LANES), F32)],
        compiler_params=_params(("arbitrary", "arbitrary")),
        name="mlstm",
    )(qkm, vm, om, gates, conv_w, cb, bg, gm, bsel)


META_E0, META_E1, META_W0, META_W1, META_L0, META_L1 = range(6)
STAT_LEN, STAT_START = 0, 1


def _outproj_kernel(attn_ref, hm_ref, x_ref, mod_ref, g_ref, wout_ref, wr_hi_ref, wr_lo_ref, br_ref,
                    x1_ref, h2_ref, meta_ref, stat_ref):
    d = x_ref.shape[1]
    tm = x_ref.shape[0]

    mod = mod_ref[0]
    y = (_dot(attn_ref[...].astype(BF16), wout_ref[0:D_ATT, :])
         + _dot(hm_ref[...].astype(BF16), wout_ref[D_ATT:, :]))
    x1 = x_ref[...] + mod[:, 2 * d:3 * d] * y
    x1_ref[...] = x1
    h2 = _rms_modulate(x1, g_ref[...], mod[:, 3 * d:4 * d], mod[:, 4 * d:5 * d])
    h2_ref[...] = h2.astype(BF16)

    h_hi = h2.astype(BF16)
    h_lo = (h2 - h_hi.astype(F32)).astype(BF16)
    logit = (_dot(h_hi, wr_hi_ref[...]) + _dot(h_lo, wr_hi_ref[...]) + _dot(h_hi, wr_lo_ref[...])
             + br_ref[...])
    lane = lax.broadcasted_iota(jnp.int32, logit.shape, 1)
    big = jnp.int32(ROUTER_LANES)

    is_g = (lane >= GROUP_LANE0) & (lane < GROUP_LANE0 + N_GROUPS)
    gmax = jnp.max(jnp.where(is_g, logit, -jnp.inf), axis=1, keepdims=True)
    gsum = jnp.sum(jnp.where(is_g, jnp.exp(logit - gmax), 0.0), axis=1, keepdims=True)
    g_w = 1.0 / gsum
    g_idx = jnp.min(jnp.where(is_g & (logit == gmax), lane, big), axis=1, keepdims=True) - GROUP_LANE0

    in_grp = (lane < N_EXPERTS) & ((lane // EXPERTS_PER_GROUP) == g_idx)
    emax = jnp.max(jnp.where(in_grp, logit, -jnp.inf), axis=1, keepdims=True)
    esum = jnp.sum(jnp.where(in_grp, jnp.exp(logit - emax), 0.0), axis=1, keepdims=True)
    e0 = jnp.min(jnp.where(in_grp & (logit == emax), lane, big), axis=1, keepdims=True)
    rest = in_grp & (lane != e0)
    e2max = jnp.max(jnp.where(rest, logit, -jnp.inf), axis=1, keepdims=True)
    e1 = jnp.min(jnp.where(rest & (logit == e2max), lane, big), axis=1, keepdims=True)
    p0 = 1.0 / esum
    p1 = jnp.exp(e2max - emax) / esum
    w0 = g_w * p0 / (p0 + p1)
    w1 = g_w * p1 / (p0 + p1)

    memb = jnp.where((lane == e0) | (lane == e1), 1.0, 0.0)
    tpos = lax.broadcasted_iota(jnp.int32, (tm, tm), 0)
    spos = lax.broadcasted_iota(jnp.int32, (tm, tm), 1)
    before = jnp.where(spos < tpos, 1.0, 0.0).astype(BF16)
    earlier = _dot(before, memb.astype(BF16))
    count = jnp.sum(memb, axis=0, keepdims=True).astype(jnp.int32)
    run_len = (((count + (RUN_ALIGN - 1)) // RUN_ALIGN) * RUN_ALIGN).astype(F32)
    epos = lax.broadcasted_iota(jnp.int32, (ROUTER_LANES, ROUTER_LANES), 0)
    fpos = lax.broadcasted_iota(jnp.int32, (ROUTER_LANES, ROUTER_LANES), 1)
    lower_e = jnp.where(epos < fpos, 1.0, 0.0).astype(BF16)
    run_start = _dot(jnp.broadcast_to(run_len, (SUBLANES, ROUTER_LANES)).astype(BF16), lower_e)[0:1, :]
    row = run_start + earlier
    l0 = jnp.sum(jnp.where(lane == e0, row, 0.0), axis=1, keepdims=True)
    l1 = jnp.sum(jnp.where(lane == e1, row, 0.0), axis=1, keepdims=True)

    meta = jnp.zeros(logit.shape, F32)
    for slot, val in ((META_E0, e0.astype(F32)), (META_E1, e1.astype(F32)), (META_W0, w0), (META_W1, w1),
                      (META_L0, l0), (META_L1, l1)):
        meta = jnp.where(lane == slot, val, meta)
    meta_ref[...] = meta
    srow = lax.broadcasted_iota(jnp.int32, (SUBLANES, ROUTER_LANES), 0)
    stat_ref[0] = jnp.where(srow == STAT_LEN, run_len, jnp.where(srow == STAT_START, run_start, 0.0))


def _outproj(attn, hm, x2d, mod3, norm_g, w_out, w_rg, b_rg, w_re, b_re, seq):
    t, d = x2d.shape
    tm = MOE_TILE
    per_batch = seq // tm
    wr = jnp.pad(jnp.concatenate([w_re, w_rg], axis=1), ((0, 0), (0, ROUTER_LANES - N_EXPERTS - N_GROUPS)))
    br = jnp.pad(jnp.concatenate([b_re, b_rg]).reshape(1, -1), ((0, 0), (0, ROUTER_LANES - N_EXPERTS - N_GROUPS)))
    wr_hi = wr.astype(BF16)
    wr_lo = (wr - wr_hi.astype(F32)).astype(BF16)
    wout = w_out.astype(BF16)
    tok = lambda w: pl.BlockSpec((tm, w), lambda i: (i, 0))
    full = lambda a: pl.BlockSpec(a.shape, lambda i: (0,) * a.ndim)
    return pl.pallas_call(
        _outproj_kernel,
        grid=(t // tm,),
        in_specs=[tok(D_ATT), tok(D_MLSTM), tok(d),
                  pl.BlockSpec((1, 1, mod3.shape[2]), lambda i: (i // per_batch, 0, 0)),
                  full(norm_g), full(wout), full(wr_hi), full(wr_lo), full(br)],
        out_specs=[tok(d), tok(d), tok(ROUTER_LANES),
                   pl.BlockSpec((1, SUBLANES, ROUTER_LANES), lambda i: (i, 0, 0))],
        out_shape=[jax.ShapeDtypeStruct((t, d), F32), jax.ShapeDtypeStruct((t, d), BF16),
                   jax.ShapeDtypeStruct((t, ROUTER_LANES), F32),
                   jax.ShapeDtypeStruct((t // tm, SUBLANES, ROUTER_LANES), F32)],
        compiler_params=_params(("arbitrary",)),
        name="outproj",
    )(attn, hm, x2d, mod3, norm_g, wout, wr_hi, wr_lo, br)


def _for_each_piece(length, pieces, fn):
    pos = jnp.int32(0)
    for size in pieces:
        take = length & size

        @pl.when(take != 0)
        def _(pos=pos, size=size):
            fn(pl.multiple_of(pos, RUN_ALIGN), size)

        pos = pos + take


def _for_each_chunk(tile, rows_ref, cdst_ref, fn):
    def body(c, carry):
        fn(pl.multiple_of(c * RUN_ALIGN, RUN_ALIGN),
           pl.multiple_of(cdst_ref[tile * CHUNKS_PER_TILE + c], RUN_ALIGN))
        return carry

    lax.fori_loop(0, rows_ref[tile] // RUN_ALIGN, body, 0)


def _dispatch_kernel(cdst_ref, rows_ref, tail_ref, tlen_ref, nu_ref, meta_ref, h2_ref, xs_ref,
                     perm_ref, zero_ref, sem, zsem):
    i = pl.program_id(0)
    rb = MOE_BLOCK_ROWS
    n_blocks = xs_ref.shape[0] // rb

    @pl.when(i == 0)
    def _():
        zero_ref[...] = jnp.zeros(zero_ref.shape, zero_ref.dtype)

        def zero_copy(row, size):
            return pltpu.make_async_copy(zero_ref.at[pl.ds(0, size)], xs_ref.at[pl.ds(row, size)], zsem)

        def fill(op):
            def per_expert(e, carry):
                start = tail_ref[e]
                _for_each_piece(tlen_ref[e], TAIL_PIECES,
                                lambda pos, size: op(zero_copy(pl.multiple_of(start + pos, RUN_ALIGN), size)))
                return carry

            lax.fori_loop(0, N_EXPERTS, per_expert, 0)

            def per_block(blk, carry):
                for part in range(rb // ZERO_ROWS):
                    op(zero_copy(pl.multiple_of(blk * rb + part * ZERO_ROWS, ZERO_ROWS), ZERO_ROWS))
                return carry

            lax.fori_loop(nu_ref[0], n_blocks, per_block, 0)

        fill(lambda cp: cp.start())
        fill(lambda cp: cp.wait())

    meta_t = meta_ref[...].T
    l0 = meta_t[META_L0:META_L0 + 1, :].astype(jnp.int32)
    l1 = meta_t[META_L1:META_L1 + 1, :].astype(jnp.int32)
    rpos = lax.broadcasted_iota(jnp.int32, (perm_ref.shape[1], meta_ref.shape[0]), 0)
    onehot = jnp.where((rpos == l0) | (rpos == l1), 1.0, 0.0).astype(BF16)
    slot = i % 2
    perm_ref[slot] = _dot(onehot, h2_ref[...]).astype(BF16)

    def run_copy(s, loc, glob, size):
        return pltpu.make_async_copy(perm_ref.at[s, pl.ds(loc, size)], xs_ref.at[pl.ds(glob, size)], sem.at[s])

    def wait_tile(tile, s):
        _for_each_piece(rows_ref[tile], TILE_PIECES, lambda pos, size: run_copy(s, 0, 0, size).wait())

    @pl.when(i > 0)
    def _():
        wait_tile(i - 1, 1 - slot)

    _for_each_chunk(i, rows_ref, cdst_ref, lambda loc, glob: run_copy(slot, loc, glob, RUN_ALIGN).start())

    @pl.when(i == pl.num_programs(0) - 1)
    def _():
        wait_tile(i, slot)


def _dispatch(plan, meta, h2):
    t, d = h2.shape
    tm = MOE_TILE
    grid_spec = pltpu.PrefetchScalarGridSpec(
        num_scalar_prefetch=5,
        grid=(t // tm,),
        in_specs=[pl.BlockSpec((tm, ROUTER_LANES), lambda i, *_: (i, 0)),
                  pl.BlockSpec((tm, d), lambda i, *_: (i, 0))],
        out_specs=pl.BlockSpec(memory_space=pl.ANY),
        scratch_shapes=[pltpu.VMEM((2, PERM_ROWS, d), BF16), pltpu.VMEM((ZERO_ROWS, d), BF16),
                        pltpu.SemaphoreType.DMA((2,)), pltpu.SemaphoreType.DMA(())],
    )
    return pl.pallas_call(
        _dispatch_kernel,
        grid_spec=grid_spec,
        out_shape=jax.ShapeDtypeStruct((plan["n_rows"], d), BF16),
        compiler_params=_params(("arbitrary",)),
        name="dispatch",
    )(plan["chunk_dst"], plan["tile_rows"], plan["tail_start"], plan["tail_len"], plan["n_used"], meta, h2)


def _expert_kernel(be_ref, nu_ref, xs_ref, w1_ref, w3_ref, w2_ref, y_ref, w1b_ref, w3b_ref, w2b_ref):
    i = pl.program_id(0)
    used = i < nu_ref[0]

    @pl.when(used & ((i == 0) | (be_ref[i] != be_ref[jnp.maximum(i, 1) - 1])))
    def _():
        w1b_ref[...] = w1_ref[0].astype(BF16)
        w3b_ref[...] = w3_ref[0].astype(BF16)
        w2b_ref[...] = w2_ref[0].astype(BF16)

    @pl.when(used)
    def _():
        x = xs_ref[...]
        a = _dot(x, w1b_ref[...])
        b = _dot(x, w3b_ref[...])
        y_ref[...] = _dot((_silu(a) * b).astype(BF16), w2b_ref[...]).astype(y_ref.dtype)

    @pl.when(jnp.logical_not(used))
    def _():
        y_ref[...] = jnp.zeros(y_ref.shape, y_ref.dtype)


def _experts(block_e, n_used, xs, w1, w3, w2):
    n_rows, d = xs.shape
    rb = MOE_BLOCK_ROWS
    de = w1.shape[2]
    used = lambda i, be, nu: jnp.minimum(i, nu[0] - 1)
    grid_spec = pltpu.PrefetchScalarGridSpec(
        num_scalar_prefetch=2,
        grid=(n_rows // rb,),
        in_specs=[pl.BlockSpec((rb, d), lambda i, be, nu: (used(i, be, nu), 0)),
                  pl.BlockSpec((1, d, de), lambda i, be, nu: (be[used(i, be, nu)], 0, 0)),
                  pl.BlockSpec((1, d, de), lambda i, be, nu: (be[used(i, be, nu)], 0, 0)),
                  pl.BlockSpec((1, de, d), lambda i, be, nu: (be[used(i, be, nu)], 0, 0))],
        out_specs=pl.BlockSpec((rb, d), lambda i, be, nu: (i, 0)),
        scratch_shapes=[pltpu.VMEM((d, de), BF16), pltpu.VMEM((d, de), BF16), pltpu.VMEM((de, d), BF16)],
    )
    return pl.pallas_call(
        _expert_kernel,
        grid_spec=grid_spec,
        out_shape=jax.ShapeDtypeStruct((n_rows, d), BF16),
        compiler_params=_params(("arbitrary",)),
        name="experts",
    )(block_e, n_used, xs, w1, w3, w2)


def _combine_kernel(final_norm, cdst_ref, rows_ref, x1_ref, meta_ref, mod_ref, g_ref, y_ref, o_ref, ybuf_ref, sem):
    tc, d = x1_ref.shape
    i = pl.program_id(0)

    slot = i % 2

    def run_copy(s, loc, glob, size):
        return pltpu.make_async_copy(y_ref.at[pl.ds(glob, size)], ybuf_ref.at[s, pl.ds(loc, size)], sem.at[s])

    def start_tile(tile, s):
        _for_each_chunk(tile, rows_ref, cdst_ref, lambda loc, glob: run_copy(s, loc, glob, RUN_ALIGN).start())

    @pl.when(i == 0)
    def _():
        ybuf_ref[...] = jnp.zeros(ybuf_ref.shape, ybuf_ref.dtype)
        start_tile(0, 0)

    @pl.when(i + 1 < pl.num_programs(0))
    def _():
        start_tile(i + 1, 1 - slot)

    _for_each_piece(rows_ref[i], TILE_PIECES, lambda pos, size: run_copy(slot, 0, 0, size).wait())

    meta = meta_ref[...]
    yb = ybuf_ref[slot]
    rpos = lax.broadcasted_iota(jnp.int32, (tc, ybuf_ref.shape[1]), 1)
    sel = jnp.zeros(rpos.shape, F32)
    for l_lane, w_lane in ((META_L0, META_W0), (META_L1, META_W1)):
        sel = jnp.where(rpos == meta[:, l_lane:l_lane + 1].astype(jnp.int32), meta[:, w_lane:w_lane + 1], sel)
    moe = _dot(sel.astype(BF16), yb)
    x2 = x1_ref[...] + mod_ref[0][:, 5 * d:6 * d] * moe
    if final_norm:
        x2 = x2 * lax.rsqrt(jnp.mean(x2 * x2, axis=-1, keepdims=True) + NORM_EPS) * g_ref[...]
    o_ref[...] = x2


def _combine(plan, x1, meta, mod3, norm_f_g, y, seq, final_norm):
    t, d = x1.shape
    tc = MOE_TILE
    per_batch = seq // tc
    gf = norm_f_g.reshape(1, d)
    grid_spec = pltpu.PrefetchScalarGridSpec(
        num_scalar_prefetch=2,
        grid=(t // tc,),
        in_specs=[pl.BlockSpec((tc, d), lambda i, *_: (i, 0)),
                  pl.BlockSpec((tc, ROUTER_LANES), lambda i, *_: (i, 0)),
                  pl.BlockSpec((1, 1, mod3.shape[2]), lambda i, *_: (i // per_batch, 0, 0)),
                  pl.BlockSpec(gf.shape, lambda i, *_: (0, 0)),
                  pl.BlockSpec(memory_space=pl.ANY)],
        out_specs=pl.BlockSpec((tc, d), lambda i, *_: (i, 0)),
        scratch_shapes=[pltpu.VMEM((2, PERM_ROWS, d), BF16), pltpu.SemaphoreType.DMA((2,))],
    )
    return pl.pallas_call(
        functools.partial(_combine_kernel, final_norm),
        grid_spec=grid_spec,
        out_shape=jax.ShapeDtypeStruct((t, d), F32),
        compiler_params=_params(("arbitrary",)),
        name="combine",
    )(plan["chunk_dst"], plan["tile_rows"], x1, meta, mod3, gf, y)


def _routing_plan(stats, n_tokens):
    rb = MOE_BLOCK_ROWS
    run_len = stats[:, STAT_LEN, :N_EXPERTS].astype(jnp.int32)
    run_loc = stats[:, STAT_START, :N_EXPERTS].astype(jnp.int32)
    n_tiles = run_len.shape[0]
    total = jnp.sum(run_len, axis=0)
    padded = ((total + rb - 1) // rb) * rb
    pad_end = jnp.cumsum(padded)
    pad_start = pad_end - padded
    run_glob = pad_start[None, :] + jnp.cumsum(run_len, axis=0) - run_len
    n_rows = n_tokens * TOP_K_EXPERTS + n_tiles * N_EXPERTS * (RUN_ALIGN - 1) + N_EXPERTS * (rb - 1)
    n_rows = ((n_rows + rb - 1) // rb) * rb
    block_row = jnp.arange(n_rows // rb, dtype=jnp.int32) * rb
    block_e = jnp.minimum(jnp.sum((pad_end[None, :] <= block_row[:, None]).astype(jnp.int32), axis=1),
                          N_EXPERTS - 1)
    local_row = jnp.arange(CHUNKS_PER_TILE, dtype=jnp.int32) * RUN_ALIGN
    chunk_e = jnp.minimum(jnp.sum(((run_loc + run_len)[:, None, :] <= local_row[None, :, None]).astype(jnp.int32),
                                  axis=-1), N_EXPERTS - 1)
    shift = jnp.sum(jnp.where(chunk_e[..., None] == jnp.arange(N_EXPERTS, dtype=jnp.int32),
                              (run_glob - run_loc)[:, None, :], 0), axis=-1)
    chunk_dst = shift + local_row[None, :]
    return dict(chunk_dst=chunk_dst.reshape(-1), tile_rows=jnp.sum(run_len, axis=1),
                tail_start=pad_start + total, tail_len=padded - total,
                n_used=(pad_end[-1:] // rb).astype(jnp.int32), block_e=block_e.astype(jnp.int32), n_rows=n_rows)


def kernel(x, c, positions, w_ada, b_ada, norm1_g, w_in, b_gate, conv_w, conv_b, attn_out_g, mlstm_out_g,
           w_out, norm2_g, w_rg, b_rg, w_re, b_re, w1, w3, w2, norm_f_g):
    b, s, d = x.shape
    depth = w_ada.shape[0]
    assert d == D_MODEL and s % MOE_TILE == 0 and s % (KV_CHUNK * MOBA_BLOCK) == 0
    x2d = x.reshape(b * s, d)
    for l in range(depth):
        mod3 = _mod(c, w_ada[l], b_ada[l]).reshape(b, 1, 6 * d)
        q, k, vt, qkm, vm, om, gates = _inproj(x2d.reshape(b, s, d), positions, mod3,
                                              norm1_g[l].reshape(1, d), w_in[l])
        attn = _moba(q, k, vt, attn_out_g[l].reshape(1, D_ATT))
        hm = _mlstm(qkm, vm, om, gates, conv_w[l], conv_b[l], b_gate[l], mlstm_out_g[l])
        x1, h2, meta, stats = _outproj(attn.reshape(b * s, D_ATT), hm.reshape(b * s, D_MLSTM), x2d, mod3,
                                       norm2_g[l].reshape(1, d), w_out[l], w_rg[l], b_rg[l], w_re[l],
                                       b_re[l], s)
        plan = _routing_plan(stats, b * s)
        xs = _dispatch(plan, meta, h2)
        y = _experts(plan["block_e"], plan["n_used"], xs, w1[l], w3[l], w2[l])
        x2d = _combine(plan, x1, meta, mod3, norm_f_g, y, s, final_norm=(l == depth - 1))
    return x2d.reshape(b, s, d)
```

```python
import functools

import jax
import jax.numpy as jnp
from jax import lax
from jax.experimental import pallas as pl
from jax.experimental.pallas import tpu as pltpu

F32 = jnp.float32
BF16 = jnp.bfloat16

D_MODEL = 1024
D_ATT = 512
ATT_HEADS = 8
ATT_HEAD_DIM = 64
D_MLSTM = 512
MLSTM_HEADS = 4
MLSTM_HEAD_DIM = 128
MOBA_BLOCK = 256
MOBA_TOPK = 3
ROPE_THETA = 10000.0
MLSTM_CHUNK = 128
MLSTM_CHUNKS_PER_STEP = 4
CONV_WIDTH = 4
N_GROUPS = 4
EXPERTS_PER_GROUP = 8
N_EXPERTS = N_GROUPS * EXPERTS_PER_GROUP
TOP_K_EXPERTS = 2
D_EXPERT = 512
MOE_BLOCK_ROWS = 512
NORM_EPS = 1e-6
NEG_INF = -1e30
LOG2_E = 1.4426950408889634
KV_CHUNK = 2
MOBA_GROUP_LANES = 512

LANES = 128
SUBLANES = 8
VMEM_LIMIT_BYTES = 56 * 1024 * 1024

INPROJ_TILE = 512
MOE_TILE = 512
RUN_ALIGN = 2 * SUBLANES
PERM_ROWS = TOP_K_EXPERTS * MOE_TILE + N_EXPERTS * RUN_ALIGN
CHUNKS_PER_TILE = PERM_ROWS // RUN_ALIGN
TAIL_PIECES = tuple(RUN_ALIGN << p for p in reversed(range(5)))
assert TAIL_PIECES[0] * 2 == MOE_BLOCK_ROWS
TILE_PIECES = tuple(RUN_ALIGN << p for p in reversed(range(7)))
ZERO_ROWS = TAIL_PIECES[0]
GATE_LANES = 2 * LANES
ROUTER_LANES = LANES
GROUP_LANE0 = N_EXPERTS


def _dot(a, b):
    return jnp.dot(a, b, preferred_element_type=F32)


def _dot_nt(a, b):
    return lax.dot_general(a, b, (((1,), (1,)), ((), ())), preferred_element_type=F32)


def _dot_tn(a, b):
    return lax.dot_general(a, b, (((0,), (0,)), ((), ())), preferred_element_type=F32)


def _split3(x):
    a = x.astype(BF16)
    r = x - a.astype(F32)
    b = r.astype(BF16)
    c = (r - b.astype(F32)).astype(BF16)
    return a, b, c


def _silu(x):
    return x * jax.nn.sigmoid(x)


def _log_sigmoid(x):
    return jnp.minimum(x, 0.0) - jnp.log1p(jnp.exp(-jnp.abs(x)))


def _params(semantics, vmem=VMEM_LIMIT_BYTES):
    return pltpu.CompilerParams(dimension_semantics=semantics, vmem_limit_bytes=vmem)


def _mod_kernel(c_ref, w_ref, b_ref, o_ref):
    sc = _silu(c_ref[...])
    o_ref[...] = jnp.dot(sc, w_ref[...], precision=lax.Precision.HIGHEST,
                         preferred_element_type=F32) + b_ref[...]


def _mod(c, w_ada, b_ada):
    b, d = c.shape
    n = w_ada.shape[1]
    tn = D_MODEL
    return pl.pallas_call(
        _mod_kernel,
        grid=(n // tn,),
        in_specs=[pl.BlockSpec((b, d), lambda i: (0, 0)),
                  pl.BlockSpec((d, tn), lambda i: (0, i)),
                  pl.BlockSpec((1, tn), lambda i: (0, i))],
        out_specs=pl.BlockSpec((b, tn), lambda i: (0, i)),
        out_shape=jax.ShapeDtypeStruct((b, n), F32),
        compiler_params=_params(("arbitrary",)),
        name="mod",
    )(c, w_ada, b_ada.reshape(1, n))


def _rms_modulate(x, g, shift, scale):
    y = x * lax.rsqrt(jnp.mean(x * x, axis=-1, keepdims=True) + NORM_EPS)
    return (y * g) * (1.0 + scale) + shift


def _rope(t, cos, sin, first_half):
    outs = []
    for c in range(t.shape[1] // LANES):
        tc = t[:, c * LANES:(c + 1) * LANES]
        rot = jnp.where(first_half, -pltpu.roll(tc, LANES - ATT_HEAD_DIM // 2, 1),
                        pltpu.roll(tc, ATT_HEAD_DIM // 2, 1))
        outs.append(tc * cos + rot * sin)
    return jnp.concatenate(outs, axis=1)


def _inproj_kernel(x_ref, pos_ref, mod_ref, g_ref, invf_ref, wq_ref, wk_ref, wvt_ref, wqk_ref,
                   wv_ref, wo_ref, wg_ref, q_ref, k_ref, vt_ref, qkm_ref, vm_ref, om_ref, gt_ref):
    d = x_ref.shape[2]
    x = x_ref[0]
    mod = mod_ref[0]
    h = _rms_modulate(x, g_ref[...], mod[:, 0:d], mod[:, d:2 * d])
    hb = h.astype(BF16)

    ang = pos_ref[0].astype(F32) * invf_ref[...]
    cos = jnp.cos(ang)
    sin = jnp.sin(ang)
    lane = lax.broadcasted_iota(jnp.int32, cos.shape, 1)
    first_half = (lane & (ATT_HEAD_DIM // 2)) == 0

    q = _rope(_dot(hb, wq_ref[...]), cos, sin, first_half)
    q_ref[0] = (q * (ATT_HEAD_DIM ** -0.5 * LOG2_E)).astype(BF16)
    k_ref[0] = _rope(_dot(hb, wk_ref[...]), cos, sin, first_half).astype(BF16)
    vt = _dot_nt(wvt_ref[...], hb).astype(BF16)
    for blk_i in range(vt_ref.shape[1]):
        vt_ref[0, blk_i] = vt[:, blk_i * MOBA_BLOCK:(blk_i + 1) * MOBA_BLOCK]
    qkm_ref[0] = _dot(hb, wqk_ref[...])
    vm_ref[0] = _dot(hb, wv_ref[...])
    om_ref[0] = _dot(hb, wo_ref[...])
    gt_ref[0] = _dot(hb, wg_ref[...])


def _inproj(x, positions, mod3, norm_g, w_in):
    b, s, d = x.shape
    tm = INPROJ_TILE
    per_tile = tm // MOBA_BLOCK
    nb = s // MOBA_BLOCK
    o = [0, D_ATT, 2 * D_ATT, 3 * D_ATT, 3 * D_ATT + 2 * D_MLSTM, 3 * D_ATT + 3 * D_MLSTM,
         3 * D_ATT + 4 * D_MLSTM, 3 * D_ATT + 4 * D_MLSTM + 2 * MLSTM_HEADS]
    wb = w_in.astype(BF16)
    wq, wk, wv_a, wqk, wv, wo, wg = (wb[:, o[i]:o[i + 1]] for i in range(7))
    wvt = wv_a.T
    lane_pad = ((0, 0), (0, LANES - MLSTM_HEADS))
    wg = jnp.concatenate([jnp.pad(wg[:, :MLSTM_HEADS], lane_pad), jnp.pad(wg[:, MLSTM_HEADS:], lane_pad)], axis=1)
    half = ATT_HEAD_DIM // 2
    inv_freq = ROPE_THETA ** (-jnp.arange(half, dtype=F32) / half)
    invf = jnp.tile(inv_freq, LANES // half).reshape(1, LANES)

    full = lambda a: pl.BlockSpec(a.shape, lambda bi, i: (0,) * a.ndim)
    tok = lambda w: pl.BlockSpec((1, tm, w), lambda bi, i: (bi, i, 0))
    out_shape = [jax.ShapeDtypeStruct((b, s, D_ATT), BF16),
                 jax.ShapeDtypeStruct((b, s, D_ATT), BF16),
                 jax.ShapeDtypeStruct((b, nb, D_ATT, MOBA_BLOCK), BF16),
                 jax.ShapeDtypeStruct((b, s, 2 * D_MLSTM), F32),
                 jax.ShapeDtypeStruct((b, s, D_MLSTM), F32),
                 jax.ShapeDtypeStruct((b, s, D_MLSTM), F32),
                 jax.ShapeDtypeStruct((b, s, GATE_LANES), F32)]
    return pl.pallas_call(
        _inproj_kernel,
        grid=(b, s // tm),
        in_specs=[tok(d), tok(1),
                  pl.BlockSpec((1, 1, mod3.shape[2]), lambda bi, i: (bi, 0, 0)),
                  full(norm_g), full(invf), full(wq), full(wk), full(wvt), full(wqk), full(wv),
                  full(wo), full(wg)],
        out_specs=[tok(D_ATT), tok(D_ATT),
                   pl.BlockSpec((1, per_tile, D_ATT, MOBA_BLOCK), lambda bi, i: (bi, i, 0, 0)),
                   tok(2 * D_MLSTM), tok(D_MLSTM), tok(D_MLSTM), tok(GATE_LANES)],
        out_shape=out_shape,
        compiler_params=_params(("arbitrary", "arbitrary")),
        name="inproj",
    )(x, positions.reshape(b, s, 1), mod3, norm_g, invf, wq, wk, wvt, wqk, wv, wo, wg)


def _moba_kernel(q_ref, k_ref, vt_ref, g_ref, o_ref, kmean_ref, sel_ref, st_ref, qh_ref, acc_ref):
    blk = MOBA_BLOCK
    nb = k_ref.shape[1] // blk
    j = pl.program_id(2)
    heads = q_ref.shape[2] // ATT_HEAD_DIM
    hpl = LANES // ATT_HEAD_DIM

    @pl.when(j == 0)
    def _():
        for n in range(nb):
            kb = k_ref[0, n * blk:(n + 1) * blk, :].astype(F32)
            kmean_ref[n:n + 1, :] = jnp.mean(kb, axis=0, keepdims=True)

    lane = lax.broadcasted_iota(jnp.int32, (blk, LANES), 1)
    blk_id = lax.broadcasted_iota(jnp.int32, (nb, blk), 0)
    km_hi = kmean_ref[...].astype(BF16)
    km_lo = (kmean_ref[...] - km_hi.astype(F32)).astype(BF16)
    past = blk_id < j
    tile = lambda a, hh: a[:, (hh // hpl) * LANES:(hh // hpl + 1) * LANES]

    gates = []
    for hh in range(heads):
        q = tile(q_ref[0], hh)
        in_head = (lane >= (hh % hpl) * ATT_HEAD_DIM) & (lane < (hh % hpl + 1) * ATT_HEAD_DIM)
        qh_ref[hh] = jnp.where(in_head, q, jnp.zeros_like(q))
        gates.append(_dot_nt(tile(km_hi, hh), qh_ref[hh]) + _dot_nt(tile(km_lo, hh), qh_ref[hh]))
    g = jnp.where(past[None], jnp.stack(gates), NEG_INF)
    sel = jnp.zeros(g.shape, F32)
    for _ in range(min(MOBA_TOPK, nb)):
        top = jnp.max(g, axis=1, keepdims=True)
        idx = jnp.min(jnp.where(g == top, blk_id[None], nb), axis=1, keepdims=True)
        pick = blk_id[None] == idx
        sel = jnp.where(pick, 1.0, sel)
        g = jnp.where(pick, -jnp.inf, g)
    sel_ref[:, 0:nb, :] = jnp.where((sel > 0.0) & past[None], 1.0, 0.0)
    sel_ref[:, nb:nb + SUBLANES, :] = jnp.zeros((heads, SUBLANES, blk), F32)

    kpos = lax.broadcasted_iota(jnp.int32, (blk, blk), 0)
    qpos = lax.broadcasted_iota(jnp.int32, (blk, blk), 1)
    causal_bias = jnp.where(kpos <= qpos, 0.0, NEG_INF)

    def slab_block(t, i):
        b_i = j - KV_CHUNK * t - (KV_CHUNK - 1 - i)
        return jnp.maximum(b_i, 0), jnp.where(b_i >= 0, b_i, nb)

    def score(t, slot, own_chunk):
        tops = [None] * heads
        slab_max = [[None] * KV_CHUNK for _ in range(heads)]
        for i in range(KV_CHUNK):
            src, row = slab_block(t, i)
            kb = k_ref[0, pl.ds(pl.multiple_of(src * blk, blk), blk), :]
            for hh in range(heads):
                st = _dot_nt(tile(kb, hh), qh_ref[hh])
                if own_chunk and i == KV_CHUNK - 1:
                    st = st + causal_bias
                    cmax = jnp.max(st, axis=0, keepdims=True)
                    smax = cmax
                else:
                    cmax = jnp.max(st, axis=0, keepdims=True)
                    smax = cmax + (sel_ref[hh, pl.ds(row, 1), :] - 1.0) * (-NEG_INF)
                st_ref[slot, hh, i * blk:(i + 1) * blk, :] = st
                slab_max[hh][i] = cmax
                tops[hh] = smax if tops[hh] is None else jnp.maximum(tops[hh], smax)
        return tuple(tops), tuple(tuple(r) for r in slab_max)

    def accumulate(t, slot, state, maxes):
        tops, slab_max = maxes
        new = []
        for hh in range(heads):
            m, l = state[2 * hh:2 * hh + 2]
            m_new = jnp.maximum(m, tops[hh])
            alpha = jnp.exp2(m - m_new)
            l = alpha * l
            acc = alpha * acc_ref[hh]
            for i in range(KV_CHUNK):
                src, row = slab_block(t, i)
                keep = sel_ref[hh, pl.ds(row, 1), :]
                if i == KV_CHUNK - 1:
                    keep = jnp.where(t == 0, 1.0, keep)
                p = jnp.exp2(st_ref[slot, hh, i * blk:(i + 1) * blk, :] - jnp.maximum(m_new, slab_max[hh][i]))
                l = l + keep * jnp.sum(p, axis=0, keepdims=True)
                v_h = vt_ref[0, src][hh * ATT_HEAD_DIM:(hh + 1) * ATT_HEAD_DIM, :]
                acc = acc + keep * _dot(v_h, p.astype(BF16))
            acc_ref[hh] = acc
            new += [m_new, l]
        return tuple(new)

    state = ()
    for hh in range(heads):
        state += (jnp.full((1, blk), NEG_INF, F32), jnp.zeros((1, blk), F32))
        acc_ref[hh] = jnp.zeros((ATT_HEAD_DIM, blk), F32)
    last = j // KV_CHUNK

    def stage_pair(t, slot, state, maxes):
        nxt = score(t + 1, 1 - slot, False)
        return accumulate(t, slot, state, maxes), nxt

    def body(t, carry):
        return lax.cond(t % 2 == 0, functools.partial(stage_pair, t, 0), functools.partial(stage_pair, t, 1),
                        *carry)

    state, maxes = lax.fori_loop(0, last, body, (state, score(0, 0, True)))
    state = lax.cond(last % 2 == 0, functools.partial(accumulate, last, 0), functools.partial(accumulate, last, 1),
                     state, maxes)

    outs = []
    for hh in range(heads):
        o_h = acc_ref[hh] / state[2 * hh + 1]
        ms = jnp.mean(o_h * o_h, axis=0, keepdims=True)
        outs.append(o_h * lax.rsqrt(ms + NORM_EPS))
    o_ref[0] = jnp.concatenate(outs, axis=0).T * g_ref[...]


def _moba(q, k, vt, g_att):
    b, s, w = q.shape
    blk = MOBA_BLOCK
    nb = s // blk
    gl = MOBA_GROUP_LANES
    groups = w // gl
    heads = gl // ATT_HEAD_DIM
    return pl.pallas_call(
        _moba_kernel,
        grid=(b, groups, nb),
        in_specs=[pl.BlockSpec((1, blk, gl), lambda bi, p, j: (bi, j, p)),
                  pl.BlockSpec((1, s, gl), lambda bi, p, j: (bi, 0, p)),
                  pl.BlockSpec((1, nb, gl, blk), lambda bi, p, j: (bi, 0, p, 0)),
                  pl.BlockSpec((1, gl), lambda bi, p, j: (0, p))],
        out_specs=pl.BlockSpec((1, blk, gl), lambda bi, p, j: (bi, j, p)),
        out_shape=jax.ShapeDtypeStruct((b, s, w), F32),
        scratch_shapes=[pltpu.VMEM((nb, gl), F32),
                        pltpu.VMEM((heads, nb + SUBLANES, blk), F32),
                        pltpu.VMEM((2, heads, KV_CHUNK * blk, blk), F32),
                        pltpu.VMEM((heads, blk, LANES), BF16),
                        pltpu.VMEM((heads, ATT_HEAD_DIM, blk), F32)],
        compiler_params=_params(("arbitrary", "arbitrary", "arbitrary")),
        name="moba",
    )(q, k, vt, g_att)


def _mlstm_kernel(qkm_ref, vm_ref, om_ref, gt_ref, cw_ref, cb_ref, bg_ref, gm_ref, bsel_ref, o_ref,
                  uext_ref, c_ref, n_ref, m_ref):
    L = MLSTM_CHUNK
    dh = MLSTM_HEAD_DIM
    nh = MLSTM_HEADS
    assert L == dh == LANES and 2 * nh == SUBLANES
    c = pl.program_id(1)

    @pl.when(c == 0)
    def _():
        uext_ref[0:SUBLANES, :] = jnp.zeros((SUBLANES, uext_ref.shape[1]), F32)
        c_ref[...] = jnp.zeros(c_ref.shape, F32)
        n_ref[...] = jnp.zeros(n_ref.shape, F32)
        m_ref[...] = jnp.zeros(m_ref.shape, F32)

    for sub in range(qkm_ref.shape[1] // L):
        _mlstm_chunk(slice(sub * L, (sub + 1) * L), qkm_ref, vm_ref, om_ref, gt_ref, cw_ref, cb_ref, bg_ref, gm_ref,
                     bsel_ref, o_ref, uext_ref, c_ref, n_ref, m_ref)


def _mlstm_chunk(rows, qkm_ref, vm_ref, om_ref, gt_ref, cw_ref, cb_ref, bg_ref, gm_ref, bsel_ref, o_ref,
                 uext_ref, c_ref, n_ref, m_ref):
    L = MLSTM_CHUNK
    dh = MLSTM_HEAD_DIM
    nh = MLSTM_HEADS

    u = qkm_ref[0, rows, :]
    uext_ref[SUBLANES:SUBLANES + L, :] = u
    cw = cw_ref[...]
    conv = cb_ref[...] + cw[CONV_WIDTH - 1:CONV_WIDTH, :] * u
    for dlt in range(1, CONV_WIDTH):
        conv = conv + cw[CONV_WIDTH - 1 - dlt:CONV_WIDTH - dlt, :] * uext_ref[SUBLANES - dlt:SUBLANES - dlt + L, :]
    uext_ref[0:SUBLANES, :] = u[L - SUBLANES:L, :]
    act = _silu(conv)

    gates = gt_ref[0, rows, :] + bg_ref[...]
    ig = gates[:, 0:LANES]
    tpos = lax.broadcasted_iota(jnp.int32, (L, L), 0)
    spos = lax.broadcasted_iota(jnp.int32, (L, L), 1)
    causal = spos <= tpos
    tril = jnp.where(causal, 1.0, 0.0).astype(BF16)
    cum = sum(_dot(tril, part) for part in _split3(_log_sigmoid(gates[:, LANES:])))
    a_tot = cum[L - 1:L, :]
    m_prev = m_ref[0:1, :]
    resid = ig - cum
    pmax = resid
    shift = 1
    while shift < L:
        pmax = jnp.maximum(pmax, jnp.where(tpos >= shift, pltpu.roll(pmax, shift, 0), -jnp.inf))
        shift *= 2
    inter = cum + m_prev
    m_t = jnp.maximum(inter, cum + pmax)
    g_loc = a_tot - cum + ig
    m_loc = jnp.max(g_loc, axis=0, keepdims=True)
    w_loc = jnp.exp(g_loc - m_loc)
    head_lane = spos < nh
    packed = jnp.where(head_lane, cum - m_t, 0.0)
    for i, qty in enumerate((inter - m_t, -m_t, g_loc - m_loc), start=1):
        packed = packed + pltpu.roll(jnp.where(head_lane, qty, 0.0), SUBLANES * i, 1)
    per_t_cols = jnp.concatenate(_split3(packed), axis=1)
    per_t_rows = (jnp.where(head_lane, resid, 0.0) + pltpu.roll(jnp.where(head_lane, w_loc, 0.0), SUBLANES, 1)).T

    m_new = jnp.maximum(a_tot + m_prev, m_loc)
    s_prev = jnp.exp(a_tot + m_prev - m_new)
    s_loc = jnp.exp(m_loc - m_new)
    m_ref[0:1, :] = m_new

    vm = vm_ref[0, rows, :]
    om = om_ref[0, rows, :]
    heads = lambda a, off=0: jnp.stack([a[:, off + h * dh:off + (h + 1) * dh] for h in range(nh)])
    bdot = lambda a, b, ca, cb: lax.dot_general(a, b, (((ca,), (cb,)), ((0,), (0,))), preferred_element_type=F32)
    q4 = heads(act)
    k4 = heads(act, nh * dh) * (dh ** -0.5)
    v4 = heads(vm)
    qb, kb, vb = q4.astype(BF16), k4.astype(BF16), v4.astype(BF16)
    c_prev = c_ref[...]
    n_all = n_ref[...]
    n_prev = jnp.stack([n_all[h:h + 1, :] for h in range(nh)])
    bcast = _dot(per_t_cols, bsel_ref[...])
    qty = lambda i: jnp.stack([bcast[:, (4 * h + i) * dh:(4 * h + i + 1) * dh] for h in range(nh)])
    decay_b = qty(0)
    w_inter_b, floor_b, w_loc_b = jnp.exp(qty(1)), jnp.exp(qty(2)), jnp.exp(qty(3))
    resid_rows = jnp.stack([per_t_rows[h:h + 1, :] for h in range(nh)])
    w_rows = jnp.stack([per_t_rows[SUBLANES + h:SUBLANES + h + 1, :] for h in range(nh)])

    s_qk = bdot(qb, kb, 2, 2) * jnp.where(causal[None], jnp.exp(decay_b + resid_rows), 0.0)
    intra = bdot(s_qk.astype(BF16), jnp.concatenate([vb, jnp.ones((nh, L, dh), BF16)], axis=2), 2, 1)
    state_rows = jnp.concatenate([c_prev, jnp.broadcast_to(n_prev, (nh, dh, dh))], axis=1).astype(BF16)
    carried = bdot(qb, state_rows, 2, 2)
    num = intra[:, :, :dh] + w_inter_b * carried[:, :, :dh]
    den = intra[:, :, dh:] + w_inter_b * carried[:, :, dh:]
    hout = num / jnp.maximum(jnp.abs(den), floor_b)

    sq = (hout * hout).astype(BF16).reshape(nh * L, dh)
    mean_sq = (_dot(sq, jnp.ones((dh, dh), BF16)) * (1.0 / dh)).reshape(nh, L, dh)
    gm = jnp.stack([gm_ref[:, h * dh:(h + 1) * dh] for h in range(nh)])
    out = hout * lax.rsqrt(mean_sq + NORM_EPS) * gm * jax.nn.sigmoid(heads(om))
    for h in range(nh):
        o_ref[0, rows, h * dh:(h + 1) * dh] = out[h]

    vw = (v4 * w_loc_b).astype(BF16)
    for h in range(nh):
        c_loc = _dot_tn(vw[h], kb[h])
        n_loc = _dot(jnp.broadcast_to(w_rows[h], (SUBLANES, L)).astype(BF16), kb[h])[0:1, :]
        c_ref[h] = s_prev[:, h:h + 1] * c_prev[h] + s_loc[:, h:h + 1] * c_loc
        n_ref[h:h + 1, :] = s_prev[:, h:h + 1] * n_all[h:h + 1, :] + s_loc[:, h:h + 1] * n_loc


def _mlstm(qkm, vm, om, gates, conv_w, conv_b, b_gate, g_m):
    b, s, _ = qkm.shape
    L = MLSTM_CHUNK
    step_rows = MLSTM_CHUNKS_PER_STEP * L
    nc = s // step_rows
    lane_pad = ((0, 0), (0, LANES - MLSTM_HEADS))
    bg = b_gate.reshape(1, -1)
    bg = jnp.concatenate([jnp.pad(bg[:, :MLSTM_HEADS], lane_pad), jnp.pad(bg[:, MLSTM_HEADS:], lane_pad)], axis=1)
    tok = lambda w: pl.BlockSpec((1, step_rows, w), lambda bi, c: (bi, c, 0))
    full = lambda a: pl.BlockSpec(a.shape, lambda bi, c: (0,) * a.ndim)
    cb = conv_b.reshape(1, -1)
    gm = g_m.reshape(1, -1)
    n_qty = 4
    row = jnp.arange(3 * LANES, dtype=jnp.int32) % LANES
    col_blk = jnp.arange(MLSTM_HEADS * n_qty * MLSTM_HEAD_DIM, dtype=jnp.int32) // MLSTM_HEAD_DIM
    bsel = (((row // SUBLANES)[:, None] == (col_blk % n_qty)[None, :])
            & ((row % SUBLANES)[:, None] == (col_blk // n_qty)[None, :])).astype(BF16)
    return pl.pallas_call(
        _mlstm_kernel,
        grid=(b, nc),
        in_specs=[tok(2 * D_MLSTM), tok(D_MLSTM), tok(D_MLSTM), tok(GATE_LANES),
                  full(conv_w), full(cb), full(bg), full(gm), full(bsel)],
        out_specs=tok(D_MLSTM),
        out_shape=jax.ShapeDtypeStruct((b, s, D_MLSTM), F32),
        scratch_shapes=[pltpu.VMEM((SUBLANES + L, 2 * D_MLSTM), F32),
                        pltpu.VMEM((MLSTM_HEADS, MLSTM_HEAD_DIM, MLSTM_HEAD_DIM), F32),
                        pltpu.VMEM((SUBLANES, MLSTM_HEAD_DIM), F32),
                        pltpu.VMEM((SUBLANES, LANES), F32)],
        compiler_params=_params(("arbitrary", "arbitrary")),
        name="mlstm",
    )(qkm, vm, om, gates, conv_w, cb, bg, gm, bsel)


META_E0, META_E1, META_W0, META_W1, META_L0, META_L1 = range(6)
STAT_LEN, STAT_START = 0, 1


def _outproj_kernel(attn_ref, hm_ref, x_ref, mod_ref, g_ref, wout_ref, wr_hi_ref, wr_lo_ref, br_ref,
                    x1_ref, h2_ref, meta_ref, stat_ref):
    d = x_ref.shape[1]
    tm = x_ref.shape[0]

    mod = mod_ref[0]
    y = (_dot(attn_ref[...].astype(BF16), wout_ref[0:D_ATT, :])
         + _dot(hm_ref[...].astype(BF16), wout_ref[D_ATT:, :]))
    x1 = x_ref[...] + mod[:, 2 * d:3 * d] * y
    x1_ref[...] = x1
    h2 = _rms_modulate(x1, g_ref[...], mod[:, 3 * d:4 * d], mod[:, 4 * d:5 * d])
    h2_ref[...] = h2.astype(BF16)

    h_hi = h2.astype(BF16)
    h_lo = (h2 - h_hi.astype(F32)).astype(BF16)
    logit = (_dot(h_hi, wr_hi_ref[...]) + _dot(h_lo, wr_hi_ref[...]) + _dot(h_hi, wr_lo_ref[...])
             + br_ref[...])
    lane = lax.broadcasted_iota(jnp.int32, logit.shape, 1)
    big = jnp.int32(ROUTER_LANES)

    is_g = (lane >= GROUP_LANE0) & (lane < GROUP_LANE0 + N_GROUPS)
    gmax = jnp.max(jnp.where(is_g, logit, -jnp.inf), axis=1, keepdims=True)
    gsum = jnp.sum(jnp.where(is_g, jnp.exp(logit - gmax), 0.0), axis=1, keepdims=True)
    g_w = 1.0 / gsum
    g_idx = jnp.min(jnp.where(is_g & (logit == gmax), lane, big), axis=1, keepdims=True) - GROUP_LANE0

    in_grp = (lane < N_EXPERTS) & ((lane // EXPERTS_PER_GROUP) == g_idx)
    emax = jnp.max(jnp.where(in_grp, logit, -jnp.inf), axis=1, keepdims=True)
    esum = jnp.sum(jnp.where(in_grp, jnp.exp(logit - emax), 0.0), axis=1, keepdims=True)
    e0 = jnp.min(jnp.where(in_grp & (logit == emax), lane, big), axis=1, keepdims=True)
    rest = in_grp & (lane != e0)
    e2max = jnp.max(jnp.where(rest, logit, -jnp.inf), axis=1, keepdims=True)
    e1 = jnp.min(jnp.where(rest & (logit == e2max), lane, big), axis=1, keepdims=True)
    p0 = 1.0 / esum
    p1 = jnp.exp(e2max - emax) / esum
    w0 = g_w * p0 / (p0 + p1)
    w1 = g_w * p1 / (p0 + p1)

    memb = jnp.where((lane == e0) | (lane == e1), 1.0, 0.0)
    tpos = lax.broadcasted_iota(jnp.int32, (tm, tm), 0)
    spos = lax.broadcasted_iota(jnp.int32, (tm, tm), 1)
    before = jnp.where(spos < tpos, 1.0, 0.0).astype(BF16)
    earlier = _dot(before, memb.astype(BF16))
    count = jnp.sum(memb, axis=0, keepdims=True).astype(jnp.int32)
    run_len = (((count + (RUN_ALIGN - 1)) // RUN_ALIGN) * RUN_ALIGN).astype(F32)
    epos = lax.broadcasted_iota(jnp.int32, (ROUTER_LANES, ROUTER_LANES), 0)
    fpos = lax.broadcasted_iota(jnp.int32, (ROUTER_LANES, ROUTER_LANES), 1)
    lower_e = jnp.where(epos < fpos, 1.0, 0.0).astype(BF16)
    run_start = _dot(jnp.broadcast_to(run_len, (SUBLANES, ROUTER_LANES)).astype(BF16), lower_e)[0:1, :]
    row = run_start + earlier
    l0 = jnp.sum(jnp.where(lane == e0, row, 0.0), axis=1, keepdims=True)
    l1 = jnp.sum(jnp.where(lane == e1, row, 0.0), axis=1, keepdims=True)

    meta = jnp.zeros(logit.shape, F32)
    for slot, val in ((META_E0, e0.astype(F32)), (META_E1, e1.astype(F32)), (META_W0, w0), (META_W1, w1),
                      (META_L0, l0), (META_L1, l1)):
        meta = jnp.where(lane == slot, val, meta)
    meta_ref[...] = meta
    srow = lax.broadcasted_iota(jnp.int32, (SUBLANES, ROUTER_LANES), 0)
    stat_ref[0] = jnp.where(srow == STAT_LEN, run_len, jnp.where(srow == STAT_START, run_start, 0.0))


def _outproj(attn, hm, x2d, mod3, norm_g, w_out, w_rg, b_rg, w_re, b_re, seq):
    t, d = x2d.shape
    tm = MOE_TILE
    per_batch = seq // tm
    wr = jnp.pad(jnp.concatenate([w_re, w_rg], axis=1), ((0, 0), (0, ROUTER_LANES - N_EXPERTS - N_GROUPS)))
    br = jnp.pad(jnp.concatenate([b_re, b_rg]).reshape(1, -1), ((0, 0), (0, ROUTER_LANES - N_EXPERTS - N_GROUPS)))
    wr_hi = wr.astype(BF16)
    wr_lo = (wr - wr_hi.astype(F32)).astype(BF16)
    wout = w_out.astype(BF16)
    tok = lambda w: pl.BlockSpec((tm, w), lambda i: (i, 0))
    full = lambda a: pl.BlockSpec(a.shape, lambda i: (0,) * a.ndim)
    return pl.pallas_call(
        _outproj_kernel,
        grid=(t // tm,),
        in_specs=[tok(D_ATT), tok(D_MLSTM), tok(d),
                  pl.BlockSpec((1, 1, mod3.shape[2]), lambda i: (i // per_batch, 0, 0)),
                  full(norm_g), full(wout), full(wr_hi), full(wr_lo), full(br)],
        out_specs=[tok(d), tok(d), tok(ROUTER_LANES),
                   pl.BlockSpec((1, SUBLANES, ROUTER_LANES), lambda i: (i, 0, 0))],
        out_shape=[jax.ShapeDtypeStruct((t, d), F32), jax.ShapeDtypeStruct((t, d), BF16),
                   jax.ShapeDtypeStruct((t, ROUTER_LANES), F32),
                   jax.ShapeDtypeStruct((t // tm, SUBLANES, ROUTER_LANES), F32)],
        compiler_params=_params(("arbitrary",)),
        name="outproj",
    )(attn, hm, x2d, mod3, norm_g, wout, wr_hi, wr_lo, br)


def _for_each_piece(length, pieces, fn):
    pos = jnp.int32(0)
    for size in pieces:
        take = length & size

        @pl.when(take != 0)
        def _(pos=pos, size=size):
            fn(pl.multiple_of(pos, RUN_ALIGN), size)

        pos = pos + take


def _for_each_chunk(tile, rows_ref, cdst_ref, fn):
    def body(c, carry):
        fn(pl.multiple_of(c * RUN_ALIGN, RUN_ALIGN),
           pl.multiple_of(cdst_ref[tile * CHUNKS_PER_TILE + c], RUN_ALIGN))
        return carry

    lax.fori_loop(0, rows_ref[tile] // RUN_ALIGN, body, 0)


def _dispatch_kernel(cdst_ref, rows_ref, tail_ref, tlen_ref, nu_ref, meta_ref, h2_ref, xs_ref,
                     perm_ref, zero_ref, sem, zsem):
    i = pl.program_id(0)
    rb = MOE_BLOCK_ROWS
    n_blocks = xs_ref.shape[0] // rb

    @pl.when(i == 0)
    def _():
        zero_ref[...] = jnp.zeros(zero_ref.shape, zero_ref.dtype)

        def zero_copy(row, size):
            return pltpu.make_async_copy(zero_ref.at[pl.ds(0, size)], xs_ref.at[pl.ds(row, size)], zsem)

        def fill(op):
            def per_expert(e, carry):
                start = tail_ref[e]
                _for_each_piece(tlen_ref[e], TAIL_PIECES,
                                lambda pos, size: op(zero_copy(pl.multiple_of(start + pos, RUN_ALIGN), size)))
                return carry

            lax.fori_loop(0, N_EXPERTS, per_expert, 0)

            def per_block(blk, carry):
                for part in range(rb // ZERO_ROWS):
                    op(zero_copy(pl.multiple_of(blk * rb + part * ZERO_ROWS, ZERO_ROWS), ZERO_ROWS))
                return carry

            lax.fori_loop(nu_ref[0], n_blocks, per_block, 0)

        fill(lambda cp: cp.start())
        fill(lambda cp: cp.wait())

    meta_t = meta_ref[...].T
    l0 = meta_t[META_L0:META_L0 + 1, :].astype(jnp.int32)
    l1 = meta_t[META_L1:META_L1 + 1, :].astype(jnp.int32)
    rpos = lax.broadcasted_iota(jnp.int32, (perm_ref.shape[1], meta_ref.shape[0]), 0)
    onehot = jnp.where((rpos == l0) | (rpos == l1), 1.0, 0.0).astype(BF16)
    slot = i % 2
    perm_ref[slot] = _dot(onehot, h2_ref[...]).astype(BF16)

    def run_copy(s, loc, glob, size):
        return pltpu.make_async_copy(perm_ref.at[s, pl.ds(loc, size)], xs_ref.at[pl.ds(glob, size)], sem.at[s])

    def wait_tile(tile, s):
        _for_each_piece(rows_ref[tile], TILE_PIECES, lambda pos, size: run_copy(s, 0, 0, size).wait())

    @pl.when(i > 0)
    def _():
        wait_tile(i - 1, 1 - slot)

    _for_each_chunk(i, rows_ref, cdst_ref, lambda loc, glob: run_copy(slot, loc, glob, RUN_ALIGN).start())

    @pl.when(i == pl.num_programs(0) - 1)
    def _():
        wait_tile(i, slot)


def _dispatch(plan, meta, h2):
    t, d = h2.shape
    tm = MOE_TILE
    grid_spec = pltpu.PrefetchScalarGridSpec(
        num_scalar_prefetch=5,
        grid=(t // tm,),
        in_specs=[pl.BlockSpec((tm, ROUTER_LANES), lambda i, *_: (i, 0)),
                  pl.BlockSpec((tm, d), lambda i, *_: (i, 0))],
        out_specs=pl.BlockSpec(memory_space=pl.ANY),
        scratch_shapes=[pltpu.VMEM((2, PERM_ROWS, d), BF16), pltpu.VMEM((ZERO_ROWS, d), BF16),
                        pltpu.SemaphoreType.DMA((2,)), pltpu.SemaphoreType.DMA(())],
    )
    return pl.pallas_call(
        _dispatch_kernel,
        grid_spec=grid_spec,
        out_shape=jax.ShapeDtypeStruct((plan["n_rows"], d), BF16),
        compiler_params=_params(("arbitrary",)),
        name="dispatch",
    )(plan["chunk_dst"], plan["tile_rows"], plan["tail_start"], plan["tail_len"], plan["n_used"], meta, h2)


def _expert_kernel(be_ref, nu_ref, xs_ref, w1_ref, w3_ref, w2_ref, y_ref, w1b_ref, w3b_ref, w2b_ref):
    i = pl.program_id(0)
    used = i < nu_ref[0]

    @pl.when(used & ((i == 0) | (be_ref[i] != be_ref[jnp.maximum(i, 1) - 1])))
    def _():
        w1b_ref[...] = w1_ref[0].astype(BF16)
        w3b_ref[...] = w3_ref[0].astype(BF16)
        w2b_ref[...] = w2_ref[0].astype(BF16)

    @pl.when(used)
    def _():
        x = xs_ref[...]
        a = _dot(x, w1b_ref[...])
        b = _dot(x, w3b_ref[...])
        y_ref[...] = _dot((_silu(a) * b).astype(BF16), w2b_ref[...]).astype(y_ref.dtype)

    @pl.when(jnp.logical_not(used))
    def _():
        y_ref[...] = jnp.zeros(y_ref.shape, y_ref.dtype)


def _experts(block_e, n_used, xs, w1, w3, w2):
    n_rows, d = xs.shape
    rb = MOE_BLOCK_ROWS
    de = w1.shape[2]
    used = lambda i, be, nu: jnp.minimum(i, nu[0] - 1)
    grid_spec = pltpu.PrefetchScalarGridSpec(
        num_scalar_prefetch=2,
        grid=(n_rows // rb,),
        in_specs=[pl.BlockSpec((rb, d), lambda i, be, nu: (used(i, be, nu), 0)),
                  pl.BlockSpec((1, d, de), lambda i, be, nu: (be[used(i, be, nu)], 0, 0)),
                  pl.BlockSpec((1, d, de), lambda i, be, nu: (be[used(i, be, nu)], 0, 0)),
                  pl.BlockSpec((1, de, d), lambda i, be, nu: (be[used(i, be, nu)], 0, 0))],
        out_specs=pl.BlockSpec((rb, d), lambda i, be, nu: (i, 0)),
        scratch_shapes=[pltpu.VMEM((d, de), BF16), pltpu.VMEM((d, de), BF16), pltpu.VMEM((de, d), BF16)],
    )
    return pl.pallas_call(
        _expert_kernel,
        grid_spec=grid_spec,
        out_shape=jax.ShapeDtypeStruct((n_rows, d), BF16),
        compiler_params=_params(("arbitrary",)),
        name="experts",
    )(block_e, n_used, xs, w1, w3, w2)


def _combine_kernel(final_norm, cdst_ref, rows_ref, x1_ref, meta_ref, mod_ref, g_ref, y_ref, o_ref, ybuf_ref, sem):
    tc, d = x1_ref.shape
    i = pl.program_id(0)

    slot = i % 2

    def run_copy(s, loc, glob, size):
        return pltpu.make_async_copy(y_ref.at[pl.ds(glob, size)], ybuf_ref.at[s, pl.ds(loc, size)], sem.at[s])

    def start_tile(tile, s):
        _for_each_chunk(tile, rows_ref, cdst_ref, lambda loc, glob: run_copy(s, loc, glob, RUN_ALIGN).start())

    @pl.when(i == 0)
    def _():
        ybuf_ref[...] = jnp.zeros(ybuf_ref.shape, ybuf_ref.dtype)
        start_tile(0, 0)

    @pl.when(i + 1 < pl.num_programs(0))
    def _():
        start_tile(i + 1, 1 - slot)

    _for_each_piece(rows_ref[i], TILE_PIECES, lambda pos, size: run_copy(slot, 0, 0, size).wait())

    meta = meta_ref[...]
    yb = ybuf_ref[slot]
    rpos = lax.broadcasted_iota(jnp.int32, (tc, ybuf_ref.shape[1]), 1)
    sel = jnp.zeros(rpos.shape, F32)
    for l_lane, w_lane in ((META_L0, META_W0), (META_L1, META_W1)):
        sel = jnp.where(rpos == meta[:, l_lane:l_lane + 1].astype(jnp.int32), meta[:, w_lane:w_lane + 1], sel)
    moe = _dot(sel.astype(BF16), yb)
    x2 = x1_ref[...] + mod_ref[0][:, 5 * d:6 * d] * moe
    if final_norm:
        x2 = x2 * lax.rsqrt(jnp.mean(x2 * x2, axis=-1, keepdims=True) + NORM_EPS) * g_ref[...]
    o_ref[...] = x2


def _combine(plan, x1, meta, mod3, norm_f_g, y, seq, final_norm):
    t, d = x1.shape
    tc = MOE_TILE
    per_batch = seq // tc
    gf = norm_f_g.reshape(1, d)
    grid_spec = pltpu.PrefetchScalarGridSpec(
        num_scalar_prefetch=2,
        grid=(t // tc,),
        in_specs=[pl.BlockSpec((tc, d), lambda i, *_: (i, 0)),
                  pl.BlockSpec((tc, ROUTER_LANES), lambda i, *_: (i, 0)),
                  pl.BlockSpec((1, 1, mod3.shape[2]), lambda i, *_: (i // per_batch, 0, 0)),
                  pl.BlockSpec(gf.shape, lambda i, *_: (0, 0)),
                  pl.BlockSpec(memory_space=pl.ANY)],
        out_specs=pl.BlockSpec((tc, d), lambda i, *_: (i, 0)),
        scratch_shapes=[pltpu.VMEM((2, PERM_ROWS, d), BF16), pltpu.SemaphoreType.DMA((2,))],
    )
    return pl.pallas_call(
        functools.partial(_combine_kernel, final_norm),
        grid_spec=grid_spec,
        out_shape=jax.ShapeDtypeStruct((t, d), F32),
        compiler_params=_params(("arbitrary",)),
        name="combine",
    )(plan["chunk_dst"], plan["tile_rows"], x1, meta, mod3, gf, y)


def _routing_plan(stats, n_tokens):
    rb = MOE_BLOCK_ROWS
    run_len = stats[:, STAT_LEN, :N_EXPERTS].astype(jnp.int32)
    run_loc = stats[:, STAT_START, :N_EXPERTS].astype(jnp.int32)
    n_tiles = run_len.shape[0]
    total = jnp.sum(run_len, axis=0)
    padded = ((total + rb - 1) // rb) * rb
    pad_end = jnp.cumsum(padded)
    pad_start = pad_end - padded
    run_glob = pad_start[None, :] + jnp.cumsum(run_len, axis=0) - run_len
    n_rows = n_tokens * TOP_K_EXPERTS + n_tiles * N_EXPERTS * (RUN_ALIGN - 1) + N_EXPERTS * (rb - 1)
    n_rows = ((n_rows + rb - 1) // rb) * rb
    block_row = jnp.arange(n_rows // rb, dtype=jnp.int32) * rb
    block_e = jnp.minimum(jnp.sum((pad_end[None, :] <= block_row[:, None]).astype(jnp.int32), axis=1),
                          N_EXPERTS - 1)
    local_row = jnp.arange(CHUNKS_PER_TILE, dtype=jnp.int32) * RUN_ALIGN
    chunk_e = jnp.minimum(jnp.sum(((run_loc + run_len)[:, None, :] <= local_row[None, :, None]).astype(jnp.int32),
                                  axis=-1), N_EXPERTS - 1)
    shift = jnp.sum(jnp.where(chunk_e[..., None] == jnp.arange(N_EXPERTS, dtype=jnp.int32),
                              (run_glob - run_loc)[:, None, :], 0), axis=-1)
    chunk_dst = shift + local_row[None, :]
    return dict(chunk_dst=chunk_dst.reshape(-1), tile_rows=jnp.sum(run_len, axis=1),
                tail_start=pad_start + total, tail_len=padded - total,
                n_used=(pad_end[-1:] // rb).astype(jnp.int32), block_e=block_e.astype(jnp.int32), n_rows=n_rows)


def kernel(x, c, positions, w_ada, b_ada, norm1_g, w_in, b_gate, conv_w, conv_b, attn_out_g, mlstm_out_g,
           w_out, norm2_g, w_rg, b_rg, w_re, b_re, w1, w3, w2, norm_f_g):
    b, s, d = x.shape
    depth = w_ada.shape[0]
    assert d == D_MODEL and s % MOE_TILE == 0 and s % (KV_CHUNK * MOBA_BLOCK) == 0
    x2d = x.reshape(b * s, d)
    for l in range(depth):
        mod3 = _mod(c, w_ada[l], b_ada[l]).reshape(b, 1, 6 * d)
        q, k, vt, qkm, vm, om, gates = _inproj(x2d.reshape(b, s, d), positions, mod3,
                                              norm1_g[l].reshape(1, d), w_in[l])
        attn = _moba(q, k, vt, attn_out_g[l].reshape(1, D_ATT))
        hm = _mlstm(qkm, vm, om, gates, conv_w[l], conv_b[l], b_gate[l], mlstm_out_g[l])
        x1, h2, meta, stats = _outproj(attn.reshape(b * s, D_ATT), hm.reshape(b * s, D_MLSTM), x2d, mod3,
                                       norm2_g[l].reshape(1, d), w_out[l], w_rg[l], b_rg[l], w_re[l],
                                       b_re[l], s)
        plan = _routing_plan(stats, b * s)
        xs = _dispatch(plan, meta, h2)
        y = _experts(plan["block_e"], plan["n_used"], xs, w1[l], w3[l], w2[l])
        x2d = _combine(plan, x1, meta, mod3, norm_f_g, y, s, final_norm=(l == depth - 1))
    return x2d.reshape(b, s, d)
```

```python
import functools

import jax
import jax.numpy as jnp
from jax import lax
from jax.experimental import pallas as pl
from jax.experimental.pallas import tpu as pltpu

F32 = jnp.float32
BF16 = jnp.bfloat16

D_MODEL = 1024
D_ATT = 512
ATT_HEADS = 8
ATT_HEAD_DIM = 64
D_MLSTM = 512
MLSTM_HEADS = 4
MLSTM_HEAD_DIM = 128
MOBA_BLOCK = 256
MOBA_TOPK = 3
ROPE_THETA = 10000.0
MLSTM_CHUNK = 128
MLSTM_CHUNKS_PER_STEP = 4
CONV_WIDTH = 4
N_GROUPS = 4
EXPERTS_PER_GROUP = 8
N_EXPERTS = N_GROUPS * EXPERTS_PER_GROUP
TOP_K_EXPERTS = 2
D_EXPERT = 512
MOE_BLOCK_ROWS = 512
NORM_EPS = 1e-6
NEG_INF = -1e30
LOG2_E = 1.4426950408889634
KV_CHUNK = 2
MOBA_GROUP_LANES = 512

LANES = 128
SUBLANES = 8
VMEM_LIMIT_BYTES = 56 * 1024 * 1024

INPROJ_TILE = 512
MOE_TILE = 512
RUN_ALIGN = 2 * SUBLANES
PERM_ROWS = TOP_K_EXPERTS * MOE_TILE + N_EXPERTS * RUN_ALIGN
CHUNKS_PER_TILE = PERM_ROWS // RUN_ALIGN
TAIL_PIECES = tuple(RUN_ALIGN << p for p in reversed(range(5)))
assert TAIL_PIECES[0] * 2 == MOE_BLOCK_ROWS
TILE_PIECES = tuple(RUN_ALIGN << p for p in reversed(range(7)))
ZERO_ROWS = TAIL_PIECES[0]
GATE_LANES = 2 * LANES
ROUTER_LANES = LANES
GROUP_LANE0 = N_EXPERTS


def _dot(a, b):
    return jnp.dot(a, b, preferred_element_type=F32)


def _dot_nt(a, b):
    return lax.dot_general(a, b, (((1,), (1,)), ((), ())), preferred_element_type=F32)


def _dot_tn(a, b):
    return lax.dot_general(a, b, (((0,), (0,)), ((), ())), preferred_element_type=F32)


def _split3(x):
    a = x.astype(BF16)
    r = x - a.astype(F32)
    b = r.astype(BF16)
    c = (r - b.astype(F32)).astype(BF16)
    return a, b, c


def _silu(x):
    return x * jax.nn.sigmoid(x)


def _log_sigmoid(x):
    return jnp.minimum(x, 0.0) - jnp.log1p(jnp.exp(-jnp.abs(x)))


def _params(semantics, vmem=VMEM_LIMIT_BYTES):
    return pltpu.CompilerParams(dimension_semantics=semantics, vmem_limit_bytes=vmem)


def _mod_kernel(c_ref, w_ref, b_ref, o_ref):
    sc = _silu(c_ref[...])
    o_ref[...] = jnp.dot(sc, w_ref[...], precision=lax.Precision.HIGHEST,
                         preferred_element_type=F32) + b_ref[...]


def _mod(c, w_ada, b_ada):
    b, d = c.shape
    n = w_ada.shape[1]
    tn = D_MODEL
    return pl.pallas_call(
        _mod_kernel,
        grid=(n // tn,),
        in_specs=[pl.BlockSpec((b, d), lambda i: (0, 0)),
                  pl.BlockSpec((d, tn), lambda i: (0, i)),
                  pl.BlockSpec((1, tn), lambda i: (0, i))],
        out_specs=pl.BlockSpec((b, tn), lambda i: (0, i)),
        out_shape=jax.ShapeDtypeStruct((b, n), F32),
        compiler_params=_params(("arbitrary",)),
        name="mod",
    )(c, w_ada, b_ada.reshape(1, n))


def _rms_modulate(x, g, shift, scale):
    y = x * lax.rsqrt(jnp.mean(x * x, axis=-1, keepdims=True) + NORM_EPS)
    return (y * g) * (1.0 + scale) + shift


def _rope(t, cos, sin, first_half):
    outs = []
    for c in range(t.shape[1] // LANES):
        tc = t[:, c * LANES:(c + 1) * LANES]
        rot = jnp.where(first_half, -pltpu.roll(tc, LANES - ATT_HEAD_DIM // 2, 1),
                        pltpu.roll(tc, ATT_HEAD_DIM // 2, 1))
        outs.append(tc * cos + rot * sin)
    return jnp.concatenate(outs, axis=1)


def _inproj_kernel(x_ref, pos_ref, mod_ref, g_ref, invf_ref, wq_ref, wk_ref, wvt_ref, wqk_ref,
                   wv_ref, wo_ref, wg_ref, q_ref, k_ref, vt_ref, qkm_ref, vm_ref, om_ref, gt_ref):
    d = x_ref.shape[2]
    x = x_ref[0]
    mod = mod_ref[0]
    h = _rms_modulate(x, g_ref[...], mod[:, 0:d], mod[:, d:2 * d])
    hb = h.astype(BF16)

    ang = pos_ref[0].astype(F32) * invf_ref[...]
    cos = jnp.cos(ang)
    sin = jnp.sin(ang)
    lane = lax.broadcasted_iota(jnp.int32, cos.shape, 1)
    first_half = (lane & (ATT_HEAD_DIM // 2)) == 0

    q = _rope(_dot(hb, wq_ref[...]), cos, sin, first_half)
    q_ref[0] = (q * (ATT_HEAD_DIM ** -0.5 * LOG2_E)).astype(BF16)
    k_ref[0] = _rope(_dot(hb, wk_ref[...]), cos, sin, first_half).astype(BF16)
    vt = _dot_nt(wvt_ref[...], hb).astype(BF16)
    for blk_i in range(vt_ref.shape[1]):
        vt_ref[0, blk_i] = vt[:, blk_i * MOBA_BLOCK:(blk_i + 1) * MOBA_BLOCK]
    qkm_ref[0] = _dot(hb, wqk_ref[...])
    vm_ref[0] = _dot(hb, wv_ref[...])
    om_ref[0] = _dot(hb, wo_ref[...])
    gt_ref[0] = _dot(hb, wg_ref[...])


def _inproj(x, positions, mod3, norm_g, w_in):
    b, s, d = x.shape
    tm = INPROJ_TILE
    per_tile = tm // MOBA_BLOCK
    nb = s // MOBA_BLOCK
    o = [0, D_ATT, 2 * D_ATT, 3 * D_ATT, 3 * D_ATT + 2 * D_MLSTM, 3 * D_ATT + 3 * D_MLSTM,
         3 * D_ATT + 4 * D_MLSTM, 3 * D_ATT + 4 * D_MLSTM + 2 * MLSTM_HEADS]
    wb = w_in.astype(BF16)
    wq, wk, wv_a, wqk, wv, wo, wg = (wb[:, o[i]:o[i + 1]] for i in range(7))
    wvt = wv_a.T
    lane_pad = ((0, 0), (0, LANES - MLSTM_HEADS))
    wg = jnp.concatenate([jnp.pad(wg[:, :MLSTM_HEADS], lane_pad), jnp.pad(wg[:, MLSTM_HEADS:], lane_pad)], axis=1)
    half = ATT_HEAD_DIM // 2
    inv_freq = ROPE_THETA ** (-jnp.arange(half, dtype=F32) / half)
    invf = jnp.tile(inv_freq, LANES // half).reshape(1, LANES)

    full = lambda a: pl.BlockSpec(a.shape, lambda bi, i: (0,) * a.ndim)
    tok = lambda w: pl.BlockSpec((1, tm, w), lambda bi, i: (bi, i, 0))
    out_shape = [jax.ShapeDtypeStruct((b, s, D_ATT), BF16),
                 jax.ShapeDtypeStruct((b, s, D_ATT), BF16),
                 jax.ShapeDtypeStruct((b, nb, D_ATT, MOBA_BLOCK), BF16),
                 jax.ShapeDtypeStruct((b, s, 2 * D_MLSTM), F32),
                 jax.ShapeDtypeStruct((b, s, D_MLSTM), F32),
                 jax.ShapeDtypeStruct((b, s, D_MLSTM), F32),
                 jax.ShapeDtypeStruct((b, s, GATE_LANES), F32)]
    return pl.pallas_call(
        _inproj_kernel,
        grid=(b, s // tm),
        in_specs=[tok(d), tok(1),
                  pl.BlockSpec((1, 1, mod3.shape[2]), lambda bi, i: (bi, 0, 0)),
                  full(norm_g), full(invf), full(wq), full(wk), full(wvt), full(wqk), full(wv),
                  full(wo), full(wg)],
        out_specs=[tok(D_ATT), tok(D_ATT),
                   pl.BlockSpec((1, per_tile, D_ATT, MOBA_BLOCK), lambda bi, i: (bi, i, 0, 0)),
                   tok(2 * D_MLSTM), tok(D_MLSTM), tok(D_MLSTM), tok(GATE_LANES)],
        out_shape=out_shape,
        compiler_params=_params(("arbitrary", "arbitrary")),
        name="inproj",
    )(x, positions.reshape(b, s, 1), mod3, norm_g, invf, wq, wk, wvt, wqk, wv, wo, wg)


def _moba_kernel(q_ref, k_ref, vt_ref, g_ref, o_ref, kmean_ref, sel_ref, st_ref, qh_ref, acc_ref):
    blk = MOBA_BLOCK
    nb = k_ref.shape[1] // blk
    j = pl.program_id(2)
    heads = q_ref.shape[2] // ATT_HEAD_DIM
    hpl = LANES // ATT_HEAD_DIM

    @pl.when(j == 0)
    def _():
        for n in range(nb):
            kb = k_ref[0, n * blk:(n + 1) * blk, :].astype(F32)
            kmean_ref[n:n + 1, :] = jnp.mean(kb, axis=0, keepdims=True)

    lane = lax.broadcasted_iota(jnp.int32, (blk, LANES), 1)
    blk_id = lax.broadcasted_iota(jnp.int32, (nb, blk), 0)
    km_hi = kmean_ref[...].astype(BF16)
    km_lo = (kmean_ref[...] - km_hi.astype(F32)).astype(BF16)
    past = blk_id < j
    tile = lambda a, hh: a[:, (hh // hpl) * LANES:(hh // hpl + 1) * LANES]

    gates = []
    for hh in range(heads):
        q = tile(q_ref[0], hh)
        in_head = (lane >= (hh % hpl) * ATT_HEAD_DIM) & (lane < (hh % hpl + 1) * ATT_HEAD_DIM)
        qh_ref[hh] = jnp.where(in_head, q, jnp.zeros_like(q))
        gates.append(_dot_nt(tile(km_hi, hh), qh_ref[hh]) + _dot_nt(tile(km_lo, hh), qh_ref[hh]))
    g = jnp.where(past[None], jnp.stack(gates), NEG_INF)
    sel = jnp.zeros(g.shape, F32)
    for _ in range(min(MOBA_TOPK, nb)):
        top = jnp.max(g, axis=1, keepdims=True)
        idx = jnp.min(jnp.where(g == top, blk_id[None], nb), axis=1, keepdims=True)
        pick = blk_id[None] == idx
        sel = jnp.where(pick, 1.0, sel)
        g = jnp.where(pick, -jnp.inf, g)
    sel_ref[:, 0:nb, :] = jnp.where((sel > 0.0) & past[None], 1.0, 0.0)
    sel_ref[:, nb:nb + SUBLANES, :] = jnp.zeros((heads, SUBLANES, blk), F32)

    kpos = lax.broadcasted_iota(jnp.int32, (blk, blk), 0)
    qpos = lax.broadcasted_iota(jnp.int32, (blk, blk), 1)
    causal_bias = jnp.where(kpos <= qpos, 0.0, NEG_INF)

    def slab_block(t, i):
        b_i = j - KV_CHUNK * t - (KV_CHUNK - 1 - i)
        return jnp.maximum(b_i, 0), jnp.where(b_i >= 0, b_i, nb)

    def score(t, slot, own_chunk):
        tops = [None] * heads
        slab_max = [[None] * KV_CHUNK for _ in range(heads)]
        for i in range(KV_CHUNK):
            src, row = slab_block(t, i)
            kb = k_ref[0, pl.ds(pl.multiple_of(src * blk, blk), blk), :]
            for hh in range(heads):
                st = _dot_nt(tile(kb, hh), qh_ref[hh])
                if own_chunk and i == KV_CHUNK - 1:
                    st = st + causal_bias
                    cmax = jnp.max(st, axis=0, keepdims=True)
                    smax = cmax
                else:
                    cmax = jnp.max(st, axis=0, keepdims=True)
                    smax = cmax + (sel_ref[hh, pl.ds(row, 1), :] - 1.0) * (-NEG_INF)
                st_ref[slot, hh, i * blk:(i + 1) * blk, :] = st
                slab_max[hh][i] = cmax
                tops[hh] = smax if tops[hh] is None else jnp.maximum(tops[hh], smax)
        return tuple(tops), tuple(tuple(r) for r in slab_max)

    ones_rows = jnp.ones((2 * SUBLANES, blk), BF16)

    def accumulate(t, slot, state, maxes):
        tops, slab_max = maxes
        new = []
        for hh in range(heads):
            m, l = state[2 * hh:2 * hh + 2]
            m_new = jnp.maximum(m, tops[hh])
            alpha = jnp.exp2(m - m_new)
            l = alpha * l
            acc = alpha * acc_ref[hh]
            for i in range(KV_CHUNK):
                src, row = slab_block(t, i)
                keep = sel_ref[hh, pl.ds(row, 1), :]
                if i == KV_CHUNK - 1:
                    keep = jnp.where(t == 0, 1.0, keep)
                p = jnp.exp2(st_ref[slot, hh, i * blk:(i + 1) * blk, :] - jnp.maximum(m_new, slab_max[hh][i]))
                v_h = vt_ref[0, src][hh * ATT_HEAD_DIM:(hh + 1) * ATT_HEAD_DIM, :]
                pv = _dot(jnp.concatenate([v_h, ones_rows], axis=0), p.astype(BF16))
                l = l + keep * pv[ATT_HEAD_DIM:ATT_HEAD_DIM + 1, :]
                acc = acc + keep * pv[0:ATT_HEAD_DIM, :]
            acc_ref[hh] = acc
            new += [m_new, l]
        return tuple(new)

    state = ()
    for hh in range(heads):
        state += (jnp.full((1, blk), NEG_INF, F32), jnp.zeros((1, blk), F32))
        acc_ref[hh] = jnp.zeros((ATT_HEAD_DIM, blk), F32)
    last = j // KV_CHUNK

    def stage_pair(t, slot, state, maxes):
        nxt = score(t + 1, 1 - slot, False)
        return accumulate(t, slot, state, maxes), nxt

    def body(t, carry):
        return lax.cond(t % 2 == 0, functools.partial(stage_pair, t, 0), functools.partial(stage_pair, t, 1),
                        *carry)

    state, maxes = lax.fori_loop(0, last, body, (state, score(0, 0, True)))
    state = lax.cond(last % 2 == 0, functools.partial(accumulate, last, 0), functools.partial(accumulate, last, 1),
                     state, maxes)

    outs = []
    for hh in range(heads):
        o_h = acc_ref[hh] / state[2 * hh + 1]
        ms = jnp.mean(o_h * o_h, axis=0, keepdims=True)
        outs.append(o_h * lax.rsqrt(ms + NORM_EPS))
    o_ref[0] = jnp.concatenate(outs, axis=0).T * g_ref[...]


def _moba(q, k, vt, g_att):
    b, s, w = q.shape
    blk = MOBA_BLOCK
    nb = s // blk
    gl = MOBA_GROUP_LANES
    groups = w // gl
    heads = gl // ATT_HEAD_DIM
    return pl.pallas_call(
        _moba_kernel,
        grid=(b, groups, nb),
        in_specs=[pl.BlockSpec((1, blk, gl), lambda bi, p, j: (bi, j, p)),
                  pl.BlockSpec((1, s, gl), lambda bi, p, j: (bi, 0, p)),
                  pl.BlockSpec((1, nb, gl, blk), lambda bi, p, j: (bi, 0, p, 0)),
                  pl.BlockSpec((1, gl), lambda bi, p, j: (0, p))],
        out_specs=pl.BlockSpec((1, blk, gl), lambda bi, p, j: (bi, j, p)),
        out_shape=jax.ShapeDtypeStruct((b, s, w), F32),
        scratch_shapes=[pltpu.VMEM((nb, gl), F32),
                        pltpu.VMEM((heads, nb + SUBLANES, blk), F32),
                        pltpu.VMEM((2, heads, KV_CHUNK * blk, blk), F32),
                        pltpu.VMEM((heads, blk, LANES), BF16),
                        pltpu.VMEM((heads, ATT_HEAD_DIM, blk), F32)],
        compiler_params=_params(("arbitrary", "arbitrary", "arbitrary")),
        name="moba",
    )(q, k, vt, g_att)


def _mlstm_kernel(qkm_ref, vm_ref, om_ref, gt_ref, cw_ref, cb_ref, bg_ref, gm_ref, bsel_ref, o_ref,
                  uext_ref, c_ref, n_ref, m_ref):
    L = MLSTM_CHUNK
    dh = MLSTM_HEAD_DIM
    nh = MLSTM_HEADS
    assert L == dh == LANES and 2 * nh == SUBLANES
    c = pl.program_id(1)

    @pl.when(c == 0)
    def _():
        uext_ref[0:SUBLANES, :] = jnp.zeros((SUBLANES, uext_ref.shape[1]), F32)
        c_ref[...] = jnp.zeros(c_ref.shape, F32)
        n_ref[...] = jnp.zeros(n_ref.shape, F32)
        m_ref[...] = jnp.zeros(m_ref.shape, F32)

    for sub in range(qkm_ref.shape[1] // L):
        _mlstm_chunk(slice(sub * L, (sub + 1) * L), qkm_ref, vm_ref, om_ref, gt_ref, cw_ref, cb_ref, bg_ref, gm_ref,
                     bsel_ref, o_ref, uext_ref, c_ref, n_ref, m_ref)


def _mlstm_chunk(rows, qkm_ref, vm_ref, om_ref, gt_ref, cw_ref, cb_ref, bg_ref, gm_ref, bsel_ref, o_ref,
                 uext_ref, c_ref, n_ref, m_ref):
    L = MLSTM_CHUNK
    dh = MLSTM_HEAD_DIM
    nh = MLSTM_HEADS

    u = qkm_ref[0, rows, :]
    uext_ref[SUBLANES:SUBLANES + L, :] = u
    cw = cw_ref[...]
    conv = cb_ref[...] + cw[CONV_WIDTH - 1:CONV_WIDTH, :] * u
    for dlt in range(1, CONV_WIDTH):
        conv = conv + cw[CONV_WIDTH - 1 - dlt:CONV_WIDTH - dlt, :] * uext_ref[SUBLANES - dlt:SUBLANES - dlt + L, :]
    uext_ref[0:SUBLANES, :] = u[L - SUBLANES:L, :]
    act = _silu(conv)

    gates = gt_ref[0, rows, :] + bg_ref[...]
    ig = gates[:, 0:LANES]
    tpos = lax.broadcasted_iota(jnp.int32, (L, L), 0)
    spos = lax.broadcasted_iota(jnp.int32, (L, L), 1)
    causal = spos <= tpos
    tril = jnp.where(causal, 1.0, 0.0).astype(BF16)
    cum = sum(_dot(tril, part) for part in _split3(_log_sigmoid(gates[:, LANES:])))
    a_tot = cum[L - 1:L, :]
    m_prev = m_ref[0:1, :]
    resid = ig - cum
    pmax = resid
    shift = 1
    while shift < L:
        pmax = jnp.maximum(pmax, jnp.where(tpos >= shift, pltpu.roll(pmax, shift, 0), -jnp.inf))
        shift *= 2
    inter = cum + m_prev
    m_t = jnp.maximum(inter, cum + pmax)
    g_loc = a_tot - cum + ig
    m_loc = jnp.max(g_loc, axis=0, keepdims=True)
    w_loc = jnp.exp(g_loc - m_loc)
    head_lane = spos < nh
    packed = jnp.where(head_lane, cum - m_t, 0.0)
    for i, qty in enumerate((inter - m_t, -m_t, g_loc - m_loc), start=1):
        packed = packed + pltpu.roll(jnp.where(head_lane, qty, 0.0), SUBLANES * i, 1)
    per_t_cols = jnp.concatenate(_split3(packed), axis=1)
    per_t_rows = (jnp.where(head_lane, resid, 0.0) + pltpu.roll(jnp.where(head_lane, w_loc, 0.0), SUBLANES, 1)).T

    m_new = jnp.maximum(a_tot + m_prev, m_loc)
    s_prev = jnp.exp(a_tot + m_prev - m_new)
    s_loc = jnp.exp(m_loc - m_new)
    m_ref[0:1, :] = m_new

    vm = vm_ref[0, rows, :]
    om = om_ref[0, rows, :]
    heads = lambda a, off=0: jnp.stack([a[:, off + h * dh:off + (h + 1) * dh] for h in range(nh)])
    bdot = lambda a, b, ca, cb: lax.dot_general(a, b, (((ca,), (cb,)), ((0,), (0,))), preferred_element_type=F32)
    q4 = heads(act)
    k4 = heads(act, nh * dh) * (dh ** -0.5)
    v4 = heads(vm)
    qb, kb, vb = q4.astype(BF16), k4.astype(BF16), v4.astype(BF16)
    c_prev = c_ref[...]
    n_all = n_ref[...]
    n_prev = jnp.stack([n_all[h:h + 1, :] for h in range(nh)])
    bcast = _dot(per_t_cols, bsel_ref[...])
    qty = lambda i: jnp.stack([bcast[:, (4 * h + i) * dh:(4 * h + i + 1) * dh] for h in range(nh)])
    decay_b = qty(0)
    w_inter_b, floor_b, w_loc_b = jnp.exp(qty(1)), jnp.exp(qty(2)), jnp.exp(qty(3))
    resid_rows = jnp.stack([per_t_rows[h:h + 1, :] for h in range(nh)])
    w_rows = jnp.stack([per_t_rows[SUBLANES + h:SUBLANES + h + 1, :] for h in range(nh)])

    s_qk = bdot(qb, kb, 2, 2) * jnp.where(causal[None], jnp.exp(decay_b + resid_rows), 0.0)
    intra = bdot(s_qk.astype(BF16), jnp.concatenate([vb, jnp.ones((nh, L, dh), BF16)], axis=2), 2, 1)
    state_rows = jnp.concatenate([c_prev, jnp.broadcast_to(n_prev, (nh, dh, dh))], axis=1).astype(BF16)
    carried = bdot(qb, state_rows, 2, 2)
    num = intra[:, :, :dh] + w_inter_b * carried[:, :, :dh]
    den = intra[:, :, dh:] + w_inter_b * carried[:, :, dh:]
    hout = num / jnp.maximum(jnp.abs(den), floor_b)

    sq = (hout * hout).astype(BF16).reshape(nh * L, dh)
    mean_sq = (_dot(sq, jnp.ones((dh, dh), BF16)) * (1.0 / dh)).reshape(nh, L, dh)
    gm = jnp.stack([gm_ref[:, h * dh:(h + 1) * dh] for h in range(nh)])
    out = hout * lax.rsqrt(mean_sq + NORM_EPS) * gm * jax.nn.sigmoid(heads(om))
    for h in range(nh):
        o_ref[0, rows, h * dh:(h + 1) * dh] = out[h]

    vw = (v4 * w_loc_b).astype(BF16)
    for h in range(nh):
        c_loc = _dot_tn(vw[h], kb[h])
        n_loc = _dot(jnp.broadcast_to(w_rows[h], (SUBLANES, L)).astype(BF16), kb[h])[0:1, :]
        c_ref[h] = s_prev[:, h:h + 1] * c_prev[h] + s_loc[:, h:h + 1] * c_loc
        n_ref[h:h + 1, :] = s_prev[:, h:h + 1] * n_all[h:h + 1, :] + s_loc[:, h:h + 1] * n_loc


def _mlstm(qkm, vm, om, gates, conv_w, conv_b, b_gate, g_m):
    b, s, _ = qkm.shape
    L = MLSTM_CHUNK
    step_rows = MLSTM_CHUNKS_PER_STEP * L
    nc = s // step_rows
    lane_pad = ((0, 0), (0, LANES - MLSTM_HEADS))
    bg = b_gate.reshape(1, -1)
    bg = jnp.concatenate([jnp.pad(bg[:, :MLSTM_HEADS], lane_pad), jnp.pad(bg[:, MLSTM_HEADS:], lane_pad)], axis=1)
    tok = lambda w: pl.BlockSpec((1, step_rows, w), lambda bi, c: (bi, c, 0))
    full = lambda a: pl.BlockSpec(a.shape, lambda bi, c: (0,) * a.ndim)
    cb = conv_b.reshape(1, -1)
    gm = g_m.reshape(1, -1)
    n_qty = 4
    row = jnp.arange(3 * LANES, dtype=jnp.int32) % LANES
    col_blk = jnp.arange(MLSTM_HEADS * n_qty * MLSTM_HEAD_DIM, dtype=jnp.int32) // MLSTM_HEAD_DIM
    bsel = (((row // SUBLANES)[:, None] == (col_blk % n_qty)[None, :])
            & ((row % SUBLANES)[:, None] == (col_blk // n_qty)[None, :])).astype(BF16)
    return pl.pallas_call(
        _mlstm_kernel,
        grid=(b, nc),
        in_specs=[tok(2 * D_MLSTM), tok(D_MLSTM), tok(D_MLSTM), tok(GATE_LANES),
                  full(conv_w), full(cb), full(bg), full(gm), full(bsel)],
        out_specs=tok(D_MLSTM),
        out_shape=jax.ShapeDtypeStruct((b, s, D_MLSTM), F32),
        scratch_shapes=[pltpu.VMEM((SUBLANES + L, 2 * D_MLSTM), F32),
                        pltpu.VMEM((MLSTM_HEADS, MLSTM_HEAD_DIM, MLSTM_HEAD_DIM), F32),
                        pltpu.VMEM((SUBLANES, MLSTM_HEAD_DIM), F32),
                        pltpu.VMEM((SUBLANES, LANES), F32)],
        compiler_params=_params(("arbitrary", "arbitrary")),
        name="mlstm",
    )(qkm, vm, om, gates, conv_w, cb, bg, gm, bsel)


META_E0, META_E1, META_W0, META_W1, META_L0, META_L1 = range(6)
STAT_LEN, STAT_START = 0, 1


def _outproj_kernel(attn_ref, hm_ref, x_ref, mod_ref, g_ref, wout_ref, wr_hi_ref, wr_lo_ref, br_ref,
                    x1_ref, h2_ref, meta_ref, stat_ref):
    d = x_ref.shape[1]
    tm = x_ref.shape[0]

    mod = mod_ref[0]
    y = (_dot(attn_ref[...].astype(BF16), wout_ref[0:D_ATT, :])
         + _dot(hm_ref[...].astype(BF16), wout_ref[D_ATT:, :]))
    x1 = x_ref[...] + mod[:, 2 * d:3 * d] * y
    x1_ref[...] = x1
    h2 = _rms_modulate(x1, g_ref[...], mod[:, 3 * d:4 * d], mod[:, 4 * d:5 * d])
    h2_ref[...] = h2.astype(BF16)

    h_hi = h2.astype(BF16)
    h_lo = (h2 - h_hi.astype(F32)).astype(BF16)
    logit = (_dot(h_hi, wr_hi_ref[...]) + _dot(h_lo, wr_hi_ref[...]) + _dot(h_hi, wr_lo_ref[...])
             + br_ref[...])
    lane = lax.broadcasted_iota(jnp.int32, logit.shape, 1)
    big = jnp.int32(ROUTER_LANES)

    is_g = (lane >= GROUP_LANE0) & (lane < GROUP_LANE0 + N_GROUPS)
    gmax = jnp.max(jnp.where(is_g, logit, -jnp.inf), axis=1, keepdims=True)
    gsum = jnp.sum(jnp.where(is_g, jnp.exp(logit - gmax), 0.0), axis=1, keepdims=True)
    g_w = 1.0 / gsum
    g_idx = jnp.min(jnp.where(is_g & (logit == gmax), lane, big), axis=1, keepdims=True) - GROUP_LANE0

    in_grp = (lane < N_EXPERTS) & ((lane // EXPERTS_PER_GROUP) == g_idx)
    emax = jnp.max(jnp.where(in_grp, logit, -jnp.inf), axis=1, keepdims=True)
    esum = jnp.sum(jnp.where(in_grp, jnp.exp(logit - emax), 0.0), axis=1, keepdims=True)
    e0 = jnp.min(jnp.where(in_grp & (logit == emax), lane, big), axis=1, keepdims=True)
    rest = in_grp & (lane != e0)
    e2max = jnp.max(jnp.where(rest, logit, -jnp.inf), axis=1, keepdims=True)
    e1 = jnp.min(jnp.where(rest & (logit == e2max), lane, big), axis=1, keepdims=True)
    p0 = 1.0 / esum
    p1 = jnp.exp(e2max - emax) / esum
    w0 = g_w * p0 / (p0 + p1)
    w1 = g_w * p1 / (p0 + p1)

    memb = jnp.where((lane == e0) | (lane == e1), 1.0, 0.0)
    tpos = lax.broadcasted_iota(jnp.int32, (tm, tm), 0)
    spos = lax.broadcasted_iota(jnp.int32, (tm, tm), 1)
    before = jnp.where(spos < tpos, 1.0, 0.0).astype(BF16)
    earlier = _dot(before, memb.astype(BF16))
    count = jnp.sum(memb, axis=0, keepdims=True).astype(jnp.int32)
    run_len = (((count + (RUN_ALIGN - 1)) // RUN_ALIGN) * RUN_ALIGN).astype(F32)
    epos = lax.broadcasted_iota(jnp.int32, (ROUTER_LANES, ROUTER_LANES), 0)
    fpos = lax.broadcasted_iota(jnp.int32, (ROUTER_LANES, ROUTER_LANES), 1)
    lower_e = jnp.where(epos < fpos, 1.0, 0.0).astype(BF16)
    run_start = _dot(jnp.broadcast_to(run_len, (SUBLANES, ROUTER_LANES)).astype(BF16), lower_e)[0:1, :]
    row = run_start + earlier
    l0 = jnp.sum(jnp.where(lane == e0, row, 0.0), axis=1, keepdims=True)
    l1 = jnp.sum(jnp.where(lane == e1, row, 0.0), axis=1, keepdims=True)

    meta = jnp.zeros(logit.shape, F32)
    for slot, val in ((META_E0, e0.astype(F32)), (META_E1, e1.astype(F32)), (META_W0, w0), (META_W1, w1),
                      (META_L0, l0), (META_L1, l1)):
        meta = jnp.where(lane == slot, val, meta)
    meta_ref[...] = meta
    srow = lax.broadcasted_iota(jnp.int32, (SUBLANES, ROUTER_LANES), 0)
    stat_ref[0] = jnp.where(srow == STAT_LEN, run_len, jnp.where(srow == STAT_START, run_start, 0.0))


def _outproj(attn, hm, x2d, mod3, norm_g, w_out, w_rg, b_rg, w_re, b_re, seq):
    t, d = x2d.shape
    tm = MOE_TILE
    per_batch = seq // tm
    wr = jnp.pad(jnp.concatenate([w_re, w_rg], axis=1), ((0, 0), (0, ROUTER_LANES - N_EXPERTS - N_GROUPS)))
    br = jnp.pad(jnp.concatenate([b_re, b_rg]).reshape(1, -1), ((0, 0), (0, ROUTER_LANES - N_EXPERTS - N_GROUPS)))
    wr_hi = wr.astype(BF16)
    wr_lo = (wr - wr_hi.astype(F32)).astype(BF16)
    wout = w_out.astype(BF16)
    tok = lambda w: pl.BlockSpec((tm, w), lambda i: (i, 0))
    full = lambda a: pl.BlockSpec(a.shape, lambda i: (0,) * a.ndim)
    return pl.pallas_call(
        _outproj_kernel,
        grid=(t // tm,),
        in_specs=[tok(D_ATT), tok(D_MLSTM), tok(d),
                  pl.BlockSpec((1, 1, mod3.shape[2]), lambda i: (i // per_batch, 0, 0)),
                  full(norm_g), full(wout), full(wr_hi), full(wr_lo), full(br)],
        out_specs=[tok(d), tok(d), tok(ROUTER_LANES),
                   pl.BlockSpec((1, SUBLANES, ROUTER_LANES), lambda i: (i, 0, 0))],
        out_shape=[jax.ShapeDtypeStruct((t, d), F32), jax.ShapeDtypeStruct((t, d), BF16),
                   jax.ShapeDtypeStruct((t, ROUTER_LANES), F32),
                   jax.ShapeDtypeStruct((t // tm, SUBLANES, ROUTER_LANES), F32)],
        compiler_params=_params(("arbitrary",)),
        name="outproj",
    )(attn, hm, x2d, mod3, norm_g, wout, wr_hi, wr_lo, br)


def _for_each_piece(length, pieces, fn):
    pos = jnp.int32(0)
    for size in pieces:
        take = length & size

        @pl.when(take != 0)
        def _(pos=pos, size=size):
            fn(pl.multiple_of(pos, RUN_ALIGN), size)

        pos = pos + take


def _for_each_chunk(tile, rows_ref, cdst_ref, fn):
    def body(c, carry):
        fn(pl.multiple_of(c * RUN_ALIGN, RUN_ALIGN),
           pl.multiple_of(cdst_ref[tile * CHUNKS_PER_TILE + c], RUN_ALIGN))
        return carry

    lax.fori_loop(0, rows_ref[tile] // RUN_ALIGN, body, 0)


def _dispatch_kernel(cdst_ref, rows_ref, tail_ref, tlen_ref, nu_ref, meta_ref, h2_ref, xs_ref,
                     perm_ref, zero_ref, sem, zsem):
    i = pl.program_id(0)
    rb = MOE_BLOCK_ROWS
    n_blocks = xs_ref.shape[0] // rb

    @pl.when(i == 0)
    def _():
        zero_ref[...] = jnp.zeros(zero_ref.shape, zero_ref.dtype)

        def zero_copy(row, size):
            return pltpu.make_async_copy(zero_ref.at[pl.ds(0, size)], xs_ref.at[pl.ds(row, size)], zsem)

        def fill(op):
            def per_expert(e, carry):
                start = tail_ref[e]
                _for_each_piece(tlen_ref[e], TAIL_PIECES,
                                lambda pos, size: op(zero_copy(pl.multiple_of(start + pos, RUN_ALIGN), size)))
                return carry

            lax.fori_loop(0, N_EXPERTS, per_expert, 0)

            def per_block(blk, carry):
                for part in range(rb // ZERO_ROWS):
                    op(zero_copy(pl.multiple_of(blk * rb + part * ZERO_ROWS, ZERO_ROWS), ZERO_ROWS))
                return carry

            lax.fori_loop(nu_ref[0], n_blocks, per_block, 0)

        fill(lambda cp: cp.start())
        fill(lambda cp: cp.wait())

    meta_t = meta_ref[...].T
    l0 = meta_t[META_L0:META_L0 + 1, :].astype(jnp.int32)
    l1 = meta_t[META_L1:META_L1 + 1, :].astype(jnp.int32)
    rpos = lax.broadcasted_iota(jnp.int32, (perm_ref.shape[1], meta_ref.shape[0]), 0)
    onehot = jnp.where((rpos == l0) | (rpos == l1), 1.0, 0.0).astype(BF16)
    slot = i % 2
    perm_ref[slot] = _dot(onehot, h2_ref[...]).astype(BF16)

    def run_copy(s, loc, glob, size):
        return pltpu.make_async_copy(perm_ref.at[s, pl.ds(loc, size)], xs_ref.at[pl.ds(glob, size)], sem.at[s])

    def wait_tile(tile, s):
        _for_each_piece(rows_ref[tile], TILE_PIECES, lambda pos, size: run_copy(s, 0, 0, size).wait())

    @pl.when(i > 0)
    def _():
        wait_tile(i - 1, 1 - slot)

    _for_each_chunk(i, rows_ref, cdst_ref, lambda loc, glob: run_copy(slot, loc, glob, RUN_ALIGN).start())

    @pl.when(i == pl.num_programs(0) - 1)
    def _():
        wait_tile(i, slot)


def _dispatch(plan, meta, h2):
    t, d = h2.shape
    tm = MOE_TILE
    grid_spec = pltpu.PrefetchScalarGridSpec(
        num_scalar_prefetch=5,
        grid=(t // tm,),
        in_specs=[pl.BlockSpec((tm, ROUTER_LANES), lambda i, *_: (i, 0)),
                  pl.BlockSpec((tm, d), lambda i, *_: (i, 0))],
        out_specs=pl.BlockSpec(memory_space=pl.ANY),
        scratch_shapes=[pltpu.VMEM((2, PERM_ROWS, d), BF16), pltpu.VMEM((ZERO_ROWS, d), BF16),
                        pltpu.SemaphoreType.DMA((2,)), pltpu.SemaphoreType.DMA(())],
    )
    return pl.pallas_call(
        _dispatch_kernel,
        grid_spec=grid_spec,
        out_shape=jax.ShapeDtypeStruct((plan["n_rows"], d), BF16),
        compiler_params=_params(("arbitrary",)),
        name="dispatch",
    )(plan["chunk_dst"], plan["tile_rows"], plan["tail_start"], plan["tail_len"], plan["n_used"], meta, h2)


def _expert_kernel(be_ref, nu_ref, xs_ref, w1_ref, w3_ref, w2_ref, y_ref, w1b_ref, w3b_ref, w2b_ref):
    i = pl.program_id(0)
    used = i < nu_ref[0]

    @pl.when(used & ((i == 0) | (be_ref[i] != be_ref[jnp.maximum(i, 1) - 1])))
    def _():
        w1b_ref[...] = w1_ref[0].astype(BF16)
        w3b_ref[...] = w3_ref[0].astype(BF16)
        w2b_ref[...] = w2_ref[0].astype(BF16)

    @pl.when(used)
    def _():
        x = xs_ref[...]
        a = _dot(x, w1b_ref[...])
        b = _dot(x, w3b_ref[...])
        y_ref[...] = _dot((_silu(a) * b).astype(BF16), w2b_ref[...]).astype(y_ref.dtype)

    @pl.when(jnp.logical_not(used))
    def _():
        y_ref[...] = jnp.zeros(y_ref.shape, y_ref.dtype)


def _experts(block_e, n_used, xs, w1, w3, w2):
    n_rows, d = xs.shape
    rb = MOE_BLOCK_ROWS
    de = w1.shape[2]
    used = lambda i, be, nu: jnp.minimum(i, nu[0] - 1)
    grid_spec = pltpu.PrefetchScalarGridSpec(
        num_scalar_prefetch=2,
        grid=(n_rows // rb,),
        in_specs=[pl.BlockSpec((rb, d), lambda i, be, nu: (used(i, be, nu), 0)),
                  pl.BlockSpec((1, d, de), lambda i, be, nu: (be[used(i, be, nu)], 0, 0)),
                  pl.BlockSpec((1, d, de), lambda i, be, nu: (be[used(i, be, nu)], 0, 0)),
                  pl.BlockSpec((1, de, d), lambda i, be, nu: (be[used(i, be, nu)], 0, 0))],
        out_specs=pl.BlockSpec((rb, d), lambda i, be, nu: (i, 0)),
        scratch_shapes=[pltpu.VMEM((d, de), BF16), pltpu.VMEM((d, de), BF16), pltpu.VMEM((de, d), BF16)],
    )
    return pl.pallas_call(
        _expert_kernel,
        grid_spec=grid_spec,
        out_shape=jax.ShapeDtypeStruct((n_rows, d), BF16),
        compiler_params=_params(("arbitrary",)),
        name="experts",
    )(block_e, n_used, xs, w1, w3, w2)


def _combine_kernel(final_norm, cdst_ref, rows_ref, x1_ref, meta_ref, mod_ref, g_ref, y_ref, o_ref, ybuf_ref, sem):
    tc, d = x1_ref.shape
    i = pl.program_id(0)

    slot = i % 2

    def run_copy(s, loc, glob, size):
        return pltpu.make_async_copy(y_ref.at[pl.ds(glob, size)], ybuf_ref.at[s, pl.ds(loc, size)], sem.at[s])

    def start_tile(tile, s):
        _for_each_chunk(tile, rows_ref, cdst_ref, lambda loc, glob: run_copy(s, loc, glob, RUN_ALIGN).start())

    @pl.when(i == 0)
    def _():
        ybuf_ref[...] = jnp.zeros(ybuf_ref.shape, ybuf_ref.dtype)
        start_tile(0, 0)

    @pl.when(i + 1 < pl.num_programs(0))
    def _():
        start_tile(i + 1, 1 - slot)

    _for_each_piece(rows_ref[i], TILE_PIECES, lambda pos, size: run_copy(slot, 0, 0, size).wait())

    meta = meta_ref[...]
    yb = ybuf_ref[slot]
    rpos = lax.broadcasted_iota(jnp.int32, (tc, ybuf_ref.shape[1]), 1)
    sel = jnp.zeros(rpos.shape, F32)
    for l_lane, w_lane in ((META_L0, META_W0), (META_L1, META_W1)):
        sel = jnp.where(rpos == meta[:, l_lane:l_lane + 1].astype(jnp.int32), meta[:, w_lane:w_lane + 1], sel)
    moe = _dot(sel.astype(BF16), yb)
    x2 = x1_ref[...] + mod_ref[0][:, 5 * d:6 * d] * moe
    if final_norm:
        x2 = x2 * lax.rsqrt(jnp.mean(x2 * x2, axis=-1, keepdims=True) + NORM_EPS) * g_ref[...]
    o_ref[...] = x2


def _combine(plan, x1, meta, mod3, norm_f_g, y, seq, final_norm):
    t, d = x1.shape
    tc = MOE_TILE
    per_batch = seq // tc
    gf = norm_f_g.reshape(1, d)
    grid_spec = pltpu.PrefetchScalarGridSpec(
        num_scalar_prefetch=2,
        grid=(t // tc,),
        in_specs=[pl.BlockSpec((tc, d), lambda i, *_: (i, 0)),
                  pl.BlockSpec((tc, ROUTER_LANES), lambda i, *_: (i, 0)),
                  pl.BlockSpec((1, 1, mod3.shape[2]), lambda i, *_: (i // per_batch, 0, 0)),
                  pl.BlockSpec(gf.shape, lambda i, *_: (0, 0)),
                  pl.BlockSpec(memory_space=pl.ANY)],
        out_specs=pl.BlockSpec((tc, d), lambda i, *_: (i, 0)),
        scratch_shapes=[pltpu.VMEM((2, PERM_ROWS, d), BF16), pltpu.SemaphoreType.DMA((2,))],
    )
    return pl.pallas_call(
        functools.partial(_combine_kernel, final_norm),
        grid_spec=grid_spec,
        out_shape=jax.ShapeDtypeStruct((t, d), F32),
        compiler_params=_params(("arbitrary",)),
        name="combine",
    )(plan["chunk_dst"], plan["tile_rows"], x1, meta, mod3, gf, y)


def _routing_plan(stats, n_tokens):
    rb = MOE_BLOCK_ROWS
    run_len = stats[:, STAT_LEN, :N_EXPERTS].astype(jnp.int32)
    run_loc = stats[:, STAT_START, :N_EXPERTS].astype(jnp.int32)
    n_tiles = run_len.shape[0]
    total = jnp.sum(run_len, axis=0)
    padded = ((total + rb - 1) // rb) * rb
    pad_end = jnp.cumsum(padded)
    pad_start = pad_end - padded
    run_glob = pad_start[None, :] + jnp.cumsum(run_len, axis=0) - run_len
    n_rows = n_tokens * TOP_K_EXPERTS + n_tiles * N_EXPERTS * (RUN_ALIGN - 1) + N_EXPERTS * (rb - 1)
    n_rows = ((n_rows + rb - 1) // rb) * rb
    block_row = jnp.arange(n_rows // rb, dtype=jnp.int32) * rb
    block_e = jnp.minimum(jnp.sum((pad_end[None, :] <= block_row[:, None]).astype(jnp.int32), axis=1),
                          N_EXPERTS - 1)
    local_row = jnp.arange(CHUNKS_PER_TILE, dtype=jnp.int32) * RUN_ALIGN
    chunk_e = jnp.minimum(jnp.sum(((run_loc + run_len)[:, None, :] <= local_row[None, :, None]).astype(jnp.int32),
                                  axis=-1), N_EXPERTS - 1)
    shift = jnp.sum(jnp.where(chunk_e[..., None] == jnp.arange(N_EXPERTS, dtype=jnp.int32),
                              (run_glob - run_loc)[:, None, :], 0), axis=-1)
    chunk_dst = shift + local_row[None, :]
    return dict(chunk_dst=chunk_dst.reshape(-1), tile_rows=jnp.sum(run_len, axis=1),
                tail_start=pad_start + total, tail_len=padded - total,
                n_used=(pad_end[-1:] // rb).astype(jnp.int32), block_e=block_e.astype(jnp.int32), n_rows=n_rows)


def kernel(x, c, positions, w_ada, b_ada, norm1_g, w_in, b_gate, conv_w, conv_b, attn_out_g, mlstm_out_g,
           w_out, norm2_g, w_rg, b_rg, w_re, b_re, w1, w3, w2, norm_f_g):
    b, s, d = x.shape
    depth = w_ada.shape[0]
    assert d == D_MODEL and s % MOE_TILE == 0 and s % (KV_CHUNK * MOBA_BLOCK) == 0
    x2d = x.reshape(b * s, d)
    for l in range(depth):
        mod3 = _mod(c, w_ada[l], b_ada[l]).reshape(b, 1, 6 * d)
        q, k, vt, qkm, vm, om, gates = _inproj(x2d.reshape(b, s, d), positions, mod3,
                                              norm1_g[l].reshape(1, d), w_in[l])
        attn = _moba(q, k, vt, attn_out_g[l].reshape(1, D_ATT))
        hm = _mlstm(qkm, vm, om, gates, conv_w[l], conv_b[l], b_gate[l], mlstm_out_g[l])
        x1, h2, meta, stats = _outproj(attn.reshape(b * s, D_ATT), hm.reshape(b * s, D_MLSTM), x2d, mod3,
                                       norm2_g[l].reshape(1, d), w_out[l], w_rg[l], b_rg[l], w_re[l],
                                       b_re[l], s)
        plan = _routing_plan(stats, b * s)
        xs = _dispatch(plan, meta, h2)
        y = _experts(plan["block_e"], plan["n_used"], xs, w1[l], w3[l], w2[l])
        x2d = _combine(plan, x1, meta, mod3, norm_f_g, y, s, final_norm=(l == depth - 1))
    return x2d.reshape(b, s, d)
```

```python
import functools

import jax
import jax.numpy as jnp
from jax import lax
from jax.experimental import pallas as pl
from jax.experimental.pallas import tpu as pltpu

F32 = jnp.float32
BF16 = jnp.bfloat16

D_MODEL = 1024
D_ATT = 512
ATT_HEADS = 8
ATT_HEAD_DIM = 64
D_MLSTM = 512
MLSTM_HEADS = 4
MLSTM_HEAD_DIM = 128
MOBA_BLOCK = 256
MOBA_TOPK = 3
ROPE_THETA = 10000.0
MLSTM_CHUNK = 128
MLSTM_CHUNKS_PER_STEP = 4
CONV_WIDTH = 4
N_GROUPS = 4
EXPERTS_PER_GROUP = 8
N_EXPERTS = N_GROUPS * EXPERTS_PER_GROUP
TOP_K_EXPERTS = 2
D_EXPERT = 512
MOE_BLOCK_ROWS = 512
NORM_EPS = 1e-6
NEG_INF = -1e30
LOG2_E = 1.4426950408889634
KV_CHUNK = 2
MOBA_GROUP_LANES = 512

LANES = 128
SUBLANES = 8
VMEM_LIMIT_BYTES = 56 * 1024 * 1024

INPROJ_TILE = 512
MOE_TILE = 512
RUN_ALIGN = 2 * SUBLANES
PERM_ROWS = TOP_K_EXPERTS * MOE_TILE + N_EXPERTS * RUN_ALIGN
CHUNKS_PER_TILE = PERM_ROWS // RUN_ALIGN
TAIL_PIECES = tuple(RUN_ALIGN << p for p in reversed(range(5)))
assert TAIL_PIECES[0] * 2 == MOE_BLOCK_ROWS
TILE_PIECES = tuple(RUN_ALIGN << p for p in reversed(range(7)))
ZERO_ROWS = TAIL_PIECES[0]
GATE_LANES = 2 * LANES
ROUTER_LANES = LANES
GROUP_LANE0 = N_EXPERTS


def _dot(a, b):
    return jnp.dot(a, b, preferred_element_type=F32)


def _dot_nt(a, b):
    return lax.dot_general(a, b, (((1,), (1,)), ((), ())), preferred_element_type=F32)


def _dot_tn(a, b):
    return lax.dot_general(a, b, (((0,), (0,)), ((), ())), preferred_element_type=F32)


def _split3(x):
    a = x.astype(BF16)
    r = x - a.astype(F32)
    b = r.astype(BF16)
    c = (r - b.astype(F32)).astype(BF16)
    return a, b, c


def _silu(x):
    return x * jax.nn.sigmoid(x)


def _log_sigmoid(x):
    return jnp.minimum(x, 0.0) - jnp.log1p(jnp.exp(-jnp.abs(x)))


def _params(semantics, vmem=VMEM_LIMIT_BYTES):
    return pltpu.CompilerParams(dimension_semantics=semantics, vmem_limit_bytes=vmem)


def _mod_kernel(c_ref, w_ref, b_ref, o_ref):
    sc = _silu(c_ref[...])
    o_ref[...] = jnp.dot(sc, w_ref[...], precision=lax.Precision.HIGHEST,
                         preferred_element_type=F32) + b_ref[...]


def _mod(c, w_ada, b_ada):
    b, d = c.shape
    n = w_ada.shape[1]
    tn = D_MODEL
    return pl.pallas_call(
        _mod_kernel,
        grid=(n // tn,),
        in_specs=[pl.BlockSpec((b, d), lambda i: (0, 0)),
                  pl.BlockSpec((d, tn), lambda i: (0, i)),
                  pl.BlockSpec((1, tn), lambda i: (0, i))],
        out_specs=pl.BlockSpec((b, tn), lambda i: (0, i)),
        out_shape=jax.ShapeDtypeStruct((b, n), F32),
        compiler_params=_params(("arbitrary",)),
        name="mod",
    )(c, w_ada, b_ada.reshape(1, n))


def _rms_modulate(x, g, shift, scale):
    y = x * lax.rsqrt(jnp.mean(x * x, axis=-1, keepdims=True) + NORM_EPS)
    return (y * g) * (1.0 + scale) + shift


def _rope(t, cos, sin, first_half):
    outs = []
    for c in range(t.shape[1] // LANES):
        tc = t[:, c * LANES:(c + 1) * LANES]
        rot = jnp.where(first_half, -pltpu.roll(tc, LANES - ATT_HEAD_DIM // 2, 1),
                        pltpu.roll(tc, ATT_HEAD_DIM // 2, 1))
        outs.append(tc * cos + rot * sin)
    return jnp.concatenate(outs, axis=1)


def _inproj_kernel(x_ref, pos_ref, mod_ref, g_ref, invf_ref, wq_ref, wk_ref, wvt_ref, wqk_ref,
                   wv_ref, wo_ref, wg_ref, q_ref, k_ref, vt_ref, qkm_ref, vm_ref, om_ref, gt_ref):
    d = x_ref.shape[2]
    x = x_ref[0]
    mod = mod_ref[0]
    h = _rms_modulate(x, g_ref[...], mod[:, 0:d], mod[:, d:2 * d])
    hb = h.astype(BF16)

    ang = pos_ref[0].astype(F32) * invf_ref[...]
    cos = jnp.cos(ang)
    sin = jnp.sin(ang)
    lane = lax.broadcasted_iota(jnp.int32, cos.shape, 1)
    first_half = (lane & (ATT_HEAD_DIM // 2)) == 0

    q = _rope(_dot(hb, wq_ref[...]), cos, sin, first_half)
    q_ref[0] = (q * (ATT_HEAD_DIM ** -0.5 * LOG2_E)).astype(BF16)
    k_ref[0] = _rope(_dot(hb, wk_ref[...]), cos, sin, first_half).astype(BF16)
    vt = _dot_nt(wvt_ref[...], hb).astype(BF16)
    for blk_i in range(vt_ref.shape[1]):
        vt_ref[0, blk_i] = vt[:, blk_i * MOBA_BLOCK:(blk_i + 1) * MOBA_BLOCK]
    qkm_ref[0] = _dot(hb, wqk_ref[...])
    vm_ref[0] = _dot(hb, wv_ref[...])
    om_ref[0] = _dot(hb, wo_ref[...])
    gt_ref[0] = _dot(hb, wg_ref[...])


def _inproj(x, positions, mod3, norm_g, w_in):
    b, s, d = x.shape
    tm = INPROJ_TILE
    per_tile = tm // MOBA_BLOCK
    nb = s // MOBA_BLOCK
    o = [0, D_ATT, 2 * D_ATT, 3 * D_ATT, 3 * D_ATT + 2 * D_MLSTM, 3 * D_ATT + 3 * D_MLSTM,
         3 * D_ATT + 4 * D_MLSTM, 3 * D_ATT + 4 * D_MLSTM + 2 * MLSTM_HEADS]
    wb = w_in.astype(BF16)
    wq, wk, wv_a, wqk, wv, wo, wg = (wb[:, o[i]:o[i + 1]] for i in range(7))
    wvt = wv_a.T
    lane_pad = ((0, 0), (0, LANES - MLSTM_HEADS))
    wg = jnp.concatenate([jnp.pad(wg[:, :MLSTM_HEADS], lane_pad), jnp.pad(wg[:, MLSTM_HEADS:], lane_pad)], axis=1)
    half = ATT_HEAD_DIM // 2
    inv_freq = ROPE_THETA ** (-jnp.arange(half, dtype=F32) / half)
    invf = jnp.tile(inv_freq, LANES // half).reshape(1, LANES)

    full = lambda a: pl.BlockSpec(a.shape, lambda bi, i: (0,) * a.ndim)
    tok = lambda w: pl.BlockSpec((1, tm, w), lambda bi, i: (bi, i, 0))
    out_shape = [jax.ShapeDtypeStruct((b, s, D_ATT), BF16),
                 jax.ShapeDtypeStruct((b, s, D_ATT), BF16),
                 jax.ShapeDtypeStruct((b, nb, D_ATT, MOBA_BLOCK), BF16),
                 jax.ShapeDtypeStruct((b, s, 2 * D_MLSTM), F32),
                 jax.ShapeDtypeStruct((b, s, D_MLSTM), F32),
                 jax.ShapeDtypeStruct((b, s, D_MLSTM), F32),
                 jax.ShapeDtypeStruct((b, s, GATE_LANES), F32)]
    return pl.pallas_call(
        _inproj_kernel,
        grid=(b, s // tm),
        in_specs=[tok(d), tok(1),
                  pl.BlockSpec((1, 1, mod3.shape[2]), lambda bi, i: (bi, 0, 0)),
                  full(norm_g), full(invf), full(wq), full(wk), full(wvt), full(wqk), full(wv),
                  full(wo), full(wg)],
        out_specs=[tok(D_ATT), tok(D_ATT),
                   pl.BlockSpec((1, per_tile, D_ATT, MOBA_BLOCK), lambda bi, i: (bi, i, 0, 0)),
                   tok(2 * D_MLSTM), tok(D_MLSTM), tok(D_MLSTM), tok(GATE_LANES)],
        out_shape=out_shape,
        compiler_params=_params(("arbitrary", "arbitrary")),
        name="inproj",
    )(x, positions.reshape(b, s, 1), mod3, norm_g, invf, wq, wk, wvt, wqk, wv, wo, wg)


def _moba_kernel(q_ref, k_ref, vt_ref, g_ref, o_ref, kmean_ref, sel_ref, st_ref, qh_ref, acc_ref):
    blk = MOBA_BLOCK
    nb = k_ref.shape[1] // blk
    j = pl.program_id(2)
    heads = q_ref.shape[2] // ATT_HEAD_DIM
    hpl = LANES // ATT_HEAD_DIM

    @pl.when(j == 0)
    def _():
        for n in range(nb):
            kb = k_ref[0, n * blk:(n + 1) * blk, :].astype(F32)
            kmean_ref[n:n + 1, :] = jnp.mean(kb, axis=0, keepdims=True)

    lane = lax.broadcasted_iota(jnp.int32, (blk, LANES), 1)
    blk_id = lax.broadcasted_iota(jnp.int32, (nb, blk), 0)
    km_hi = kmean_ref[...].astype(BF16)
    km_lo = (kmean_ref[...] - km_hi.astype(F32)).astype(BF16)
    past = blk_id < j
    tile = lambda a, hh: a[:, (hh // hpl) * LANES:(hh // hpl + 1) * LANES]

    gates = []
    for hh in range(heads):
        q = tile(q_ref[0], hh)
        in_head = (lane >= (hh % hpl) * ATT_HEAD_DIM) & (lane < (hh % hpl + 1) * ATT_HEAD_DIM)
        qh_ref[hh] = jnp.where(in_head, q, jnp.zeros_like(q))
        gates.append(_dot_nt(tile(km_hi, hh), qh_ref[hh]) + _dot_nt(tile(km_lo, hh), qh_ref[hh]))
    g = jnp.where(past[None], jnp.stack(gates), NEG_INF)
    sel = jnp.zeros(g.shape, F32)
    for _ in range(min(MOBA_TOPK, nb)):
        top = jnp.max(g, axis=1, keepdims=True)
        idx = jnp.min(jnp.where(g == top, blk_id[None], nb), axis=1, keepdims=True)
        pick = blk_id[None] == idx
        sel = jnp.where(pick, 1.0, sel)
        g = jnp.where(pick, -jnp.inf, g)
    sel_ref[:, 0:nb, :] = jnp.where((sel > 0.0) & past[None], 1.0, 0.0)
    sel_ref[:, nb:nb + SUBLANES, :] = jnp.zeros((heads, SUBLANES, blk), F32)

    kpos = lax.broadcasted_iota(jnp.int32, (blk, blk), 0)
    qpos = lax.broadcasted_iota(jnp.int32, (blk, blk), 1)
    causal_bias = jnp.where(kpos <= qpos, 0.0, NEG_INF)

    def slab_block(t, i):
        b_i = j - KV_CHUNK * t - (KV_CHUNK - 1 - i)
        return jnp.maximum(b_i, 0), jnp.where(b_i >= 0, b_i, nb)

    def score(t, slot, own_chunk):
        tops = [None] * heads
        slab_max = [[None] * KV_CHUNK for _ in range(heads)]
        for i in range(KV_CHUNK):
            src, row = slab_block(t, i)
            kb = k_ref[0, pl.ds(pl.multiple_of(src * blk, blk), blk), :]
            for hh in range(heads):
                st = _dot_nt(tile(kb, hh), qh_ref[hh])
                if own_chunk and i == KV_CHUNK - 1:
                    st = st + causal_bias
                    cmax = jnp.max(st, axis=0, keepdims=True)
                    smax = cmax
                else:
                    cmax = jnp.max(st, axis=0, keepdims=True)
                    smax = cmax + (sel_ref[hh, pl.ds(row, 1), :] - 1.0) * (-NEG_INF)
                st_ref[slot, hh, i * blk:(i + 1) * blk, :] = st
                slab_max[hh][i] = cmax
                tops[hh] = smax if tops[hh] is None else jnp.maximum(tops[hh], smax)
        return tuple(tops), tuple(tuple(r) for r in slab_max)

    ones_rows = jnp.ones((2 * SUBLANES, blk), BF16)

    def accumulate(t, slot, state, maxes):
        tops, slab_max = maxes
        new = []
        for hh in range(heads):
            m, l = state[2 * hh:2 * hh + 2]
            m_new = jnp.maximum(m, tops[hh])
            alpha = jnp.exp2(m - m_new)
            l = alpha * l
            acc = alpha * acc_ref[hh]
            for i in range(KV_CHUNK):
                src, row = slab_block(t, i)
                keep = sel_ref[hh, pl.ds(row, 1), :]
                if i == KV_CHUNK - 1:
                    keep = jnp.where(t == 0, 1.0, keep)
                p = jnp.exp2(st_ref[slot, hh, i * blk:(i + 1) * blk, :] - jnp.maximum(m_new, slab_max[hh][i]))
                v_h = vt_ref[0, src][hh * ATT_HEAD_DIM:(hh + 1) * ATT_HEAD_DIM, :]
                pv = _dot(jnp.concatenate([v_h, ones_rows], axis=0), p.astype(BF16))
                l = l + keep * pv[ATT_HEAD_DIM:ATT_HEAD_DIM + 1, :]
                acc = acc + keep * pv[0:ATT_HEAD_DIM, :]
            acc_ref[hh] = acc
            new += [m_new, l]
        return tuple(new)

    state = ()
    for hh in range(heads):
        state += (jnp.full((1, blk), NEG_INF, F32), jnp.zeros((1, blk), F32))
        acc_ref[hh] = jnp.zeros((ATT_HEAD_DIM, blk), F32)
    last = j // KV_CHUNK

    def stage_pair(t, slot, state, maxes):
        nxt = score(t + 1, 1 - slot, False)
        return accumulate(t, slot, state, maxes), nxt

    def body(t, carry):
        return lax.cond(t % 2 == 0, functools.partial(stage_pair, t, 0), functools.partial(stage_pair, t, 1),
                        *carry)

    state, maxes = lax.fori_loop(0, last, body, (state, score(0, 0, True)))
    state = lax.cond(last % 2 == 0, functools.partial(accumulate, last, 0), functools.partial(accumulate, last, 1),
                     state, maxes)

    outs = []
    for hh in range(heads):
        o_h = acc_ref[hh] / state[2 * hh + 1]
        ms = jnp.mean(o_h * o_h, axis=0, keepdims=True)
        outs.append(o_h * lax.rsqrt(ms + NORM_EPS))
    o_ref[0] = jnp.concatenate(outs, axis=0).T * g_ref[...]


def _moba(q, k, vt, g_att):
    b, s, w = q.shape
    blk = MOBA_BLOCK
    nb = s // blk
    gl = MOBA_GROUP_LANES
    groups = w // gl
    heads = gl // ATT_HEAD_DIM
    return pl.pallas_call(
        _moba_kernel,
        grid=(b, groups, nb),
        in_specs=[pl.BlockSpec((1, blk, gl), lambda bi, p, j: (bi, j, p)),
                  pl.BlockSpec((1, s, gl), lambda bi, p, j: (bi, 0, p)),
                  pl.BlockSpec((1, nb, gl, blk), lambda bi, p, j: (bi, 0, p, 0)),
                  pl.BlockSpec((1, gl), lambda bi, p, j: (0, p))],
        out_specs=pl.BlockSpec((1, blk, gl), lambda bi, p, j: (bi, j, p)),
        out_shape=jax.ShapeDtypeStruct((b, s, w), F32),
        scratch_shapes=[pltpu.VMEM((nb, gl), F32),
                        pltpu.VMEM((heads, nb + SUBLANES, blk), F32),
                        pltpu.VMEM((2, heads, KV_CHUNK * blk, blk), F32),
                        pltpu.VMEM((heads, blk, LANES), BF16),
                        pltpu.VMEM((heads, ATT_HEAD_DIM, blk), F32)],
        compiler_params=_params(("arbitrary", "arbitrary", "arbitrary")),
        name="moba",
    )(q, k, vt, g_att)


def _mlstm_kernel(qkm_ref, vm_ref, om_ref, gt_ref, cw_ref, cb_ref, bg_ref, gm_ref, bsel_ref, o_ref,
                  uext_ref, c_ref, n_ref, m_ref):
    L = MLSTM_CHUNK
    dh = MLSTM_HEAD_DIM
    nh = MLSTM_HEADS
    assert L == dh == LANES and 2 * nh == SUBLANES
    c = pl.program_id(1)

    @pl.when(c == 0)
    def _():
        uext_ref[0:SUBLANES, :] = jnp.zeros((SUBLANES, uext_ref.shape[1]), F32)
        c_ref[...] = jnp.zeros(c_ref.shape, F32)
        n_ref[...] = jnp.zeros(n_ref.shape, F32)
        m_ref[...] = jnp.zeros(m_ref.shape, F32)

    for sub in range(qkm_ref.shape[1] // L):
        _mlstm_chunk(slice(sub * L, (sub + 1) * L), qkm_ref, vm_ref, om_ref, gt_ref, cw_ref, cb_ref, bg_ref, gm_ref,
                     bsel_ref, o_ref, uext_ref, c_ref, n_ref, m_ref)


def _mlstm_chunk(rows, qkm_ref, vm_ref, om_ref, gt_ref, cw_ref, cb_ref, bg_ref, gm_ref, bsel_ref, o_ref,
                 uext_ref, c_ref, n_ref, m_ref):
    L = MLSTM_CHUNK
    dh = MLSTM_HEAD_DIM
    nh = MLSTM_HEADS

    u = qkm_ref[0, rows, :]
    uext_ref[SUBLANES:SUBLANES + L, :] = u
    cw = cw_ref[...]
    conv = cb_ref[...] + cw[CONV_WIDTH - 1:CONV_WIDTH, :] * u
    for dlt in range(1, CONV_WIDTH):
        conv = conv + cw[CONV_WIDTH - 1 - dlt:CONV_WIDTH - dlt, :] * uext_ref[SUBLANES - dlt:SUBLANES - dlt + L, :]
    uext_ref[0:SUBLANES, :] = u[L - SUBLANES:L, :]
    act = _silu(conv)

    gates = gt_ref[0, rows, :] + bg_ref[...]
    ig = gates[:, 0:LANES]
    tpos = lax.broadcasted_iota(jnp.int32, (L, L), 0)
    spos = lax.broadcasted_iota(jnp.int32, (L, L), 1)
    causal = spos <= tpos
    tril = jnp.where(causal, 1.0, 0.0).astype(BF16)
    cum = sum(_dot(tril, part) for part in _split3(_log_sigmoid(gates[:, LANES:])))
    a_tot = cum[L - 1:L, :]
    m_prev = m_ref[0:1, :]
    resid = ig - cum
    pmax = resid
    shift = 1
    while shift < L:
        pmax = jnp.maximum(pmax, jnp.where(tpos >= shift, pltpu.roll(pmax, shift, 0), -jnp.inf))
        shift *= 2
    inter = cum + m_prev
    m_t = jnp.maximum(inter, cum + pmax)
    g_loc = a_tot - cum + ig
    m_loc = jnp.max(g_loc, axis=0, keepdims=True)
    w_loc = jnp.exp(g_loc - m_loc)
    head_lane = spos < nh
    packed = jnp.where(head_lane, cum - m_t, 0.0)
    for i, qty in enumerate((inter - m_t, -m_t, g_loc - m_loc), start=1):
        packed = packed + pltpu.roll(jnp.where(head_lane, qty, 0.0), SUBLANES * i, 1)
    per_t_cols = jnp.concatenate(_split3(packed), axis=1)
    per_t_rows = (jnp.where(head_lane, resid, 0.0) + pltpu.roll(jnp.where(head_lane, w_loc, 0.0), SUBLANES, 1)).T

    m_new = jnp.maximum(a_tot + m_prev, m_loc)
    s_prev = jnp.exp(a_tot + m_prev - m_new)
    s_loc = jnp.exp(m_loc - m_new)
    m_ref[0:1, :] = m_new

    vm = vm_ref[0, rows, :]
    om = om_ref[0, rows, :]
    heads = lambda a, off=0: jnp.stack([a[:, off + h * dh:off + (h + 1) * dh] for h in range(nh)])
    bdot = lambda a, b, ca, cb: lax.dot_general(a, b, (((ca,), (cb,)), ((0,), (0,))), preferred_element_type=F32)
    q4 = heads(act)
    k4 = heads(act, nh * dh) * (dh ** -0.5)
    v4 = heads(vm)
    qb, kb, vb = q4.astype(BF16), k4.astype(BF16), v4.astype(BF16)
    c_prev = c_ref[...]
    n_all = n_ref[...]
    n_prev = jnp.stack([n_all[h:h + 1, :] for h in range(nh)])
    bcast = _dot(per_t_cols, bsel_ref[...])
    qty = lambda i: jnp.stack([bcast[:, (4 * h + i) * dh:(4 * h + i + 1) * dh] for h in range(nh)])
    decay_b = qty(0)
    w_inter_b, floor_b, w_loc_b = jnp.exp(qty(1)), jnp.exp(qty(2)), jnp.exp(qty(3))
    resid_rows = jnp.stack([per_t_rows[h:h + 1, :] for h in range(nh)])
    w_rows = jnp.stack([per_t_rows[SUBLANES + h:SUBLANES + h + 1, :] for h in range(nh)])

    s_qk = bdot(qb, kb, 2, 2) * jnp.where(causal[None], jnp.exp(decay_b + resid_rows), 0.0)
    intra = bdot(s_qk.astype(BF16), jnp.concatenate([vb, jnp.ones((nh, L, dh), BF16)], axis=2), 2, 1)
    state_rows = jnp.concatenate([c_prev, jnp.broadcast_to(n_prev, (nh, dh, dh))], axis=1).astype(BF16)
    carried = bdot(qb, state_rows, 2, 2)
    num = intra[:, :, :dh] + w_inter_b * carried[:, :, :dh]
    den = intra[:, :, dh:] + w_inter_b * carried[:, :, dh:]
    hout = num / jnp.maximum(jnp.abs(den), floor_b)

    sq = (hout * hout).astype(BF16).reshape(nh * L, dh)
    mean_sq = (_dot(sq, jnp.ones((dh, dh), BF16)) * (1.0 / dh)).reshape(nh, L, dh)
    gm = jnp.stack([gm_ref[:, h * dh:(h + 1) * dh] for h in range(nh)])
    out = hout * lax.rsqrt(mean_sq + NORM_EPS) * gm * jax.nn.sigmoid(heads(om))
    for h in range(nh):
        o_ref[0, rows, h * dh:(h + 1) * dh] = out[h]

    vw = (v4 * w_loc_b).astype(BF16)
    for h in range(nh):
        c_loc = _dot_tn(vw[h], kb[h])
        n_loc = _dot(jnp.broadcast_to(w_rows[h], (SUBLANES, L)).astype(BF16), kb[h])[0:1, :]
        c_ref[h] = s_prev[:, h:h + 1] * c_prev[h] + s_loc[:, h:h + 1] * c_loc
        n_ref[h:h + 1, :] = s_prev[:, h:h + 1] * n_all[h:h + 1, :] + s_loc[:, h:h + 1] * n_loc


def _mlstm(qkm, vm, om, gates, conv_w, conv_b, b_gate, g_m):
    b, s, _ = qkm.shape
    L = MLSTM_CHUNK
    step_rows = MLSTM_CHUNKS_PER_STEP * L
    nc = s // step_rows
    lane_pad = ((0, 0), (0, LANES - MLSTM_HEADS))
    bg = b_gate.reshape(1, -1)
    bg = jnp.concatenate([jnp.pad(bg[:, :MLSTM_HEADS], lane_pad), jnp.pad(bg[:, MLSTM_HEADS:], lane_pad)], axis=1)
    tok = lambda w: pl.BlockSpec((1, step_rows, w), lambda bi, c: (bi, c, 0))
    full = lambda a: pl.BlockSpec(a.shape, lambda bi, c: (0,) * a.ndim)
    cb = conv_b.reshape(1, -1)
    gm = g_m.reshape(1, -1)
    n_qty = 4
    row = jnp.arange(3 * LANES, dtype=jnp.int32) % LANES
    col_blk = jnp.arange(MLSTM_HEADS * n_qty * MLSTM_HEAD_DIM, dtype=jnp.int32) // MLSTM_HEAD_DIM
    bsel = (((row // SUBLANES)[:, None] == (col_blk % n_qty)[None, :])
            & ((row % SUBLANES)[:, None] == (col_blk // n_qty)[None, :])).astype(BF16)
    return pl.pallas_call(
        _mlstm_kernel,
        grid=(b, nc),
        in_specs=[tok(2 * D_MLSTM), tok(D_MLSTM), tok(D_MLSTM), tok(GATE_LANES),
                  full(conv_w), full(cb), full(bg), full(gm), full(bsel)],
        out_specs=tok(D_MLSTM),
        out_shape=jax.ShapeDtypeStruct((b, s, D_MLSTM), F32),
        scratch_shapes=[pltpu.VMEM((SUBLANES + L, 2 * D_MLSTM), F32),
                        pltpu.VMEM((MLSTM_HEADS, MLSTM_HEAD_DIM, MLSTM_HEAD_DIM), F32),
                        pltpu.VMEM((SUBLANES, MLSTM_HEAD_DIM), F32),
                        pltpu.VMEM((SUBLANES, LANES), F32)],
        compiler_params=_params(("arbitrary", "arbitrary")),
        name="mlstm",
    )(qkm, vm, om, gates, conv_w, cb, bg, gm, bsel)


META_E0, META_E1, META_W0, META_W1, META_L0, META_L1 = range(6)
STAT_LEN, STAT_START = 0, 1


def _outproj_kernel(attn_ref, hm_ref, x_ref, mod_ref, g_ref, wout_ref, wr_hi_ref, wr_lo_ref, br_ref,
                    x1_ref, h2_ref, meta_ref, stat_ref):
    d = x_ref.shape[1]
    tm = x_ref.shape[0]

    mod = mod_ref[0]
    y = (_dot(attn_ref[...].astype(BF16), wout_ref[0:D_ATT, :])
         + _dot(hm_ref[...].astype(BF16), wout_ref[D_ATT:, :]))
    x1 = x_ref[...] + mod[:, 2 * d:3 * d] * y
    x1_ref[...] = x1
    h2 = _rms_modulate(x1, g_ref[...], mod[:, 3 * d:4 * d], mod[:, 4 * d:5 * d])
    h2_ref[...] = h2.astype(BF16)

    h_hi = h2.astype(BF16)
    h_lo = (h2 - h_hi.astype(F32)).astype(BF16)
    logit = (_dot_nt(wr_hi_ref[...], h_hi) + _dot_nt(wr_hi_ref[...], h_lo) + _dot_nt(wr_lo_ref[...], h_hi)
             + br_ref[...])
    rid = lax.broadcasted_iota(jnp.int32, logit.shape, 0)
    big = jnp.int32(ROUTER_LANES)

    is_g = (rid >= GROUP_LANE0) & (rid < GROUP_LANE0 + N_GROUPS)
    gmax = jnp.max(jnp.where(is_g, logit, -jnp.inf), axis=0, keepdims=True)
    gsum = jnp.sum(jnp.where(is_g, jnp.exp(logit - gmax), 0.0), axis=0, keepdims=True)
    g_w = 1.0 / gsum
    g_idx = jnp.min(jnp.where(is_g & (logit == gmax), rid, big), axis=0, keepdims=True) - GROUP_LANE0

    in_grp = (rid < N_EXPERTS) & ((rid // EXPERTS_PER_GROUP) == g_idx)
    emax = jnp.max(jnp.where(in_grp, logit, -jnp.inf), axis=0, keepdims=True)
    esum = jnp.sum(jnp.where(in_grp, jnp.exp(logit - emax), 0.0), axis=0, keepdims=True)
    e0 = jnp.min(jnp.where(in_grp & (logit == emax), rid, big), axis=0, keepdims=True)
    rest = in_grp & (rid != e0)
    e2max = jnp.max(jnp.where(rest, logit, -jnp.inf), axis=0, keepdims=True)
    e1 = jnp.min(jnp.where(rest & (logit == e2max), rid, big), axis=0, keepdims=True)
    p0 = 1.0 / esum
    p1 = jnp.exp(e2max - emax) / esum
    w0 = g_w * p0 / (p0 + p1)
    w1 = g_w * p1 / (p0 + p1)

    memb = jnp.where((rid == e0) | (rid == e1), 1.0, 0.0).astype(BF16)
    tpos = lax.broadcasted_iota(jnp.int32, (tm, tm), 0)
    spos = lax.broadcasted_iota(jnp.int32, (tm, tm), 1)
    earlier = _dot(memb, jnp.where(tpos < spos, 1.0, 0.0).astype(BF16))
    count = _dot(memb, jnp.ones((tm, tm), BF16)).astype(jnp.int32)
    run_len = (((count + (RUN_ALIGN - 1)) // RUN_ALIGN) * RUN_ALIGN).astype(F32)
    epos = lax.broadcasted_iota(jnp.int32, (ROUTER_LANES, ROUTER_LANES), 0)
    fpos = lax.broadcasted_iota(jnp.int32, (ROUTER_LANES, ROUTER_LANES), 1)
    run_start = _dot(jnp.where(fpos < epos, 1.0, 0.0).astype(BF16), run_len.astype(BF16))
    row = run_start + earlier
    l0 = jnp.sum(jnp.where(rid == e0, row, 0.0), axis=0, keepdims=True)
    l1 = jnp.sum(jnp.where(rid == e1, row, 0.0), axis=0, keepdims=True)

    meta_t = jnp.zeros(logit.shape, F32)
    for slot, val in ((META_E0, e0.astype(F32)), (META_E1, e1.astype(F32)), (META_W0, w0), (META_W1, w1),
                      (META_L0, l0), (META_L1, l1)):
        meta_t = jnp.where(rid == slot, val, meta_t)
    meta_ref[...] = meta_t.T
    srow = lax.broadcasted_iota(jnp.int32, (SUBLANES, ROUTER_LANES), 0)
    len_row = run_len[:, 0:ROUTER_LANES].T[0:SUBLANES, :]
    start_row = run_start[:, 0:ROUTER_LANES].T[0:SUBLANES, :]
    stat_ref[0] = jnp.where(srow == STAT_LEN, len_row, jnp.where(srow == STAT_START, start_row, 0.0))


def _outproj(attn, hm, x2d, mod3, norm_g, w_out, w_rg, b_rg, w_re, b_re, seq):
    t, d = x2d.shape
    tm = MOE_TILE
    per_batch = seq // tm
    wr = jnp.pad(jnp.concatenate([w_re, w_rg], axis=1), ((0, 0), (0, ROUTER_LANES - N_EXPERTS - N_GROUPS)))
    br = jnp.pad(jnp.concatenate([b_re, b_rg]).reshape(1, -1), ((0, 0), (0, ROUTER_LANES - N_EXPERTS - N_GROUPS)))
    wr = wr.T
    br = jnp.broadcast_to(br.T, (ROUTER_LANES, tm))
    wr_hi = wr.astype(BF16)
    wr_lo = (wr - wr_hi.astype(F32)).astype(BF16)
    wout = w_out.astype(BF16)
    tok = lambda w: pl.BlockSpec((tm, w), lambda i: (i, 0))
    full = lambda a: pl.BlockSpec(a.shape, lambda i: (0,) * a.ndim)
    return pl.pallas_call(
        _outproj_kernel,
        grid=(t // tm,),
        in_specs=[tok(D_ATT), tok(D_MLSTM), tok(d),
                  pl.BlockSpec((1, 1, mod3.shape[2]), lambda i: (i // per_batch, 0, 0)),
                  full(norm_g), full(wout), full(wr_hi), full(wr_lo), full(br)],
        out_specs=[tok(d), tok(d), tok(ROUTER_LANES),
                   pl.BlockSpec((1, SUBLANES, ROUTER_LANES), lambda i: (i, 0, 0))],
        out_shape=[jax.ShapeDtypeStruct((t, d), F32), jax.ShapeDtypeStruct((t, d), BF16),
                   jax.ShapeDtypeStruct((t, ROUTER_LANES), F32),
                   jax.ShapeDtypeStruct((t // tm, SUBLANES, ROUTER_LANES), F32)],
        compiler_params=_params(("arbitrary",)),
        name="outproj",
    )(attn, hm, x2d, mod3, norm_g, wout, wr_hi, wr_lo, br)


def _for_each_piece(length, pieces, fn):
    pos = jnp.int32(0)
    for size in pieces:
        take = length & size

        @pl.when(take != 0)
        def _(pos=pos, size=size):
            fn(pl.multiple_of(pos, RUN_ALIGN), size)

        pos = pos + take


def _for_each_chunk(tile, rows_ref, cdst_ref, fn):
    def body(c, carry):
        fn(pl.multiple_of(c * RUN_ALIGN, RUN_ALIGN),
           pl.multiple_of(cdst_ref[tile * CHUNKS_PER_TILE + c], RUN_ALIGN))
        return carry

    lax.fori_loop(0, rows_ref[tile] // RUN_ALIGN, body, 0)


def _dispatch_kernel(cdst_ref, rows_ref, tail_ref, tlen_ref, nu_ref, meta_ref, h2_ref, xs_ref,
                     perm_ref, zero_ref, sem, zsem):
    i = pl.program_id(0)
    rb = MOE_BLOCK_ROWS
    n_blocks = xs_ref.shape[0] // rb

    @pl.when(i == 0)
    def _():
        zero_ref[...] = jnp.zeros(zero_ref.shape, zero_ref.dtype)

        def zero_copy(row, size):
            return pltpu.make_async_copy(zero_ref.at[pl.ds(0, size)], xs_ref.at[pl.ds(row, size)], zsem)

        def fill(op):
            def per_expert(e, carry):
                start = tail_ref[e]
                _for_each_piece(tlen_ref[e], TAIL_PIECES,
                                lambda pos, size: op(zero_copy(pl.multiple_of(start + pos, RUN_ALIGN), size)))
                return carry

            lax.fori_loop(0, N_EXPERTS, per_expert, 0)

            def per_block(blk, carry):
                for part in range(rb // ZERO_ROWS):
                    op(zero_copy(pl.multiple_of(blk * rb + part * ZERO_ROWS, ZERO_ROWS), ZERO_ROWS))
                return carry

            lax.fori_loop(nu_ref[0], n_blocks, per_block, 0)

        fill(lambda cp: cp.start())
        fill(lambda cp: cp.wait())

    meta_t = meta_ref[...].T
    l0 = meta_t[META_L0:META_L0 + 1, :].astype(jnp.int32)
    l1 = meta_t[META_L1:META_L1 + 1, :].astype(jnp.int32)
    rpos = lax.broadcasted_iota(jnp.int32, (perm_ref.shape[1], meta_ref.shape[0]), 0)
    onehot = jnp.where((rpos == l0) | (rpos == l1), 1.0, 0.0).astype(BF16)
    slot = i % 2
    perm_ref[slot] = _dot(onehot, h2_ref[...]).astype(BF16)

    def run_copy(s, loc, glob, size):
        return pltpu.make_async_copy(perm_ref.at[s, pl.ds(loc, size)], xs_ref.at[pl.ds(glob, size)], sem.at[s])

    def wait_tile(tile, s):
        _for_each_piece(rows_ref[tile], TILE_PIECES, lambda pos, size: run_copy(s, 0, 0, size).wait())

    @pl.when(i > 0)
    def _():
        wait_tile(i - 1, 1 - slot)

    _for_each_chunk(i, rows_ref, cdst_ref, lambda loc, glob: run_copy(slot, loc, glob, RUN_ALIGN).start())

    @pl.when(i == pl.num_programs(0) - 1)
    def _():
        wait_tile(i, slot)


def _dispatch(plan, meta, h2):
    t, d = h2.shape
    tm = MOE_TILE
    grid_spec = pltpu.PrefetchScalarGridSpec(
        num_scalar_prefetch=5,
        grid=(t // tm,),
        in_specs=[pl.BlockSpec((tm, ROUTER_LANES), lambda i, *_: (i, 0)),
                  pl.BlockSpec((tm, d), lambda i, *_: (i, 0))],
        out_specs=pl.BlockSpec(memory_space=pl.ANY),
        scratch_shapes=[pltpu.VMEM((2, PERM_ROWS, d), BF16), pltpu.VMEM((ZERO_ROWS, d), BF16),
                        pltpu.SemaphoreType.DMA((2,)), pltpu.SemaphoreType.DMA(())],
    )
    return pl.pallas_call(
        _dispatch_kernel,
        grid_spec=grid_spec,
        out_shape=jax.ShapeDtypeStruct((plan["n_rows"], d), BF16),
        compiler_params=_params(("arbitrary",)),
        name="dispatch",
    )(plan["chunk_dst"], plan["tile_rows"], plan["tail_start"], plan["tail_len"], plan["n_used"], meta, h2)


def _expert_kernel(be_ref, nu_ref, xs_ref, w1_ref, w3_ref, w2_ref, y_ref, w1b_ref, w3b_ref, w2b_ref):
    i = pl.program_id(0)
    used = i < nu_ref[0]

    @pl.when(used & ((i == 0) | (be_ref[i] != be_ref[jnp.maximum(i, 1) - 1])))
    def _():
        w1b_ref[...] = w1_ref[0].astype(BF16)
        w3b_ref[...] = w3_ref[0].astype(BF16)
        w2b_ref[...] = w2_ref[0].astype(BF16)

    @pl.when(used)
    def _():
        x = xs_ref[...]
        a = _dot(x, w1b_ref[...])
        b = _dot(x, w3b_ref[...])
        y_ref[...] = _dot((_silu(a) * b).astype(BF16), w2b_ref[...]).astype(y_ref.dtype)

    @pl.when(jnp.logical_not(used))
    def _():
        y_ref[...] = jnp.zeros(y_ref.shape, y_ref.dtype)


def _experts(block_e, n_used, xs, w1, w3, w2):
    n_rows, d = xs.shape
    rb = MOE_BLOCK_ROWS
    de = w1.shape[2]
    used = lambda i, be, nu: jnp.minimum(i, nu[0] - 1)
    grid_spec = pltpu.PrefetchScalarGridSpec(
        num_scalar_prefetch=2,
        grid=(n_rows // rb,),
        in_specs=[pl.BlockSpec((rb, d), lambda i, be, nu: (used(i, be, nu), 0)),
                  pl.BlockSpec((1, d, de), lambda i, be, nu: (be[used(i, be, nu)], 0, 0)),
                  pl.BlockSpec((1, d, de), lambda i, be, nu: (be[used(i, be, nu)], 0, 0)),
                  pl.BlockSpec((1, de, d), lambda i, be, nu: (be[used(i, be, nu)], 0, 0))],
        out_specs=pl.BlockSpec((rb, d), lambda i, be, nu: (i, 0)),
        scratch_shapes=[pltpu.VMEM((d, de), BF16), pltpu.VMEM((d, de), BF16), pltpu.VMEM((de, d), BF16)],
    )
    return pl.pallas_call(
        _expert_kernel,
        grid_spec=grid_spec,
        out_shape=jax.ShapeDtypeStruct((n_rows, d), BF16),
        compiler_params=_params(("arbitrary",)),
        name="experts",
    )(block_e, n_used, xs, w1, w3, w2)


def _combine_kernel(final_norm, cdst_ref, rows_ref, x1_ref, meta_ref, mod_ref, g_ref, y_ref, o_ref, ybuf_ref, sem):
    tc, d = x1_ref.shape
    i = pl.program_id(0)

    slot = i % 2

    def run_copy(s, loc, glob, size):
        return pltpu.make_async_copy(y_ref.at[pl.ds(glob, size)], ybuf_ref.at[s, pl.ds(loc, size)], sem.at[s])

    def start_tile(tile, s):
        _for_each_chunk(tile, rows_ref, cdst_ref, lambda loc, glob: run_copy(s, loc, glob, RUN_ALIGN).start())

    @pl.when(i == 0)
    def _():
        ybuf_ref[...] = jnp.zeros(ybuf_ref.shape, ybuf_ref.dtype)
        start_tile(0, 0)

    @pl.when(i + 1 < pl.num_programs(0))
    def _():
        start_tile(i + 1, 1 - slot)

    _for_each_piece(rows_ref[i], TILE_PIECES, lambda pos, size: run_copy(slot, 0, 0, size).wait())

    meta = meta_ref[...]
    yb = ybuf_ref[slot]
    rpos = lax.broadcasted_iota(jnp.int32, (tc, ybuf_ref.shape[1]), 1)
    sel = jnp.zeros(rpos.shape, F32)
    for l_lane, w_lane in ((META_L0, META_W0), (META_L1, META_W1)):
        sel = jnp.where(rpos == meta[:, l_lane:l_lane + 1].astype(jnp.int32), meta[:, w_lane:w_lane + 1], sel)
    moe = _dot(sel.astype(BF16), yb)
    x2 = x1_ref[...] + mod_ref[0][:, 5 * d:6 * d] * moe
    if final_norm:
        x2 = x2 * lax.rsqrt(jnp.mean(x2 * x2, axis=-1, keepdims=True) + NORM_EPS) * g_ref[...]
    o_ref[...] = x2


def _combine(plan, x1, meta, mod3, norm_f_g, y, seq, final_norm):
    t, d = x1.shape
    tc = MOE_TILE
    per_batch = seq // tc
    gf = norm_f_g.reshape(1, d)
    grid_spec = pltpu.PrefetchScalarGridSpec(
        num_scalar_prefetch=2,
        grid=(t // tc,),
        in_specs=[pl.BlockSpec((tc, d), lambda i, *_: (i, 0)),
                  pl.BlockSpec((tc, ROUTER_LANES), lambda i, *_: (i, 0)),
                  pl.BlockSpec((1, 1, mod3.shape[2]), lambda i, *_: (i // per_batch, 0, 0)),
                  pl.BlockSpec(gf.shape, lambda i, *_: (0, 0)),
                  pl.BlockSpec(memory_space=pl.ANY)],
        out_specs=pl.BlockSpec((tc, d), lambda i, *_: (i, 0)),
        scratch_shapes=[pltpu.VMEM((2, PERM_ROWS, d), BF16), pltpu.SemaphoreType.DMA((2,))],
    )
    return pl.pallas_call(
        functools.partial(_combine_kernel, final_norm),
        grid_spec=grid_spec,
        out_shape=jax.ShapeDtypeStruct((t, d), F32),
        compiler_params=_params(("arbitrary",)),
        name="combine",
    )(plan["chunk_dst"], plan["tile_rows"], x1, meta, mod3, gf, y)


def _routing_plan(stats, n_tokens):
    rb = MOE_BLOCK_ROWS
    run_len = stats[:, STAT_LEN, :N_EXPERTS].astype(jnp.int32)
    run_loc = stats[:, STAT_START, :N_EXPERTS].astype(jnp.int32)
    n_tiles = run_len.shape[0]
    total = jnp.sum(run_len, axis=0)
    padded = ((total + rb - 1) // rb) * rb
    pad_end = jnp.cumsum(padded)
    pad_start = pad_end - padded
    run_glob = pad_start[None, :] + jnp.cumsum(run_len, axis=0) - run_len
    n_rows = n_tokens * TOP_K_EXPERTS + n_tiles * N_EXPERTS * (RUN_ALIGN - 1) + N_EXPERTS * (rb - 1)
    n_rows = ((n_rows + rb - 1) // rb) * rb
    block_row = jnp.arange(n_rows // rb, dtype=jnp.int32) * rb
    block_e = jnp.minimum(jnp.sum((pad_end[None, :] <= block_row[:, None]).astype(jnp.int32), axis=1),
                          N_EXPERTS - 1)
    local_row = jnp.arange(CHUNKS_PER_TILE, dtype=jnp.int32) * RUN_ALIGN
    chunk_e = jnp.minimum(jnp.sum(((run_loc + run_len)[:, None, :] <= local_row[None, :, None]).astype(jnp.int32),
                                  axis=-1), N_EXPERTS - 1)
    shift = jnp.sum(jnp.where(chunk_e[..., None] == jnp.arange(N_EXPERTS, dtype=jnp.int32),
                              (run_glob - run_loc)[:, None, :], 0), axis=-1)
    chunk_dst = shift + local_row[None, :]
    return dict(chunk_dst=chunk_dst.reshape(-1), tile_rows=jnp.sum(run_len, axis=1),
                tail_start=pad_start + total, tail_len=padded - total,
                n_used=(pad_end[-1:] // rb).astype(jnp.int32), block_e=block_e.astype(jnp.int32), n_rows=n_rows)


def kernel(x, c, positions, w_ada, b_ada, norm1_g, w_in, b_gate, conv_w, conv_b, attn_out_g, mlstm_out_g,
           w_out, norm2_g, w_rg, b_rg, w_re, b_re, w1, w3, w2, norm_f_g):
    b, s, d = x.shape
    depth = w_ada.shape[0]
    assert d == D_MODEL and s % MOE_TILE == 0 and s % (KV_CHUNK * MOBA_BLOCK) == 0
    x2d = x.reshape(b * s, d)
    for l in range(depth):
        mod3 = _mod(c, w_ada[l], b_ada[l]).reshape(b, 1, 6 * d)
        q, k, vt, qkm, vm, om, gates = _inproj(x2d.reshape(b, s, d), positions, mod3,
                                              norm1_g[l].reshape(1, d), w_in[l])
        attn = _moba(q, k, vt, attn_out_g[l].reshape(1, D_ATT))
        hm = _mlstm(qkm, vm, om, gates, conv_w[l], conv_b[l], b_gate[l], mlstm_out_g[l])
        x1, h2, meta, stats = _outproj(attn.reshape(b * s, D_ATT), hm.reshape(b * s, D_MLSTM), x2d, mod3,
                                       norm2_g[l].reshape(1, d), w_out[l], w_rg[l], b_rg[l], w_re[l],
                                       b_re[l], s)
        plan = _routing_plan(stats, b * s)
        xs = _dispatch(plan, meta, h2)
        y = _experts(plan["block_e"], plan["n_used"], xs, w1[l], w3[l], w2[l])
        x2d = _combine(plan, x1, meta, mod3, norm_f_g, y, s, final_norm=(l == depth - 1))
    return x2d.reshape(b, s, d)
```

```python
import functools

import jax
import jax.numpy as jnp
from jax import lax
from jax.experimental import pallas as pl
from jax.experimental.pallas import tpu as pltpu

F32 = jnp.float32
BF16 = jnp.bfloat16

D_MODEL = 1024
D_ATT = 512
ATT_HEADS = 8
ATT_HEAD_DIM = 64
D_MLSTM = 512
MLSTM_HEADS = 4
MLSTM_HEAD_DIM = 128
MOBA_BLOCK = 256
MOBA_TOPK = 3
ROPE_THETA = 10000.0
MLSTM_CHUNK = 128
MLSTM_CHUNKS_PER_STEP = 8
CONV_WIDTH = 4
N_GROUPS = 4
EXPERTS_PER_GROUP = 8
N_EXPERTS = N_GROUPS * EXPERTS_PER_GROUP
TOP_K_EXPERTS = 2
D_EXPERT = 512
MOE_BLOCK_ROWS = 512
NORM_EPS = 1e-6
NEG_INF = -1e30
LOG2_E = 1.4426950408889634
KV_CHUNK = 2
MOBA_GROUP_LANES = 512

LANES = 128
SUBLANES = 8
VMEM_LIMIT_BYTES = 56 * 1024 * 1024

INPROJ_TILE = 512
MOE_TILE = 512
RUN_ALIGN = 2 * SUBLANES
PERM_ROWS = TOP_K_EXPERTS * MOE_TILE + N_EXPERTS * RUN_ALIGN
CHUNKS_PER_TILE = PERM_ROWS // RUN_ALIGN
TAIL_PIECES = tuple(RUN_ALIGN << p for p in reversed(range(5)))
assert TAIL_PIECES[0] * 2 == MOE_BLOCK_ROWS
TILE_PIECES = tuple(RUN_ALIGN << p for p in reversed(range(7)))
ZERO_ROWS = TAIL_PIECES[0]
GATE_LANES = 2 * LANES
ROUTER_LANES = LANES
GROUP_LANE0 = N_EXPERTS


def _dot(a, b):
    return jnp.dot(a, b, preferred_element_type=F32)


def _dot_nt(a, b):
    return lax.dot_general(a, b, (((1,), (1,)), ((), ())), preferred_element_type=F32)


def _dot_tn(a, b):
    return lax.dot_general(a, b, (((0,), (0,)), ((), ())), preferred_element_type=F32)


def _split3(x):
    a = x.astype(BF16)
    r = x - a.astype(F32)
    b = r.astype(BF16)
    c = (r - b.astype(F32)).astype(BF16)
    return a, b, c


def _silu(x):
    return x * jax.nn.sigmoid(x)


def _log_sigmoid(x):
    return jnp.minimum(x, 0.0) - jnp.log1p(jnp.exp(-jnp.abs(x)))


def _params(semantics, vmem=VMEM_LIMIT_BYTES):
    return pltpu.CompilerParams(dimension_semantics=semantics, vmem_limit_bytes=vmem)


def _mod_kernel(c_ref, w_ref, b_ref, o_ref):
    sc = _silu(c_ref[...])
    o_ref[...] = jnp.dot(sc, w_ref[...], precision=lax.Precision.HIGHEST,
                         preferred_element_type=F32) + b_ref[...]


def _mod(c, w_ada, b_ada):
    b, d = c.shape
    n = w_ada.shape[1]
    tn = D_MODEL
    return pl.pallas_call(
        _mod_kernel,
        grid=(n // tn,),
        in_specs=[pl.BlockSpec((b, d), lambda i: (0, 0)),
                  pl.BlockSpec((d, tn), lambda i: (0, i)),
                  pl.BlockSpec((1, tn), lambda i: (0, i))],
        out_specs=pl.BlockSpec((b, tn), lambda i: (0, i)),
        out_shape=jax.ShapeDtypeStruct((b, n), F32),
        compiler_params=_params(("arbitrary",)),
        name="mod",
    )(c, w_ada, b_ada.reshape(1, n))


def _rms_modulate(x, g, shift, scale):
    y = x * lax.rsqrt(jnp.mean(x * x, axis=-1, keepdims=True) + NORM_EPS)
    return (y * g) * (1.0 + scale) + shift


def _rope(t, cos, sin, first_half):
    outs = []
    for c in range(t.shape[1] // LANES):
        tc = t[:, c * LANES:(c + 1) * LANES]
        rot = jnp.where(first_half, -pltpu.roll(tc, LANES - ATT_HEAD_DIM // 2, 1),
                        pltpu.roll(tc, ATT_HEAD_DIM // 2, 1))
        outs.append(tc * cos + rot * sin)
    return jnp.concatenate(outs, axis=1)


def _inproj_kernel(x_ref, pos_ref, mod_ref, g_ref, invf_ref, wq_ref, wk_ref, wvt_ref, wqk_ref,
                   wv_ref, wo_ref, wg_ref, q_ref, k_ref, vt_ref, qkm_ref, vm_ref, om_ref, gt_ref):
    d = x_ref.shape[2]
    x = x_ref[0]
    mod = mod_ref[0]
    h = _rms_modulate(x, g_ref[...], mod[:, 0:d], mod[:, d:2 * d])
    hb = h.astype(BF16)

    ang = pos_ref[0].astype(F32) * invf_ref[...]
    cos = jnp.cos(ang)
    sin = jnp.sin(ang)
    lane = lax.broadcasted_iota(jnp.int32, cos.shape, 1)
    first_half = (lane & (ATT_HEAD_DIM // 2)) == 0

    q = _rope(_dot(hb, wq_ref[...]), cos, sin, first_half)
    q_ref[0] = (q * (ATT_HEAD_DIM ** -0.5 * LOG2_E)).astype(BF16)
    k_ref[0] = _rope(_dot(hb, wk_ref[...]), cos, sin, first_half).astype(BF16)
    vt = _dot_nt(wvt_ref[...], hb).astype(BF16)
    for blk_i in range(vt_ref.shape[1]):
        vt_ref[0, blk_i] = vt[:, blk_i * MOBA_BLOCK:(blk_i + 1) * MOBA_BLOCK]
    qkm_ref[0] = _dot(hb, wqk_ref[...])
    vm_ref[0] = _dot(hb, wv_ref[...])
    om_ref[0] = _dot(hb, wo_ref[...])
    gt_ref[0] = _dot(hb, wg_ref[...])


def _inproj(x, positions, mod3, norm_g, w_in):
    b, s, d = x.shape
    tm = INPROJ_TILE
    per_tile = tm // MOBA_BLOCK
    nb = s // MOBA_BLOCK
    o = [0, D_ATT, 2 * D_ATT, 3 * D_ATT, 3 * D_ATT + 2 * D_MLSTM, 3 * D_ATT + 3 * D_MLSTM,
         3 * D_ATT + 4 * D_MLSTM, 3 * D_ATT + 4 * D_MLSTM + 2 * MLSTM_HEADS]
    wb = w_in.astype(BF16)
    wq, wk, wv_a, wqk, wv, wo, wg = (wb[:, o[i]:o[i + 1]] for i in range(7))
    wvt = wv_a.T
    lane_pad = ((0, 0), (0, LANES - MLSTM_HEADS))
    wg = jnp.concatenate([jnp.pad(wg[:, :MLSTM_HEADS], lane_pad), jnp.pad(wg[:, MLSTM_HEADS:], lane_pad)], axis=1)
    half = ATT_HEAD_DIM // 2
    inv_freq = ROPE_THETA ** (-jnp.arange(half, dtype=F32) / half)
    invf = jnp.tile(inv_freq, LANES // half).reshape(1, LANES)

    full = lambda a: pl.BlockSpec(a.shape, lambda bi, i: (0,) * a.ndim)
    tok = lambda w: pl.BlockSpec((1, tm, w), lambda bi, i: (bi, i, 0))
    out_shape = [jax.ShapeDtypeStruct((b, s, D_ATT), BF16),
                 jax.ShapeDtypeStruct((b, s, D_ATT), BF16),
                 jax.ShapeDtypeStruct((b, nb, D_ATT, MOBA_BLOCK), BF16),
                 jax.ShapeDtypeStruct((b, s, 2 * D_MLSTM), F32),
                 jax.ShapeDtypeStruct((b, s, D_MLSTM), F32),
                 jax.ShapeDtypeStruct((b, s, D_MLSTM), F32),
                 jax.ShapeDtypeStruct((b, s, GATE_LANES), F32)]
    return pl.pallas_call(
        _inproj_kernel,
        grid=(b, s // tm),
        in_specs=[tok(d), tok(1),
                  pl.BlockSpec((1, 1, mod3.shape[2]), lambda bi, i: (bi, 0, 0)),
                  full(norm_g), full(invf), full(wq), full(wk), full(wvt), full(wqk), full(wv),
                  full(wo), full(wg)],
        out_specs=[tok(D_ATT), tok(D_ATT),
                   pl.BlockSpec((1, per_tile, D_ATT, MOBA_BLOCK), lambda bi, i: (bi, i, 0, 0)),
                   tok(2 * D_MLSTM), tok(D_MLSTM), tok(D_MLSTM), tok(GATE_LANES)],
        out_shape=out_shape,
        compiler_params=_params(("arbitrary", "arbitrary")),
        name="inproj",
    )(x, positions.reshape(b, s, 1), mod3, norm_g, invf, wq, wk, wvt, wqk, wv, wo, wg)


def _moba_kernel(q_ref, k_ref, vt_ref, g_ref, o_ref, kmean_ref, sel_ref, st_ref, qh_ref, acc_ref):
    blk = MOBA_BLOCK
    nb = k_ref.shape[1] // blk
    j = pl.program_id(2)
    heads = q_ref.shape[2] // ATT_HEAD_DIM
    hpl = LANES // ATT_HEAD_DIM

    @pl.when(j == 0)
    def _():
        for n in range(nb):
            kb = k_ref[0, n * blk:(n + 1) * blk, :].astype(F32)
            kmean_ref[n:n + 1, :] = jnp.mean(kb, axis=0, keepdims=True)

    lane = lax.broadcasted_iota(jnp.int32, (blk, LANES), 1)
    blk_id = lax.broadcasted_iota(jnp.int32, (nb, blk), 0)
    km_hi = kmean_ref[...].astype(BF16)
    km_lo = (kmean_ref[...] - km_hi.astype(F32)).astype(BF16)
    past = blk_id < j
    tile = lambda a, hh: a[:, (hh // hpl) * LANES:(hh // hpl + 1) * LANES]

    gates = []
    for hh in range(heads):
        q = tile(q_ref[0], hh)
        in_head = (lane >= (hh % hpl) * ATT_HEAD_DIM) & (lane < (hh % hpl + 1) * ATT_HEAD_DIM)
        qh_ref[hh] = jnp.where(in_head, q, jnp.zeros_like(q))
        gates.append(_dot_nt(tile(km_hi, hh), qh_ref[hh]) + _dot_nt(tile(km_lo, hh), qh_ref[hh]))
    g = jnp.where(past[None], jnp.stack(gates), NEG_INF)
    sel = jnp.zeros(g.shape, F32)
    for _ in range(min(MOBA_TOPK, nb)):
        top = jnp.max(g, axis=1, keepdims=True)
        idx = jnp.min(jnp.where(g == top, blk_id[None], nb), axis=1, keepdims=True)
        pick = blk_id[None] == idx
        sel = jnp.where(pick, 1.0, sel)
        g = jnp.where(pick, -jnp.inf, g)
    sel_ref[:, 0:nb, :] = jnp.where((sel > 0.0) & past[None], 1.0, 0.0)
    sel_ref[:, nb:nb + SUBLANES, :] = jnp.zeros((heads, SUBLANES, blk), F32)

    kpos = lax.broadcasted_iota(jnp.int32, (blk, blk), 0)
    qpos = lax.broadcasted_iota(jnp.int32, (blk, blk), 1)
    causal_bias = jnp.where(kpos <= qpos, 0.0, NEG_INF)

    def slab_block(t, i):
        b_i = j - KV_CHUNK * t - (KV_CHUNK - 1 - i)
        return jnp.maximum(b_i, 0), jnp.where(b_i >= 0, b_i, nb)

    def score(t, slot, own_chunk):
        tops = [None] * heads
        slab_max = [[None] * KV_CHUNK for _ in range(heads)]
        for i in range(KV_CHUNK):
            src, row = slab_block(t, i)
            kb = k_ref[0, pl.ds(pl.multiple_of(src * blk, blk), blk), :]
            for hh in range(heads):
                st = _dot_nt(tile(kb, hh), qh_ref[hh])
                if own_chunk and i == KV_CHUNK - 1:
                    st = st + causal_bias
                    cmax = jnp.max(st, axis=0, keepdims=True)
                    smax = cmax
                else:
                    cmax = jnp.max(st, axis=0, keepdims=True)
                    smax = cmax + (sel_ref[hh, pl.ds(row, 1), :] - 1.0) * (-NEG_INF)
                st_ref[slot, hh, i * blk:(i + 1) * blk, :] = st
                slab_max[hh][i] = cmax
                tops[hh] = smax if tops[hh] is None else jnp.maximum(tops[hh], smax)
        return tuple(tops), tuple(tuple(r) for r in slab_max)

    ones_rows = jnp.ones((2 * SUBLANES, blk), BF16)

    def accumulate(t, slot, state, maxes):
        tops, slab_max = maxes
        new = []
        for hh in range(heads):
            m, l = state[2 * hh:2 * hh + 2]
            m_new = jnp.maximum(m, tops[hh])
            alpha = jnp.exp2(m - m_new)
            l = alpha * l
            acc = alpha * acc_ref[hh]
            for i in range(KV_CHUNK):
                src, row = slab_block(t, i)
                keep = sel_ref[hh, pl.ds(row, 1), :]
                if i == KV_CHUNK - 1:
                    keep = jnp.where(t == 0, 1.0, keep)
                p = jnp.exp2(st_ref[slot, hh, i * blk:(i + 1) * blk, :] - jnp.maximum(m_new, slab_max[hh][i]))
                v_h = vt_ref[0, src][hh * ATT_HEAD_DIM:(hh + 1) * ATT_HEAD_DIM, :]
                pv = _dot(jnp.concatenate([v_h, ones_rows], axis=0), p.astype(BF16))
                l = l + keep * pv[ATT_HEAD_DIM:ATT_HEAD_DIM + 1, :]
                acc = acc + keep * pv[0:ATT_HEAD_DIM, :]
            acc_ref[hh] = acc
            new += [m_new, l]
        return tuple(new)

    state = ()
    for hh in range(heads):
        state += (jnp.full((1, blk), NEG_INF, F32), jnp.zeros((1, blk), F32))
        acc_ref[hh] = jnp.zeros((ATT_HEAD_DIM, blk), F32)
    last = j // KV_CHUNK

    def stage_pair(t, slot, state, maxes):
        nxt = score(t + 1, 1 - slot, False)
        return accumulate(t, slot, state, maxes), nxt

    def body(t, carry):
        return lax.cond(t % 2 == 0, functools.partial(stage_pair, t, 0), functools.partial(stage_pair, t, 1),
                        *carry)

    state, maxes = lax.fori_loop(0, last, body, (state, score(0, 0, True)))
    state = lax.cond(last % 2 == 0, functools.partial(accumulate, last, 0), functools.partial(accumulate, last, 1),
                     state, maxes)

    outs = []
    for hh in range(heads):
        o_h = acc_ref[hh] / state[2 * hh + 1]
        ms = jnp.mean(o_h * o_h, axis=0, keepdims=True)
        outs.append(o_h * lax.rsqrt(ms + NORM_EPS))
    o_ref[0] = jnp.concatenate(outs, axis=0).T * g_ref[...]


def _moba(q, k, vt, g_att):
    b, s, w = q.shape
    blk = MOBA_BLOCK
    nb = s // blk
    gl = MOBA_GROUP_LANES
    groups = w // gl
    heads = gl // ATT_HEAD_DIM
    return pl.pallas_call(
        _moba_kernel,
        grid=(b, groups, nb),
        in_specs=[pl.BlockSpec((1, blk, gl), lambda bi, p, j: (bi, j, p)),
                  pl.BlockSpec((1, s, gl), lambda bi, p, j: (bi, 0, p)),
                  pl.BlockSpec((1, nb, gl, blk), lambda bi, p, j: (bi, 0, p, 0)),
                  pl.BlockSpec((1, gl), lambda bi, p, j: (0, p))],
        out_specs=pl.BlockSpec((1, blk, gl), lambda bi, p, j: (bi, j, p)),
        out_shape=jax.ShapeDtypeStruct((b, s, w), F32),
        scratch_shapes=[pltpu.VMEM((nb, gl), F32),
                        pltpu.VMEM((heads, nb + SUBLANES, blk), F32),
                        pltpu.VMEM((2, heads, KV_CHUNK * blk, blk), F32),
                        pltpu.VMEM((heads, blk, LANES), BF16),
                        pltpu.VMEM((heads, ATT_HEAD_DIM, blk), F32)],
        compiler_params=_params(("arbitrary", "arbitrary", "arbitrary")),
        name="moba",
    )(q, k, vt, g_att)


def _mlstm_kernel(qkm_ref, vm_ref, om_ref, gt_ref, cw_ref, cb_ref, bg_ref, gm_ref, bsel_ref, o_ref,
                  uext_ref, c_ref, n_ref, m_ref):
    L = MLSTM_CHUNK
    dh = MLSTM_HEAD_DIM
    nh = MLSTM_HEADS
    assert L == dh == LANES and 2 * nh == SUBLANES
    c = pl.program_id(1)

    @pl.when(c == 0)
    def _():
        uext_ref[0:SUBLANES, :] = jnp.zeros((SUBLANES, uext_ref.shape[1]), F32)
        c_ref[...] = jnp.zeros(c_ref.shape, F32)
        n_ref[...] = jnp.zeros(n_ref.shape, F32)
        m_ref[...] = jnp.zeros(m_ref.shape, F32)

    for sub in range(qkm_ref.shape[1] // L):
        _mlstm_chunk(slice(sub * L, (sub + 1) * L), qkm_ref, vm_ref, om_ref, gt_ref, cw_ref, cb_ref, bg_ref, gm_ref,
                     bsel_ref, o_ref, uext_ref, c_ref, n_ref, m_ref)


def _mlstm_chunk(rows, qkm_ref, vm_ref, om_ref, gt_ref, cw_ref, cb_ref, bg_ref, gm_ref, bsel_ref, o_ref,
                 uext_ref, c_ref, n_ref, m_ref):
    L = MLSTM_CHUNK
    dh = MLSTM_HEAD_DIM
    nh = MLSTM_HEADS

    u = qkm_ref[0, rows, :]
    uext_ref[SUBLANES:SUBLANES + L, :] = u
    cw = cw_ref[...]
    conv = cb_ref[...] + cw[CONV_WIDTH - 1:CONV_WIDTH, :] * u
    for dlt in range(1, CONV_WIDTH):
        conv = conv + cw[CONV_WIDTH - 1 - dlt:CONV_WIDTH - dlt, :] * uext_ref[SUBLANES - dlt:SUBLANES - dlt + L, :]
    uext_ref[0:SUBLANES, :] = u[L - SUBLANES:L, :]
    act = _silu(conv)

    gates = gt_ref[0, rows, :] + bg_ref[...]
    ig = gates[:, 0:LANES]
    tpos = lax.broadcasted_iota(jnp.int32, (L, L), 0)
    spos = lax.broadcasted_iota(jnp.int32, (L, L), 1)
    causal = spos <= tpos
    tril = jnp.where(causal, 1.0, 0.0).astype(BF16)
    cum = sum(_dot(tril, part) for part in _split3(_log_sigmoid(gates[:, LANES:])))
    a_tot = cum[L - 1:L, :]
    m_prev = m_ref[0:1, :]
    resid = ig - cum
    pmax = resid
    shift = 1
    while shift < L:
        pmax = jnp.maximum(pmax, jnp.where(tpos >= shift, pltpu.roll(pmax, shift, 0), -jnp.inf))
        shift *= 2
    inter = cum + m_prev
    m_t = jnp.maximum(inter, cum + pmax)
    g_loc = a_tot - cum + ig
    m_loc = jnp.max(g_loc, axis=0, keepdims=True)
    w_loc = jnp.exp(g_loc - m_loc)
    head_lane = spos < nh
    packed = jnp.where(head_lane, cum - m_t, 0.0)
    for i, qty in enumerate((inter - m_t, -m_t, g_loc - m_loc), start=1):
        packed = packed + pltpu.roll(jnp.where(head_lane, qty, 0.0), SUBLANES * i, 1)
    per_t_cols = jnp.concatenate(_split3(packed), axis=1)
    per_t_rows = (jnp.where(head_lane, resid, 0.0) + pltpu.roll(jnp.where(head_lane, w_loc, 0.0), SUBLANES, 1)).T

    m_new = jnp.maximum(a_tot + m_prev, m_loc)
    s_prev = jnp.exp(a_tot + m_prev - m_new)
    s_loc = jnp.exp(m_loc - m_new)
    m_ref[0:1, :] = m_new

    vm = vm_ref[0, rows, :]
    om = om_ref[0, rows, :]
    heads = lambda a, off=0: jnp.stack([a[:, off + h * dh:off + (h + 1) * dh] for h in range(nh)])
    bdot = lambda a, b, ca, cb: lax.dot_general(a, b, (((ca,), (cb,)), ((0,), (0,))), preferred_element_type=F32)
    q4 = heads(act)
    k4 = heads(act, nh * dh) * (dh ** -0.5)
    v4 = heads(vm)
    qb, kb, vb = q4.astype(BF16), k4.astype(BF16), v4.astype(BF16)
    c_prev = c_ref[...]
    n_all = n_ref[...]
    n_prev = jnp.stack([n_all[h:h + 1, :] for h in range(nh)])
    bcast = _dot(per_t_cols, bsel_ref[...])
    qty = lambda i: jnp.stack([bcast[:, (4 * h + i) * dh:(4 * h + i + 1) * dh] for h in range(nh)])
    decay_b = qty(0)
    w_inter_b, floor_b, w_loc_b = jnp.exp(qty(1)), jnp.exp(qty(2)), jnp.exp(qty(3))
    resid_rows = jnp.stack([per_t_rows[h:h + 1, :] for h in range(nh)])
    w_rows = jnp.stack([per_t_rows[SUBLANES + h:SUBLANES + h + 1, :] for h in range(nh)])

    s_qk = bdot(qb, kb, 2, 2) * jnp.where(causal[None], jnp.exp(decay_b + resid_rows), 0.0)
    intra = bdot(s_qk.astype(BF16), jnp.concatenate([vb, jnp.ones((nh, L, dh), BF16)], axis=2), 2, 1)
    state_rows = jnp.concatenate([c_prev, jnp.broadcast_to(n_prev, (nh, dh, dh))], axis=1).astype(BF16)
    carried = bdot(qb, state_rows, 2, 2)
    num = intra[:, :, :dh] + w_inter_b * carried[:, :, :dh]
    den = intra[:, :, dh:] + w_inter_b * carried[:, :, dh:]
    hout = num / jnp.maximum(jnp.abs(den), floor_b)

    sq = (hout * hout).astype(BF16).reshape(nh * L, dh)
    mean_sq = (_dot(sq, jnp.ones((dh, dh), BF16)) * (1.0 / dh)).reshape(nh, L, dh)
    gm = jnp.stack([gm_ref[:, h * dh:(h + 1) * dh] for h in range(nh)])
    out = hout * lax.rsqrt(mean_sq + NORM_EPS) * gm * jax.nn.sigmoid(heads(om))
    for h in range(nh):
        o_ref[0, rows, h * dh:(h + 1) * dh] = out[h]

    vw = (v4 * w_loc_b).astype(BF16)
    for h in range(nh):
        c_loc = _dot_tn(vw[h], kb[h])
        n_loc = _dot(jnp.broadcast_to(w_rows[h], (SUBLANES, L)).astype(BF16), kb[h])[0:1, :]
        c_ref[h] = s_prev[:, h:h + 1] * c_prev[h] + s_loc[:, h:h + 1] * c_loc
        n_ref[h:h + 1, :] = s_prev[:, h:h + 1] * n_all[h:h + 1, :] + s_loc[:, h:h + 1] * n_loc


def _mlstm(qkm, vm, om, gates, conv_w, conv_b, b_gate, g_m):
    b, s, _ = qkm.shape
    L = MLSTM_CHUNK
    step_rows = MLSTM_CHUNKS_PER_STEP * L
    nc = s // step_rows
    lane_pad = ((0, 0), (0, LANES - MLSTM_HEADS))
    bg = b_gate.reshape(1, -1)
    bg = jnp.concatenate([jnp.pad(bg[:, :MLSTM_HEADS], lane_pad), jnp.pad(bg[:, MLSTM_HEADS:], lane_pad)], axis=1)
    tok = lambda w: pl.BlockSpec((1, step_rows, w), lambda bi, c: (bi, c, 0))
    full = lambda a: pl.BlockSpec(a.shape, lambda bi, c: (0,) * a.ndim)
    cb = conv_b.reshape(1, -1)
    gm = g_m.reshape(1, -1)
    n_qty = 4
    row = jnp.arange(3 * LANES, dtype=jnp.int32) % LANES
    col_blk = jnp.arange(MLSTM_HEADS * n_qty * MLSTM_HEAD_DIM, dtype=jnp.int32) // MLSTM_HEAD_DIM
    bsel = (((row // SUBLANES)[:, None] == (col_blk % n_qty)[None, :])
            & ((row % SUBLANES)[:, None] == (col_blk // n_qty)[None, :])).astype(BF16)
    return pl.pallas_call(
        _mlstm_kernel,
        grid=(b, nc),
        in_specs=[tok(2 * D_MLSTM), tok(D_MLSTM), tok(D_MLSTM), tok(GATE_LANES),
                  full(conv_w), full(cb), full(bg), full(gm), full(bsel)],
        out_specs=tok(D_MLSTM),
        out_shape=jax.ShapeDtypeStruct((b, s, D_MLSTM), F32),
        scratch_shapes=[pltpu.VMEM((SUBLANES + L, 2 * D_MLSTM), F32),
                        pltpu.VMEM((MLSTM_HEADS, MLSTM_HEAD_DIM, MLSTM_HEAD_DIM), F32),
                        pltpu.VMEM((SUBLANES, MLSTM_HEAD_DIM), F32),
                        pltpu.VMEM((SUBLANES, LANES), F32)],
        compiler_params=_params(("arbitrary", "arbitrary")),
        name="mlstm",
    )(qkm, vm, om, gates, conv_w, cb, bg, gm, bsel)


META_E0, META_E1, META_W0, META_W1, META_L0, META_L1 = range(6)
STAT_LEN, STAT_START = 0, 1


def _outproj_kernel(attn_ref, hm_ref, x_ref, mod_ref, g_ref, wout_ref, wr_hi_ref, wr_lo_ref, br_ref,
                    x1_ref, h2_ref, meta_ref, stat_ref):
    d = x_ref.shape[1]
    tm = x_ref.shape[0]

    mod = mod_ref[0]
    y = (_dot(attn_ref[...].astype(BF16), wout_ref[0:D_ATT, :])
         + _dot(hm_ref[...].astype(BF16), wout_ref[D_ATT:, :]))
    x1 = x_ref[...] + mod[:, 2 * d:3 * d] * y
    x1_ref[...] = x1
    h2 = _rms_modulate(x1, g_ref[...], mod[:, 3 * d:4 * d], mod[:, 4 * d:5 * d])
    h2_ref[...] = h2.astype(BF16)

    h_hi = h2.astype(BF16)
    h_lo = (h2 - h_hi.astype(F32)).astype(BF16)
    logit = (_dot_nt(wr_hi_ref[...], h_hi) + _dot_nt(wr_hi_ref[...], h_lo) + _dot_nt(wr_lo_ref[...], h_hi)
             + br_ref[...])
    rid = lax.broadcasted_iota(jnp.int32, logit.shape, 0)
    big = jnp.int32(ROUTER_LANES)

    is_g = (rid >= GROUP_LANE0) & (rid < GROUP_LANE0 + N_GROUPS)
    gmax = jnp.max(jnp.where(is_g, logit, -jnp.inf), axis=0, keepdims=True)
    gsum = jnp.sum(jnp.where(is_g, jnp.exp(logit - gmax), 0.0), axis=0, keepdims=True)
    g_w = 1.0 / gsum
    g_idx = jnp.min(jnp.where(is_g & (logit == gmax), rid, big), axis=0, keepdims=True) - GROUP_LANE0

    in_grp = (rid < N_EXPERTS) & ((rid // EXPERTS_PER_GROUP) == g_idx)
    emax = jnp.max(jnp.where(in_grp, logit, -jnp.inf), axis=0, keepdims=True)
    esum = jnp.sum(jnp.where(in_grp, jnp.exp(logit - emax), 0.0), axis=0, keepdims=True)
    e0 = jnp.min(jnp.where(in_grp & (logit == emax), rid, big), axis=0, keepdims=True)
    rest = in_grp & (rid != e0)
    e2max = jnp.max(jnp.where(rest, logit, -jnp.inf), axis=0, keepdims=True)
    e1 = jnp.min(jnp.where(rest & (logit == e2max), rid, big), axis=0, keepdims=True)
    p0 = 1.0 / esum
    p1 = jnp.exp(e2max - emax) / esum
    w0 = g_w * p0 / (p0 + p1)
    w1 = g_w * p1 / (p0 + p1)

    memb = jnp.where((rid == e0) | (rid == e1), 1.0, 0.0).astype(BF16)
    tpos = lax.broadcasted_iota(jnp.int32, (tm, tm), 0)
    spos = lax.broadcasted_iota(jnp.int32, (tm, tm), 1)
    earlier = _dot(memb, jnp.where(tpos < spos, 1.0, 0.0).astype(BF16))
    count = _dot(memb, jnp.ones((tm, tm), BF16)).astype(jnp.int32)
    run_len = (((count + (RUN_ALIGN - 1)) // RUN_ALIGN) * RUN_ALIGN).astype(F32)
    epos = lax.broadcasted_iota(jnp.int32, (ROUTER_LANES, ROUTER_LANES), 0)
    fpos = lax.broadcasted_iota(jnp.int32, (ROUTER_LANES, ROUTER_LANES), 1)
    run_start = _dot(jnp.where(fpos < epos, 1.0, 0.0).astype(BF16), run_len.astype(BF16))
    row = run_start + earlier
    l0 = jnp.sum(jnp.where(rid == e0, row, 0.0), axis=0, keepdims=True)
    l1 = jnp.sum(jnp.where(rid == e1, row, 0.0), axis=0, keepdims=True)

    meta_t = jnp.zeros(logit.shape, F32)
    for slot, val in ((META_E0, e0.astype(F32)), (META_E1, e1.astype(F32)), (META_W0, w0), (META_W1, w1),
                      (META_L0, l0), (META_L1, l1)):
        meta_t = jnp.where(rid == slot, val, meta_t)
    meta_ref[...] = meta_t.T
    srow = lax.broadcasted_iota(jnp.int32, (SUBLANES, ROUTER_LANES), 0)
    len_row = run_len[:, 0:ROUTER_LANES].T[0:SUBLANES, :]
    start_row = run_start[:, 0:ROUTER_LANES].T[0:SUBLANES, :]
    stat_ref[0] = jnp.where(srow == STAT_LEN, len_row, jnp.where(srow == STAT_START, start_row, 0.0))


def _outproj(attn, hm, x2d, mod3, norm_g, w_out, w_rg, b_rg, w_re, b_re, seq):
    t, d = x2d.shape
    tm = MOE_TILE
    per_batch = seq // tm
    wr = jnp.pad(jnp.concatenate([w_re, w_rg], axis=1), ((0, 0), (0, ROUTER_LANES - N_EXPERTS - N_GROUPS)))
    br = jnp.pad(jnp.concatenate([b_re, b_rg]).reshape(1, -1), ((0, 0), (0, ROUTER_LANES - N_EXPERTS - N_GROUPS)))
    wr = wr.T
    br = jnp.broadcast_to(br.T, (ROUTER_LANES, tm))
    wr_hi = wr.astype(BF16)
    wr_lo = (wr - wr_hi.astype(F32)).astype(BF16)
    wout = w_out.astype(BF16)
    tok = lambda w: pl.BlockSpec((tm, w), lambda i: (i, 0))
    full = lambda a: pl.BlockSpec(a.shape, lambda i: (0,) * a.ndim)
    return pl.pallas_call(
        _outproj_kernel,
        grid=(t // tm,),
        in_specs=[tok(D_ATT), tok(D_MLSTM), tok(d),
                  pl.BlockSpec((1, 1, mod3.shape[2]), lambda i: (i // per_batch, 0, 0)),
                  full(norm_g), full(wout), full(wr_hi), full(wr_lo), full(br)],
        out_specs=[tok(d), tok(d), tok(ROUTER_LANES),
                   pl.BlockSpec((1, SUBLANES, ROUTER_LANES), lambda i: (i, 0, 0))],
        out_shape=[jax.ShapeDtypeStruct((t, d), F32), jax.ShapeDtypeStruct((t, d), BF16),
                   jax.ShapeDtypeStruct((t, ROUTER_LANES), F32),
                   jax.ShapeDtypeStruct((t // tm, SUBLANES, ROUTER_LANES), F32)],
        compiler_params=_params(("arbitrary",)),
        name="outproj",
    )(attn, hm, x2d, mod3, norm_g, wout, wr_hi, wr_lo, br)


def _for_each_piece(length, pieces, fn):
    pos = jnp.int32(0)
    for size in pieces:
        take = length & size

        @pl.when(take != 0)
        def _(pos=pos, size=size):
            fn(pl.multiple_of(pos, RUN_ALIGN), size)

        pos = pos + take


def _for_each_chunk(tile, rows_ref, cdst_ref, fn):
    def body(c, carry):
        fn(pl.multiple_of(c * RUN_ALIGN, RUN_ALIGN),
           pl.multiple_of(cdst_ref[tile * CHUNKS_PER_TILE + c], RUN_ALIGN))
        return carry

    lax.fori_loop(0, rows_ref[tile] // RUN_ALIGN, body, 0)


def _dispatch_kernel(cdst_ref, rows_ref, tail_ref, tlen_ref, nu_ref, meta_ref, h2_ref, xs_ref,
                     perm_ref, zero_ref, sem, zsem):
    i = pl.program_id(0)
    rb = MOE_BLOCK_ROWS
    n_blocks = xs_ref.shape[0] // rb

    @pl.when(i == 0)
    def _():
        zero_ref[...] = jnp.zeros(zero_ref.shape, zero_ref.dtype)

        def zero_copy(row, size):
            return pltpu.make_async_copy(zero_ref.at[pl.ds(0, size)], xs_ref.at[pl.ds(row, size)], zsem)

        def fill(op):
            def per_expert(e, carry):
                start = tail_ref[e]
                _for_each_piece(tlen_ref[e], TAIL_PIECES,
                                lambda pos, size: op(zero_copy(pl.multiple_of(start + pos, RUN_ALIGN), size)))
                return carry

            lax.fori_loop(0, N_EXPERTS, per_expert, 0)

            def per_block(blk, carry):
                for part in range(rb // ZERO_ROWS):
                    op(zero_copy(pl.multiple_of(blk * rb + part * ZERO_ROWS, ZERO_ROWS), ZERO_ROWS))
                return carry

            lax.fori_loop(nu_ref[0], n_blocks, per_block, 0)

        fill(lambda cp: cp.start())
        fill(lambda cp: cp.wait())

    meta_t = meta_ref[...].T
    l0 = meta_t[META_L0:META_L0 + 1, :].astype(jnp.int32)
    l1 = meta_t[META_L1:META_L1 + 1, :].astype(jnp.int32)
    rpos = lax.broadcasted_iota(jnp.int32, (perm_ref.shape[1], meta_ref.shape[0]), 0)
    onehot = jnp.where((rpos == l0) | (rpos == l1), 1.0, 0.0).astype(BF16)
    slot = i % 2
    perm_ref[slot] = _dot(onehot, h2_ref[...]).astype(BF16)

    def run_copy(s, loc, glob, size):
        return pltpu.make_async_copy(perm_ref.at[s, pl.ds(loc, size)], xs_ref.at[pl.ds(glob, size)], sem.at[s])

    def wait_tile(tile, s):
        _for_each_piece(rows_ref[tile], TILE_PIECES, lambda pos, size: run_copy(s, 0, 0, size).wait())

    @pl.when(i > 0)
    def _():
        wait_tile(i - 1, 1 - slot)

    _for_each_chunk(i, rows_ref, cdst_ref, lambda loc, glob: run_copy(slot, loc, glob, RUN_ALIGN).start())

    @pl.when(i == pl.num_programs(0) - 1)
    def _():
        wait_tile(i, slot)


def _dispatch(plan, meta, h2):
    t, d = h2.shape
    tm = MOE_TILE
    grid_spec = pltpu.PrefetchScalarGridSpec(
        num_scalar_prefetch=5,
        grid=(t // tm,),
        in_specs=[pl.BlockSpec((tm, ROUTER_LANES), lambda i, *_: (i, 0)),
                  pl.BlockSpec((tm, d), lambda i, *_: (i, 0))],
        out_specs=pl.BlockSpec(memory_space=pl.ANY),
        scratch_shapes=[pltpu.VMEM((2, PERM_ROWS, d), BF16), pltpu.VMEM((ZERO_ROWS, d), BF16),
                        pltpu.SemaphoreType.DMA((2,)), pltpu.SemaphoreType.DMA(())],
    )
    return pl.pallas_call(
        _dispatch_kernel,
        grid_spec=grid_spec,
        out_shape=jax.ShapeDtypeStruct((plan["n_rows"], d), BF16),
        compiler_params=_params(("arbitrary",)),
        name="dispatch",
    )(plan["chunk_dst"], plan["tile_rows"], plan["tail_start"], plan["tail_len"], plan["n_used"], meta, h2)


def _expert_kernel(be_ref, nu_ref, xs_ref, w1_ref, w3_ref, w2_ref, y_ref, w1b_ref, w3b_ref, w2b_ref):
    i = pl.program_id(0)
    used = i < nu_ref[0]

    @pl.when(used & ((i == 0) | (be_ref[i] != be_ref[jnp.maximum(i, 1) - 1])))
    def _():
        w1b_ref[...] = w1_ref[0].astype(BF16)
        w3b_ref[...] = w3_ref[0].astype(BF16)
        w2b_ref[...] = w2_ref[0].astype(BF16)

    @pl.when(used)
    def _():
        x = xs_ref[...]
        a = _dot(x, w1b_ref[...])
        b = _dot(x, w3b_ref[...])
        y_ref[...] = _dot((_silu(a) * b).astype(BF16), w2b_ref[...]).astype(y_ref.dtype)

    @pl.when(jnp.logical_not(used))
    def _():
        y_ref[...] = jnp.zeros(y_ref.shape, y_ref.dtype)


def _experts(block_e, n_used, xs, w1, w3, w2):
    n_rows, d = xs.shape
    rb = MOE_BLOCK_ROWS
    de = w1.shape[2]
    used = lambda i, be, nu: jnp.minimum(i, nu[0] - 1)
    grid_spec = pltpu.PrefetchScalarGridSpec(
        num_scalar_prefetch=2,
        grid=(n_rows // rb,),
        in_specs=[pl.BlockSpec((rb, d), lambda i, be, nu: (used(i, be, nu), 0)),
                  pl.BlockSpec((1, d, de), lambda i, be, nu: (be[used(i, be, nu)], 0, 0)),
                  pl.BlockSpec((1, d, de), lambda i, be, nu: (be[used(i, be, nu)], 0, 0)),
                  pl.BlockSpec((1, de, d), lambda i, be, nu: (be[used(i, be, nu)], 0, 0))],
        out_specs=pl.BlockSpec((rb, d), lambda i, be, nu: (i, 0)),
        scratch_shapes=[pltpu.VMEM((d, de), BF16), pltpu.VMEM((d, de), BF16), pltpu.VMEM((de, d), BF16)],
    )
    return pl.pallas_call(
        _expert_kernel,
        grid_spec=grid_spec,
        out_shape=jax.ShapeDtypeStruct((n_rows, d), BF16),
        compiler_params=_params(("arbitrary",)),
        name="experts",
    )(block_e, n_used, xs, w1, w3, w2)


def _combine_kernel(final_norm, cdst_ref, rows_ref, x1_ref, meta_ref, mod_ref, g_ref, y_ref, o_ref, ybuf_ref, sem):
    tc, d = x1_ref.shape
    i = pl.program_id(0)

    slot = i % 2

    def run_copy(s, loc, glob, size):
        return pltpu.make_async_copy(y_ref.at[pl.ds(glob, size)], ybuf_ref.at[s, pl.ds(loc, size)], sem.at[s])

    def start_tile(tile, s):
        _for_each_chunk(tile, rows_ref, cdst_ref, lambda loc, glob: run_copy(s, loc, glob, RUN_ALIGN).start())

    @pl.when(i == 0)
    def _():
        ybuf_ref[...] = jnp.zeros(ybuf_ref.shape, ybuf_ref.dtype)
        start_tile(0, 0)

    @pl.when(i + 1 < pl.num_programs(0))
    def _():
        start_tile(i + 1, 1 - slot)

    _for_each_piece(rows_ref[i], TILE_PIECES, lambda pos, size: run_copy(slot, 0, 0, size).wait())

    meta = meta_ref[...]
    yb = ybuf_ref[slot]
    rpos = lax.broadcasted_iota(jnp.int32, (tc, ybuf_ref.shape[1]), 1)
    sel = jnp.zeros(rpos.shape, F32)
    for l_lane, w_lane in ((META_L0, META_W0), (META_L1, META_W1)):
        sel = jnp.where(rpos == meta[:, l_lane:l_lane + 1].astype(jnp.int32), meta[:, w_lane:w_lane + 1], sel)
    moe = _dot(sel.astype(BF16), yb)
    x2 = x1_ref[...] + mod_ref[0][:, 5 * d:6 * d] * moe
    if final_norm:
        x2 = x2 * lax.rsqrt(jnp.mean(x2 * x2, axis=-1, keepdims=True) + NORM_EPS) * g_ref[...]
    o_ref[...] = x2


def _combine(plan, x1, meta, mod3, norm_f_g, y, seq, final_norm):
    t, d = x1.shape
    tc = MOE_TILE
    per_batch = seq // tc
    gf = norm_f_g.reshape(1, d)
    grid_spec = pltpu.PrefetchScalarGridSpec(
        num_scalar_prefetch=2,
        grid=(t // tc,),
        in_specs=[pl.BlockSpec((tc, d), lambda i, *_: (i, 0)),
                  pl.BlockSpec((tc, ROUTER_LANES), lambda i, *_: (i, 0)),
                  pl.BlockSpec((1, 1, mod3.shape[2]), lambda i, *_: (i // per_batch, 0, 0)),
                  pl.BlockSpec(gf.shape, lambda i, *_: (0, 0)),
                  pl.BlockSpec(memory_space=pl.ANY)],
        out_specs=pl.BlockSpec((tc, d), lambda i, *_: (i, 0)),
        scratch_shapes=[pltpu.VMEM((2, PERM_ROWS, d), BF16), pltpu.SemaphoreType.DMA((2,))],
    )
    return pl.pallas_call(
        functools.partial(_combine_kernel, final_norm),
        grid_spec=grid_spec,
        out_shape=jax.ShapeDtypeStruct((t, d), F32),
        compiler_params=_params(("arbitrary",)),
        name="combine",
    )(plan["chunk_dst"], plan["tile_rows"], x1, meta, mod3, gf, y)


def _routing_plan(stats, n_tokens):
    rb = MOE_BLOCK_ROWS
    run_len = stats[:, STAT_LEN, :N_EXPERTS].astype(jnp.int32)
    run_loc = stats[:, STAT_START, :N_EXPERTS].astype(jnp.int32)
    n_tiles = run_len.shape[0]
    total = jnp.sum(run_len, axis=0)
    padded = ((total + rb - 1) // rb) * rb
    pad_end = jnp.cumsum(padded)
    pad_start = pad_end - padded
    run_glob = pad_start[None, :] + jnp.cumsum(run_len, axis=0) - run_len
    n_rows = n_tokens * TOP_K_EXPERTS + n_tiles * N_EXPERTS * (RUN_ALIGN - 1) + N_EXPERTS * (rb - 1)
    n_rows = ((n_rows + rb - 1) // rb) * rb
    block_row = jnp.arange(n_rows // rb, dtype=jnp.int32) * rb
    block_e = jnp.minimum(jnp.sum((pad_end[None, :] <= block_row[:, None]).astype(jnp.int32), axis=1),
                          N_EXPERTS - 1)
    local_row = jnp.arange(CHUNKS_PER_TILE, dtype=jnp.int32) * RUN_ALIGN
    chunk_e = jnp.minimum(jnp.sum(((run_loc + run_len)[:, None, :] <= local_row[None, :, None]).astype(jnp.int32),
                                  axis=-1), N_EXPERTS - 1)
    shift = jnp.sum(jnp.where(chunk_e[..., None] == jnp.arange(N_EXPERTS, dtype=jnp.int32),
                              (run_glob - run_loc)[:, None, :], 0), axis=-1)
    chunk_dst = shift + local_row[None, :]
    return dict(chunk_dst=chunk_dst.reshape(-1), tile_rows=jnp.sum(run_len, axis=1),
                tail_start=pad_start + total, tail_len=padded - total,
                n_used=(pad_end[-1:] // rb).astype(jnp.int32), block_e=block_e.astype(jnp.int32), n_rows=n_rows)


def kernel(x, c, positions, w_ada, b_ada, norm1_g, w_in, b_gate, conv_w, conv_b, attn_out_g, mlstm_out_g,
           w_out, norm2_g, w_rg, b_rg, w_re, b_re, w1, w3, w2, norm_f_g):
    b, s, d = x.shape
    depth = w_ada.shape[0]
    assert d == D_MODEL and s % MOE_TILE == 0 and s % INPROJ_TILE == 0 and s % (KV_CHUNK * MOBA_BLOCK) == 0
    assert s % (MLSTM_CHUNKS_PER_STEP * MLSTM_CHUNK) == 0
    x2d = x.reshape(b * s, d)
    for l in range(depth):
        mod3 = _mod(c, w_ada[l], b_ada[l]).reshape(b, 1, 6 * d)
        q, k, vt, qkm, vm, om, gates = _inproj(x2d.reshape(b, s, d), positions, mod3,
                                              norm1_g[l].reshape(1, d), w_in[l])
        attn = _moba(q, k, vt, attn_out_g[l].reshape(1, D_ATT))
        hm = _mlstm(qkm, vm, om, gates, conv_w[l], conv_b[l], b_gate[l], mlstm_out_g[l])
        x1, h2, meta, stats = _outproj(attn.reshape(b * s, D_ATT), hm.reshape(b * s, D_MLSTM), x2d, mod3,
                                       norm2_g[l].reshape(1, d), w_out[l], w_rg[l], b_rg[l], w_re[l],
                                       b_re[l], s)
        plan = _routing_plan(stats, b * s)
        xs = _dispatch(plan, meta, h2)
        y = _experts(plan["block_e"], plan["n_used"], xs, w1[l], w3[l], w2[l])
        x2d = _combine(plan, x1, meta, mod3, norm_f_g, y, s, final_norm=(l == depth - 1))
    return x2d.reshape(b, s, d)
```

```python
import functools

import jax
import jax.numpy as jnp
from jax import lax
from jax.experimental import pallas as pl
from jax.experimental.pallas import tpu as pltpu

F32 = jnp.float32
BF16 = jnp.bfloat16

D_MODEL = 1024
D_ATT = 512
ATT_HEADS = 8
ATT_HEAD_DIM = 64
D_MLSTM = 512
MLSTM_HEADS = 4
MLSTM_HEAD_DIM = 128
MOBA_BLOCK = 256
MOBA_TOPK = 3
ROPE_THETA = 10000.0
MLSTM_CHUNK = 128
MLSTM_CHUNKS_PER_STEP = 8
CONV_WIDTH = 4
N_GROUPS = 4
EXPERTS_PER_GROUP = 8
N_EXPERTS = N_GROUPS * EXPERTS_PER_GROUP
TOP_K_EXPERTS = 2
D_EXPERT = 512
MOE_BLOCK_ROWS = 512
NORM_EPS = 1e-6
NEG_INF = -1e30
LOG2_E = 1.4426950408889634
KV_CHUNK = 2
MOBA_GROUP_LANES = 512

LANES = 128
SUBLANES = 8
VMEM_LIMIT_BYTES = 56 * 1024 * 1024

INPROJ_TILE = 512
MOE_TILE = 512
RUN_ALIGN = 2 * SUBLANES
PERM_ROWS = TOP_K_EXPERTS * MOE_TILE + N_EXPERTS * RUN_ALIGN
CHUNKS_PER_TILE = PERM_ROWS // RUN_ALIGN
TAIL_PIECES = tuple(RUN_ALIGN << p for p in reversed(range(5)))
assert TAIL_PIECES[0] * 2 == MOE_BLOCK_ROWS
TILE_PIECES = tuple(RUN_ALIGN << p for p in reversed(range(7)))
ZERO_ROWS = TAIL_PIECES[0]
GATE_LANES = 2 * LANES
ROUTER_LANES = LANES
GROUP_LANE0 = N_EXPERTS


def _dot(a, b):
    return jnp.dot(a, b, preferred_element_type=F32)


def _dot_nt(a, b):
    return lax.dot_general(a, b, (((1,), (1,)), ((), ())), preferred_element_type=F32)


def _dot_tn(a, b):
    return lax.dot_general(a, b, (((0,), (0,)), ((), ())), preferred_element_type=F32)


def _split3(x):
    a = x.astype(BF16)
    r = x - a.astype(F32)
    b = r.astype(BF16)
    c = (r - b.astype(F32)).astype(BF16)
    return a, b, c


def _silu(x):
    return x * jax.nn.sigmoid(x)


def _log_sigmoid(x):
    return jnp.minimum(x, 0.0) - jnp.log1p(jnp.exp(-jnp.abs(x)))


def _params(semantics, vmem=VMEM_LIMIT_BYTES):
    return pltpu.CompilerParams(dimension_semantics=semantics, vmem_limit_bytes=vmem)


def _mod_kernel(c_ref, w_ref, b_ref, o_ref):
    sc = _silu(c_ref[...])
    o_ref[...] = jnp.dot(sc, w_ref[...], precision=lax.Precision.HIGHEST,
                         preferred_element_type=F32) + b_ref[...]


def _mod(c, w_ada, b_ada):
    b, d = c.shape
    n = w_ada.shape[1]
    tn = D_MODEL
    return pl.pallas_call(
        _mod_kernel,
        grid=(n // tn,),
        in_specs=[pl.BlockSpec((b, d), lambda i: (0, 0)),
                  pl.BlockSpec((d, tn), lambda i: (0, i)),
                  pl.BlockSpec((1, tn), lambda i: (0, i))],
        out_specs=pl.BlockSpec((b, tn), lambda i: (0, i)),
        out_shape=jax.ShapeDtypeStruct((b, n), F32),
        compiler_params=_params(("arbitrary",)),
        name="mod",
    )(c, w_ada, b_ada.reshape(1, n))


def _rms_modulate(x, g, shift, scale):
    y = x * lax.rsqrt(jnp.mean(x * x, axis=-1, keepdims=True) + NORM_EPS)
    return (y * g) * (1.0 + scale) + shift


def _rope(t, cos, sin, first_half):
    outs = []
    for c in range(t.shape[1] // LANES):
        tc = t[:, c * LANES:(c + 1) * LANES]
        rot = jnp.where(first_half, -pltpu.roll(tc, LANES - ATT_HEAD_DIM // 2, 1),
                        pltpu.roll(tc, ATT_HEAD_DIM // 2, 1))
        outs.append(tc * cos + rot * sin)
    return jnp.concatenate(outs, axis=1)


def _inproj_kernel(x_ref, pos_ref, mod_ref, g_ref, invf_ref, wq_ref, wk_ref, wvt_ref, wqk_ref,
                   wv_ref, wo_ref, wg_ref, q_ref, k_ref, vt_ref, qkm_ref, vm_ref, om_ref, gt_ref):
    d = x_ref.shape[2]
    x = x_ref[0]
    mod = mod_ref[0]
    h = _rms_modulate(x, g_ref[...], mod[:, 0:d], mod[:, d:2 * d])
    hb = h.astype(BF16)

    ang = pos_ref[0].astype(F32) * invf_ref[...]
    cos = jnp.cos(ang)
    sin = jnp.sin(ang)
    lane = lax.broadcasted_iota(jnp.int32, cos.shape, 1)
    first_half = (lane & (ATT_HEAD_DIM // 2)) == 0

    q = _rope(_dot(hb, wq_ref[...]), cos, sin, first_half)
    q_ref[0] = (q * (ATT_HEAD_DIM ** -0.5 * LOG2_E)).astype(BF16)
    k_ref[0] = _rope(_dot(hb, wk_ref[...]), cos, sin, first_half).astype(BF16)
    vt = _dot_nt(wvt_ref[...], hb).astype(BF16)
    for blk_i in range(vt_ref.shape[1]):
        vt_ref[0, blk_i] = vt[:, blk_i * MOBA_BLOCK:(blk_i + 1) * MOBA_BLOCK]
    qkm_ref[0] = _dot(hb, wqk_ref[...])
    vm_ref[0] = _dot(hb, wv_ref[...])
    om_ref[0] = _dot(hb, wo_ref[...])
    gt_ref[0] = _dot(hb, wg_ref[...])


def _inproj(x, positions, mod3, norm_g, w_in):
    b, s, d = x.shape
    tm = INPROJ_TILE
    per_tile = tm // MOBA_BLOCK
    nb = s // MOBA_BLOCK
    o = [0, D_ATT, 2 * D_ATT, 3 * D_ATT, 3 * D_ATT + 2 * D_MLSTM, 3 * D_ATT + 3 * D_MLSTM,
         3 * D_ATT + 4 * D_MLSTM, 3 * D_ATT + 4 * D_MLSTM + 2 * MLSTM_HEADS]
    wb = w_in.astype(BF16)
    wq, wk, wv_a, wqk, wv, wo, wg = (wb[:, o[i]:o[i + 1]] for i in range(7))
    wvt = wv_a.T
    lane_pad = ((0, 0), (0, LANES - MLSTM_HEADS))
    wg = jnp.concatenate([jnp.pad(wg[:, :MLSTM_HEADS], lane_pad), jnp.pad(wg[:, MLSTM_HEADS:], lane_pad)], axis=1)
    half = ATT_HEAD_DIM // 2
    inv_freq = ROPE_THETA ** (-jnp.arange(half, dtype=F32) / half)
    invf = jnp.tile(inv_freq, LANES // half).reshape(1, LANES)

    full = lambda a: pl.BlockSpec(a.shape, lambda bi, i: (0,) * a.ndim)
    tok = lambda w: pl.BlockSpec((1, tm, w), lambda bi, i: (bi, i, 0))
    out_shape = [jax.ShapeDtypeStruct((b, s, D_ATT), BF16),
                 jax.ShapeDtypeStruct((b, s, D_ATT), BF16),
                 jax.ShapeDtypeStruct((b, nb, D_ATT, MOBA_BLOCK), BF16),
                 jax.ShapeDtypeStruct((b, s, 2 * D_MLSTM), F32),
                 jax.ShapeDtypeStruct((b, s, D_MLSTM), F32),
                 jax.ShapeDtypeStruct((b, s, D_MLSTM), F32),
                 jax.ShapeDtypeStruct((b, s, GATE_LANES), F32)]
    return pl.pallas_call(
        _inproj_kernel,
        grid=(b, s // tm),
        in_specs=[tok(d), tok(1),
                  pl.BlockSpec((1, 1, mod3.shape[2]), lambda bi, i: (bi, 0, 0)),
                  full(norm_g), full(invf), full(wq), full(wk), full(wvt), full(wqk), full(wv),
                  full(wo), full(wg)],
        out_specs=[tok(D_ATT), tok(D_ATT),
                   pl.BlockSpec((1, per_tile, D_ATT, MOBA_BLOCK), lambda bi, i: (bi, i, 0, 0)),
                   tok(2 * D_MLSTM), tok(D_MLSTM), tok(D_MLSTM), tok(GATE_LANES)],
        out_shape=out_shape,
        compiler_params=_params(("arbitrary", "arbitrary")),
        name="inproj",
    )(x, positions.reshape(b, s, 1), mod3, norm_g, invf, wq, wk, wvt, wqk, wv, wo, wg)


def _moba_kernel(q_ref, k_ref, vt_ref, g_ref, o_ref, kmean_ref, sel_ref, st_ref, qh_ref, acc_ref):
    blk = MOBA_BLOCK
    nb = k_ref.shape[1] // blk
    j = pl.program_id(2)
    heads = q_ref.shape[2] // ATT_HEAD_DIM
    hpl = LANES // ATT_HEAD_DIM

    @pl.when(j == 0)
    def _():
        for n in range(nb):
            kb = k_ref[0, n * blk:(n + 1) * blk, :].astype(F32)
            kmean_ref[n:n + 1, :] = jnp.mean(kb, axis=0, keepdims=True)

    lane = lax.broadcasted_iota(jnp.int32, (blk, LANES), 1)
    blk_id = lax.broadcasted_iota(jnp.int32, (nb, blk), 0)
    km_hi = kmean_ref[...].astype(BF16)
    km_lo = (kmean_ref[...] - km_hi.astype(F32)).astype(BF16)
    past = blk_id < j
    tile = lambda a, hh: a[:, (hh // hpl) * LANES:(hh // hpl + 1) * LANES]

    gates = []
    for hh in range(heads):
        q = tile(q_ref[0], hh)
        in_head = (lane >= (hh % hpl) * ATT_HEAD_DIM) & (lane < (hh % hpl + 1) * ATT_HEAD_DIM)
        qh_ref[hh] = jnp.where(in_head, q, jnp.zeros_like(q))
        gates.append(_dot_nt(tile(km_hi, hh), qh_ref[hh]) + _dot_nt(tile(km_lo, hh), qh_ref[hh]))
    g = jnp.where(past[None], jnp.stack(gates), NEG_INF)
    sel = jnp.zeros(g.shape, F32)
    for _ in range(min(MOBA_TOPK, nb)):
        top = jnp.max(g, axis=1, keepdims=True)
        idx = jnp.min(jnp.where(g == top, blk_id[None], nb), axis=1, keepdims=True)
        pick = blk_id[None] == idx
        sel = jnp.where(pick, 1.0, sel)
        g = jnp.where(pick, -jnp.inf, g)
    sel_ref[:, 0:nb, :] = jnp.where((sel > 0.0) & past[None], 1.0, 0.0)
    sel_ref[:, nb:nb + SUBLANES, :] = jnp.zeros((heads, SUBLANES, blk), F32)

    kpos = lax.broadcasted_iota(jnp.int32, (blk, blk), 0)
    qpos = lax.broadcasted_iota(jnp.int32, (blk, blk), 1)
    causal_bias = jnp.where(kpos <= qpos, 0.0, NEG_INF)

    def slab_block(t, i):
        b_i = j - KV_CHUNK * t - (KV_CHUNK - 1 - i)
        return jnp.maximum(b_i, 0), jnp.where(b_i >= 0, b_i, nb)

    def score(t, slot, own_chunk):
        tops = [None] * heads
        slab_max = [[None] * KV_CHUNK for _ in range(heads)]
        for i in range(KV_CHUNK):
            src, row = slab_block(t, i)
            kb = k_ref[0, pl.ds(pl.multiple_of(src * blk, blk), blk), :]
            for hh in range(heads):
                st = _dot_nt(tile(kb, hh), qh_ref[hh])
                if own_chunk and i == KV_CHUNK - 1:
                    st = st + causal_bias
                    cmax = jnp.max(st, axis=0, keepdims=True)
                    smax = cmax
                else:
                    cmax = jnp.max(st, axis=0, keepdims=True)
                    smax = cmax + (sel_ref[hh, pl.ds(row, 1), :] - 1.0) * (-NEG_INF)
                st_ref[slot, hh, i * blk:(i + 1) * blk, :] = st
                slab_max[hh][i] = cmax
                tops[hh] = smax if tops[hh] is None else jnp.maximum(tops[hh], smax)
        return tuple(tops), tuple(tuple(r) for r in slab_max)

    ones_rows = jnp.ones((2 * SUBLANES, blk), BF16)

    def accumulate(t, slot, state, maxes):
        tops, slab_max = maxes
        new = []
        for hh in range(heads):
            m, l = state[2 * hh:2 * hh + 2]
            m_new = jnp.maximum(m, tops[hh])
            alpha = jnp.exp2(m - m_new)
            l = alpha * l
            acc = alpha * acc_ref[hh]
            for i in range(KV_CHUNK):
                src, row = slab_block(t, i)
                keep = sel_ref[hh, pl.ds(row, 1), :]
                if i == KV_CHUNK - 1:
                    keep = jnp.where(t == 0, 1.0, keep)
                p = jnp.exp2(st_ref[slot, hh, i * blk:(i + 1) * blk, :] - jnp.maximum(m_new, slab_max[hh][i]))
                v_h = vt_ref[0, src][hh * ATT_HEAD_DIM:(hh + 1) * ATT_HEAD_DIM, :]
                pv = _dot(jnp.concatenate([v_h, ones_rows], axis=0), p.astype(BF16))
                l = l + keep * pv[ATT_HEAD_DIM:ATT_HEAD_DIM + 1, :]
                acc = acc + keep * pv[0:ATT_HEAD_DIM, :]
            acc_ref[hh] = acc
            new += [m_new, l]
        return tuple(new)

    state = ()
    for hh in range(heads):
        state += (jnp.full((1, blk), NEG_INF, F32), jnp.zeros((1, blk), F32))
        acc_ref[hh] = jnp.zeros((ATT_HEAD_DIM, blk), F32)
    last = j // KV_CHUNK

    def stage_pair(t, slot, state, maxes):
        nxt = score(t + 1, 1 - slot, False)
        return accumulate(t, slot, state, maxes), nxt

    def body(t, carry):
        return lax.cond(t % 2 == 0, functools.partial(stage_pair, t, 0), functools.partial(stage_pair, t, 1),
                        *carry)

    state, maxes = lax.fori_loop(0, last, body, (state, score(0, 0, True)))
    state = lax.cond(last % 2 == 0, functools.partial(accumulate, last, 0), functools.partial(accumulate, last, 1),
                     state, maxes)

    outs = []
    for hh in range(heads):
        o_h = acc_ref[hh] / state[2 * hh + 1]
        ms = jnp.mean(o_h * o_h, axis=0, keepdims=True)
        outs.append(o_h * lax.rsqrt(ms + NORM_EPS))
    o_ref[0] = jnp.concatenate(outs, axis=0).T * g_ref[...]


def _moba(q, k, vt, g_att):
    b, s, w = q.shape
    blk = MOBA_BLOCK
    nb = s // blk
    gl = MOBA_GROUP_LANES
    groups = w // gl
    heads = gl // ATT_HEAD_DIM
    return pl.pallas_call(
        _moba_kernel,
        grid=(b, groups, nb),
        in_specs=[pl.BlockSpec((1, blk, gl), lambda bi, p, j: (bi, j, p)),
                  pl.BlockSpec((1, s, gl), lambda bi, p, j: (bi, 0, p)),
                  pl.BlockSpec((1, nb, gl, blk), lambda bi, p, j: (bi, 0, p, 0)),
                  pl.BlockSpec((1, gl), lambda bi, p, j: (0, p))],
        out_specs=pl.BlockSpec((1, blk, gl), lambda bi, p, j: (bi, j, p)),
        out_shape=jax.ShapeDtypeStruct((b, s, w), F32),
        scratch_shapes=[pltpu.VMEM((nb, gl), F32),
                        pltpu.VMEM((heads, nb + SUBLANES, blk), F32),
                        pltpu.VMEM((2, heads, KV_CHUNK * blk, blk), F32),
                        pltpu.VMEM((heads, blk, LANES), BF16),
                        pltpu.VMEM((heads, ATT_HEAD_DIM, blk), F32)],
        compiler_params=_params(("arbitrary", "arbitrary", "arbitrary")),
        name="moba",
    )(q, k, vt, g_att)


def _mlstm_kernel(qkm_ref, vm_ref, om_ref, gt_ref, cw_ref, cb_ref, bg_ref, gm_ref, bsel_ref, o_ref,
                  uext_ref, c_ref, n_ref, m_ref):
    L = MLSTM_CHUNK
    dh = MLSTM_HEAD_DIM
    nh = MLSTM_HEADS
    assert L == dh == LANES and 2 * nh == SUBLANES
    c = pl.program_id(1)

    @pl.when(c == 0)
    def _():
        uext_ref[0:SUBLANES, :] = jnp.zeros((SUBLANES, uext_ref.shape[1]), F32)
        c_ref[...] = jnp.zeros(c_ref.shape, F32)
        n_ref[...] = jnp.zeros(n_ref.shape, F32)
        m_ref[...] = jnp.zeros(m_ref.shape, F32)

    for sub in range(qkm_ref.shape[1] // L):
        _mlstm_chunk(slice(sub * L, (sub + 1) * L), qkm_ref, vm_ref, om_ref, gt_ref, cw_ref, cb_ref, bg_ref, gm_ref,
                     bsel_ref, o_ref, uext_ref, c_ref, n_ref, m_ref)


def _mlstm_chunk(rows, qkm_ref, vm_ref, om_ref, gt_ref, cw_ref, cb_ref, bg_ref, gm_ref, bsel_ref, o_ref,
                 uext_ref, c_ref, n_ref, m_ref):
    L = MLSTM_CHUNK
    dh = MLSTM_HEAD_DIM
    nh = MLSTM_HEADS

    u = qkm_ref[0, rows, :]
    uext_ref[SUBLANES:SUBLANES + L, :] = u
    cw = cw_ref[...]
    conv = cb_ref[...] + cw[CONV_WIDTH - 1:CONV_WIDTH, :] * u
    for dlt in range(1, CONV_WIDTH):
        conv = conv + cw[CONV_WIDTH - 1 - dlt:CONV_WIDTH - dlt, :] * uext_ref[SUBLANES - dlt:SUBLANES - dlt + L, :]
    uext_ref[0:SUBLANES, :] = u[L - SUBLANES:L, :]
    act = _silu(conv)

    gates = gt_ref[0, rows, :] + bg_ref[...]
    ig = gates[:, 0:LANES]
    tpos = lax.broadcasted_iota(jnp.int32, (L, L), 0)
    spos = lax.broadcasted_iota(jnp.int32, (L, L), 1)
    causal = spos <= tpos
    tril = jnp.where(causal, 1.0, 0.0).astype(BF16)
    cum = sum(_dot(tril, part) for part in _split3(_log_sigmoid(gates[:, LANES:])))
    a_tot = cum[L - 1:L, :]
    m_prev = m_ref[0:1, :]
    resid = ig - cum
    pmax = resid
    shift = 1
    while shift < L:
        pmax = jnp.maximum(pmax, jnp.where(tpos >= shift, pltpu.roll(pmax, shift, 0), -jnp.inf))
        shift *= 2
    inter = cum + m_prev
    m_t = jnp.maximum(inter, cum + pmax)
    g_loc = a_tot - cum + ig
    m_loc = jnp.max(g_loc, axis=0, keepdims=True)
    w_loc = jnp.exp(g_loc - m_loc)
    head_lane = spos < nh
    packed = jnp.where(head_lane, cum - m_t, 0.0)
    for i, qty in enumerate((inter - m_t, -m_t, g_loc - m_loc), start=1):
        packed = packed + pltpu.roll(jnp.where(head_lane, qty, 0.0), SUBLANES * i, 1)
    per_t_cols = jnp.concatenate(_split3(packed), axis=1)
    per_t_rows = (jnp.where(head_lane, resid, 0.0) + pltpu.roll(jnp.where(head_lane, w_loc, 0.0), SUBLANES, 1)).T

    m_new = jnp.maximum(a_tot + m_prev, m_loc)
    s_prev = jnp.exp(a_tot + m_prev - m_new)
    s_loc = jnp.exp(m_loc - m_new)
    m_ref[0:1, :] = m_new

    vm = vm_ref[0, rows, :]
    om = om_ref[0, rows, :]
    heads = lambda a, off=0: jnp.stack([a[:, off + h * dh:off + (h + 1) * dh] for h in range(nh)])
    bdot = lambda a, b, ca, cb: lax.dot_general(a, b, (((ca,), (cb,)), ((0,), (0,))), preferred_element_type=F32)
    q4 = heads(act)
    k4 = heads(act, nh * dh) * (dh ** -0.5)
    v4 = heads(vm)
    qb, kb, vb = q4.astype(BF16), k4.astype(BF16), v4.astype(BF16)
    c_prev = c_ref[...]
    n_all = n_ref[...]
    n_prev = jnp.stack([n_all[h:h + 1, :] for h in range(nh)])
    bcast = _dot(per_t_cols, bsel_ref[...])
    qty = lambda i: jnp.stack([bcast[:, (4 * h + i) * dh:(4 * h + i + 1) * dh] for h in range(nh)])
    decay_b = qty(0)
    w_inter_b, floor_b, w_loc_b = jnp.exp(qty(1)), jnp.exp(qty(2)), jnp.exp(qty(3))
    resid_rows = jnp.stack([per_t_rows[h:h + 1, :] for h in range(nh)])
    w_rows = jnp.stack([per_t_rows[SUBLANES + h:SUBLANES + h + 1, :] for h in range(nh)])

    s_qk = bdot(qb, kb, 2, 2) * jnp.where(causal[None], jnp.exp(decay_b + resid_rows), 0.0)
    intra = bdot(s_qk.astype(BF16), jnp.concatenate([vb, jnp.ones((nh, L, dh), BF16)], axis=2), 2, 1)
    state_rows = jnp.concatenate([c_prev, jnp.broadcast_to(n_prev, (nh, dh, dh))], axis=1).astype(BF16)
    carried = bdot(qb, state_rows, 2, 2)
    num = intra[:, :, :dh] + w_inter_b * carried[:, :, :dh]
    den = intra[:, :, dh:] + w_inter_b * carried[:, :, dh:]
    hout = num / jnp.maximum(jnp.abs(den), floor_b)

    sq = (hout * hout).astype(BF16).reshape(nh * L, dh)
    mean_sq = (_dot(sq, jnp.ones((dh, dh), BF16)) * (1.0 / dh)).reshape(nh, L, dh)
    gm = jnp.stack([gm_ref[:, h * dh:(h + 1) * dh] for h in range(nh)])
    out = hout * lax.rsqrt(mean_sq + NORM_EPS) * gm * jax.nn.sigmoid(heads(om))
    for h in range(nh):
        o_ref[0, rows, h * dh:(h + 1) * dh] = out[h]

    vw = (v4 * w_loc_b).astype(BF16)
    for h in range(nh):
        c_loc = _dot_tn(vw[h], kb[h])
        n_loc = _dot(jnp.broadcast_to(w_rows[h], (SUBLANES, L)).astype(BF16), kb[h])[0:1, :]
        c_ref[h] = s_prev[:, h:h + 1] * c_prev[h] + s_loc[:, h:h + 1] * c_loc
        n_ref[h:h + 1, :] = s_prev[:, h:h + 1] * n_all[h:h + 1, :] + s_loc[:, h:h + 1] * n_loc


def _mlstm(qkm, vm, om, gates, conv_w, conv_b, b_gate, g_m):
    b, s, _ = qkm.shape
    L = MLSTM_CHUNK
    step_rows = MLSTM_CHUNKS_PER_STEP * L
    nc = s // step_rows
    lane_pad = ((0, 0), (0, LANES - MLSTM_HEADS))
    bg = b_gate.reshape(1, -1)
    bg = jnp.concatenate([jnp.pad(bg[:, :MLSTM_HEADS], lane_pad), jnp.pad(bg[:, MLSTM_HEADS:], lane_pad)], axis=1)
    tok = lambda w: pl.BlockSpec((1, step_rows, w), lambda bi, c: (bi, c, 0))
    full = lambda a: pl.BlockSpec(a.shape, lambda bi, c: (0,) * a.ndim)
    cb = conv_b.reshape(1, -1)
    gm = g_m.reshape(1, -1)
    n_qty = 4
    row = jnp.arange(3 * LANES, dtype=jnp.int32) % LANES
    col_blk = jnp.arange(MLSTM_HEADS * n_qty * MLSTM_HEAD_DIM, dtype=jnp.int32) // MLSTM_HEAD_DIM
    bsel = (((row // SUBLANES)[:, None] == (col_blk % n_qty)[None, :])
            & ((row % SUBLANES)[:, None] == (col_blk // n_qty)[None, :])).astype(BF16)
    return pl.pallas_call(
        _mlstm_kernel,
        grid=(b, nc),
        in_specs=[tok(2 * D_MLSTM), tok(D_MLSTM), tok(D_MLSTM), tok(GATE_LANES),
                  full(conv_w), full(cb), full(bg), full(gm), full(bsel)],
        out_specs=tok(D_MLSTM),
        out_shape=jax.ShapeDtypeStruct((b, s, D_MLSTM), F32),
        scratch_shapes=[pltpu.VMEM((SUBLANES + L, 2 * D_MLSTM), F32),
                        pltpu.VMEM((MLSTM_HEADS, MLSTM_HEAD_DIM, MLSTM_HEAD_DIM), F32),
                        pltpu.VMEM((SUBLANES, MLSTM_HEAD_DIM), F32),
                        pltpu.VMEM((SUBLANES, LANES), F32)],
        compiler_params=_params(("arbitrary", "arbitrary")),
        name="mlstm",
    )(qkm, vm, om, gates, conv_w, cb, bg, gm, bsel)


META_E0, META_E1, META_W0, META_W1, META_L0, META_L1 = range(6)
STAT_LEN, STAT_START = 0, 1


def _outproj_kernel(attn_ref, hm_ref, x_ref, mod_ref, g_ref, wout_ref, wr_hi_ref, wr_lo_ref, br_ref,
                    x1_ref, h2_ref, meta_ref, stat_ref):
    d = x_ref.shape[1]
    tm = x_ref.shape[0]

    mod = mod_ref[0]
    y = (_dot(attn_ref[...].astype(BF16), wout_ref[0:D_ATT, :])
         + _dot(hm_ref[...].astype(BF16), wout_ref[D_ATT:, :]))
    x1 = x_ref[...] + mod[:, 2 * d:3 * d] * y
    x1_ref[...] = x1
    h2 = _rms_modulate(x1, g_ref[...], mod[:, 3 * d:4 * d], mod[:, 4 * d:5 * d])
    h2_ref[...] = h2.astype(BF16)

    h_hi = h2.astype(BF16)
    h_lo = (h2 - h_hi.astype(F32)).astype(BF16)
    logit = (_dot_nt(wr_hi_ref[...], h_hi) + _dot_nt(wr_hi_ref[...], h_lo) + _dot_nt(wr_lo_ref[...], h_hi)
             + br_ref[...])
    rid = lax.broadcasted_iota(jnp.int32, logit.shape, 0)
    big = jnp.int32(ROUTER_LANES)

    is_g = (rid >= GROUP_LANE0) & (rid < GROUP_LANE0 + N_GROUPS)
    gmax = jnp.max(jnp.where(is_g, logit, -jnp.inf), axis=0, keepdims=True)
    gsum = jnp.sum(jnp.where(is_g, jnp.exp(logit - gmax), 0.0), axis=0, keepdims=True)
    g_w = 1.0 / gsum
    g_idx = jnp.min(jnp.where(is_g & (logit == gmax), rid, big), axis=0, keepdims=True) - GROUP_LANE0

    in_grp = (rid < N_EXPERTS) & ((rid // EXPERTS_PER_GROUP) == g_idx)
    emax = jnp.max(jnp.where(in_grp, logit, -jnp.inf), axis=0, keepdims=True)
    esum = jnp.sum(jnp.where(in_grp, jnp.exp(logit - emax), 0.0), axis=0, keepdims=True)
    e0 = jnp.min(jnp.where(in_grp & (logit == emax), rid, big), axis=0, keepdims=True)
    rest = in_grp & (rid != e0)
    e2max = jnp.max(jnp.where(rest, logit, -jnp.inf), axis=0, keepdims=True)
    e1 = jnp.min(jnp.where(rest & (logit == e2max), rid, big), axis=0, keepdims=True)
    p0 = 1.0 / esum
    p1 = jnp.exp(e2max - emax) / esum
    w0 = g_w * p0 / (p0 + p1)
    w1 = g_w * p1 / (p0 + p1)

    memb = jnp.where((rid == e0) | (rid == e1), 1.0, 0.0).astype(BF16)
    tpos = lax.broadcasted_iota(jnp.int32, (tm, tm), 0)
    spos = lax.broadcasted_iota(jnp.int32, (tm, tm), 1)
    earlier = _dot(memb, jnp.where(tpos < spos, 1.0, 0.0).astype(BF16))
    count = _dot(memb, jnp.ones((tm, tm), BF16)).astype(jnp.int32)
    run_len = (((count + (RUN_ALIGN - 1)) // RUN_ALIGN) * RUN_ALIGN).astype(F32)
    epos = lax.broadcasted_iota(jnp.int32, (ROUTER_LANES, ROUTER_LANES), 0)
    fpos = lax.broadcasted_iota(jnp.int32, (ROUTER_LANES, ROUTER_LANES), 1)
    run_start = _dot(jnp.where(fpos < epos, 1.0, 0.0).astype(BF16), run_len.astype(BF16))
    row = run_start + earlier
    l0 = jnp.sum(jnp.where(rid == e0, row, 0.0), axis=0, keepdims=True)
    l1 = jnp.sum(jnp.where(rid == e1, row, 0.0), axis=0, keepdims=True)

    meta_t = jnp.zeros(logit.shape, F32)
    for slot, val in ((META_E0, e0.astype(F32)), (META_E1, e1.astype(F32)), (META_W0, w0), (META_W1, w1),
                      (META_L0, l0), (META_L1, l1)):
        meta_t = jnp.where(rid == slot, val, meta_t)
    meta_ref[...] = meta_t.T
    srow = lax.broadcasted_iota(jnp.int32, (SUBLANES, ROUTER_LANES), 0)
    len_row = run_len[:, 0:ROUTER_LANES].T[0:SUBLANES, :]
    start_row = run_start[:, 0:ROUTER_LANES].T[0:SUBLANES, :]
    stat_ref[0] = jnp.where(srow == STAT_LEN, len_row, jnp.where(srow == STAT_START, start_row, 0.0))


def _outproj(attn, hm, x2d, mod3, norm_g, w_out, w_rg, b_rg, w_re, b_re, seq):
    t, d = x2d.shape
    tm = MOE_TILE
    per_batch = seq // tm
    wr = jnp.pad(jnp.concatenate([w_re, w_rg], axis=1), ((0, 0), (0, ROUTER_LANES - N_EXPERTS - N_GROUPS)))
    br = jnp.pad(jnp.concatenate([b_re, b_rg]).reshape(1, -1), ((0, 0), (0, ROUTER_LANES - N_EXPERTS - N_GROUPS)))
    wr = wr.T
    br = jnp.broadcast_to(br.T, (ROUTER_LANES, tm))
    wr_hi = wr.astype(BF16)
    wr_lo = (wr - wr_hi.astype(F32)).astype(BF16)
    wout = w_out.astype(BF16)
    tok = lambda w: pl.BlockSpec((tm, w), lambda i: (i, 0))
    full = lambda a: pl.BlockSpec(a.shape, lambda i: (0,) * a.ndim)
    return pl.pallas_call(
        _outproj_kernel,
        grid=(t // tm,),
        in_specs=[tok(D_ATT), tok(D_MLSTM), tok(d),
                  pl.BlockSpec((1, 1, mod3.shape[2]), lambda i: (i // per_batch, 0, 0)),
                  full(norm_g), full(wout), full(wr_hi), full(wr_lo), full(br)],
        out_specs=[tok(d), tok(d), tok(ROUTER_LANES),
                   pl.BlockSpec((1, SUBLANES, ROUTER_LANES), lambda i: (i, 0, 0))],
        out_shape=[jax.ShapeDtypeStruct((t, d), F32), jax.ShapeDtypeStruct((t, d), BF16),
                   jax.ShapeDtypeStruct((t, ROUTER_LANES), F32),
                   jax.ShapeDtypeStruct((t // tm, SUBLANES, ROUTER_LANES), F32)],
        compiler_params=_params(("arbitrary",)),
        name="outproj",
    )(attn, hm, x2d, mod3, norm_g, wout, wr_hi, wr_lo, br)


def _for_each_piece(length, pieces, fn):
    pos = jnp.int32(0)
    for size in pieces:
        take = length & size

        @pl.when(take != 0)
        def _(pos=pos, size=size):
            fn(pl.multiple_of(pos, RUN_ALIGN), size)

        pos = pos + take


def _for_each_chunk(tile, rows_ref, cdst_ref, fn):
    def chunk(c, priority):
        fn(pl.multiple_of(c * RUN_ALIGN, RUN_ALIGN),
           pl.multiple_of(cdst_ref[tile * CHUNKS_PER_TILE + c], RUN_ALIGN), priority)

    def body(pair, carry):
        for priority in range(2):
            chunk(2 * pair + priority, priority)
        return carry

    n_chunks = rows_ref[tile] // RUN_ALIGN
    lax.fori_loop(0, n_chunks // 2, body, 0)

    @pl.when(n_chunks % 2 == 1)
    def _():
        chunk(n_chunks - 1, 0)


def _dispatch_kernel(cdst_ref, rows_ref, tail_ref, tlen_ref, nu_ref, meta_ref, h2_ref, xs_ref,
                     perm_ref, zero_ref, sem, zsem):
    i = pl.program_id(0)
    rb = MOE_BLOCK_ROWS
    n_blocks = xs_ref.shape[0] // rb

    @pl.when(i == 0)
    def _():
        zero_ref[...] = jnp.zeros(zero_ref.shape, zero_ref.dtype)

        def zero_copy(row, size):
            return pltpu.make_async_copy(zero_ref.at[pl.ds(0, size)], xs_ref.at[pl.ds(row, size)], zsem)

        def fill(op):
            def per_expert(e, carry):
                start = tail_ref[e]
                _for_each_piece(tlen_ref[e], TAIL_PIECES,
                                lambda pos, size: op(zero_copy(pl.multiple_of(start + pos, RUN_ALIGN), size)))
                return carry

            lax.fori_loop(0, N_EXPERTS, per_expert, 0)

            def per_block(blk, carry):
                for part in range(rb // ZERO_ROWS):
                    op(zero_copy(pl.multiple_of(blk * rb + part * ZERO_ROWS, ZERO_ROWS), ZERO_ROWS))
                return carry

            lax.fori_loop(nu_ref[0], n_blocks, per_block, 0)

        fill(lambda cp: cp.start())
        fill(lambda cp: cp.wait())

    meta_t = meta_ref[...].T
    l0 = meta_t[META_L0:META_L0 + 1, :].astype(jnp.int32)
    l1 = meta_t[META_L1:META_L1 + 1, :].astype(jnp.int32)
    rpos = lax.broadcasted_iota(jnp.int32, (perm_ref.shape[1], meta_ref.shape[0]), 0)
    onehot = jnp.where((rpos == l0) | (rpos == l1), 1.0, 0.0).astype(BF16)
    slot = i % 2
    perm_ref[slot] = _dot(onehot, h2_ref[...]).astype(BF16)

    def run_copy(s, loc, glob, size):
        return pltpu.make_async_copy(perm_ref.at[s, pl.ds(loc, size)], xs_ref.at[pl.ds(glob, size)], sem.at[s])

    def wait_tile(tile, s):
        _for_each_piece(rows_ref[tile], TILE_PIECES, lambda pos, size: run_copy(s, 0, 0, size).wait())

    @pl.when(i > 0)
    def _():
        wait_tile(i - 1, 1 - slot)

    _for_each_chunk(i, rows_ref, cdst_ref,
                    lambda loc, glob, priority: run_copy(slot, loc, glob, RUN_ALIGN).start(priority=priority))

    @pl.when(i == pl.num_programs(0) - 1)
    def _():
        wait_tile(i, slot)


def _dispatch(plan, meta, h2):
    t, d = h2.shape
    tm = MOE_TILE
    grid_spec = pltpu.PrefetchScalarGridSpec(
        num_scalar_prefetch=5,
        grid=(t // tm,),
        in_specs=[pl.BlockSpec((tm, ROUTER_LANES), lambda i, *_: (i, 0)),
                  pl.BlockSpec((tm, d), lambda i, *_: (i, 0))],
        out_specs=pl.BlockSpec(memory_space=pl.ANY),
        scratch_shapes=[pltpu.VMEM((2, PERM_ROWS, d), BF16), pltpu.VMEM((ZERO_ROWS, d), BF16),
                        pltpu.SemaphoreType.DMA((2,)), pltpu.SemaphoreType.DMA(())],
    )
    return pl.pallas_call(
        _dispatch_kernel,
        grid_spec=grid_spec,
        out_shape=jax.ShapeDtypeStruct((plan["n_rows"], d), BF16),
        compiler_params=_params(("arbitrary",)),
        name="dispatch",
    )(plan["chunk_dst"], plan["tile_rows"], plan["tail_start"], plan["tail_len"], plan["n_used"], meta, h2)


def _expert_kernel(be_ref, nu_ref, xs_ref, w1_ref, w3_ref, w2_ref, y_ref, w1b_ref, w3b_ref, w2b_ref):
    i = pl.program_id(0)
    used = i < nu_ref[0]

    @pl.when(used & ((i == 0) | (be_ref[i] != be_ref[jnp.maximum(i, 1) - 1])))
    def _():
        w1b_ref[...] = w1_ref[0].astype(BF16)
        w3b_ref[...] = w3_ref[0].astype(BF16)
        w2b_ref[...] = w2_ref[0].astype(BF16)

    @pl.when(used)
    def _():
        x = xs_ref[...]
        a = _dot(x, w1b_ref[...])
        b = _dot(x, w3b_ref[...])
        y_ref[...] = _dot((_silu(a) * b).astype(BF16), w2b_ref[...]).astype(y_ref.dtype)

    @pl.when(jnp.logical_not(used))
    def _():
        y_ref[...] = jnp.zeros(y_ref.shape, y_ref.dtype)


def _experts(block_e, n_used, xs, w1, w3, w2):
    n_rows, d = xs.shape
    rb = MOE_BLOCK_ROWS
    de = w1.shape[2]
    used = lambda i, be, nu: jnp.minimum(i, nu[0] - 1)
    grid_spec = pltpu.PrefetchScalarGridSpec(
        num_scalar_prefetch=2,
        grid=(n_rows // rb,),
        in_specs=[pl.BlockSpec((rb, d), lambda i, be, nu: (used(i, be, nu), 0)),
                  pl.BlockSpec((1, d, de), lambda i, be, nu: (be[used(i, be, nu)], 0, 0)),
                  pl.BlockSpec((1, d, de), lambda i, be, nu: (be[used(i, be, nu)], 0, 0)),
                  pl.BlockSpec((1, de, d), lambda i, be, nu: (be[used(i, be, nu)], 0, 0))],
        out_specs=pl.BlockSpec((rb, d), lambda i, be, nu: (i, 0)),
        scratch_shapes=[pltpu.VMEM((d, de), BF16), pltpu.VMEM((d, de), BF16), pltpu.VMEM((de, d), BF16)],
    )
    return pl.pallas_call(
        _expert_kernel,
        grid_spec=grid_spec,
        out_shape=jax.ShapeDtypeStruct((n_rows, d), BF16),
        compiler_params=_params(("arbitrary",)),
        name="experts",
    )(block_e, n_used, xs, w1, w3, w2)


def _combine_kernel(final_norm, cdst_ref, rows_ref, x1_ref, meta_ref, mod_ref, g_ref, y_ref, o_ref, ybuf_ref, sem):
    tc, d = x1_ref.shape
    i = pl.program_id(0)

    slot = i % 2

    def run_copy(s, loc, glob, size):
        return pltpu.make_async_copy(y_ref.at[pl.ds(glob, size)], ybuf_ref.at[s, pl.ds(loc, size)], sem.at[s])

    def start_tile(tile, s):
        _for_each_chunk(tile, rows_ref, cdst_ref,
                        lambda loc, glob, priority: run_copy(s, loc, glob, RUN_ALIGN).start(priority=priority))

    @pl.when(i == 0)
    def _():
        ybuf_ref[...] = jnp.zeros(ybuf_ref.shape, ybuf_ref.dtype)
        start_tile(0, 0)

    @pl.when(i + 1 < pl.num_programs(0))
    def _():
        start_tile(i + 1, 1 - slot)

    _for_each_piece(rows_ref[i], TILE_PIECES, lambda pos, size: run_copy(slot, 0, 0, size).wait())

    meta = meta_ref[...]
    yb = ybuf_ref[slot]
    rpos = lax.broadcasted_iota(jnp.int32, (tc, ybuf_ref.shape[1]), 1)
    sel = jnp.zeros(rpos.shape, F32)
    for l_lane, w_lane in ((META_L0, META_W0), (META_L1, META_W1)):
        sel = jnp.where(rpos == meta[:, l_lane:l_lane + 1].astype(jnp.int32), meta[:, w_lane:w_lane + 1], sel)
    moe = _dot(sel.astype(BF16), yb)
    x2 = x1_ref[...] + mod_ref[0][:, 5 * d:6 * d] * moe
    if final_norm:
        x2 = x2 * lax.rsqrt(jnp.mean(x2 * x2, axis=-1, keepdims=True) + NORM_EPS) * g_ref[...]
    o_ref[...] = x2


def _combine(plan, x1, meta, mod3, norm_f_g, y, seq, final_norm):
    t, d = x1.shape
    tc = MOE_TILE
    per_batch = seq // tc
    gf = norm_f_g.reshape(1, d)
    grid_spec = pltpu.PrefetchScalarGridSpec(
        num_scalar_prefetch=2,
        grid=(t // tc,),
        in_specs=[pl.BlockSpec((tc, d), lambda i, *_: (i, 0)),
                  pl.BlockSpec((tc, ROUTER_LANES), lambda i, *_: (i, 0)),
                  pl.BlockSpec((1, 1, mod3.shape[2]), lambda i, *_: (i // per_batch, 0, 0)),
                  pl.BlockSpec(gf.shape, lambda i, *_: (0, 0)),
                  pl.BlockSpec(memory_space=pl.ANY)],
        out_specs=pl.BlockSpec((tc, d), lambda i, *_: (i, 0)),
        scratch_shapes=[pltpu.VMEM((2, PERM_ROWS, d), BF16), pltpu.SemaphoreType.DMA((2,))],
    )
    return pl.pallas_call(
        functools.partial(_combine_kernel, final_norm),
        grid_spec=grid_spec,
        out_shape=jax.ShapeDtypeStruct((t, d), F32),
        compiler_params=_params(("arbitrary",)),
        name="combine",
    )(plan["chunk_dst"], plan["tile_rows"], x1, meta, mod3, gf, y)


def _routing_plan(stats, n_tokens):
    rb = MOE_BLOCK_ROWS
    run_len = stats[:, STAT_LEN, :N_EXPERTS].astype(jnp.int32)
    run_loc = stats[:, STAT_START, :N_EXPERTS].astype(jnp.int32)
    n_tiles = run_len.shape[0]
    total = jnp.sum(run_len, axis=0)
    padded = ((total + rb - 1) // rb) * rb
    pad_end = jnp.cumsum(padded)
    pad_start = pad_end - padded
    run_glob = pad_start[None, :] + jnp.cumsum(run_len, axis=0) - run_len
    n_rows = n_tokens * TOP_K_EXPERTS + n_tiles * N_EXPERTS * (RUN_ALIGN - 1) + N_EXPERTS * (rb - 1)
    n_rows = ((n_rows + rb - 1) // rb) * rb
    block_row = jnp.arange(n_rows // rb, dtype=jnp.int32) * rb
    block_e = jnp.minimum(jnp.sum((pad_end[None, :] <= block_row[:, None]).astype(jnp.int32), axis=1),
                          N_EXPERTS - 1)
    local_row = jnp.arange(CHUNKS_PER_TILE, dtype=jnp.int32) * RUN_ALIGN
    chunk_e = jnp.minimum(jnp.sum(((run_loc + run_len)[:, None, :] <= local_row[None, :, None]).astype(jnp.int32),
                                  axis=-1), N_EXPERTS - 1)
    shift = jnp.sum(jnp.where(chunk_e[..., None] == jnp.arange(N_EXPERTS, dtype=jnp.int32),
                              (run_glob - run_loc)[:, None, :], 0), axis=-1)
    chunk_dst = shift + local_row[None, :]
    return dict(chunk_dst=chunk_dst.reshape(-1), tile_rows=jnp.sum(run_len, axis=1),
                tail_start=pad_start + total, tail_len=padded - total,
                n_used=(pad_end[-1:] // rb).astype(jnp.int32), block_e=block_e.astype(jnp.int32), n_rows=n_rows)


def kernel(x, c, positions, w_ada, b_ada, norm1_g, w_in, b_gate, conv_w, conv_b, attn_out_g, mlstm_out_g,
           w_out, norm2_g, w_rg, b_rg, w_re, b_re, w1, w3, w2, norm_f_g):
    b, s, d = x.shape
    depth = w_ada.shape[0]
    assert d == D_MODEL and s % MOE_TILE == 0 and s % INPROJ_TILE == 0 and s % (KV_CHUNK * MOBA_BLOCK) == 0
    assert s % (MLSTM_CHUNKS_PER_STEP * MLSTM_CHUNK) == 0
    x2d = x.reshape(b * s, d)
    for l in range(depth):
        mod3 = _mod(c, w_ada[l], b_ada[l]).reshape(b, 1, 6 * d)
        q, k, vt, qkm, vm, om, gates = _inproj(x2d.reshape(b, s, d), positions, mod3,
                                              norm1_g[l].reshape(1, d), w_in[l])
        attn = _moba(q, k, vt, attn_out_g[l].reshape(1, D_ATT))
        hm = _mlstm(qkm, vm, om, gates, conv_w[l], conv_b[l], b_gate[l], mlstm_out_g[l])
        x1, h2, meta, stats = _outproj(attn.reshape(b * s, D_ATT), hm.reshape(b * s, D_MLSTM), x2d, mod3,
                                       norm2_g[l].reshape(1, d), w_out[l], w_rg[l], b_rg[l], w_re[l],
                                       b_re[l], s)
        plan = _routing_plan(stats, b * s)
        xs = _dispatch(plan, meta, h2)
        y = _experts(plan["block_e"], plan["n_used"], xs, w1[l], w3[l], w2[l])
        x2d = _combine(plan, x1, meta, mod3, norm_f_g, y, s, final_norm=(l == depth - 1))
    return x2d.reshape(b, s, d)
```

```python
import functools

import jax
import jax.numpy as jnp
from jax import lax
from jax.experimental import pallas as pl
from jax.experimental.pallas import tpu as pltpu

F32 = jnp.float32
BF16 = jnp.bfloat16

D_MODEL = 1024
D_ATT = 512
ATT_HEADS = 8
ATT_HEAD_DIM = 64
D_MLSTM = 512
MLSTM_HEADS = 4
MLSTM_HEAD_DIM = 128
MOBA_BLOCK = 256
MOBA_TOPK = 3
ROPE_THETA = 10000.0
MLSTM_CHUNK = 128
MLSTM_CHUNKS_PER_STEP = 8
CONV_WIDTH = 4
N_GROUPS = 4
EXPERTS_PER_GROUP = 8
N_EXPERTS = N_GROUPS * EXPERTS_PER_GROUP
TOP_K_EXPERTS = 2
D_EXPERT = 512
MOE_BLOCK_ROWS = 512
NORM_EPS = 1e-6
NEG_INF = -1e30
LOG2_E = 1.4426950408889634
KV_CHUNK = 2
MOBA_GROUP_LANES = 512

LANES = 128
SUBLANES = 8
VMEM_LIMIT_BYTES = 56 * 1024 * 1024

INPROJ_TILE = 512
MOE_TILE = 512
RUN_ALIGN = 2 * SUBLANES
PERM_ROWS = TOP_K_EXPERTS * MOE_TILE + N_EXPERTS * RUN_ALIGN
CHUNKS_PER_TILE = PERM_ROWS // RUN_ALIGN
TAIL_PIECES = tuple(RUN_ALIGN << p for p in reversed(range(5)))
assert TAIL_PIECES[0] * 2 == MOE_BLOCK_ROWS
TILE_PIECES = tuple(RUN_ALIGN << p for p in reversed(range(7)))
ZERO_ROWS = TAIL_PIECES[0]
GATE_LANES = 2 * LANES
ROUTER_LANES = LANES
GROUP_LANE0 = N_EXPERTS


def _dot(a, b):
    return jnp.dot(a, b, preferred_element_type=F32)


def _dot_nt(a, b):
    return lax.dot_general(a, b, (((1,), (1,)), ((), ())), preferred_element_type=F32)


def _dot_tn(a, b):
    return lax.dot_general(a, b, (((0,), (0,)), ((), ())), preferred_element_type=F32)


def _split3(x):
    a = x.astype(BF16)
    r = x - a.astype(F32)
    b = r.astype(BF16)
    c = (r - b.astype(F32)).astype(BF16)
    return a, b, c


def _silu(x):
    return x * jax.nn.sigmoid(x)


def _log_sigmoid(x):
    return jnp.minimum(x, 0.0) - jnp.log1p(jnp.exp(-jnp.abs(x)))


def _params(semantics, vmem=VMEM_LIMIT_BYTES):
    return pltpu.CompilerParams(dimension_semantics=semantics, vmem_limit_bytes=vmem)


def _mod_kernel(c_ref, w_ref, b_ref, o_ref):
    a0, a1, a2 = _split3(_silu(c_ref[...]))
    w0, w1, w2 = _split3(w_ref[...])
    o_ref[...] = (_dot(a0, w0) + (_dot(a0, w1) + _dot(a1, w0))
                  + (_dot(a0, w2) + _dot(a1, w1) + _dot(a2, w0))) + b_ref[...]


def _mod(c, w_ada, b_ada):
    b, d = c.shape
    n = w_ada.shape[1]
    tn = D_MODEL
    return pl.pallas_call(
        _mod_kernel,
        grid=(n // tn,),
        in_specs=[pl.BlockSpec((b, d), lambda i: (0, 0)),
                  pl.BlockSpec((d, tn), lambda i: (0, i)),
                  pl.BlockSpec((1, tn), lambda i: (0, i))],
        out_specs=pl.BlockSpec((b, tn), lambda i: (0, i)),
        out_shape=jax.ShapeDtypeStruct((b, n), F32),
        compiler_params=_params(("arbitrary",)),
        name="mod",
    )(c, w_ada, b_ada.reshape(1, n))


def _rms_modulate(x, g, shift, scale):
    y = x * lax.rsqrt(jnp.mean(x * x, axis=-1, keepdims=True) + NORM_EPS)
    return (y * g) * (1.0 + scale) + shift


def _rope(t, cos, sin, first_half):
    outs = []
    for c in range(t.shape[1] // LANES):
        tc = t[:, c * LANES:(c + 1) * LANES]
        rot = jnp.where(first_half, -pltpu.roll(tc, LANES - ATT_HEAD_DIM // 2, 1),
                        pltpu.roll(tc, ATT_HEAD_DIM // 2, 1))
        outs.append(tc * cos + rot * sin)
    return jnp.concatenate(outs, axis=1)


def _inproj_kernel(x_ref, pos_ref, mod_ref, g_ref, invf_ref, wq_ref, wk_ref, wvt_ref, wqk_ref,
                   wv_ref, wo_ref, wg_ref, q_ref, k_ref, vt_ref, qkm_ref, vm_ref, om_ref, gt_ref):
    d = x_ref.shape[2]
    x = x_ref[0]
    mod = mod_ref[0]
    h = _rms_modulate(x, g_ref[...], mod[:, 0:d], mod[:, d:2 * d])
    hb = h.astype(BF16)

    ang = pos_ref[0].astype(F32) * invf_ref[...]
    cos = jnp.cos(ang)
    sin = jnp.sin(ang)
    lane = lax.broadcasted_iota(jnp.int32, cos.shape, 1)
    first_half = (lane & (ATT_HEAD_DIM // 2)) == 0

    q = _rope(_dot(hb, wq_ref[...]), cos, sin, first_half)
    q_ref[0] = (q * (ATT_HEAD_DIM ** -0.5 * LOG2_E)).astype(BF16)
    k_ref[0] = _rope(_dot(hb, wk_ref[...]), cos, sin, first_half).astype(BF16)
    vt = _dot_nt(wvt_ref[...], hb).astype(BF16)
    for blk_i in range(vt_ref.shape[1]):
        vt_ref[0, blk_i] = vt[:, blk_i * MOBA_BLOCK:(blk_i + 1) * MOBA_BLOCK]
    qkm_ref[0] = _dot(hb, wqk_ref[...])
    vm_ref[0] = _dot(hb, wv_ref[...])
    om_ref[0] = _dot(hb, wo_ref[...])
    gt_ref[0] = _dot(hb, wg_ref[...])


def _inproj(x, positions, mod3, norm_g, w_in):
    b, s, d = x.shape
    tm = INPROJ_TILE
    per_tile = tm // MOBA_BLOCK
    nb = s // MOBA_BLOCK
    o = [0, D_ATT, 2 * D_ATT, 3 * D_ATT, 3 * D_ATT + 2 * D_MLSTM, 3 * D_ATT + 3 * D_MLSTM,
         3 * D_ATT + 4 * D_MLSTM, 3 * D_ATT + 4 * D_MLSTM + 2 * MLSTM_HEADS]
    wb = w_in.astype(BF16)
    wq, wk, wv_a, wqk, wv, wo, wg = (wb[:, o[i]:o[i + 1]] for i in range(7))
    wvt = wv_a.T
    lane_pad = ((0, 0), (0, LANES - MLSTM_HEADS))
    wg = jnp.concatenate([jnp.pad(wg[:, :MLSTM_HEADS], lane_pad), jnp.pad(wg[:, MLSTM_HEADS:], lane_pad)], axis=1)
    half = ATT_HEAD_DIM // 2
    inv_freq = ROPE_THETA ** (-jnp.arange(half, dtype=F32) / half)
    invf = jnp.tile(inv_freq, LANES // half).reshape(1, LANES)

    full = lambda a: pl.BlockSpec(a.shape, lambda bi, i: (0,) * a.ndim)
    tok = lambda w: pl.BlockSpec((1, tm, w), lambda bi, i: (bi, i, 0))
    out_shape = [jax.ShapeDtypeStruct((b, s, D_ATT), BF16),
                 jax.ShapeDtypeStruct((b, s, D_ATT), BF16),
                 jax.ShapeDtypeStruct((b, nb, D_ATT, MOBA_BLOCK), BF16),
                 jax.ShapeDtypeStruct((b, s, 2 * D_MLSTM), F32),
                 jax.ShapeDtypeStruct((b, s, D_MLSTM), F32),
                 jax.ShapeDtypeStruct((b, s, D_MLSTM), F32),
                 jax.ShapeDtypeStruct((b, s, GATE_LANES), F32)]
    return pl.pallas_call(
        _inproj_kernel,
        grid=(b, s // tm),
        in_specs=[tok(d), tok(1),
                  pl.BlockSpec((1, 1, mod3.shape[2]), lambda bi, i: (bi, 0, 0)),
                  full(norm_g), full(invf), full(wq), full(wk), full(wvt), full(wqk), full(wv),
                  full(wo), full(wg)],
        out_specs=[tok(D_ATT), tok(D_ATT),
                   pl.BlockSpec((1, per_tile, D_ATT, MOBA_BLOCK), lambda bi, i: (bi, i, 0, 0)),
                   tok(2 * D_MLSTM), tok(D_MLSTM), tok(D_MLSTM), tok(GATE_LANES)],
        out_shape=out_shape,
        compiler_params=_params(("arbitrary", "arbitrary")),
        name="inproj",
    )(x, positions.reshape(b, s, 1), mod3, norm_g, invf, wq, wk, wvt, wqk, wv, wo, wg)


def _moba_kernel(q_ref, k_ref, vt_ref, g_ref, o_ref, kmean_ref, sel_ref, st_ref, qh_ref, acc_ref):
    blk = MOBA_BLOCK
    nb = k_ref.shape[1] // blk
    j = pl.program_id(2)
    heads = q_ref.shape[2] // ATT_HEAD_DIM
    hpl = LANES // ATT_HEAD_DIM

    @pl.when(j == 0)
    def _():
        for n in range(nb):
            kb = k_ref[0, n * blk:(n + 1) * blk, :].astype(F32)
            kmean_ref[n:n + 1, :] = jnp.mean(kb, axis=0, keepdims=True)

    lane = lax.broadcasted_iota(jnp.int32, (blk, LANES), 1)
    blk_id = lax.broadcasted_iota(jnp.int32, (nb, blk), 0)
    km_hi = kmean_ref[...].astype(BF16)
    km_lo = (kmean_ref[...] - km_hi.astype(F32)).astype(BF16)
    past = blk_id < j
    tile = lambda a, hh: a[:, (hh // hpl) * LANES:(hh // hpl + 1) * LANES]

    gates = []
    for hh in range(heads):
        q = tile(q_ref[0], hh)
        in_head = (lane >= (hh % hpl) * ATT_HEAD_DIM) & (lane < (hh % hpl + 1) * ATT_HEAD_DIM)
        qh_ref[hh] = jnp.where(in_head, q, jnp.zeros_like(q))
        gates.append(_dot_nt(tile(km_hi, hh), qh_ref[hh]) + _dot_nt(tile(km_lo, hh), qh_ref[hh]))
    g = jnp.where(past[None], jnp.stack(gates), NEG_INF)
    sel = jnp.zeros(g.shape, F32)
    for _ in range(min(MOBA_TOPK, nb)):
        top = jnp.max(g, axis=1, keepdims=True)
        idx = jnp.min(jnp.where(g == top, blk_id[None], nb), axis=1, keepdims=True)
        pick = blk_id[None] == idx
        sel = jnp.where(pick, 1.0, sel)
        g = jnp.where(pick, -jnp.inf, g)
    sel_ref[:, 0:nb, :] = jnp.where((sel > 0.0) & past[None], 1.0, 0.0)
    sel_ref[:, nb:nb + SUBLANES, :] = jnp.zeros((heads, SUBLANES, blk), F32)

    kpos = lax.broadcasted_iota(jnp.int32, (blk, blk), 0)
    qpos = lax.broadcasted_iota(jnp.int32, (blk, blk), 1)
    causal_bias = jnp.where(kpos <= qpos, 0.0, NEG_INF)

    def slab_block(t, i):
        b_i = j - KV_CHUNK * t - (KV_CHUNK - 1 - i)
        return jnp.maximum(b_i, 0), jnp.where(b_i >= 0, b_i, nb)

    def score(t, slot, own_chunk):
        tops = [None] * heads
        slab_max = [[None] * KV_CHUNK for _ in range(heads)]
        for i in range(KV_CHUNK):
            src, row = slab_block(t, i)
            kb = k_ref[0, pl.ds(pl.multiple_of(src * blk, blk), blk), :]
            for hh in range(heads):
                st = _dot_nt(tile(kb, hh), qh_ref[hh])
                if own_chunk and i == KV_CHUNK - 1:
                    st = st + causal_bias
                    cmax = jnp.max(st, axis=0, keepdims=True)
                    smax = cmax
                else:
                    cmax = jnp.max(st, axis=0, keepdims=True)
                    smax = cmax + (sel_ref[hh, pl.ds(row, 1), :] - 1.0) * (-NEG_INF)
                st_ref[slot, hh, i * blk:(i + 1) * blk, :] = st
                slab_max[hh][i] = cmax
                tops[hh] = smax if tops[hh] is None else jnp.maximum(tops[hh], smax)
        return tuple(tops), tuple(tuple(r) for r in slab_max)

    ones_rows = jnp.ones((2 * SUBLANES, blk), BF16)

    def accumulate(t, slot, state, maxes):
        tops, slab_max = maxes
        new = []
        for hh in range(heads):
            m, l = state[2 * hh:2 * hh + 2]
            m_new = jnp.maximum(m, tops[hh])
            alpha = jnp.exp2(m - m_new)
            l = alpha * l
            acc = alpha * acc_ref[hh]
            for i in range(KV_CHUNK):
                src, row = slab_block(t, i)
                keep = sel_ref[hh, pl.ds(row, 1), :]
                if i == KV_CHUNK - 1:
                    keep = jnp.where(t == 0, 1.0, keep)
                p = jnp.exp2(st_ref[slot, hh, i * blk:(i + 1) * blk, :] - jnp.maximum(m_new, slab_max[hh][i]))
                v_h = vt_ref[0, src][hh * ATT_HEAD_DIM:(hh + 1) * ATT_HEAD_DIM, :]
                pv = _dot(jnp.concatenate([v_h, ones_rows], axis=0), p.astype(BF16))
                l = l + keep * pv[ATT_HEAD_DIM:ATT_HEAD_DIM + 1, :]
                acc = acc + keep * pv[0:ATT_HEAD_DIM, :]
            acc_ref[hh] = acc
            new += [m_new, l]
        return tuple(new)

    state = ()
    for hh in range(heads):
        state += (jnp.full((1, blk), NEG_INF, F32), jnp.zeros((1, blk), F32))
        acc_ref[hh] = jnp.zeros((ATT_HEAD_DIM, blk), F32)
    last = j // KV_CHUNK

    def stage_pair(t, slot, state, maxes):
        nxt = score(t + 1, 1 - slot, False)
        return accumulate(t, slot, state, maxes), nxt

    def body(t, carry):
        return lax.cond(t % 2 == 0, functools.partial(stage_pair, t, 0), functools.partial(stage_pair, t, 1),
                        *carry)

    state, maxes = lax.fori_loop(0, last, body, (state, score(0, 0, True)))
    state = lax.cond(last % 2 == 0, functools.partial(accumulate, last, 0), functools.partial(accumulate, last, 1),
                     state, maxes)

    outs = []
    for hh in range(heads):
        o_h = acc_ref[hh] / state[2 * hh + 1]
        ms = jnp.mean(o_h * o_h, axis=0, keepdims=True)
        outs.append(o_h * lax.rsqrt(ms + NORM_EPS))
    o_ref[0] = jnp.concatenate(outs, axis=0).T * g_ref[...]


def _moba(q, k, vt, g_att):
    b, s, w = q.shape
    blk = MOBA_BLOCK
    nb = s // blk
    gl = MOBA_GROUP_LANES
    groups = w // gl
    heads = gl // ATT_HEAD_DIM
    return pl.pallas_call(
        _moba_kernel,
        grid=(b, groups, nb),
        in_specs=[pl.BlockSpec((1, blk, gl), lambda bi, p, j: (bi, j, p)),
                  pl.BlockSpec((1, s, gl), lambda bi, p, j: (bi, 0, p)),
                  pl.BlockSpec((1, nb, gl, blk), lambda bi, p, j: (bi, 0, p, 0)),
                  pl.BlockSpec((1, gl), lambda bi, p, j: (0, p))],
        out_specs=pl.BlockSpec((1, blk, gl), lambda bi, p, j: (bi, j, p)),
        out_shape=jax.ShapeDtypeStruct((b, s, w), F32),
        scratch_shapes=[pltpu.VMEM((nb, gl), F32),
                        pltpu.VMEM((heads, nb + SUBLANES, blk), F32),
                        pltpu.VMEM((2, heads, KV_CHUNK * blk, blk), F32),
                        pltpu.VMEM((heads, blk, LANES), BF16),
                        pltpu.VMEM((heads, ATT_HEAD_DIM, blk), F32)],
        compiler_params=_params(("arbitrary", "arbitrary", "arbitrary")),
        name="moba",
    )(q, k, vt, g_att)


def _mlstm_kernel(qkm_ref, vm_ref, om_ref, gt_ref, cw_ref, cb_ref, bg_ref, gm_ref, bsel_ref, o_ref,
                  uext_ref, c_ref, n_ref, m_ref):
    L = MLSTM_CHUNK
    dh = MLSTM_HEAD_DIM
    nh = MLSTM_HEADS
    assert L == dh == LANES and 2 * nh == SUBLANES
    c = pl.program_id(1)

    @pl.when(c == 0)
    def _():
        uext_ref[0:SUBLANES, :] = jnp.zeros((SUBLANES, uext_ref.shape[1]), F32)
        c_ref[...] = jnp.zeros(c_ref.shape, F32)
        n_ref[...] = jnp.zeros(n_ref.shape, F32)
        m_ref[...] = jnp.zeros(m_ref.shape, F32)

    for sub in range(qkm_ref.shape[1] // L):
        _mlstm_chunk(slice(sub * L, (sub + 1) * L), qkm_ref, vm_ref, om_ref, gt_ref, cw_ref, cb_ref, bg_ref, gm_ref,
                     bsel_ref, o_ref, uext_ref, c_ref, n_ref, m_ref)


def _mlstm_chunk(rows, qkm_ref, vm_ref, om_ref, gt_ref, cw_ref, cb_ref, bg_ref, gm_ref, bsel_ref, o_ref,
                 uext_ref, c_ref, n_ref, m_ref):
    L = MLSTM_CHUNK
    dh = MLSTM_HEAD_DIM
    nh = MLSTM_HEADS

    u = qkm_ref[0, rows, :]
    uext_ref[SUBLANES:SUBLANES + L, :] = u
    cw = cw_ref[...]
    conv = cb_ref[...] + cw[CONV_WIDTH - 1:CONV_WIDTH, :] * u
    for dlt in range(1, CONV_WIDTH):
        conv = conv + cw[CONV_WIDTH - 1 - dlt:CONV_WIDTH - dlt, :] * uext_ref[SUBLANES - dlt:SUBLANES - dlt + L, :]
    uext_ref[0:SUBLANES, :] = u[L - SUBLANES:L, :]
    act = _silu(conv)

    gates = gt_ref[0, rows, :] + bg_ref[...]
    ig = gates[:, 0:LANES]
    tpos = lax.broadcasted_iota(jnp.int32, (L, L), 0)
    spos = lax.broadcasted_iota(jnp.int32, (L, L), 1)
    causal = spos <= tpos
    tril = jnp.where(causal, 1.0, 0.0).astype(BF16)
    cum = sum(_dot(tril, part) for part in _split3(_log_sigmoid(gates[:, LANES:])))
    a_tot = cum[L - 1:L, :]
    m_prev = m_ref[0:1, :]
    resid = ig - cum
    pmax = resid
    shift = 1
    while shift < L:
        pmax = jnp.maximum(pmax, jnp.where(tpos >= shift, pltpu.roll(pmax, shift, 0), -jnp.inf))
        shift *= 2
    inter = cum + m_prev
    m_t = jnp.maximum(inter, cum + pmax)
    g_loc = a_tot - cum + ig
    m_loc = jnp.max(g_loc, axis=0, keepdims=True)
    w_loc = jnp.exp(g_loc - m_loc)
    head_lane = spos < nh
    packed = jnp.where(head_lane, cum - m_t, 0.0)
    for i, qty in enumerate((inter - m_t, -m_t, g_loc - m_loc), start=1):
        packed = packed + pltpu.roll(jnp.where(head_lane, qty, 0.0), SUBLANES * i, 1)
    per_t_cols = jnp.concatenate(_split3(packed), axis=1)
    per_t_rows = (jnp.where(head_lane, resid, 0.0) + pltpu.roll(jnp.where(head_lane, w_loc, 0.0), SUBLANES, 1)).T

    m_new = jnp.maximum(a_tot + m_prev, m_loc)
    s_prev = jnp.exp(a_tot + m_prev - m_new)
    s_loc = jnp.exp(m_loc - m_new)
    m_ref[0:1, :] = m_new

    vm = vm_ref[0, rows, :]
    om = om_ref[0, rows, :]
    heads = lambda a, off=0: jnp.stack([a[:, off + h * dh:off + (h + 1) * dh] for h in range(nh)])
    bdot = lambda a, b, ca, cb: lax.dot_general(a, b, (((ca,), (cb,)), ((0,), (0,))), preferred_element_type=F32)
    q4 = heads(act)
    k4 = heads(act, nh * dh) * (dh ** -0.5)
    v4 = heads(vm)
    qb, kb, vb = q4.astype(BF16), k4.astype(BF16), v4.astype(BF16)
    c_prev = c_ref[...]
    n_all = n_ref[...]
    n_prev = jnp.stack([n_all[h:h + 1, :] for h in range(nh)])
    bcast = _dot(per_t_cols, bsel_ref[...])
    qty = lambda i: jnp.stack([bcast[:, (4 * h + i) * dh:(4 * h + i + 1) * dh] for h in range(nh)])
    decay_b = qty(0)
    w_inter_b, floor_b, w_loc_b = jnp.exp(qty(1)), jnp.exp(qty(2)), jnp.exp(qty(3))
    resid_rows = jnp.stack([per_t_rows[h:h + 1, :] for h in range(nh)])
    w_rows = jnp.stack([per_t_rows[SUBLANES + h:SUBLANES + h + 1, :] for h in range(nh)])

    s_qk = bdot(qb, kb, 2, 2) * jnp.where(causal[None], jnp.exp(decay_b + resid_rows), 0.0)
    intra = bdot(s_qk.astype(BF16), jnp.concatenate([vb, jnp.ones((nh, L, dh), BF16)], axis=2), 2, 1)
    state_rows = jnp.concatenate([c_prev, jnp.broadcast_to(n_prev, (nh, dh, dh))], axis=1).astype(BF16)
    carried = bdot(qb, state_rows, 2, 2)
    num = intra[:, :, :dh] + w_inter_b * carried[:, :, :dh]
    den = intra[:, :, dh:] + w_inter_b * carried[:, :, dh:]
    hout = num / jnp.maximum(jnp.abs(den), floor_b)

    sq = (hout * hout).astype(BF16).reshape(nh * L, dh)
    mean_sq = (_dot(sq, jnp.ones((dh, dh), BF16)) * (1.0 / dh)).reshape(nh, L, dh)
    gm = jnp.stack([gm_ref[:, h * dh:(h + 1) * dh] for h in range(nh)])
    out = hout * lax.rsqrt(mean_sq + NORM_EPS) * gm * jax.nn.sigmoid(heads(om))
    for h in range(nh):
        o_ref[0, rows, h * dh:(h + 1) * dh] = out[h]

    vw = (v4 * w_loc_b).astype(BF16)
    for h in range(nh):
        c_loc = _dot_tn(vw[h], kb[h])
        n_loc = _dot(jnp.broadcast_to(w_rows[h], (SUBLANES, L)).astype(BF16), kb[h])[0:1, :]
        c_ref[h] = s_prev[:, h:h + 1] * c_prev[h] + s_loc[:, h:h + 1] * c_loc
        n_ref[h:h + 1, :] = s_prev[:, h:h + 1] * n_all[h:h + 1, :] + s_loc[:, h:h + 1] * n_loc


def _mlstm(qkm, vm, om, gates, conv_w, conv_b, b_gate, g_m):
    b, s, _ = qkm.shape
    L = MLSTM_CHUNK
    step_rows = MLSTM_CHUNKS_PER_STEP * L
    nc = s // step_rows
    lane_pad = ((0, 0), (0, LANES - MLSTM_HEADS))
    bg = b_gate.reshape(1, -1)
    bg = jnp.concatenate([jnp.pad(bg[:, :MLSTM_HEADS], lane_pad), jnp.pad(bg[:, MLSTM_HEADS:], lane_pad)], axis=1)
    tok = lambda w: pl.BlockSpec((1, step_rows, w), lambda bi, c: (bi, c, 0))
    full = lambda a: pl.BlockSpec(a.shape, lambda bi, c: (0,) * a.ndim)
    cb = conv_b.reshape(1, -1)
    gm = g_m.reshape(1, -1)
    n_qty = 4
    row = jnp.arange(3 * LANES, dtype=jnp.int32) % LANES
    col_blk = jnp.arange(MLSTM_HEADS * n_qty * MLSTM_HEAD_DIM, dtype=jnp.int32) // MLSTM_HEAD_DIM
    bsel = (((row // SUBLANES)[:, None] == (col_blk % n_qty)[None, :])
            & ((row % SUBLANES)[:, None] == (col_blk // n_qty)[None, :])).astype(BF16)
    return pl.pallas_call(
        _mlstm_kernel,
        grid=(b, nc),
        in_specs=[tok(2 * D_MLSTM), tok(D_MLSTM), tok(D_MLSTM), tok(GATE_LANES),
                  full(conv_w), full(cb), full(bg), full(gm), full(bsel)],
        out_specs=tok(D_MLSTM),
        out_shape=jax.ShapeDtypeStruct((b, s, D_MLSTM), F32),
        scratch_shapes=[pltpu.VMEM((SUBLANES + L, 2 * D_MLSTM), F32),
                        pltpu.VMEM((MLSTM_HEADS, MLSTM_HEAD_DIM, MLSTM_HEAD_DIM), F32),
                        pltpu.VMEM((SUBLANES, MLSTM_HEAD_DIM), F32),
                        pltpu.VMEM((SUBLANES, LANES), F32)],
        compiler_params=_params(("arbitrary", "arbitrary")),
        name="mlstm",
    )(qkm, vm, om, gates, conv_w, cb, bg, gm, bsel)


META_E0, META_E1, META_W0, META_W1, META_L0, META_L1 = range(6)
STAT_LEN, STAT_START = 0, 1


def _outproj_kernel(attn_ref, hm_ref, x_ref, mod_ref, g_ref, wout_ref, wr_hi_ref, wr_lo_ref, br_ref,
                    x1_ref, h2_ref, meta_ref, stat_ref):
    d = x_ref.shape[1]
    tm = x_ref.shape[0]

    mod = mod_ref[0]
    y = (_dot(attn_ref[...].astype(BF16), wout_ref[0:D_ATT, :])
         + _dot(hm_ref[...].astype(BF16), wout_ref[D_ATT:, :]))
    x1 = x_ref[...] + mod[:, 2 * d:3 * d] * y
    x1_ref[...] = x1
    h2 = _rms_modulate(x1, g_ref[...], mod[:, 3 * d:4 * d], mod[:, 4 * d:5 * d])
    h2_ref[...] = h2.astype(BF16)

    h_hi = h2.astype(BF16)
    h_lo = (h2 - h_hi.astype(F32)).astype(BF16)
    logit = (_dot_nt(wr_hi_ref[...], h_hi) + _dot_nt(wr_hi_ref[...], h_lo) + _dot_nt(wr_lo_ref[...], h_hi)
             + br_ref[...])
    rid = lax.broadcasted_iota(jnp.int32, logit.shape, 0)
    big = jnp.int32(ROUTER_LANES)

    is_g = (rid >= GROUP_LANE0) & (rid < GROUP_LANE0 + N_GROUPS)
    gmax = jnp.max(jnp.where(is_g, logit, -jnp.inf), axis=0, keepdims=True)
    gsum = jnp.sum(jnp.where(is_g, jnp.exp(logit - gmax), 0.0), axis=0, keepdims=True)
    g_w = 1.0 / gsum
    g_idx = jnp.min(jnp.where(is_g & (logit == gmax), rid, big), axis=0, keepdims=True) - GROUP_LANE0

    in_grp = (rid < N_EXPERTS) & ((rid // EXPERTS_PER_GROUP) == g_idx)
    emax = jnp.max(jnp.where(in_grp, logit, -jnp.inf), axis=0, keepdims=True)
    esum = jnp.sum(jnp.where(in_grp, jnp.exp(logit - emax), 0.0), axis=0, keepdims=True)
    e0 = jnp.min(jnp.where(in_grp & (logit == emax), rid, big), axis=0, keepdims=True)
    rest = in_grp & (rid != e0)
    e2max = jnp.max(jnp.where(rest, logit, -jnp.inf), axis=0, keepdims=True)
    e1 = jnp.min(jnp.where(rest & (logit == e2max), rid, big), axis=0, keepdims=True)
    p0 = 1.0 / esum
    p1 = jnp.exp(e2max - emax) / esum
    w0 = g_w * p0 / (p0 + p1)
    w1 = g_w * p1 / (p0 + p1)

    memb = jnp.where((rid == e0) | (rid == e1), 1.0, 0.0).astype(BF16)
    tpos = lax.broadcasted_iota(jnp.int32, (tm, tm), 0)
    spos = lax.broadcasted_iota(jnp.int32, (tm, tm), 1)
    earlier = _dot(memb, jnp.where(tpos < spos, 1.0, 0.0).astype(BF16))
    count = _dot(memb, jnp.ones((tm, tm), BF16)).astype(jnp.int32)
    run_len = (((count + (RUN_ALIGN - 1)) // RUN_ALIGN) * RUN_ALIGN).astype(F32)
    epos = lax.broadcasted_iota(jnp.int32, (ROUTER_LANES, ROUTER_LANES), 0)
    fpos = lax.broadcasted_iota(jnp.int32, (ROUTER_LANES, ROUTER_LANES), 1)
    run_start = _dot(jnp.where(fpos < epos, 1.0, 0.0).astype(BF16), run_len.astype(BF16))
    row = run_start + earlier
    l0 = jnp.sum(jnp.where(rid == e0, row, 0.0), axis=0, keepdims=True)
    l1 = jnp.sum(jnp.where(rid == e1, row, 0.0), axis=0, keepdims=True)

    meta_t = jnp.zeros(logit.shape, F32)
    for slot, val in ((META_E0, e0.astype(F32)), (META_E1, e1.astype(F32)), (META_W0, w0), (META_W1, w1),
                      (META_L0, l0), (META_L1, l1)):
        meta_t = jnp.where(rid == slot, val, meta_t)
    meta_ref[...] = meta_t.T
    srow = lax.broadcasted_iota(jnp.int32, (SUBLANES, ROUTER_LANES), 0)
    len_row = run_len[:, 0:ROUTER_LANES].T[0:SUBLANES, :]
    start_row = run_start[:, 0:ROUTER_LANES].T[0:SUBLANES, :]
    stat_ref[0] = jnp.where(srow == STAT_LEN, len_row, jnp.where(srow == STAT_START, start_row, 0.0))


def _outproj(attn, hm, x2d, mod3, norm_g, w_out, w_rg, b_rg, w_re, b_re, seq):
    t, d = x2d.shape
    tm = MOE_TILE
    per_batch = seq // tm
    wr = jnp.pad(jnp.concatenate([w_re, w_rg], axis=1), ((0, 0), (0, ROUTER_LANES - N_EXPERTS - N_GROUPS)))
    br = jnp.pad(jnp.concatenate([b_re, b_rg]).reshape(1, -1), ((0, 0), (0, ROUTER_LANES - N_EXPERTS - N_GROUPS)))
    wr = wr.T
    br = jnp.broadcast_to(br.T, (ROUTER_LANES, tm))
    wr_hi = wr.astype(BF16)
    wr_lo = (wr - wr_hi.astype(F32)).astype(BF16)
    wout = w_out.astype(BF16)
    tok = lambda w: pl.BlockSpec((tm, w), lambda i: (i, 0))
    full = lambda a: pl.BlockSpec(a.shape, lambda i: (0,) * a.ndim)
    return pl.pallas_call(
        _outproj_kernel,
        grid=(t // tm,),
        in_specs=[tok(D_ATT), tok(D_MLSTM), tok(d),
                  pl.BlockSpec((1, 1, mod3.shape[2]), lambda i: (i // per_batch, 0, 0)),
                  full(norm_g), full(wout), full(wr_hi), full(wr_lo), full(br)],
        out_specs=[tok(d), tok(d), tok(ROUTER_LANES),
                   pl.BlockSpec((1, SUBLANES, ROUTER_LANES), lambda i: (i, 0, 0))],
        out_shape=[jax.ShapeDtypeStruct((t, d), F32), jax.ShapeDtypeStruct((t, d), BF16),
                   jax.ShapeDtypeStruct((t, ROUTER_LANES), F32),
                   jax.ShapeDtypeStruct((t // tm, SUBLANES, ROUTER_LANES), F32)],
        compiler_params=_params(("arbitrary",)),
        name="outproj",
    )(attn, hm, x2d, mod3, norm_g, wout, wr_hi, wr_lo, br)


def _for_each_piece(length, pieces, fn):
    pos = jnp.int32(0)
    for size in pieces:
        take = length & size

        @pl.when(take != 0)
        def _(pos=pos, size=size):
            fn(pl.multiple_of(pos, RUN_ALIGN), size)

        pos = pos + take


def _for_each_chunk(tile, rows_ref, cdst_ref, fn):
    def chunk(c, priority):
        fn(pl.multiple_of(c * RUN_ALIGN, RUN_ALIGN),
           pl.multiple_of(cdst_ref[tile * CHUNKS_PER_TILE + c], RUN_ALIGN), priority)

    def body(pair, carry):
        for priority in range(2):
            chunk(2 * pair + priority, priority)
        return carry

    n_chunks = rows_ref[tile] // RUN_ALIGN
    lax.fori_loop(0, n_chunks // 2, body, 0)

    @pl.when(n_chunks % 2 == 1)
    def _():
        chunk(n_chunks - 1, 0)


def _dispatch_kernel(cdst_ref, rows_ref, tail_ref, tlen_ref, nu_ref, meta_ref, h2_ref, xs_ref,
                     perm_ref, zero_ref, sem, zsem):
    i = pl.program_id(0)
    rb = MOE_BLOCK_ROWS
    n_blocks = xs_ref.shape[0] // rb

    @pl.when(i == 0)
    def _():
        zero_ref[...] = jnp.zeros(zero_ref.shape, zero_ref.dtype)

        def zero_copy(row, size):
            return pltpu.make_async_copy(zero_ref.at[pl.ds(0, size)], xs_ref.at[pl.ds(row, size)], zsem)

        def fill(op):
            def per_expert(e, carry):
                start = tail_ref[e]
                _for_each_piece(tlen_ref[e], TAIL_PIECES,
                                lambda pos, size: op(zero_copy(pl.multiple_of(start + pos, RUN_ALIGN), size)))
                return carry

            lax.fori_loop(0, N_EXPERTS, per_expert, 0)

            def per_block(blk, carry):
                for part in range(rb // ZERO_ROWS):
                    op(zero_copy(pl.multiple_of(blk * rb + part * ZERO_ROWS, ZERO_ROWS), ZERO_ROWS))
                return carry

            lax.fori_loop(nu_ref[0], n_blocks, per_block, 0)

        fill(lambda cp: cp.start())
        fill(lambda cp: cp.wait())

    meta_t = meta_ref[...].T
    l0 = meta_t[META_L0:META_L0 + 1, :].astype(jnp.int32)
    l1 = meta_t[META_L1:META_L1 + 1, :].astype(jnp.int32)
    rpos = lax.broadcasted_iota(jnp.int32, (perm_ref.shape[1], meta_ref.shape[0]), 0)
    onehot = jnp.where((rpos == l0) | (rpos == l1), 1.0, 0.0).astype(BF16)
    slot = i % 2
    perm_ref[slot] = _dot(onehot, h2_ref[...]).astype(BF16)

    def run_copy(s, loc, glob, size):
        return pltpu.make_async_copy(perm_ref.at[s, pl.ds(loc, size)], xs_ref.at[pl.ds(glob, size)], sem.at[s])

    def wait_tile(tile, s):
        _for_each_piece(rows_ref[tile], TILE_PIECES, lambda pos, size: run_copy(s, 0, 0, size).wait())

    @pl.when(i > 0)
    def _():
        wait_tile(i - 1, 1 - slot)

    _for_each_chunk(i, rows_ref, cdst_ref,
                    lambda loc, glob, priority: run_copy(slot, loc, glob, RUN_ALIGN).start(priority=priority))

    @pl.when(i == pl.num_programs(0) - 1)
    def _():
        wait_tile(i, slot)


def _dispatch(plan, meta, h2):
    t, d = h2.shape
    tm = MOE_TILE
    grid_spec = pltpu.PrefetchScalarGridSpec(
        num_scalar_prefetch=5,
        grid=(t // tm,),
        in_specs=[pl.BlockSpec((tm, ROUTER_LANES), lambda i, *_: (i, 0)),
                  pl.BlockSpec((tm, d), lambda i, *_: (i, 0))],
        out_specs=pl.BlockSpec(memory_space=pl.ANY),
        scratch_shapes=[pltpu.VMEM((2, PERM_ROWS, d), BF16), pltpu.VMEM((ZERO_ROWS, d), BF16),
                        pltpu.SemaphoreType.DMA((2,)), pltpu.SemaphoreType.DMA(())],
    )
    return pl.pallas_call(
        _dispatch_kernel,
        grid_spec=grid_spec,
        out_shape=jax.ShapeDtypeStruct((plan["n_rows"], d), BF16),
        compiler_params=_params(("arbitrary",)),
        name="dispatch",
    )(plan["chunk_dst"], plan["tile_rows"], plan["tail_start"], plan["tail_len"], plan["n_used"], meta, h2)


def _expert_kernel(be_ref, nu_ref, xs_ref, w1_ref, w3_ref, w2_ref, y_ref, w1b_ref, w3b_ref, w2b_ref):
    i = pl.program_id(0)
    used = i < nu_ref[0]

    @pl.when(used & ((i == 0) | (be_ref[i] != be_ref[jnp.maximum(i, 1) - 1])))
    def _():
        w1b_ref[...] = w1_ref[0].astype(BF16)
        w3b_ref[...] = w3_ref[0].astype(BF16)
        w2b_ref[...] = w2_ref[0].astype(BF16)

    @pl.when(used)
    def _():
        x = xs_ref[...]
        a = _dot(x, w1b_ref[...])
        b = _dot(x, w3b_ref[...])
        y_ref[...] = _dot((_silu(a) * b).astype(BF16), w2b_ref[...]).astype(y_ref.dtype)

    @pl.when(jnp.logical_not(used))
    def _():
        y_ref[...] = jnp.zeros(y_ref.shape, y_ref.dtype)


def _experts(block_e, n_used, xs, w1, w3, w2):
    n_rows, d = xs.shape
    rb = MOE_BLOCK_ROWS
    de = w1.shape[2]
    used = lambda i, be, nu: jnp.minimum(i, nu[0] - 1)
    grid_spec = pltpu.PrefetchScalarGridSpec(
        num_scalar_prefetch=2,
        grid=(n_rows // rb,),
        in_specs=[pl.BlockSpec((rb, d), lambda i, be, nu: (used(i, be, nu), 0)),
                  pl.BlockSpec((1, d, de), lambda i, be, nu: (be[used(i, be, nu)], 0, 0)),
                  pl.BlockSpec((1, d, de), lambda i, be, nu: (be[used(i, be, nu)], 0, 0)),
                  pl.BlockSpec((1, de, d), lambda i, be, nu: (be[used(i, be, nu)], 0, 0))],
        out_specs=pl.BlockSpec((rb, d), lambda i, be, nu: (i, 0)),
        scratch_shapes=[pltpu.VMEM((d, de), BF16), pltpu.VMEM((d, de), BF16), pltpu.VMEM((de, d), BF16)],
    )
    return pl.pallas_call(
        _expert_kernel,
        grid_spec=grid_spec,
        out_shape=jax.ShapeDtypeStruct((n_rows, d), BF16),
        compiler_params=_params(("arbitrary",)),
        name="experts",
    )(block_e, n_used, xs, w1, w3, w2)


def _combine_kernel(final_norm, cdst_ref, rows_ref, x1_ref, meta_ref, mod_ref, g_ref, y_ref, o_ref, ybuf_ref, sem):
    tc, d = x1_ref.shape
    i = pl.program_id(0)

    slot = i % 2

    def run_copy(s, loc, glob, size):
        return pltpu.make_async_copy(y_ref.at[pl.ds(glob, size)], ybuf_ref.at[s, pl.ds(loc, size)], sem.at[s])

    def start_tile(tile, s):
        _for_each_chunk(tile, rows_ref, cdst_ref,
                        lambda loc, glob, priority: run_copy(s, loc, glob, RUN_ALIGN).start(priority=priority))

    @pl.when(i == 0)
    def _():
        ybuf_ref[...] = jnp.zeros(ybuf_ref.shape, ybuf_ref.dtype)
        start_tile(0, 0)

    @pl.when(i + 1 < pl.num_programs(0))
    def _():
        start_tile(i + 1, 1 - slot)

    _for_each_piece(rows_ref[i], TILE_PIECES, lambda pos, size: run_copy(slot, 0, 0, size).wait())

    meta = meta_ref[...]
    yb = ybuf_ref[slot]
    rpos = lax.broadcasted_iota(jnp.int32, (tc, ybuf_ref.shape[1]), 1)
    sel = jnp.zeros(rpos.shape, F32)
    for l_lane, w_lane in ((META_L0, META_W0), (META_L1, META_W1)):
        sel = jnp.where(rpos == meta[:, l_lane:l_lane + 1].astype(jnp.int32), meta[:, w_lane:w_lane + 1], sel)
    moe = _dot(sel.astype(BF16), yb)
    x2 = x1_ref[...] + mod_ref[0][:, 5 * d:6 * d] * moe
    if final_norm:
        x2 = x2 * lax.rsqrt(jnp.mean(x2 * x2, axis=-1, keepdims=True) + NORM_EPS) * g_ref[...]
    o_ref[...] = x2


def _combine(plan, x1, meta, mod3, norm_f_g, y, seq, final_norm):
    t, d = x1.shape
    tc = MOE_TILE
    per_batch = seq // tc
    gf = norm_f_g.reshape(1, d)
    grid_spec = pltpu.PrefetchScalarGridSpec(
        num_scalar_prefetch=2,
        grid=(t // tc,),
        in_specs=[pl.BlockSpec((tc, d), lambda i, *_: (i, 0)),
                  pl.BlockSpec((tc, ROUTER_LANES), lambda i, *_: (i, 0)),
                  pl.BlockSpec((1, 1, mod3.shape[2]), lambda i, *_: (i // per_batch, 0, 0)),
                  pl.BlockSpec(gf.shape, lambda i, *_: (0, 0)),
                  pl.BlockSpec(memory_space=pl.ANY)],
        out_specs=pl.BlockSpec((tc, d), lambda i, *_: (i, 0)),
        scratch_shapes=[pltpu.VMEM((2, PERM_ROWS, d), BF16), pltpu.SemaphoreType.DMA((2,))],
    )
    return pl.pallas_call(
        functools.partial(_combine_kernel, final_norm),
        grid_spec=grid_spec,
        out_shape=jax.ShapeDtypeStruct((t, d), F32),
        compiler_params=_params(("arbitrary",)),
        name="combine",
    )(plan["chunk_dst"], plan["tile_rows"], x1, meta, mod3, gf, y)


def _routing_plan(stats, n_tokens):
    rb = MOE_BLOCK_ROWS
    run_len = stats[:, STAT_LEN, :N_EXPERTS].astype(jnp.int32)
    run_loc = stats[:, STAT_START, :N_EXPERTS].astype(jnp.int32)
    n_tiles = run_len.shape[0]
    total = jnp.sum(run_len, axis=0)
    padded = ((total + rb - 1) // rb) * rb
    pad_end = jnp.cumsum(padded)
    pad_start = pad_end - padded
    run_glob = pad_start[None, :] + jnp.cumsum(run_len, axis=0) - run_len
    n_rows = n_tokens * TOP_K_EXPERTS + n_tiles * N_EXPERTS * (RUN_ALIGN - 1) + N_EXPERTS * (rb - 1)
    n_rows = ((n_rows + rb - 1) // rb) * rb
    block_row = jnp.arange(n_rows // rb, dtype=jnp.int32) * rb
    block_e = jnp.minimum(jnp.sum((pad_end[None, :] <= block_row[:, None]).astype(jnp.int32), axis=1),
                          N_EXPERTS - 1)
    local_row = jnp.arange(CHUNKS_PER_TILE, dtype=jnp.int32) * RUN_ALIGN
    chunk_e = jnp.minimum(jnp.sum(((run_loc + run_len)[:, None, :] <= local_row[None, :, None]).astype(jnp.int32),
                                  axis=-1), N_EXPERTS - 1)
    shift = jnp.sum(jnp.where(chunk_e[..., None] == jnp.arange(N_EXPERTS, dtype=jnp.int32),
                              (run_glob - run_loc)[:, None, :], 0), axis=-1)
    chunk_dst = shift + local_row[None, :]
    return dict(chunk_dst=chunk_dst.reshape(-1), tile_rows=jnp.sum(run_len, axis=1),
                tail_start=pad_start + total, tail_len=padded - total,
                n_used=(pad_end[-1:] // rb).astype(jnp.int32), block_e=block_e.astype(jnp.int32), n_rows=n_rows)


def kernel(x, c, positions, w_ada, b_ada, norm1_g, w_in, b_gate, conv_w, conv_b, attn_out_g, mlstm_out_g,
           w_out, norm2_g, w_rg, b_rg, w_re, b_re, w1, w3, w2, norm_f_g):
    b, s, d = x.shape
    depth = w_ada.shape[0]
    assert d == D_MODEL and s % MOE_TILE == 0 and s % INPROJ_TILE == 0 and s % (KV_CHUNK * MOBA_BLOCK) == 0
    assert s % (MLSTM_CHUNKS_PER_STEP * MLSTM_CHUNK) == 0
    x2d = x.reshape(b * s, d)
    for l in range(depth):
        mod3 = _mod(c, w_ada[l], b_ada[l]).reshape(b, 1, 6 * d)
        q, k, vt, qkm, vm, om, gates = _inproj(x2d.reshape(b, s, d), positions, mod3,
                                              norm1_g[l].reshape(1, d), w_in[l])
        attn = _moba(q, k, vt, attn_out_g[l].reshape(1, D_ATT))
        hm = _mlstm(qkm, vm, om, gates, conv_w[l], conv_b[l], b_gate[l], mlstm_out_g[l])
        x1, h2, meta, stats = _outproj(attn.reshape(b * s, D_ATT), hm.reshape(b * s, D_MLSTM), x2d, mod3,
                                       norm2_g[l].reshape(1, d), w_out[l], w_rg[l], b_rg[l], w_re[l],
                                       b_re[l], s)
        plan = _routing_plan(stats, b * s)
        xs = _dispatch(plan, meta, h2)
        y = _experts(plan["block_e"], plan["n_used"], xs, w1[l], w3[l], w2[l])
        x2d = _combine(plan, x1, meta, mod3, norm_f_g, y, s, final_norm=(l == depth - 1))
    return x2d.reshape(b, s, d)
```
